```python
import jax
import jax.numpy as jnp
from jax import lax
import numpy as np

D_MODEL = 2048
BATCH = 4
SEQ = 2048
DEPTH = 4

CHUNK = 64
QBLOCK = 128
ROPE_BASE = 10000.0
MAX_POS_OFFSET = 4096
NORM_EPS = 1e-6

RET_HEADS = 8
RET_QK_DIM = 128
RET_V_DIM = 128
RET_WIDTH = RET_HEADS * RET_V_DIM

LRU_WIDTH = 1024
LRU_BLOCKS = 8
LRU_BLOCK_DIM = LRU_WIDTH // LRU_BLOCKS
CONV_WIDTH = 4
LRU_C = 8.0

MLA_HEADS = 8
MLA_NOPE_DIM = 128
MLA_ROPE_DIM = 64
MLA_V_DIM = 128
MLA_Q_LORA = 512
MLA_KV_LORA = 512
MLA_WIDTH = MLA_HEADS * MLA_V_DIM

N_BRANCH = 3
MIX_WIDTH = RET_WIDTH + LRU_WIDTH + MLA_WIDTH
IN_SPLITS = (
    RET_HEADS * RET_QK_DIM,
    RET_HEADS * RET_QK_DIM,
    RET_WIDTH,
    RET_WIDTH,
    LRU_WIDTH,
    LRU_WIDTH,
    MLA_Q_LORA,
    MLA_KV_LORA,
    MLA_ROPE_DIM,
    MLA_WIDTH,
    N_BRANCH * D_MODEL,
)
IN_WIDTH = sum(IN_SPLITS)

kernel_name = 'hybrid_retention_rglru_mla_streaming_block'


def rms_norm(x, gain):
    xf = x.astype(jnp.float32)
    y = xf * lax.rsqrt(jnp.mean(xf * xf, axis=-1, keepdims=True) + NORM_EPS)
    return (y * gain.astype(jnp.float32)).astype(x.dtype)


def rope_tables(positions, dim):
    inv_freq = ROPE_BASE ** (-jnp.arange(0, dim, 2, dtype=jnp.float32) / dim)
    ang = positions.astype(jnp.float32)[:, :, None, None] * inv_freq
    return jnp.cos(ang), jnp.sin(ang)


def apply_rope(x, cos, sin):
    half = x.shape[-1] // 2
    xf = x.astype(jnp.float32)
    x1, x2 = xf[..., :half], xf[..., half:]
    return jnp.concatenate([x1 * cos - x2 * sin, x2 * cos + x1 * sin], axis=-1).astype(x.dtype)


def retention_branch(q, k, v, gate, gn, cos, sin):
    B, S = q.shape[:2]
    NC = S // CHUNK
    q = apply_rope(q.reshape(B, S, RET_HEADS, RET_QK_DIM), cos, sin) * (RET_QK_DIM ** -0.5)
    k = apply_rope(k.reshape(B, S, RET_HEADS, RET_QK_DIM), cos, sin)
    q = q.reshape(B, NC, CHUNK, RET_HEADS, RET_QK_DIM)
    k = k.reshape(B, NC, CHUNK, RET_HEADS, RET_QK_DIM)
    v = v.reshape(B, NC, CHUNK, RET_HEADS, RET_V_DIM)

    log_gamma = jnp.log1p(-jnp.exp2(-5.0 - jnp.arange(RET_HEADS, dtype=jnp.float32)))
    idx = jnp.arange(CHUNK, dtype=jnp.float32)
    intra_decay = jnp.exp(log_gamma[:, None, None] * jnp.abs(idx[:, None] - idx[None, :]))

    scores = jnp.einsum('bnihd,bnjhd->bhnij', q, k) * intra_decay[None, :, None]
    o_intra = jnp.einsum('bhnij,bnjhe->bnihe', scores, v)

    k_dec = k * jnp.exp(log_gamma[None, :] * (CHUNK - 1 - idx)[:, None])[None, None, :, :, None]
    kv_chunk = jnp.einsum('bnjhd,bnjhe->nbhde', k_dec, v)
    chunk_decay = jnp.exp(log_gamma * CHUNK)[None, :, None, None]

    def step(state, kv_n):
        return state * chunk_decay + kv_n, state

    _, prev_state = lax.scan(step, jnp.zeros(kv_chunk.shape[1:], kv_chunk.dtype), kv_chunk)
    q_dec = q * jnp.exp(log_gamma[None, :] * (idx + 1.0)[:, None])[None, None, :, :, None]
    o_inter = jnp.einsum('bnihd,nbhde->bnihe', q_dec, prev_state)

    o = (o_intra + o_inter).reshape(B, S, RET_HEADS, RET_V_DIM).astype(jnp.float32)
    mean = jnp.mean(o, axis=-1, keepdims=True)
    var = jnp.mean(jnp.square(o - mean), axis=-1, keepdims=True)
    o = ((o - mean) * lax.rsqrt(var + NORM_EPS)).reshape(B, S, RET_WIDTH) * gn.astype(jnp.float32)
    return o.astype(gate.dtype) * jax.nn.silu(gate)


def rglru_branch(xb, gate, conv_w, conv_b, wa, ba, wx, bx, lam):
    B, S, W = xb.shape
    xc = lax.conv_general_dilated(
        xb, conv_w[:, None, :].astype(xb.dtype), window_strides=(1,),
        padding=[(CONV_WIDTH - 1, 0)], dimension_numbers=('NWC', 'WIO', 'NWC'),
        feature_group_count=W) + conv_b
    xr = xc.reshape(B, S, LRU_BLOCKS, LRU_BLOCK_DIM)
    r = jax.nn.sigmoid(jnp.einsum('bsnc,ncd->bsnd', xr, wa).reshape(B, S, W) + ba)
    i = jax.nn.sigmoid(jnp.einsum('bsnc,ncd->bsnd', xr, wx).reshape(B, S, W) + bx)
    log_a = -LRU_C * r.astype(jnp.float32) * jax.nn.softplus(-lam.astype(jnp.float32))
    a = jnp.exp(log_a)
    b = jnp.sqrt(-jnp.expm1(2.0 * log_a)) * (i * xc).astype(jnp.float32)

    def combine(left, right):
        a1, b1 = left
        a2, b2 = right
        return a1 * a2, a2 * b1 + b2

    _, h = lax.associative_scan(combine, (a, b), axis=1)
    return h.astype(xb.dtype) * jax.nn.silu(gate)


def mla_branch(q_lat, kv_lat, k_rope, gate, q_norm, w_uq, kv_norm, w_ukv, cos, sin):
    B, S = q_lat.shape[:2]
    q = (rms_norm(q_lat, q_norm) @ w_uq).reshape(B, S, MLA_HEADS, MLA_NOPE_DIM + MLA_ROPE_DIM)
    q_nope = q[..., :MLA_NOPE_DIM]
    q_rope = apply_rope(q[..., MLA_NOPE_DIM:], cos, sin)
    kv = (rms_norm(kv_lat, kv_norm) @ w_ukv).reshape(B, S, MLA_HEADS, MLA_NOPE_DIM + MLA_V_DIM)
    k_nope, v = kv[..., :MLA_NOPE_DIM], kv[..., MLA_NOPE_DIM:]
    k_rope = apply_rope(k_rope[:, :, None, :], cos, sin)[:, :, 0]
    scale = (MLA_NOPE_DIM + MLA_ROPE_DIM) ** -0.5

    outs = []
    for qb in range(S // QBLOCK):
        qs, qe = qb * QBLOCK, (qb + 1) * QBLOCK
        s = (jnp.einsum('bqhd,bkhd->bhqk', q_nope[:, qs:qe], k_nope[:, :qe])
             + jnp.einsum('bqhr,bkr->bhqk', q_rope[:, qs:qe], k_rope[:, :qe]))
        s = s.astype(jnp.float32) * scale
        q_chunk = (qs + jnp.arange(QBLOCK)) // CHUNK
        k_chunk = jnp.arange(qe) // CHUNK
        mask = k_chunk[None, :] <= q_chunk[:, None]
        p = jax.nn.softmax(jnp.where(mask, s, -1e30), axis=-1).astype(v.dtype)
        outs.append(jnp.einsum('bhqk,bkhd->bqhd', p, v[:, :qe]))
    o = jnp.concatenate(outs, axis=1).reshape(B, S, MLA_WIDTH)
    return o * jax.nn.silu(gate)


def hybrid_layer(x, c_act, ada_w, ada_b, norm_pre, norm_post, w_in, ret_gn,
                 lru_conv_w, lru_conv_b, lru_wa, lru_ba, lru_wx, lru_bx, lru_lambda,
                 mla_q_norm, mla_w_uq, mla_kv_norm, mla_w_ukv, w_branch, w_out,
                 cos_ret, sin_ret, cos_mla, sin_mla):
    B, S, _ = x.shape
    mod = c_act @ ada_w + ada_b
    shift, scale, res_gate = jnp.split(mod, 3, axis=-1)
    h = rms_norm(x, norm_pre) * (1.0 + scale[:, None, :]) + shift[:, None, :]

    proj = h @ w_in
    offsets = [int(o) for o in np.cumsum(IN_SPLITS)[:-1]]
    (rq, rk, rv, rg, lx, lg, mq, mkv, mkr, mg, merge_logits) = jnp.split(proj, offsets, axis=-1)

    y_ret = retention_branch(rq, rk, rv, rg, ret_gn, cos_ret, sin_ret)
    y_lru = rglru_branch(lx, lg, lru_conv_w, lru_conv_b, lru_wa, lru_ba, lru_wx, lru_bx, lru_lambda)
    y_mla = mla_branch(mq, mkv, mkr, mg, mla_q_norm, mla_w_uq, mla_kv_norm, mla_w_ukv, cos_mla, sin_mla)

    gates = jax.nn.sigmoid(merge_logits.astype(jnp.float32)).astype(x.dtype).reshape(B, S, N_BRANCH, D_MODEL)
    wb_ret = w_branch[:RET_WIDTH]
    wb_lru = w_branch[RET_WIDTH:RET_WIDTH + LRU_WIDTH]
    wb_mla = w_branch[RET_WIDTH + LRU_WIDTH:]
    merged = (gates[:, :, 0] * (y_ret @ wb_ret)
              + gates[:, :, 1] * (y_lru @ wb_lru)
              + gates[:, :, 2] * (y_mla @ wb_mla))
    y = merged @ w_out
    return x + (1.0 + res_gate[:, None, :]) * rms_norm(y, norm_post)


def setup_inputs(seed: int = 0) -> dict:
    key = jax.random.key(seed)
    ks = jax.random.split(key, 24)
    f32 = jnp.float32

    def nrm(k, shape, s):
        return jax.random.normal(k, shape, f32) * s

    x = nrm(ks[0], (BATCH, SEQ, D_MODEL), 1.0)
    c = nrm(ks[1], (BATCH, D_MODEL), 1.0)
    positions = (jnp.arange(SEQ, dtype=jnp.int32)[None, :]
                 + jax.random.randint(ks[2], (BATCH, 1), 0, MAX_POS_OFFSET, dtype=jnp.int32))
    ada_w = nrm(ks[3], (DEPTH, D_MODEL, 3 * D_MODEL), 0.5 * D_MODEL ** -0.5)
    ada_b = nrm(ks[4], (DEPTH, 3 * D_MODEL), 0.01)
    norm_pre = 1.0 + nrm(ks[5], (DEPTH, D_MODEL), 0.01)
    norm_post = 1.0 + nrm(ks[6], (DEPTH, D_MODEL), 0.01)
    w_in = nrm(ks[7], (DEPTH, D_MODEL, IN_WIDTH), D_MODEL ** -0.5)
    ret_gn = 1.0 + nrm(ks[8], (DEPTH, RET_WIDTH), 0.01)
    lru_conv_w = nrm(ks[9], (DEPTH, CONV_WIDTH, LRU_WIDTH), CONV_WIDTH ** -0.5)
    lru_conv_b = nrm(ks[10], (DEPTH, LRU_WIDTH), 0.01)
    lru_wa = nrm(ks[11], (DEPTH, LRU_BLOCKS, LRU_BLOCK_DIM, LRU_BLOCK_DIM), LRU_BLOCK_DIM ** -0.5)
    lru_ba = nrm(ks[12], (DEPTH, LRU_WIDTH), 0.01)
    lru_wx = nrm(ks[13], (DEPTH, LRU_BLOCKS, LRU_BLOCK_DIM, LRU_BLOCK_DIM), LRU_BLOCK_DIM ** -0.5)
    lru_bx = nrm(ks[14], (DEPTH, LRU_WIDTH), 0.01)
    u = jax.random.uniform(ks[15], (DEPTH, LRU_WIDTH), f32, 0.9, 0.999)
    a0 = u ** (1.0 / LRU_C)
    lru_lambda = jnp.log(a0) - jnp.log1p(-a0)
    mla_q_norm = 1.0 + nrm(ks[16], (DEPTH, MLA_Q_LORA), 0.01)
    mla_w_uq = nrm(ks[17], (DEPTH, MLA_Q_LORA, MLA_HEADS * (MLA_NOPE_DIM + MLA_ROPE_DIM)), MLA_Q_LORA ** -0.5)
    mla_kv_norm = 1.0 + nrm(ks[18], (DEPTH, MLA_KV_LORA), 0.01)
    mla_w_ukv = nrm(ks[19], (DEPTH, MLA_KV_LORA, MLA_HEADS * (MLA_NOPE_DIM + MLA_V_DIM)), MLA_KV_LORA ** -0.5)
    w_branch = nrm(ks[20], (DEPTH, MIX_WIDTH, D_MODEL), (MIX_WIDTH // N_BRANCH) ** -0.5)
    w_out = nrm(ks[21], (DEPTH, D_MODEL, D_MODEL), D_MODEL ** -0.5)
    return {'x': x, 'c': c, 'positions': positions, 'ada_w': ada_w, 'ada_b': ada_b,
            'norm_pre': norm_pre, 'norm_post': norm_post, 'w_in': w_in, 'ret_gn': ret_gn,
            'lru_conv_w': lru_conv_w, 'lru_conv_b': lru_conv_b, 'lru_wa': lru_wa, 'lru_ba': lru_ba,
            'lru_wx': lru_wx, 'lru_bx': lru_bx, 'lru_lambda': lru_lambda,
            'mla_q_norm': mla_q_norm, 'mla_w_uq': mla_w_uq, 'mla_kv_norm': mla_kv_norm,
            'mla_w_ukv': mla_w_ukv, 'w_branch': w_branch, 'w_out': w_out}


def reference(x, c, positions, ada_w, ada_b, norm_pre, norm_post, w_in, ret_gn,
              lru_conv_w, lru_conv_b, lru_wa, lru_ba, lru_wx, lru_bx, lru_lambda,
              mla_q_norm, mla_w_uq, mla_kv_norm, mla_w_ukv, w_branch, w_out):
    c_act = jax.nn.silu(c)
    cos_ret, sin_ret = rope_tables(positions, RET_QK_DIM)
    cos_mla, sin_mla = rope_tables(positions, MLA_ROPE_DIM)
    for l in range(DEPTH):
        x = hybrid_layer(x, c_act, ada_w[l], ada_b[l], norm_pre[l], norm_post[l], w_in[l], ret_gn[l],
                         lru_conv_w[l], lru_conv_b[l], lru_wa[l], lru_ba[l], lru_wx[l], lru_bx[l],
                         lru_lambda[l], mla_q_norm[l], mla_w_uq[l], mla_kv_norm[l], mla_w_ukv[l],
                         w_branch[l], w_out[l], cos_ret, sin_ret, cos_mla, sin_mla)
    return x
```

```python
import functools
import math

import jax
import jax.numpy as jnp
from jax import lax
from jax.experimental import pallas as pl
from jax.experimental.pallas import tpu as pltpu

F32 = jnp.float32
BF16 = jnp.bfloat16

D_MODEL = 2048
DEPTH = 4
CHUNK = 64
ROPE_BASE = 10000.0
NORM_EPS = 1e-6

RET_HEADS = 8
RET_DIM = 128
RET_WIDTH = RET_HEADS * RET_DIM

LRU_WIDTH = 1024
LRU_BLOCKS = 8
LRU_BLOCK_DIM = 128
CONV_WIDTH = 4
LRU_C = 8.0

MLA_HEADS = 8
MLA_NOPE = 128
MLA_ROPE = 64
MLA_V = 128
MLA_LORA = 512
MLA_WIDTH = MLA_HEADS * MLA_V
MLA_QK_PAD = 256

LANE = 128
SUBLANE = 8
VMEM_LIMIT = 56 * 1024 * 1024

COL_MERGE = 0
COL_RQ = 3 * D_MODEL
COL_RK = COL_RQ + 1024
COL_RV = COL_RK + 1024
COL_RG = COL_RV + 1024
COL_LX = COL_RG + 1024
COL_LG = COL_LX + 1024
COL_MQ = COL_LG + 1024
COL_MKV = COL_MQ + MLA_LORA
COL_MG = COL_MKV + MLA_LORA
COL_MKR = COL_MG + MLA_WIDTH
PROJ_TN = 768
PROJ_WIDTH = 19 * PROJ_TN
assert COL_MKR + LANE <= PROJ_WIDTH

PROJ_TM = 1024
NORM_ROWS = 32

RET_T = 256
LRU_T = 256
MLA_TQ = 256
MERGE_TM = 256
ADA_TN = 1024


def _silu(v):
    return v * jax.nn.sigmoid(v)


def _dot(a, b):
    return jnp.dot(a, b, preferred_element_type=F32)


def _dot_nt(a, b):
    return lax.dot_general(a, b, (((1,), (1,)), ((), ())), preferred_element_type=F32)


def _dot_tn(a, b):
    return lax.dot_general(a, b, (((0,), (0,)), ((), ())), preferred_element_type=F32)


def _params(*sem):
    return pltpu.CompilerParams(dimension_semantics=sem, vmem_limit_bytes=VMEM_LIMIT)


def _ada_kernel(c_ref, w_ref, b_ref, o_ref):
    c_act = _silu(c_ref[...]).astype(BF16)
    o_ref[0] = _dot(c_act, w_ref[0].astype(BF16)) + b_ref[0]


def _ada_call(c_pad, ada_w, ada_b):
    depth, d, n = ada_w.shape
    rows = c_pad.shape[0]
    return pl.pallas_call(
        _ada_kernel,
        grid=(depth, n // ADA_TN),
        in_specs=[
            pl.BlockSpec((rows, d), lambda l, j: (0, 0)),
            pl.BlockSpec((1, d, ADA_TN), lambda l, j: (l, 0, j)),
            pl.BlockSpec((1, 1, ADA_TN), lambda l, j: (l, 0, j)),
        ],
        out_specs=pl.BlockSpec((1, rows, ADA_TN), lambda l, j: (l, 0, j)),
        out_shape=jax.ShapeDtypeStruct((depth, rows, n), F32),
        compiler_params=_params("arbitrary", "arbitrary"),
        name="ada_mod",
    )(c_pad, ada_w, ada_b.reshape(depth, 1, n))


def _inproj_kernel(x_ref, mod_ref, g_ref, w_ref, o_ref, h_ref):
    @pl.when(pl.program_id(1) == 0)
    def _():
        gain = g_ref[...]
        shift = mod_ref[0, 0:1, :]
        scale1 = 1.0 + mod_ref[0, 1:2, :]

        def body(i, carry):
            r0 = pl.multiple_of(i * NORM_ROWS, NORM_ROWS)
            x = x_ref[pl.ds(r0, NORM_ROWS), :]
            inv = lax.rsqrt(jnp.mean(x * x, axis=-1, keepdims=True) + NORM_EPS)
            h_ref[pl.ds(r0, NORM_ROWS), :] = ((x * inv * gain) * scale1 + shift).astype(BF16)
            return carry

        lax.fori_loop(0, PROJ_TM // NORM_ROWS, body, 0)

    o_ref[...] = _dot(h_ref[...], w_ref[...]).astype(o_ref.dtype)


def _inproj_call(x2, mod, gain, w_p, seq):
    m, d = x2.shape
    tiles_per_batch = seq // PROJ_TM
    return pl.pallas_call(
        _inproj_kernel,
        grid=(m // PROJ_TM, PROJ_WIDTH // PROJ_TN),
        in_specs=[
            pl.BlockSpec((PROJ_TM, d), lambda i, j: (i, 0)),
            pl.BlockSpec((1, 3, d), lambda i, j: (i // tiles_per_batch, 0, 0)),
            pl.BlockSpec((1, d), lambda i, j: (0, 0)),
            pl.BlockSpec((d, PROJ_TN), lambda i, j: (0, j)),
        ],
        out_specs=pl.BlockSpec((PROJ_TM, PROJ_TN), lambda i, j: (i, j)),
        out_shape=jax.ShapeDtypeStruct((m, PROJ_WIDTH), BF16),
        scratch_shapes=[pltpu.VMEM((PROJ_TM, d), BF16)],
        compiler_params=_params("arbitrary", "arbitrary"),
        name="in_proj",
    )(x2, mod, gain, w_p)


_LOG_GAMMA = tuple(math.log1p(-(2.0 ** (-5.0 - h))) for h in range(RET_HEADS))


def _ret_kernel(q_ref, k_ref, v_ref, g_ref, cos_ref, sin_ref, gn_ref, o_ref,
                state_ref, dmat_ref, dq_ref, dk_ref):
    t_blk = RET_T

    @pl.when((pl.program_id(0) == 0) & (pl.program_id(1) == 0))
    def _():
        ri = lax.broadcasted_iota(jnp.int32, (t_blk, t_blk), 0)
        ci = lax.broadcasted_iota(jnp.int32, (t_blk, t_blk), 1)
        dist = jnp.abs(ri - ci).astype(F32)
        visible = (ci // CHUNK) <= (ri // CHUNK)
        row = lax.broadcasted_iota(jnp.int32, (t_blk, RET_DIM), 0).astype(F32)
        for h in range(RET_HEADS):
            lg = _LOG_GAMMA[h]
            dmat_ref[h] = jnp.where(visible, jnp.exp(lg * dist), 0.0)
            dq_ref[h] = jnp.exp(lg * (row + 1.0))
            dk_ref[h] = jnp.exp(lg * ((t_blk - 1.0) - row))

    @pl.when(pl.program_id(1) == 0)
    def _():
        state_ref[...] = jnp.zeros_like(state_ref)

    cos = cos_ref[...]
    sin = sin_ref[...]
    for h in range(RET_HEADS):
        sl = slice(h * RET_DIM, (h + 1) * RET_DIM)
        q = q_ref[:, sl].astype(F32)
        k = k_ref[:, sl].astype(F32)
        q = (q * cos + pltpu.roll(q, RET_DIM // 2, 1) * sin) * (RET_DIM ** -0.5)
        k = k * cos + pltpu.roll(k, RET_DIM // 2, 1) * sin
        v = v_ref[:, sl]
        scores = _dot_nt(q.astype(BF16), k.astype(BF16)) * dmat_ref[h]
        o = _dot(scores.astype(BF16), v)
        state = state_ref[h]
        o = o + _dot((q * dq_ref[h]).astype(BF16), state.astype(BF16))
        k_dec = (k * dk_ref[h]).astype(BF16)
        state_ref[h] = state * math.exp(_LOG_GAMMA[h] * t_blk) + _dot_tn(k_dec, v)
        mean = jnp.mean(o, axis=-1, keepdims=True)
        cen = o - mean
        var = jnp.mean(cen * cen, axis=-1, keepdims=True)
        normed = cen * lax.rsqrt(var + NORM_EPS) * gn_ref[:, sl]
        o_ref[:, sl] = (normed * _silu(g_ref[:, sl].astype(F32))).astype(o_ref.dtype)


def _ret_call(proj, cos_r, sin_r, gn, batch, seq):
    m = proj.shape[0]
    nt = seq // RET_T
    w = RET_WIDTH

    def col(c):
        return pl.BlockSpec((RET_T, w), lambda b, t, c=c: (b * nt + t, c // w))

    tab = pl.BlockSpec((RET_T, RET_DIM), lambda b, t: (b * nt + t, 0))
    return pl.pallas_call(
        _ret_kernel,
        grid=(batch, nt),
        in_specs=[col(COL_RQ), col(COL_RK), col(COL_RV), col(COL_RG), tab, tab,
                  pl.BlockSpec((1, w), lambda b, t: (0, 0))],
        out_specs=pl.BlockSpec((RET_T, w), lambda b, t: (b * nt + t, 0)),
        out_shape=jax.ShapeDtypeStruct((m, w), BF16),
        scratch_shapes=[
            pltpu.VMEM((RET_HEADS, RET_DIM, RET_DIM), F32),
            pltpu.VMEM((RET_HEADS, RET_T, RET_T), F32),
            pltpu.VMEM((RET_HEADS, RET_T, RET_DIM), F32),
            pltpu.VMEM((RET_HEADS, RET_T, RET_DIM), F32),
        ],
        compiler_params=_params("arbitrary", "arbitrary"),
        name="retention",
    )(proj, proj, proj, proj, cos_r, sin_r, gn)


def _lru_kernel(x_ref, g_ref, cw_ref, cb_ref, wa_ref, ba_ref, wx_ref, bx_ref, lam_ref, o_ref,
                xbuf, hcar, a_s, b_s):
    t_blk = LRU_T
    halo = SUBLANE

    @pl.when(pl.program_id(1) == 0)
    def _():
        xbuf[0:halo, :] = jnp.zeros((halo, LRU_WIDTH), F32)
        hcar[...] = jnp.zeros_like(hcar)

    xbuf[halo:halo + t_blk, :] = x_ref[...].astype(F32)
    xc = cb_ref[...]
    for w in range(CONV_WIDTH):
        off = halo - (CONV_WIDTH - 1) + w
        xc = xc + cw_ref[w:w + 1, :] * xbuf[off:off + t_blk, :]
    xbuf[0:halo, :] = xbuf[t_blk:t_blk + halo, :]

    neg_lam = -lam_ref[...]
    softplus = jnp.maximum(neg_lam, 0.0) + jnp.log1p(jnp.exp(-jnp.abs(neg_lam)))
    for n in range(LRU_BLOCKS):
        sl = slice(n * LRU_BLOCK_DIM, (n + 1) * LRU_BLOCK_DIM)
        xn = xc[:, sl]
        xn_b = xn.astype(BF16)
        r = jax.nn.sigmoid(_dot(xn_b, wa_ref[n]) + ba_ref[:, sl])
        i = jax.nn.sigmoid(_dot(xn_b, wx_ref[n]) + bx_ref[:, sl])
        log_a = (-LRU_C) * r * softplus[:, sl]
        a = jnp.exp(log_a)
        a_s[:, sl] = a
        b_s[:, sl] = jnp.sqrt(jnp.tanh(-log_a) * (a * a + 1.0)) * (i * xn)

    row = lax.broadcasted_iota(jnp.int32, (SUBLANE, LRU_WIDTH), 0)

    def body(i, h):
        r0 = pl.multiple_of(i * SUBLANE, SUBLANE)
        a = a_s[pl.ds(r0, SUBLANE), :]
        b = b_s[pl.ds(r0, SUBLANE), :]
        for s in (1, 2, 4):
            keep = row >= s
            a_prev = jnp.where(keep, pltpu.roll(a, s, 0), 1.0)
            b_prev = jnp.where(keep, pltpu.roll(b, s, 0), 0.0)
            b = a * b_prev + b
            a = a * a_prev
        hrows = a * h + b
        b_s[pl.ds(r0, SUBLANE), :] = hrows
        return jnp.broadcast_to(hrows[SUBLANE - 1:SUBLANE, :], (SUBLANE, LRU_WIDTH))

    hcar[...] = lax.fori_loop(0, t_blk // SUBLANE, body, hcar[...])
    o_ref[...] = (b_s[...] * _silu(g_ref[...].astype(F32))).astype(o_ref.dtype)


def _lru_call(proj, cw, cb, wa, ba, wx, bx, lam, batch, seq):
    m = proj.shape[0]
    nt = seq // LRU_T
    w = LRU_WIDTH

    def col(c):
        return pl.BlockSpec((LRU_T, w), lambda b, t, c=c: (b * nt + t, c // w))

    def vec():
        return pl.BlockSpec((1, w), lambda b, t: (0, 0))

    def blk():
        return pl.BlockSpec((LRU_BLOCKS, LRU_BLOCK_DIM, LRU_BLOCK_DIM), lambda b, t: (0, 0, 0))

    return pl.pallas_call(
        _lru_kernel,
        grid=(batch, nt),
        in_specs=[col(COL_LX), col(COL_LG),
                  pl.BlockSpec((CONV_WIDTH, w), lambda b, t: (0, 0)), vec(),
                  blk(), vec(), blk(), vec(), vec()],
        out_specs=pl.BlockSpec((LRU_T, w), lambda b, t: (b * nt + t, 0)),
        out_shape=jax.ShapeDtypeStruct((m, w), BF16),
        scratch_shapes=[
            pltpu.VMEM((LRU_T + SUBLANE, w), F32),
            pltpu.VMEM((SUBLANE, w), F32),
            pltpu.VMEM((LRU_T, w), F32),
            pltpu.VMEM((LRU_T, w), F32),
        ],
        compiler_params=_params("arbitrary", "arbitrary"),
        name="rg_lru",
    )(proj, proj, cw, cb, wa, ba, wx, bx, lam)


def _rope64(x, cos_t, sin_t):
    swapped = pltpu.roll(x, MLA_ROPE // 2, 1) + pltpu.roll(x, LANE - MLA_ROPE // 2, 1)
    return x * cos_t + swapped * sin_t


def _rms(x, gain):
    return x * lax.rsqrt(jnp.mean(x * x, axis=-1, keepdims=True) + NORM_EPS) * gain


def _mla_kernel(mq_ref, mkv_ref, mkr_ref, mg_ref, wq_ref, wkv_ref, qn_ref, kvn_ref,
                cos_ref, sin_ref, o_ref, qlat, kvlat, krope, qs, ks, vs, *, seq):
    norm_rows = 256

    @pl.when(pl.program_id(1) == 0)
    def _():
        for i in range(seq // norm_rows):
            rs = slice(i * norm_rows, (i + 1) * norm_rows)
            qlat[rs, :] = _rms(mq_ref[rs, :].astype(F32), qn_ref[...]).astype(BF16)
            kvlat[rs, :] = _rms(mkv_ref[rs, :].astype(F32), kvn_ref[...]).astype(BF16)
            krope[rs, :] = _rope64(mkr_ref[rs, :].astype(F32), cos_ref[rs, :], sin_ref[rs, :]).astype(BF16)

    scale = (MLA_NOPE + MLA_ROPE) ** -0.5
    for i in range(seq // norm_rows):
        rs = slice(i * norm_rows, (i + 1) * norm_rows)
        q = _dot(qlat[rs, :], wq_ref[...])
        qs[rs, 0:MLA_NOPE] = (q[:, 0:MLA_NOPE] * scale).astype(BF16)
        qs[rs, MLA_NOPE:] = (_rope64(q[:, MLA_NOPE:], cos_ref[rs, :], sin_ref[rs, :]) * scale).astype(BF16)
        kv = _dot(kvlat[rs, :], wkv_ref[...])
        ks[rs, 0:MLA_NOPE] = kv[:, 0:MLA_NOPE].astype(BF16)
        ks[rs, MLA_NOPE:] = krope[rs, :]
        vs[rs, :] = kv[:, MLA_NOPE:].astype(BF16)

    tq = MLA_TQ
    ri = lax.broadcasted_iota(jnp.int32, (tq, tq), 0)
    ci = lax.broadcasted_iota(jnp.int32, (tq, tq), 1)
    visible = (ci // CHUNK) <= (ri // CHUNK)
    for i in range(seq // tq):
        q0, q1 = i * tq, (i + 1) * tq
        qi = qs[q0:q1, :]
        s_d = jnp.where(visible, _dot_nt(qi, ks[q0:q1, :]), -1e30)
        m = jnp.max(s_d, axis=-1, keepdims=True)
        if i > 0:
            s_o = _dot_nt(qi, ks[0:q0, :])
            m = jnp.maximum(m, jnp.max(s_o, axis=-1, keepdims=True))
        p_d = jnp.exp(s_d - m)
        l = jnp.sum(p_d, axis=-1, keepdims=True)
        acc = _dot(p_d.astype(BF16), vs[q0:q1, :])
        if i > 0:
            p_o = jnp.exp(s_o - m)
            l = l + jnp.sum(p_o, axis=-1, keepdims=True)
            acc = acc + _dot(p_o.astype(BF16), vs[0:q0, :])
        out = acc / l
        o_ref[q0:q1, :] = (out * _silu(mg_ref[q0:q1, :].astype(F32))).astype(o_ref.dtype)


def _mla_call(proj, wq_p, wkv, qn, kvn, cos_m, sin_m, batch, seq):
    m = proj.shape[0]
    lora = MLA_LORA
    return pl.pallas_call(
        functools.partial(_mla_kernel, seq=seq),
        grid=(batch, MLA_HEADS),
        in_specs=[
            pl.BlockSpec((seq, lora), lambda b, h: (b, COL_MQ // lora)),
            pl.BlockSpec((seq, lora), lambda b, h: (b, COL_MKV // lora)),
            pl.BlockSpec((seq, LANE), lambda b, h: (b, COL_MKR // LANE)),
            pl.BlockSpec((seq, MLA_V), lambda b, h: (b, COL_MG // MLA_V + h)),
            pl.BlockSpec((lora, MLA_QK_PAD), lambda b, h: (0, h)),
            pl.BlockSpec((lora, MLA_NOPE + MLA_V), lambda b, h: (0, h)),
            pl.BlockSpec((1, lora), lambda b, h: (0, 0)),
            pl.BlockSpec((1, lora), lambda b, h: (0, 0)),
            pl.BlockSpec((seq, LANE), lambda b, h: (b, 0)),
            pl.BlockSpec((seq, LANE), lambda b, h: (b, 0)),
        ],
        out_specs=pl.BlockSpec((seq, MLA_V), lambda b, h: (b, h)),
        out_shape=jax.ShapeDtypeStruct((m, MLA_WIDTH), BF16),
        scratch_shapes=[
            pltpu.VMEM((seq, lora), BF16),
            pltpu.VMEM((seq, lora), BF16),
            pltpu.VMEM((seq, LANE), BF16),
            pltpu.VMEM((seq, MLA_QK_PAD), BF16),
            pltpu.VMEM((seq, MLA_QK_PAD), BF16),
            pltpu.VMEM((seq, MLA_V), BF16),
        ],
        compiler_params=_params("arbitrary", "arbitrary"),
        name="mla",
    )(proj, proj, proj, proj, wq_p, wkv, qn, kvn, cos_m, sin_m)


def _merge_kernel(yr_ref, yl_ref, ym_ref, l0_ref, l1_ref, l2_ref, x_ref, mod_ref, g_ref,
                  wb_ref, wo_ref, o_ref):
    merged = None
    for i, (y_ref, l_ref) in enumerate(((yr_ref, l0_ref), (yl_ref, l1_ref), (ym_ref, l2_ref))):
        z = _dot(y_ref[...], wb_ref[i * 1024:(i + 1) * 1024, :])
        z = z * jax.nn.sigmoid(l_ref[...].astype(F32))
        merged = z if merged is None else merged + z
    y = _dot(merged.astype(BF16), wo_ref[...])
    normed = _rms(y, g_ref[...])
    o_ref[...] = x_ref[...] + (1.0 + mod_ref[0, 2:3, :]) * normed


def _merge_call(y_ret, y_lru, y_mla, proj, x2, mod, gain, wb, wo, seq):
    m, d = x2.shape
    tm = MERGE_TM
    tiles_per_batch = seq // tm

    def rows(width, c=0):
        return pl.BlockSpec((tm, width), lambda i, c=c: (i, c))

    def resident(shape):
        return pl.BlockSpec(shape, lambda i: (0, 0), pipeline_mode=pl.Buffered(1))

    return pl.pallas_call(
        _merge_kernel,
        grid=(m // tm,),
        in_specs=[rows(1024), rows(1024), rows(1024),
                  rows(d, 0), rows(d, 1), rows(d, 2),
                  rows(d),
                  pl.BlockSpec((1, 3, d), lambda i: (i // tiles_per_batch, 0, 0)),
                  pl.BlockSpec((1, d), lambda i: (0, 0)),
                  resident(wb.shape), resident(wo.shape)],
        out_specs=rows(d),
        out_shape=jax.ShapeDtypeStruct((m, d), F32),
        compiler_params=_params("arbitrary"),
        name="merge_out",
    )(y_ret, y_lru, y_mla, proj, proj, proj, x2, mod, gain, wb, wo)


def _prep_w_in(w_in):
    d = w_in.shape[0]
    pre = w_in[:, :7168]
    mkr = w_in[:, 7168:7232]
    mg = w_in[:, 7232:8256]
    merge = w_in[:, 8256:]
    pad = jnp.zeros((d, PROJ_WIDTH - COL_MKR - MLA_ROPE), w_in.dtype)
    return jnp.concatenate([merge, pre, mg, mkr, pad], axis=1).astype(BF16)


def _prep_w_uq(w_uq):
    w = w_uq.reshape(MLA_LORA, MLA_HEADS, MLA_NOPE + MLA_ROPE)
    w = jnp.pad(w, ((0, 0), (0, 0), (0, MLA_QK_PAD - MLA_NOPE - MLA_ROPE)))
    return w.reshape(MLA_LORA, MLA_HEADS * MLA_QK_PAD).astype(BF16)


def _rope_tables(positions, dim):
    inv_freq = ROPE_BASE ** (-jnp.arange(0, dim, 2, dtype=F32) / dim)
    ang = positions.astype(F32)[:, :, None] * inv_freq
    return jnp.cos(ang), jnp.sin(ang)


def kernel(x, c, positions, ada_w, ada_b, norm_pre, norm_post, w_in, ret_gn, lru_conv_w, lru_conv_b,
           lru_wa, lru_ba, lru_wx, lru_bx, lru_lambda, mla_q_norm, mla_w_uq, mla_kv_norm, mla_w_ukv,
           w_branch, w_out):
    batch, seq, d = x.shape
    m = batch * seq
    assert d == D_MODEL and seq % PROJ_TM == 0 and seq % RET_T == 0 and seq % LRU_T == 0
    assert seq % MLA_TQ == 0 and m % MERGE_TM == 0 and seq % MERGE_TM == 0

    cos_r, sin_r = _rope_tables(positions, RET_DIM)
    cos_r = jnp.concatenate([cos_r, cos_r], axis=-1).reshape(m, RET_DIM)
    sin_r = jnp.concatenate([-sin_r, sin_r], axis=-1).reshape(m, RET_DIM)
    cos_m, sin_m = _rope_tables(positions, MLA_ROPE)
    tail = jnp.zeros((batch, seq, LANE - MLA_ROPE), F32)
    cos_m = jnp.concatenate([cos_m, cos_m, tail], axis=-1).reshape(m, LANE)
    sin_m = jnp.concatenate([-sin_m, sin_m, tail], axis=-1).reshape(m, LANE)

    c_pad = jnp.pad(c, ((0, SUBLANE - batch), (0, 0)))
    mod_all = _ada_call(c_pad, ada_w, ada_b)[:, :batch].reshape(DEPTH, batch, 3, d)

    x2 = x.reshape(m, d)
    for l in range(DEPTH):
        proj = _inproj_call(x2, mod_all[l], norm_pre[l][None], _prep_w_in(w_in[l]), seq)
        y_ret = _ret_call(proj, cos_r, sin_r, ret_gn[l][None], batch, seq)
        y_lru = _lru_call(proj, lru_conv_w[l], lru_conv_b[l][None],
                          lru_wa[l].astype(BF16), lru_ba[l][None],
                          lru_wx[l].astype(BF16), lru_bx[l][None], lru_lambda[l][None], batch, seq)
        y_mla = _mla_call(proj, _prep_w_uq(mla_w_uq[l]), mla_w_ukv[l].astype(BF16),
                          mla_q_norm[l][None], mla_kv_norm[l][None], cos_m, sin_m, batch, seq)
        x2 = _merge_call(y_ret, y_lru, y_mla, proj, x2, mod_all[l], norm_post[l][None],
                         w_branch[l].astype(BF16), w_out[l].astype(BF16), seq)
    return x2.reshape(batch, seq, d)
```

```python
import functools
import math

import jax
import jax.numpy as jnp
from jax import lax
from jax.experimental import pallas as pl
from jax.experimental.pallas import tpu as pltpu

F32 = jnp.float32
BF16 = jnp.bfloat16

D_MODEL = 2048
CHUNK = 64
ROPE_BASE = 10000.0
NORM_EPS = 1e-6

RET_HEADS = 8
RET_DIM = 128
RET_WIDTH = RET_HEADS * RET_DIM

LRU_WIDTH = 1024
LRU_BLOCKS = 8
LRU_BLOCK_DIM = 128
CONV_WIDTH = 4
LRU_C = 8.0

MLA_HEADS = 8
MLA_NOPE = 128
MLA_ROPE = 64
MLA_V = 128
MLA_LORA = 512
MLA_WIDTH = MLA_HEADS * MLA_V
MLA_QK = MLA_NOPE + MLA_ROPE
MLA_QK_PAD = 256

LANE = 128
SUBLANE = 8
VMEM_LIMIT = 56 * 1024 * 1024

SRC_MKR = 4 * 1024 + 2 * 1024 + 2 * MLA_LORA
SRC_WIDTH = SRC_MKR + MLA_ROPE + MLA_WIDTH + 3 * D_MODEL
COL_RQ = 0
COL_RK = 1024
COL_RV = 2048
COL_RG = 3072
COL_LX = 4096
COL_LG = 5120
COL_MQ = 6144
COL_MKV = COL_MQ + MLA_LORA
COL_MG = SRC_MKR
COL_MERGE = COL_MG + MLA_WIDTH
PROJ_WIDTH = COL_MERGE + 3 * D_MODEL
PROJ_TN = 1024
PROJ_TM = 1024
ALIGNED_TILES = SRC_MKR // PROJ_TN
CAST_ROWS = 64
assert SRC_MKR % PROJ_TN == 0 and PROJ_WIDTH % PROJ_TN == 0 and COL_MERGE % D_MODEL == 0

NORM_TM = 512
RET_T = 256
LRU_T = 256
MLA_TQ = 256
MERGE_TM = 256
ADA_TK = 256


def _silu(v):
    return v * jax.nn.sigmoid(v)


def _dot(a, b):
    return jnp.dot(a, b, preferred_element_type=F32)


def _dot_nt(a, b):
    return lax.dot_general(a, b, (((1,), (1,)), ((), ())), preferred_element_type=F32)


def _dot_tn(a, b):
    return lax.dot_general(a, b, (((0,), (0,)), ((), ())), preferred_element_type=F32)


def _rms(x, gain):
    return x * lax.rsqrt(jnp.mean(x * x, axis=-1, keepdims=True) + NORM_EPS) * gain


def _params(*sem):
    return pltpu.CompilerParams(dimension_semantics=sem, vmem_limit_bytes=VMEM_LIMIT)


def _layer_vec(arr):
    return arr.reshape(arr.shape[0], 1, arr.shape[1])


def _vec_spec(width, layer, ngrid):
    if ngrid == 1:
        return pl.BlockSpec((1, 1, width), lambda i: (layer, 0, 0))
    return pl.BlockSpec((1, 1, width), lambda i, j: (layer, 0, 0))


def _ada_kernel(c_ref, w_ref, b_ref, o_ref):
    @pl.when(pl.program_id(1) == 0)
    def _():
        o_ref[0] = jnp.broadcast_to(b_ref[0], o_ref.shape[1:])

    c_act = _silu(c_ref[...]).astype(BF16)
    o_ref[0] += _dot(c_act, w_ref[0].astype(BF16))


def _ada_call(c_pad, ada_w, ada_b):
    depth, d, n = ada_w.shape
    rows = c_pad.shape[0]
    return pl.pallas_call(
        _ada_kernel,
        grid=(depth, d // ADA_TK),
        in_specs=[
            pl.BlockSpec((rows, ADA_TK), lambda l, k: (0, k)),
            pl.BlockSpec((1, ADA_TK, n), lambda l, k: (l, k, 0)),
            pl.BlockSpec((1, 1, n), lambda l, k: (l, 0, 0)),
        ],
        out_specs=pl.BlockSpec((1, rows, n), lambda l, k: (l, 0, 0)),
        out_shape=jax.ShapeDtypeStruct((depth, rows, n), F32),
        compiler_params=_params("arbitrary", "arbitrary"),
        name="ada_mod",
    )(c_pad, ada_w, _layer_vec(ada_b))


def _modulate(x, gain, shift_ref, scale_ref, b):
    shift = shift_ref[0, pl.ds(b, 1), :]
    scale = scale_ref[0, pl.ds(b, 1), :]
    return _rms(x, gain) * (1.0 + scale) + shift


def _prenorm_kernel(x_ref, g_ref, shift_ref, scale_ref, o_ref, *, tiles_per_batch):
    b = pl.program_id(0) // tiles_per_batch
    o_ref[...] = _modulate(x_ref[...], g_ref[0], shift_ref, scale_ref, b).astype(o_ref.dtype)


def _prenorm_call(x2, mod_all, norm_pre, layer, seq):
    m, d = x2.shape
    rows = mod_all.shape[1]
    return pl.pallas_call(
        functools.partial(_prenorm_kernel, tiles_per_batch=seq // NORM_TM),
        grid=(m // NORM_TM,),
        in_specs=[
            pl.BlockSpec((NORM_TM, d), lambda i: (i, 0)),
            _vec_spec(d, layer, 1),
            pl.BlockSpec((1, rows, d), lambda i: (layer, 0, 0)),
            pl.BlockSpec((1, rows, d), lambda i: (layer, 0, 1)),
        ],
        out_specs=pl.BlockSpec((NORM_TM, d), lambda i: (i, 0)),
        out_shape=jax.ShapeDtypeStruct((m, d), BF16),
        compiler_params=_params("arbitrary"),
        name="pre_norm",
    )(x2, _layer_vec(norm_pre), mod_all, mod_all)


def _inproj_kernel(h_ref, w_ref, wx_ref, o_ref, wb_ref):
    j = pl.program_id(0)
    first_row_tile = pl.program_id(1) == 0
    n_chunks = D_MODEL // CAST_ROWS

    @pl.when(first_row_tile & (j < ALIGNED_TILES))
    def _():
        def body(i, carry):
            r0 = pl.multiple_of(i * CAST_ROWS, CAST_ROWS)
            wb_ref[pl.ds(r0, CAST_ROWS), :] = w_ref[0, pl.ds(r0, CAST_ROWS), :].astype(BF16)
            return carry

        lax.fori_loop(0, n_chunks, body, 0)

    @pl.when(first_row_tile & (j >= ALIGNED_TILES))
    def _():
        def body(i, carry):
            r0 = pl.multiple_of(i * CAST_ROWS, CAST_ROWS)
            window = jnp.concatenate(
                [w_ref[0, pl.ds(r0, CAST_ROWS), :], wx_ref[0, pl.ds(r0, CAST_ROWS), :]], axis=1)
            wb_ref[pl.ds(r0, CAST_ROWS), :] = window[:, MLA_ROPE:MLA_ROPE + PROJ_TN].astype(BF16)
            return carry

        lax.fori_loop(0, n_chunks, body, 0)

    o_ref[...] = _dot(h_ref[...], wb_ref[...]).astype(o_ref.dtype)


def _inproj_call(h, w_in, layer):
    m, d = h.shape
    last_lane_block = SRC_WIDTH // LANE
    per_tile = PROJ_TN // LANE
    return pl.pallas_call(
        _inproj_kernel,
        grid=(PROJ_WIDTH // PROJ_TN, m // PROJ_TM),
        in_specs=[
            pl.BlockSpec((PROJ_TM, d), lambda j, i: (i, 0)),
            pl.BlockSpec((1, d, PROJ_TN), lambda j, i: (layer, 0, j)),
            pl.BlockSpec((1, d, LANE),
                         lambda j, i: (layer, 0, jnp.minimum((j + 1) * per_tile, last_lane_block))),
        ],
        out_specs=pl.BlockSpec((PROJ_TM, PROJ_TN), lambda j, i: (i, j)),
        out_shape=jax.ShapeDtypeStruct((m, PROJ_WIDTH), BF16),
        scratch_shapes=[pltpu.VMEM((d, PROJ_TN), BF16)],
        compiler_params=_params("arbitrary", "arbitrary"),
        name="in_proj",
    )(h, w_in, w_in)


_LOG_GAMMA = tuple(math.log1p(-(2.0 ** (-5.0 - h))) for h in range(RET_HEADS))


def _ret_kernel(q_ref, k_ref, v_ref, g_ref, cos_ref, sin_ref, gn_ref, o_ref,
                state_ref, dmat_ref, dq_ref, dk_ref):
    t_blk = RET_T

    @pl.when((pl.program_id(0) == 0) & (pl.program_id(1) == 0))
    def _():
        ri = lax.broadcasted_iota(jnp.int32, (t_blk, t_blk), 0)
        ci = lax.broadcasted_iota(jnp.int32, (t_blk, t_blk), 1)
        dist = jnp.abs(ri - ci).astype(F32)
        visible = (ci // CHUNK) <= (ri // CHUNK)
        row = lax.broadcasted_iota(jnp.int32, (t_blk, RET_DIM), 0).astype(F32)
        for h in range(RET_HEADS):
            lg = _LOG_GAMMA[h]
            dmat_ref[h] = jnp.where(visible, jnp.exp(lg * dist), 0.0)
            dq_ref[h] = jnp.exp(lg * (row + 1.0))
            dk_ref[h] = jnp.exp(lg * ((t_blk - 1.0) - row))

    @pl.when(pl.program_id(1) == 0)
    def _():
        state_ref[...] = jnp.zeros_like(state_ref)

    cos = cos_ref[...]
    sin = sin_ref[...]
    for h in range(RET_HEADS):
        sl = slice(h * RET_DIM, (h + 1) * RET_DIM)
        q = q_ref[:, sl].astype(F32)
        k = k_ref[:, sl].astype(F32)
        q = (q * cos + pltpu.roll(q, RET_DIM // 2, 1) * sin) * (RET_DIM ** -0.5)
        k = k * cos + pltpu.roll(k, RET_DIM // 2, 1) * sin
        v = v_ref[:, sl]
        scores = _dot_nt(q.astype(BF16), k.astype(BF16)) * dmat_ref[h]
        o = _dot(scores.astype(BF16), v)
        state = state_ref[h]
        o = o + _dot((q * dq_ref[h]).astype(BF16), state.astype(BF16))
        k_dec = (k * dk_ref[h]).astype(BF16)
        state_ref[h] = state * math.exp(_LOG_GAMMA[h] * t_blk) + _dot_tn(k_dec, v)
        mean = jnp.mean(o, axis=-1, keepdims=True)
        cen = o - mean
        var = jnp.mean(cen * cen, axis=-1, keepdims=True)
        normed = cen * lax.rsqrt(var + NORM_EPS) * gn_ref[0, :, sl]
        o_ref[:, sl] = (normed * _silu(g_ref[:, sl].astype(F32))).astype(o_ref.dtype)


def _ret_call(proj, cos_r, sin_r, ret_gn, layer, batch, seq):
    m = proj.shape[0]
    nt = seq // RET_T
    w = RET_WIDTH

    def col(c):
        return pl.BlockSpec((RET_T, w), lambda b, t, c=c: (b * nt + t, c // w))

    tab = pl.BlockSpec((RET_T, RET_DIM), lambda b, t: (b * nt + t, 0))
    return pl.pallas_call(
        _ret_kernel,
        grid=(batch, nt),
        in_specs=[col(COL_RQ), col(COL_RK), col(COL_RV), col(COL_RG), tab, tab,
                  _vec_spec(w, layer, 2)],
        out_specs=pl.BlockSpec((RET_T, w), lambda b, t: (b * nt + t, 0)),
        out_shape=jax.ShapeDtypeStruct((m, w), BF16),
        scratch_shapes=[
            pltpu.VMEM((RET_HEADS, RET_DIM, RET_DIM), F32),
            pltpu.VMEM((RET_HEADS, RET_T, RET_T), F32),
            pltpu.VMEM((RET_HEADS, RET_T, RET_DIM), F32),
            pltpu.VMEM((RET_HEADS, RET_T, RET_DIM), F32),
        ],
        compiler_params=_params("arbitrary", "arbitrary"),
        name="retention",
    )(proj, proj, proj, proj, cos_r, sin_r, _layer_vec(ret_gn))


def _lru_kernel(x_ref, g_ref, cw_ref, cb_ref, wa_ref, ba_ref, wx_ref, bx_ref, lam_ref, o_ref,
                xbuf, hcar, a_s, b_s):
    t_blk = LRU_T
    halo = SUBLANE

    @pl.when(pl.program_id(1) == 0)
    def _():
        xbuf[0:halo, :] = jnp.zeros((halo, LRU_WIDTH), F32)
        hcar[...] = jnp.zeros_like(hcar)

    xbuf[halo:halo + t_blk, :] = x_ref[...].astype(F32)
    xc = cb_ref[0]
    for w in range(CONV_WIDTH):
        off = halo - (CONV_WIDTH - 1) + w
        xc = xc + cw_ref[0, w:w + 1, :] * xbuf[off:off + t_blk, :]
    xbuf[0:halo, :] = xbuf[t_blk:t_blk + halo, :]

    neg_lam = -lam_ref[0]
    softplus = jnp.maximum(neg_lam, 0.0) + jnp.log1p(jnp.exp(-jnp.abs(neg_lam)))
    for n in range(LRU_BLOCKS):
        sl = slice(n * LRU_BLOCK_DIM, (n + 1) * LRU_BLOCK_DIM)
        xn = xc[:, sl]
        xn_b = xn.astype(BF16)
        r = jax.nn.sigmoid(_dot(xn_b, wa_ref[0, n].astype(BF16)) + ba_ref[0, :, sl])
        i = jax.nn.sigmoid(_dot(xn_b, wx_ref[0, n].astype(BF16)) + bx_ref[0, :, sl])
        log_a = (-LRU_C) * r * softplus[:, sl]
        a = jnp.exp(log_a)
        a_s[:, sl] = a
        b_s[:, sl] = jnp.sqrt(jnp.tanh(-log_a) * (a * a + 1.0)) * (i * xn)

    row = lax.broadcasted_iota(jnp.int32, (SUBLANE, LRU_WIDTH), 0)

    def body(i, h):
        r0 = pl.multiple_of(i * SUBLANE, SUBLANE)
        a = a_s[pl.ds(r0, SUBLANE), :]
        b = b_s[pl.ds(r0, SUBLANE), :]
        for s in (1, 2, 4):
            keep = row >= s
            a_prev = jnp.where(keep, pltpu.roll(a, s, 0), 1.0)
            b_prev = jnp.where(keep, pltpu.roll(b, s, 0), 0.0)
            b = a * b_prev + b
            a = a * a_prev
        hrows = a * h + b
        b_s[pl.ds(r0, SUBLANE), :] = hrows
        return jnp.broadcast_to(hrows[SUBLANE - 1:SUBLANE, :], (SUBLANE, LRU_WIDTH))

    hcar[...] = lax.fori_loop(0, t_blk // SUBLANE, body, hcar[...])
    o_ref[...] = (b_s[...] * _silu(g_ref[...].astype(F32))).astype(o_ref.dtype)


def _lru_call(proj, cw, cb, wa, ba, wx, bx, lam, layer, batch, seq):
    m = proj.shape[0]
    nt = seq // LRU_T
    w = LRU_WIDTH

    def col(c):
        return pl.BlockSpec((LRU_T, w), lambda b, t, c=c: (b * nt + t, c // w))

    def vec():
        return _vec_spec(w, layer, 2)

    def blk():
        return pl.BlockSpec((1, LRU_BLOCKS, LRU_BLOCK_DIM, LRU_BLOCK_DIM), lambda b, t: (layer, 0, 0, 0))

    return pl.pallas_call(
        _lru_kernel,
        grid=(batch, nt),
        in_specs=[col(COL_LX), col(COL_LG),
                  pl.BlockSpec((1, CONV_WIDTH, w), lambda b, t: (layer, 0, 0)), vec(),
                  blk(), vec(), blk(), vec(), vec()],
        out_specs=pl.BlockSpec((LRU_T, w), lambda b, t: (b * nt + t, 0)),
        out_shape=jax.ShapeDtypeStruct((m, w), BF16),
        scratch_shapes=[
            pltpu.VMEM((LRU_T + SUBLANE, w), F32),
            pltpu.VMEM((SUBLANE, w), F32),
            pltpu.VMEM((LRU_T, w), F32),
            pltpu.VMEM((LRU_T, w), F32),
        ],
        compiler_params=_params("arbitrary", "arbitrary"),
        name="rg_lru",
    )(proj, proj, cw, _layer_vec(cb), wa, _layer_vec(ba), wx, _layer_vec(bx), _layer_vec(lam))


def _rope64(x, cos_t, sin_t):
    swapped = pltpu.roll(x, MLA_ROPE // 2, 1) + pltpu.roll(x, LANE - MLA_ROPE // 2, 1)
    return x * cos_t + swapped * sin_t


def _mla_kernel(h_ref, wkr_ref, mq_ref, mkv_ref, mg_ref, wq_ref, wkv_ref, qn_ref, kvn_ref,
                cos_ref, sin_ref, o_ref, qlat, kvlat, krope, wq_s, qs, ks, vs, *, seq):
    rows = 256
    head = pl.program_id(1)

    @pl.when(head == 0)
    def _():
        lane = lax.broadcasted_iota(jnp.int32, (rows, LANE), 1)
        wkr = wkr_ref[0].astype(BF16)
        for i in range(seq // rows):
            rs = slice(i * rows, (i + 1) * rows)
            qlat[rs, :] = _rms(mq_ref[rs, :].astype(F32), qn_ref[0]).astype(BF16)
            kvlat[rs, :] = _rms(mkv_ref[rs, :].astype(F32), kvn_ref[0]).astype(BF16)
            kr = jnp.where(lane < MLA_ROPE, _dot(h_ref[rs, :], wkr), 0.0)
            krope[rs, :] = _rope64(kr, cos_ref[rs, :], sin_ref[rs, :]).astype(BF16)
        zeros = jnp.zeros((MLA_LORA, MLA_QK_PAD - MLA_QK), F32)
        wq_all = wq_ref[0]
        for hh in range(MLA_HEADS):
            w_head = wq_all[:, hh * MLA_QK:(hh + 1) * MLA_QK]
            wq_s[hh] = jnp.concatenate([w_head, zeros], axis=1).astype(BF16)

    scale = MLA_QK ** -0.5
    wq = wq_s[head]
    wkv = wkv_ref[0].astype(BF16)
    for i in range(seq // rows):
        rs = slice(i * rows, (i + 1) * rows)
        q = _dot(qlat[rs, :], wq)
        qs[rs, 0:MLA_NOPE] = (q[:, 0:MLA_NOPE] * scale).astype(BF16)
        qs[rs, MLA_NOPE:] = (_rope64(q[:, MLA_NOPE:], cos_ref[rs, :], sin_ref[rs, :]) * scale).astype(BF16)
        kv = _dot(kvlat[rs, :], wkv)
        ks[rs, 0:MLA_NOPE] = kv[:, 0:MLA_NOPE].astype(BF16)
        ks[rs, MLA_NOPE:] = krope[rs, :]
        vs[rs, :] = kv[:, MLA_NOPE:].astype(BF16)

    tq = MLA_TQ
    ri = lax.broadcasted_iota(jnp.int32, (tq, tq), 0)
    ci = lax.broadcasted_iota(jnp.int32, (tq, tq), 1)
    visible = (ci // CHUNK) <= (ri // CHUNK)
    for i in range(seq // tq):
        q0, q1 = i * tq, (i + 1) * tq
        qi = qs[q0:q1, :]
        s_d = jnp.where(visible, _dot_nt(qi, ks[q0:q1, :]), -1e30)
        m = jnp.max(s_d, axis=-1, keepdims=True)
        if i > 0:
            s_o = _dot_nt(qi, ks[0:q0, :])
            m = jnp.maximum(m, jnp.max(s_o, axis=-1, keepdims=True))
        p_d = jnp.exp(s_d - m)
        l = jnp.sum(p_d, axis=-1, keepdims=True)
        acc = _dot(p_d.astype(BF16), vs[q0:q1, :])
        if i > 0:
            p_o = jnp.exp(s_o - m)
            l = l + jnp.sum(p_o, axis=-1, keepdims=True)
            acc = acc + _dot(p_o.astype(BF16), vs[0:q0, :])
        out = acc / l
        o_ref[q0:q1, :] = (out * _silu(mg_ref[q0:q1, :].astype(F32))).astype(o_ref.dtype)


def _mla_call(h, w_in, proj, w_uq, w_ukv, q_norm, kv_norm, cos_m, sin_m, layer, batch, seq):
    m, d = h.shape
    lora = MLA_LORA
    return pl.pallas_call(
        functools.partial(_mla_kernel, seq=seq),
        grid=(batch, MLA_HEADS),
        in_specs=[
            pl.BlockSpec((seq, d), lambda b, hd: (b, 0)),
            pl.BlockSpec((1, d, LANE), lambda b, hd: (layer, 0, SRC_MKR // LANE)),
            pl.BlockSpec((seq, lora), lambda b, hd: (b, COL_MQ // lora)),
            pl.BlockSpec((seq, lora), lambda b, hd: (b, COL_MKV // lora)),
            pl.BlockSpec((seq, MLA_V), lambda b, hd: (b, COL_MG // MLA_V + hd)),
            pl.BlockSpec((1, lora, MLA_HEADS * MLA_QK), lambda b, hd: (layer, 0, 0)),
            pl.BlockSpec((1, lora, MLA_NOPE + MLA_V), lambda b, hd: (layer, 0, hd)),
            _vec_spec(lora, layer, 2),
            _vec_spec(lora, layer, 2),
            pl.BlockSpec((seq, LANE), lambda b, hd: (b, 0)),
            pl.BlockSpec((seq, LANE), lambda b, hd: (b, 0)),
        ],
        out_specs=pl.BlockSpec((seq, MLA_V), lambda b, hd: (b, hd)),
        out_shape=jax.ShapeDtypeStruct((m, MLA_WIDTH), BF16),
        scratch_shapes=[
            pltpu.VMEM((seq, lora), BF16),
            pltpu.VMEM((seq, lora), BF16),
            pltpu.VMEM((seq, LANE), BF16),
            pltpu.VMEM((MLA_HEADS, lora, MLA_QK_PAD), BF16),
            pltpu.VMEM((seq, MLA_QK_PAD), BF16),
            pltpu.VMEM((seq, MLA_QK_PAD), BF16),
            pltpu.VMEM((seq, MLA_V), BF16),
        ],
        compiler_params=_params("arbitrary", "arbitrary"),
        name="mla",
    )(h, w_in, proj, proj, proj, w_uq, w_ukv, _layer_vec(q_norm), _layer_vec(kv_norm), cos_m, sin_m)


def _merge_kernel(yr_ref, yl_ref, ym_ref, l0_ref, l1_ref, l2_ref, x_ref, res_ref, g_ref,
                  wb_ref, wo_ref, *rest, tiles_per_batch, emit_next):
    b = pl.program_id(0) // tiles_per_batch
    merged = None
    for i, (y_ref, l_ref) in enumerate(((yr_ref, l0_ref), (yl_ref, l1_ref), (ym_ref, l2_ref))):
        z = _dot(y_ref[...], wb_ref[0, i * 1024:(i + 1) * 1024, :])
        z = z * jax.nn.sigmoid(l_ref[...].astype(F32))
        merged = z if merged is None else merged + z
    y = _dot(merged.astype(BF16), wo_ref[0])
    x_new = x_ref[...] + (1.0 + res_ref[0, pl.ds(b, 1), :]) * _rms(y, g_ref[0])
    if emit_next:
        gn_ref, shift_ref, scale_ref, o_ref, h_ref = rest
        h_ref[...] = _modulate(x_new, gn_ref[0], shift_ref, scale_ref, b).astype(h_ref.dtype)
    else:
        (o_ref,) = rest
    o_ref[...] = x_new


def _merge_call(y_ret, y_lru, y_mla, proj, x2, mod_all, norm_post, norm_pre, wb, wo, layer, seq):
    m, d = x2.shape
    tm = MERGE_TM
    depth, mod_rows = mod_all.shape[:2]
    emit_next = layer + 1 < depth

    def rows(width, c=0):
        return pl.BlockSpec((tm, width), lambda i, c=c: (i, c))

    def resident(shape):
        return pl.BlockSpec((1,) + shape[1:], lambda i: (layer, 0, 0), pipeline_mode=pl.Buffered(1))

    def mod_part(lyr, part):
        return pl.BlockSpec((1, mod_rows, d), lambda i: (lyr, 0, part))

    merge_col = COL_MERGE // d
    in_specs = [rows(1024), rows(1024), rows(1024),
                rows(d, merge_col), rows(d, merge_col + 1), rows(d, merge_col + 2),
                rows(d), mod_part(layer, 2), _vec_spec(d, layer, 1),
                resident(wb.shape), resident(wo.shape)]
    args = [y_ret, y_lru, y_mla, proj, proj, proj, x2, mod_all, _layer_vec(norm_post), wb, wo]
    out_specs = rows(d)
    out_shape = jax.ShapeDtypeStruct((m, d), F32)
    if emit_next:
        in_specs += [_vec_spec(d, layer + 1, 1), mod_part(layer + 1, 0), mod_part(layer + 1, 1)]
        args += [_layer_vec(norm_pre), mod_all, mod_all]
        out_specs = (out_specs, rows(d))
        out_shape = (out_shape, jax.ShapeDtypeStruct((m, d), BF16))
    return pl.pallas_call(
        functools.partial(_merge_kernel, tiles_per_batch=seq // tm, emit_next=emit_next),
        grid=(m // tm,),
        in_specs=in_specs,
        out_specs=out_specs,
        out_shape=out_shape,
        compiler_params=_params("arbitrary"),
        name="merge_out",
    )(*args)


def _rope_tables(positions, dim):
    inv_freq = ROPE_BASE ** (-jnp.arange(0, dim, 2, dtype=F32) / dim)
    ang = positions.astype(F32)[:, :, None] * inv_freq
    return jnp.cos(ang), jnp.sin(ang)


def kernel(x, c, positions, ada_w, ada_b, norm_pre, norm_post, w_in, ret_gn, lru_conv_w, lru_conv_b,
           lru_wa, lru_ba, lru_wx, lru_bx, lru_lambda, mla_q_norm, mla_w_uq, mla_kv_norm, mla_w_ukv,
           w_branch, w_out):
    batch, seq, d = x.shape
    depth = w_in.shape[0]
    m = batch * seq
    assert d == D_MODEL and w_in.shape[2] == SRC_WIDTH and batch <= SUBLANE
    assert seq % PROJ_TM == 0 and seq % RET_T == 0 and seq % LRU_T == 0 and seq % MLA_TQ == 0
    assert seq % MERGE_TM == 0 and seq % NORM_TM == 0

    cos_r, sin_r = _rope_tables(positions, RET_DIM)
    cos_r = jnp.concatenate([cos_r, cos_r], axis=-1).reshape(m, RET_DIM)
    sin_r = jnp.concatenate([-sin_r, sin_r], axis=-1).reshape(m, RET_DIM)
    cos_m, sin_m = _rope_tables(positions, MLA_ROPE)
    tail = jnp.zeros((batch, seq, LANE - MLA_ROPE), F32)
    cos_m = jnp.concatenate([cos_m, cos_m, tail], axis=-1).reshape(m, LANE)
    sin_m = jnp.concatenate([-sin_m, sin_m, tail], axis=-1).reshape(m, LANE)

    c_pad = jnp.pad(c, ((0, SUBLANE - batch), (0, 0)))
    mod_all = _ada_call(c_pad, ada_w, ada_b)
    wb_all = w_branch.astype(BF16)
    wo_all = w_out.astype(BF16)

    x2 = x.reshape(m, d)
    h = _prenorm_call(x2, mod_all, norm_pre, 0, seq)
    for l in range(depth):
        proj = _inproj_call(h, w_in, l)
        y_ret = _ret_call(proj, cos_r, sin_r, ret_gn, l, batch, seq)
        y_lru = _lru_call(proj, lru_conv_w, lru_conv_b, lru_wa, lru_ba, lru_wx, lru_bx, lru_lambda,
                          l, batch, seq)
        y_mla = _mla_call(h, w_in, proj, mla_w_uq, mla_w_ukv, mla_q_norm, mla_kv_norm, cos_m, sin_m,
                          l, batch, seq)
        out = _merge_call(y_ret, y_lru, y_mla, proj, x2, mod_all, norm_post, norm_pre, wb_all, wo_all,
                          l, seq)
        if l + 1 < depth:
            x2, h = out
        else:
            x2 = out
    return x2.reshape(batch, seq, d)
```

```python
import functools
import math

import jax
import jax.numpy as jnp
from jax import lax
from jax.experimental import pallas as pl
from jax.experimental.pallas import tpu as pltpu

F32 = jnp.float32
BF16 = jnp.bfloat16

D_MODEL = 2048
CHUNK = 64
ROPE_BASE = 10000.0
NORM_EPS = 1e-6

RET_HEADS = 8
RET_DIM = 128
RET_WIDTH = RET_HEADS * RET_DIM

LRU_WIDTH = 1024
LRU_BLOCKS = 8
LRU_BLOCK_DIM = 128
CONV_WIDTH = 4
LRU_C = 8.0

MLA_HEADS = 8
MLA_NOPE = 128
MLA_ROPE = 64
MLA_V = 128
MLA_LORA = 512
MLA_WIDTH = MLA_HEADS * MLA_V
MLA_QK = MLA_NOPE + MLA_ROPE
MLA_QK_PAD = 256

LANE = 128
SUBLANE = 8
VMEM_LIMIT = 56 * 1024 * 1024

SRC_MKR = 4 * 1024 + 2 * 1024 + 2 * MLA_LORA
SRC_WIDTH = SRC_MKR + MLA_ROPE + MLA_WIDTH + 3 * D_MODEL
COL_RQ = 0
COL_RK = 1024
COL_RV = 2048
COL_RG = 3072
COL_LX = 4096
COL_LG = 5120
COL_MQ = 6144
COL_MKV = COL_MQ + MLA_LORA
COL_MG = SRC_MKR
COL_MERGE = COL_MG + MLA_WIDTH
PROJ_WIDTH = COL_MERGE + 3 * D_MODEL
PROJ_TN = 1024
PROJ_TM = 1024
ALIGNED_TILES = SRC_MKR // PROJ_TN
CAST_ROWS = 64
assert SRC_MKR % PROJ_TN == 0 and PROJ_WIDTH % PROJ_TN == 0 and COL_MERGE % D_MODEL == 0

NORM_TM = 512
RET_T = 256
LRU_T = 256
MLA_TQ = 256
MERGE_TM = 256
ADA_TK = 256


def _silu(v):
    return v * jax.nn.sigmoid(v)


def _dot(a, b):
    return jnp.dot(a, b, preferred_element_type=F32)


def _dot_nt(a, b):
    return lax.dot_general(a, b, (((1,), (1,)), ((), ())), preferred_element_type=F32)


def _dot_tn(a, b):
    return lax.dot_general(a, b, (((0,), (0,)), ((), ())), preferred_element_type=F32)


def _rms(x, gain):
    return x * lax.rsqrt(jnp.mean(x * x, axis=-1, keepdims=True) + NORM_EPS) * gain


def _params(*sem):
    return pltpu.CompilerParams(dimension_semantics=sem, vmem_limit_bytes=VMEM_LIMIT)


def _layer_vec(arr):
    return arr.reshape(arr.shape[0], 1, arr.shape[1])


def _vec_spec(width, layer, ngrid):
    if ngrid == 1:
        return pl.BlockSpec((1, 1, width), lambda i: (layer, 0, 0))
    return pl.BlockSpec((1, 1, width), lambda i, j: (layer, 0, 0))


def _ada_kernel(c_ref, w_ref, b_ref, o_ref):
    @pl.when(pl.program_id(1) == 0)
    def _():
        o_ref[0] = jnp.broadcast_to(b_ref[0], o_ref.shape[1:])

    c_act = _silu(c_ref[...]).astype(BF16)
    o_ref[0] += _dot(c_act, w_ref[0].astype(BF16))


def _ada_call(c_pad, ada_w, ada_b):
    depth, d, n = ada_w.shape
    rows = c_pad.shape[0]
    return pl.pallas_call(
        _ada_kernel,
        grid=(depth, d // ADA_TK),
        in_specs=[
            pl.BlockSpec((rows, ADA_TK), lambda l, k: (0, k)),
            pl.BlockSpec((1, ADA_TK, n), lambda l, k: (l, k, 0)),
            pl.BlockSpec((1, 1, n), lambda l, k: (l, 0, 0)),
        ],
        out_specs=pl.BlockSpec((1, rows, n), lambda l, k: (l, 0, 0)),
        out_shape=jax.ShapeDtypeStruct((depth, rows, n), F32),
        compiler_params=_params("arbitrary", "arbitrary"),
        name="ada_mod",
    )(c_pad, ada_w, _layer_vec(ada_b))


def _modulate(x, gain, shift_ref, scale_ref, b):
    shift = shift_ref[0, pl.ds(b, 1), :]
    scale = scale_ref[0, pl.ds(b, 1), :]
    return _rms(x, gain) * (1.0 + scale) + shift


def _prenorm_kernel(x_ref, g_ref, shift_ref, scale_ref, o_ref, *, tiles_per_batch):
    b = pl.program_id(0) // tiles_per_batch
    o_ref[...] = _modulate(x_ref[...], g_ref[0], shift_ref, scale_ref, b).astype(o_ref.dtype)


def _prenorm_call(x2, mod_all, norm_pre, layer, seq):
    m, d = x2.shape
    rows = mod_all.shape[1]
    return pl.pallas_call(
        functools.partial(_prenorm_kernel, tiles_per_batch=seq // NORM_TM),
        grid=(m // NORM_TM,),
        in_specs=[
            pl.BlockSpec((NORM_TM, d), lambda i: (i, 0)),
            _vec_spec(d, layer, 1),
            pl.BlockSpec((1, rows, d), lambda i: (layer, 0, 0)),
            pl.BlockSpec((1, rows, d), lambda i: (layer, 0, 1)),
        ],
        out_specs=pl.BlockSpec((NORM_TM, d), lambda i: (i, 0)),
        out_shape=jax.ShapeDtypeStruct((m, d), BF16),
        compiler_params=_params("arbitrary"),
        name="pre_norm",
    )(x2, _layer_vec(norm_pre), mod_all, mod_all)


def _inproj_kernel(h_ref, w_ref, wx_ref, o_ref, wb_ref):
    j = pl.program_id(0)
    first_row_tile = pl.program_id(1) == 0
    n_chunks = PROJ_TN // CAST_ROWS

    @pl.when(first_row_tile & (j < ALIGNED_TILES))
    def _():
        def body(i, carry):
            r0 = pl.multiple_of(i * CAST_ROWS, CAST_ROWS)
            wb_ref[pl.ds(r0, CAST_ROWS), :] = w_ref[0, pl.ds(r0, CAST_ROWS), :].astype(BF16)
            return carry

        lax.fori_loop(0, n_chunks, body, 0)

    @pl.when(first_row_tile & (j >= ALIGNED_TILES))
    def _():
        def body(i, carry):
            r0 = pl.multiple_of(i * CAST_ROWS, CAST_ROWS)
            wb_ref[pl.ds(r0, CAST_ROWS), :] = w_ref[0, pl.ds(r0 + MLA_ROPE, CAST_ROWS), :].astype(BF16)
            return carry

        lax.fori_loop(0, n_chunks - 1, body, 0)
        wb_ref[PROJ_TN - MLA_ROPE:, :] = wx_ref[0].astype(BF16)

    o_ref[...] = _dot_nt(h_ref[...], wb_ref[...]).astype(o_ref.dtype)


def _inproj_call(h, w_in_t, layer):
    m, d = h.shape
    per_tile = PROJ_TN // MLA_ROPE
    last_block = SRC_WIDTH // MLA_ROPE - 1
    return pl.pallas_call(
        _inproj_kernel,
        grid=(PROJ_WIDTH // PROJ_TN, m // PROJ_TM),
        in_specs=[
            pl.BlockSpec((PROJ_TM, d), lambda j, i: (i, 0)),
            pl.BlockSpec((1, PROJ_TN, d), lambda j, i: (layer, j, 0)),
            pl.BlockSpec((1, MLA_ROPE, d),
                         lambda j, i: (layer, jnp.minimum((j + 1) * per_tile, last_block), 0)),
        ],
        out_specs=pl.BlockSpec((PROJ_TM, PROJ_TN), lambda j, i: (i, j)),
        out_shape=jax.ShapeDtypeStruct((m, PROJ_WIDTH), BF16),
        scratch_shapes=[pltpu.VMEM((PROJ_TN, d), BF16)],
        compiler_params=_params("arbitrary", "arbitrary"),
        name="in_proj",
    )(h, w_in_t, w_in_t)


_LOG_GAMMA = tuple(math.log1p(-(2.0 ** (-5.0 - h))) for h in range(RET_HEADS))


def _ret_kernel(q_ref, k_ref, v_ref, g_ref, cos_ref, sin_ref, gn_ref, o_ref,
                state_ref, dmat_ref, dq_ref, dk_ref):
    t_blk = RET_T

    @pl.when((pl.program_id(0) == 0) & (pl.program_id(1) == 0))
    def _():
        ri = lax.broadcasted_iota(jnp.int32, (t_blk, t_blk), 0)
        ci = lax.broadcasted_iota(jnp.int32, (t_blk, t_blk), 1)
        dist = jnp.abs(ri - ci).astype(F32)
        visible = (ci // CHUNK) <= (ri // CHUNK)
        row = lax.broadcasted_iota(jnp.int32, (t_blk, RET_DIM), 0).astype(F32)
        for h in range(RET_HEADS):
            lg = _LOG_GAMMA[h]
            dmat_ref[h] = jnp.where(visible, jnp.exp(lg * dist), 0.0)
            dq_ref[h] = jnp.exp(lg * (row + 1.0))
            dk_ref[h] = jnp.exp(lg * ((t_blk - 1.0) - row))

    @pl.when(pl.program_id(1) == 0)
    def _():
        state_ref[...] = jnp.zeros_like(state_ref)

    cos = cos_ref[...]
    sin = sin_ref[...]
    for h in range(RET_HEADS):
        sl = slice(h * RET_DIM, (h + 1) * RET_DIM)
        q = q_ref[:, sl].astype(F32)
        k = k_ref[:, sl].astype(F32)
        q = (q * cos + pltpu.roll(q, RET_DIM // 2, 1) * sin) * (RET_DIM ** -0.5)
        k = k * cos + pltpu.roll(k, RET_DIM // 2, 1) * sin
        v = v_ref[:, sl]
        scores = _dot_nt(q.astype(BF16), k.astype(BF16)) * dmat_ref[h]
        o = _dot(scores.astype(BF16), v)
        state = state_ref[h]
        o = o + _dot((q * dq_ref[h]).astype(BF16), state.astype(BF16))
        k_dec = (k * dk_ref[h]).astype(BF16)
        state_ref[h] = state * math.exp(_LOG_GAMMA[h] * t_blk) + _dot_tn(k_dec, v)
        mean = jnp.mean(o, axis=-1, keepdims=True)
        cen = o - mean
        var = jnp.mean(cen * cen, axis=-1, keepdims=True)
        normed = cen * lax.rsqrt(var + NORM_EPS) * gn_ref[0, :, sl]
        o_ref[:, sl] = (normed * _silu(g_ref[:, sl].astype(F32))).astype(o_ref.dtype)


def _ret_call(proj, cos_r, sin_r, ret_gn, layer, batch, seq):
    m = proj.shape[0]
    nt = seq // RET_T
    w = RET_WIDTH

    def col(c):
        return pl.BlockSpec((RET_T, w), lambda b, t, c=c: (b * nt + t, c // w))

    tab = pl.BlockSpec((RET_T, RET_DIM), lambda b, t: (b * nt + t, 0))
    return pl.pallas_call(
        _ret_kernel,
        grid=(batch, nt),
        in_specs=[col(COL_RQ), col(COL_RK), col(COL_RV), col(COL_RG), tab, tab,
                  _vec_spec(w, layer, 2)],
        out_specs=pl.BlockSpec((RET_T, w), lambda b, t: (b * nt + t, 0)),
        out_shape=jax.ShapeDtypeStruct((m, w), BF16),
        scratch_shapes=[
            pltpu.VMEM((RET_HEADS, RET_DIM, RET_DIM), F32),
            pltpu.VMEM((RET_HEADS, RET_T, RET_T), F32),
            pltpu.VMEM((RET_HEADS, RET_T, RET_DIM), F32),
            pltpu.VMEM((RET_HEADS, RET_T, RET_DIM), F32),
        ],
        compiler_params=_params("arbitrary", "arbitrary"),
        name="retention",
    )(proj, proj, proj, proj, cos_r, sin_r, _layer_vec(ret_gn))


def _lru_kernel(x_ref, g_ref, cw_ref, cb_ref, wa_ref, ba_ref, wx_ref, bx_ref, lam_ref, o_ref,
                xbuf, hcar, a_s, b_s):
    t_blk = LRU_T
    halo = SUBLANE

    @pl.when(pl.program_id(1) == 0)
    def _():
        xbuf[0:halo, :] = jnp.zeros((halo, LRU_WIDTH), F32)
        hcar[...] = jnp.zeros_like(hcar)

    xbuf[halo:halo + t_blk, :] = x_ref[...].astype(F32)
    xc = cb_ref[0]
    for w in range(CONV_WIDTH):
        off = halo - (CONV_WIDTH - 1) + w
        xc = xc + cw_ref[0, w:w + 1, :] * xbuf[off:off + t_blk, :]
    xbuf[0:halo, :] = xbuf[t_blk:t_blk + halo, :]

    neg_lam = -lam_ref[0]
    softplus = jnp.maximum(neg_lam, 0.0) + jnp.log1p(jnp.exp(-jnp.abs(neg_lam)))
    for n in range(LRU_BLOCKS):
        sl = slice(n * LRU_BLOCK_DIM, (n + 1) * LRU_BLOCK_DIM)
        xn = xc[:, sl]
        xn_b = xn.astype(BF16)
        r = jax.nn.sigmoid(_dot(xn_b, wa_ref[0, n].astype(BF16)) + ba_ref[0, :, sl])
        i = jax.nn.sigmoid(_dot(xn_b, wx_ref[0, n].astype(BF16)) + bx_ref[0, :, sl])
        log_a = (-LRU_C) * r * softplus[:, sl]
        a = jnp.exp(log_a)
        a_s[:, sl] = a
        b_s[:, sl] = jnp.sqrt(jnp.tanh(-log_a) * (a * a + 1.0)) * (i * xn)

    row = lax.broadcasted_iota(jnp.int32, (SUBLANE, LRU_WIDTH), 0)

    def body(i, h):
        r0 = pl.multiple_of(i * SUBLANE, SUBLANE)
        a = a_s[pl.ds(r0, SUBLANE), :]
        b = b_s[pl.ds(r0, SUBLANE), :]
        for s in (1, 2, 4):
            keep = row >= s
            a_prev = jnp.where(keep, pltpu.roll(a, s, 0), 1.0)
            b_prev = jnp.where(keep, pltpu.roll(b, s, 0), 0.0)
            b = a * b_prev + b
            a = a * a_prev
        hrows = a * h + b
        b_s[pl.ds(r0, SUBLANE), :] = hrows
        return jnp.broadcast_to(hrows[SUBLANE - 1:SUBLANE, :], (SUBLANE, LRU_WIDTH))

    hcar[...] = lax.fori_loop(0, t_blk // SUBLANE, body, hcar[...])
    o_ref[...] = (b_s[...] * _silu(g_ref[...].astype(F32))).astype(o_ref.dtype)


def _lru_call(proj, cw, cb, wa, ba, wx, bx, lam, layer, batch, seq):
    m = proj.shape[0]
    nt = seq // LRU_T
    w = LRU_WIDTH

    def col(c):
        return pl.BlockSpec((LRU_T, w), lambda b, t, c=c: (b * nt + t, c // w))

    def vec():
        return _vec_spec(w, layer, 2)

    def blk():
        return pl.BlockSpec((1, LRU_BLOCKS, LRU_BLOCK_DIM, LRU_BLOCK_DIM), lambda b, t: (layer, 0, 0, 0))

    return pl.pallas_call(
        _lru_kernel,
        grid=(batch, nt),
        in_specs=[col(COL_LX), col(COL_LG),
                  pl.BlockSpec((1, CONV_WIDTH, w), lambda b, t: (layer, 0, 0)), vec(),
                  blk(), vec(), blk(), vec(), vec()],
        out_specs=pl.BlockSpec((LRU_T, w), lambda b, t: (b * nt + t, 0)),
        out_shape=jax.ShapeDtypeStruct((m, w), BF16),
        scratch_shapes=[
            pltpu.VMEM((LRU_T + SUBLANE, w), F32),
            pltpu.VMEM((SUBLANE, w), F32),
            pltpu.VMEM((LRU_T, w), F32),
            pltpu.VMEM((LRU_T, w), F32),
        ],
        compiler_params=_params("arbitrary", "arbitrary"),
        name="rg_lru",
    )(proj, proj, cw, _layer_vec(cb), wa, _layer_vec(ba), wx, _layer_vec(bx), _layer_vec(lam))


def _rope64(x, cos_t, sin_t):
    swapped = pltpu.roll(x, MLA_ROPE // 2, 1) + pltpu.roll(x, LANE - MLA_ROPE // 2, 1)
    return x * cos_t + swapped * sin_t


def _mla_kernel(h_ref, wkr_ref, mq_ref, mkv_ref, mg_ref, wq_ref, wkv_ref, qn_ref, kvn_ref,
                cos_ref, sin_ref, o_ref, qlat, kvlat, krope, wq_s, qs, ks, vs, *, seq):
    rows = 256
    head = pl.program_id(1)

    @pl.when(head == 0)
    def _():
        wkr = jnp.concatenate([wkr_ref[0], jnp.zeros((LANE - MLA_ROPE, D_MODEL), F32)], axis=0).astype(BF16)
        for i in range(seq // rows):
            rs = slice(i * rows, (i + 1) * rows)
            qlat[rs, :] = _rms(mq_ref[rs, :].astype(F32), qn_ref[0]).astype(BF16)
            kvlat[rs, :] = _rms(mkv_ref[rs, :].astype(F32), kvn_ref[0]).astype(BF16)
            kr = _dot_nt(h_ref[rs, :], wkr)
            krope[rs, :] = _rope64(kr, cos_ref[rs, :], sin_ref[rs, :]).astype(BF16)
        vs[:, MLA_V:] = jnp.ones((seq, MLA_V), BF16)
        zeros = jnp.zeros((MLA_LORA, MLA_QK_PAD - MLA_QK), F32)
        wq_all = wq_ref[0]
        for hh in range(MLA_HEADS):
            w_head = wq_all[:, hh * MLA_QK:(hh + 1) * MLA_QK]
            wq_s[hh] = jnp.concatenate([w_head, zeros], axis=1).astype(BF16)

    scale = (MLA_QK ** -0.5) * math.log2(math.e)
    wq = wq_s[head]
    wkv = wkv_ref[0].astype(BF16)
    for i in range(seq // rows):
        rs = slice(i * rows, (i + 1) * rows)
        q = _dot(qlat[rs, :], wq)
        qs[rs, 0:MLA_NOPE] = (q[:, 0:MLA_NOPE] * scale).astype(BF16)
        qs[rs, MLA_NOPE:] = (_rope64(q[:, MLA_NOPE:], cos_ref[rs, :], sin_ref[rs, :]) * scale).astype(BF16)
        kv = _dot(kvlat[rs, :], wkv)
        ks[rs, 0:MLA_NOPE] = kv[:, 0:MLA_NOPE].astype(BF16)
        ks[rs, MLA_NOPE:] = krope[rs, :]
        vs[rs, 0:MLA_V] = kv[:, MLA_NOPE:].astype(BF16)

    tq = MLA_TQ
    ri = lax.broadcasted_iota(jnp.int32, (tq, tq), 0)
    ci = lax.broadcasted_iota(jnp.int32, (tq, tq), 1)
    visible = (ci // CHUNK) <= (ri // CHUNK)
    for i in range(seq // tq):
        q0, q1 = i * tq, (i + 1) * tq
        qi = qs[q0:q1, :]
        s_d = jnp.where(visible, _dot_nt(qi, ks[q0:q1, :]), -1e30)
        m = jnp.max(s_d, axis=-1, keepdims=True)
        if i > 0:
            s_o = _dot_nt(qi, ks[0:q0, :])
            m = jnp.maximum(m, jnp.max(s_o, axis=-1, keepdims=True))
        acc = _dot(jnp.exp2(s_d - m).astype(BF16), vs[q0:q1, :])
        if i > 0:
            acc = acc + _dot(jnp.exp2(s_o - m).astype(BF16), vs[0:q0, :])
        out = acc[:, 0:MLA_V] / acc[:, MLA_V:]
        o_ref[q0:q1, :] = (out * _silu(mg_ref[q0:q1, :].astype(F32))).astype(o_ref.dtype)


def _mla_call(h, w_in_t, proj, w_uq, w_ukv, q_norm, kv_norm, cos_m, sin_m, layer, batch, seq):
    m, d = h.shape
    lora = MLA_LORA
    return pl.pallas_call(
        functools.partial(_mla_kernel, seq=seq),
        grid=(batch, MLA_HEADS),
        in_specs=[
            pl.BlockSpec((seq, d), lambda b, hd: (b, 0)),
            pl.BlockSpec((1, MLA_ROPE, d), lambda b, hd: (layer, SRC_MKR // MLA_ROPE, 0)),
            pl.BlockSpec((seq, lora), lambda b, hd: (b, COL_MQ // lora)),
            pl.BlockSpec((seq, lora), lambda b, hd: (b, COL_MKV // lora)),
            pl.BlockSpec((seq, MLA_V), lambda b, hd: (b, COL_MG // MLA_V + hd)),
            pl.BlockSpec((1, lora, MLA_HEADS * MLA_QK), lambda b, hd: (layer, 0, 0)),
            pl.BlockSpec((1, lora, MLA_NOPE + MLA_V), lambda b, hd: (layer, 0, hd)),
            _vec_spec(lora, layer, 2),
            _vec_spec(lora, layer, 2),
            pl.BlockSpec((seq, LANE), lambda b, hd: (b, 0)),
            pl.BlockSpec((seq, LANE), lambda b, hd: (b, 0)),
        ],
        out_specs=pl.BlockSpec((seq, MLA_V), lambda b, hd: (b, hd)),
        out_shape=jax.ShapeDtypeStruct((m, MLA_WIDTH), BF16),
        scratch_shapes=[
            pltpu.VMEM((seq, lora), BF16),
            pltpu.VMEM((seq, lora), BF16),
            pltpu.VMEM((seq, LANE), BF16),
            pltpu.VMEM((MLA_HEADS, lora, MLA_QK_PAD), BF16),
            pltpu.VMEM((seq, MLA_QK_PAD), BF16),
            pltpu.VMEM((seq, MLA_QK_PAD), BF16),
            pltpu.VMEM((seq, 2 * MLA_V), BF16),
        ],
        compiler_params=_params("arbitrary", "arbitrary"),
        name="mla",
    )(h, w_in_t, proj, proj, proj, w_uq, w_ukv, _layer_vec(q_norm), _layer_vec(kv_norm), cos_m, sin_m)


def _merge_kernel(yr_ref, yl_ref, ym_ref, l0_ref, l1_ref, l2_ref, x_ref, res_ref, g_ref,
                  wb_ref, wo_ref, *rest, tiles_per_batch, emit_next):
    b = pl.program_id(0) // tiles_per_batch
    merged = None
    for i, (y_ref, l_ref) in enumerate(((yr_ref, l0_ref), (yl_ref, l1_ref), (ym_ref, l2_ref))):
        z = _dot(y_ref[...], wb_ref[0, i * 1024:(i + 1) * 1024, :])
        z = z * jax.nn.sigmoid(l_ref[...].astype(F32))
        merged = z if merged is None else merged + z
    y = _dot(merged.astype(BF16), wo_ref[0])
    x_new = x_ref[...] + (1.0 + res_ref[0, pl.ds(b, 1), :]) * _rms(y, g_ref[0])
    if emit_next:
        gn_ref, shift_ref, scale_ref, o_ref, h_ref = rest
        h_ref[...] = _modulate(x_new, gn_ref[0], shift_ref, scale_ref, b).astype(h_ref.dtype)
    else:
        (o_ref,) = rest
    o_ref[...] = x_new


def _merge_call(y_ret, y_lru, y_mla, proj, x2, mod_all, norm_post, norm_pre, wb, wo, layer, seq):
    m, d = x2.shape
    tm = MERGE_TM
    depth, mod_rows = mod_all.shape[:2]
    emit_next = layer + 1 < depth

    def rows(width, c=0):
        return pl.BlockSpec((tm, width), lambda i, c=c: (i, c))

    def resident(shape):
        return pl.BlockSpec((1,) + shape[1:], lambda i: (layer, 0, 0), pipeline_mode=pl.Buffered(1))

    def mod_part(lyr, part):
        return pl.BlockSpec((1, mod_rows, d), lambda i: (lyr, 0, part))

    merge_col = COL_MERGE // d
    in_specs = [rows(1024), rows(1024), rows(1024),
                rows(d, merge_col), rows(d, merge_col + 1), rows(d, merge_col + 2),
                rows(d), mod_part(layer, 2), _vec_spec(d, layer, 1),
                resident(wb.shape), resident(wo.shape)]
    args = [y_ret, y_lru, y_mla, proj, proj, proj, x2, mod_all, _layer_vec(norm_post), wb, wo]
    out_specs = rows(d)
    out_shape = jax.ShapeDtypeStruct((m, d), F32)
    if emit_next:
        in_specs += [_vec_spec(d, layer + 1, 1), mod_part(layer + 1, 0), mod_part(layer + 1, 1)]
        args += [_layer_vec(norm_pre), mod_all, mod_all]
        out_specs = (out_specs, rows(d))
        out_shape = (out_shape, jax.ShapeDtypeStruct((m, d), BF16))
    return pl.pallas_call(
        functools.partial(_merge_kernel, tiles_per_batch=seq // tm, emit_next=emit_next),
        grid=(m // tm,),
        in_specs=in_specs,
        out_specs=out_specs,
        out_shape=out_shape,
        compiler_params=_params("arbitrary"),
        name="merge_out",
    )(*args)


def _rope_tables(positions, dim):
    inv_freq = ROPE_BASE ** (-jnp.arange(0, dim, 2, dtype=F32) / dim)
    ang = positions.astype(F32)[:, :, None] * inv_freq
    return jnp.cos(ang), jnp.sin(ang)


def kernel(x, c, positions, ada_w, ada_b, norm_pre, norm_post, w_in, ret_gn, lru_conv_w, lru_conv_b,
           lru_wa, lru_ba, lru_wx, lru_bx, lru_lambda, mla_q_norm, mla_w_uq, mla_kv_norm, mla_w_ukv,
           w_branch, w_out):
    batch, seq, d = x.shape
    depth = w_in.shape[0]
    m = batch * seq
    assert d == D_MODEL and w_in.shape[2] == SRC_WIDTH and batch <= SUBLANE
    assert seq % PROJ_TM == 0 and seq % RET_T == 0 and seq % LRU_T == 0 and seq % MLA_TQ == 0
    assert seq % MERGE_TM == 0 and seq % NORM_TM == 0

    cos_r, sin_r = _rope_tables(positions, RET_DIM)
    cos_r = jnp.concatenate([cos_r, cos_r], axis=-1).reshape(m, RET_DIM)
    sin_r = jnp.concatenate([-sin_r, sin_r], axis=-1).reshape(m, RET_DIM)
    cos_m, sin_m = _rope_tables(positions, MLA_ROPE)
    tail = jnp.zeros((batch, seq, LANE - MLA_ROPE), F32)
    cos_m = jnp.concatenate([cos_m, cos_m, tail], axis=-1).reshape(m, LANE)
    sin_m = jnp.concatenate([-sin_m, sin_m, tail], axis=-1).reshape(m, LANE)

    c_pad = jnp.pad(c, ((0, SUBLANE - batch), (0, 0)))
    mod_all = _ada_call(c_pad, ada_w, ada_b)
    wb_all = w_branch.astype(BF16)
    wo_all = w_out.astype(BF16)

    w_in_t = jnp.swapaxes(w_in, 1, 2)

    x2 = x.reshape(m, d)
    h = _prenorm_call(x2, mod_all, norm_pre, 0, seq)
    for l in range(depth):
        proj = _inproj_call(h, w_in_t, l)
        y_ret = _ret_call(proj, cos_r, sin_r, ret_gn, l, batch, seq)
        y_lru = _lru_call(proj, lru_conv_w, lru_conv_b, lru_wa, lru_ba, lru_wx, lru_bx, lru_lambda,
                          l, batch, seq)
        y_mla = _mla_call(h, w_in_t, proj, mla_w_uq, mla_w_ukv, mla_q_norm, mla_kv_norm, cos_m, sin_m,
                          l, batch, seq)
        out = _merge_call(y_ret, y_lru, y_mla, proj, x2, mod_all, norm_post, norm_pre, wb_all, wo_all,
                          l, seq)
        if l + 1 < depth:
            x2, h = out
        else:
            x2 = out
    return x2.reshape(batch, seq, d)
```

```python
import functools
import math

import jax
import jax.numpy as jnp
from jax import lax
from jax.experimental import pallas as pl
from jax.experimental.pallas import tpu as pltpu

F32 = jnp.float32
BF16 = jnp.bfloat16

D_MODEL = 2048
CHUNK = 64
ROPE_BASE = 10000.0
NORM_EPS = 1e-6

RET_HEADS = 8
RET_DIM = 128
RET_WIDTH = RET_HEADS * RET_DIM

LRU_WIDTH = 1024
LRU_BLOCKS = 8
LRU_BLOCK_DIM = 128
CONV_WIDTH = 4
LRU_C = 8.0

MLA_HEADS = 8
MLA_NOPE = 128
MLA_ROPE = 64
MLA_V = 128
MLA_LORA = 512
MLA_WIDTH = MLA_HEADS * MLA_V
MLA_QK = MLA_NOPE + MLA_ROPE
MLA_QK_PAD = 256

LANE = 128
SUBLANE = 8
VMEM_LIMIT = 58 * 1024 * 1024

SRC_MKR = 4 * 1024 + 2 * 1024 + 2 * MLA_LORA
SRC_WIDTH = SRC_MKR + MLA_ROPE + MLA_WIDTH + 3 * D_MODEL
COL_RQ = 0
COL_RK = 1024
COL_RV = 2048
COL_RG = 3072
COL_LX = 4096
COL_LG = 5120
COL_MQ = 6144
COL_MKV = COL_MQ + MLA_LORA
COL_MG = SRC_MKR
COL_MERGE = COL_MG + MLA_WIDTH
PROJ_WIDTH = COL_MERGE + 3 * D_MODEL
PROJ_TN = 1024
PROJ_TM = 2048
ALIGNED_TILES = SRC_MKR // PROJ_TN
CAST_ROWS = 64
assert SRC_MKR % PROJ_TN == 0 and PROJ_WIDTH % PROJ_TN == 0 and COL_MERGE % D_MODEL == 0

NORM_TM = 512
RET_T = 256
LRU_T = 512
MLA_TQ = 256
MERGE_TM = 256
ADA_TK = 256


def _silu(v):
    return v * jax.nn.sigmoid(v)


def _dot(a, b):
    return jnp.dot(a, b, preferred_element_type=F32)


def _dot_nt(a, b):
    return lax.dot_general(a, b, (((1,), (1,)), ((), ())), preferred_element_type=F32)


def _dot_tn(a, b):
    return lax.dot_general(a, b, (((0,), (0,)), ((), ())), preferred_element_type=F32)


def _rms(x, gain):
    return x * lax.rsqrt(jnp.mean(x * x, axis=-1, keepdims=True) + NORM_EPS) * gain


def _params(*sem):
    return pltpu.CompilerParams(dimension_semantics=sem, vmem_limit_bytes=VMEM_LIMIT)


def _layer_vec(arr):
    return arr.reshape(arr.shape[0], 1, arr.shape[1])


def _vec_spec(width, layer, ngrid):
    if ngrid == 1:
        return pl.BlockSpec((1, 1, width), lambda i: (layer, 0, 0))
    return pl.BlockSpec((1, 1, width), lambda i, j: (layer, 0, 0))


def _ada_kernel(c_ref, w_ref, b_ref, o_ref):
    @pl.when(pl.program_id(1) == 0)
    def _():
        o_ref[0] = jnp.broadcast_to(b_ref[0], o_ref.shape[1:])

    c_act = _silu(c_ref[...]).astype(BF16)
    o_ref[0] += _dot(c_act, w_ref[0].astype(BF16))


def _ada_call(c_pad, ada_w, ada_b):
    depth, d, n = ada_w.shape
    rows = c_pad.shape[0]
    return pl.pallas_call(
        _ada_kernel,
        grid=(depth, d // ADA_TK),
        in_specs=[
            pl.BlockSpec((rows, ADA_TK), lambda l, k: (0, k)),
            pl.BlockSpec((1, ADA_TK, n), lambda l, k: (l, k, 0)),
            pl.BlockSpec((1, 1, n), lambda l, k: (l, 0, 0)),
        ],
        out_specs=pl.BlockSpec((1, rows, n), lambda l, k: (l, 0, 0)),
        out_shape=jax.ShapeDtypeStruct((depth, rows, n), F32),
        compiler_params=_params("arbitrary", "arbitrary"),
        name="ada_mod",
    )(c_pad, ada_w, _layer_vec(ada_b))


def _modulate(x, gain, shift_ref, scale_ref, b):
    shift = shift_ref[0, pl.ds(b, 1), :]
    scale = scale_ref[0, pl.ds(b, 1), :]
    return _rms(x, gain) * (1.0 + scale) + shift


def _prenorm_kernel(x_ref, g_ref, shift_ref, scale_ref, o_ref, *, tiles_per_batch):
    b = pl.program_id(0) // tiles_per_batch
    o_ref[...] = _modulate(x_ref[...], g_ref[0], shift_ref, scale_ref, b).astype(o_ref.dtype)


def _prenorm_call(x2, mod_all, norm_pre, layer, seq):
    m, d = x2.shape
    rows = mod_all.shape[1]
    return pl.pallas_call(
        functools.partial(_prenorm_kernel, tiles_per_batch=seq // NORM_TM),
        grid=(m // NORM_TM,),
        in_specs=[
            pl.BlockSpec((NORM_TM, d), lambda i: (i, 0)),
            _vec_spec(d, layer, 1),
            pl.BlockSpec((1, rows, d), lambda i: (layer, 0, 0)),
            pl.BlockSpec((1, rows, d), lambda i: (layer, 0, 1)),
        ],
        out_specs=pl.BlockSpec((NORM_TM, d), lambda i: (i, 0)),
        out_shape=jax.ShapeDtypeStruct((m, d), BF16),
        compiler_params=_params("arbitrary"),
        name="pre_norm",
    )(x2, _layer_vec(norm_pre), mod_all, mod_all)


def _inproj_kernel(h_ref, w_ref, wx_ref, o_ref, wb_ref):
    j = pl.program_id(0)
    first_row_tile = pl.program_id(1) == 0
    n_chunks = PROJ_TN // CAST_ROWS

    @pl.when(first_row_tile & (j < ALIGNED_TILES))
    def _():
        def body(i, carry):
            r0 = pl.multiple_of(i * CAST_ROWS, CAST_ROWS)
            wb_ref[pl.ds(r0, CAST_ROWS), :] = w_ref[0, pl.ds(r0, CAST_ROWS), :].astype(BF16)
            return carry

        lax.fori_loop(0, n_chunks, body, 0)

    @pl.when(first_row_tile & (j >= ALIGNED_TILES))
    def _():
        def body(i, carry):
            r0 = pl.multiple_of(i * CAST_ROWS, CAST_ROWS)
            wb_ref[pl.ds(r0, CAST_ROWS), :] = w_ref[0, pl.ds(r0 + MLA_ROPE, CAST_ROWS), :].astype(BF16)
            return carry

        lax.fori_loop(0, n_chunks - 1, body, 0)
        wb_ref[PROJ_TN - MLA_ROPE:, :] = wx_ref[0].astype(BF16)

    o_ref[...] = _dot_nt(h_ref[...], wb_ref[...]).astype(o_ref.dtype)


def _inproj_call(h, w_in_t, layer):
    m, d = h.shape
    per_tile = PROJ_TN // MLA_ROPE
    last_block = SRC_WIDTH // MLA_ROPE - 1
    return pl.pallas_call(
        _inproj_kernel,
        grid=(PROJ_WIDTH // PROJ_TN, m // PROJ_TM),
        in_specs=[
            pl.BlockSpec((PROJ_TM, d), lambda j, i: (i, 0)),
            pl.BlockSpec((1, PROJ_TN, d), lambda j, i: (layer, j, 0)),
            pl.BlockSpec((1, MLA_ROPE, d),
                         lambda j, i: (layer, jnp.minimum((j + 1) * per_tile, last_block), 0)),
        ],
        out_specs=pl.BlockSpec((PROJ_TM, PROJ_TN), lambda j, i: (i, j)),
        out_shape=jax.ShapeDtypeStruct((m, PROJ_WIDTH), BF16),
        scratch_shapes=[pltpu.VMEM((PROJ_TN, d), BF16)],
        compiler_params=_params("arbitrary", "arbitrary"),
        name="in_proj",
    )(h, w_in_t, w_in_t)


_LOG_GAMMA = tuple(math.log1p(-(2.0 ** (-5.0 - h))) for h in range(RET_HEADS))


def _ret_kernel(q_ref, k_ref, v_ref, g_ref, cos_ref, sin_ref, gn_ref, o_ref,
                state_ref, dmat_ref, dq_ref, dk_ref):
    t_blk = RET_T

    @pl.when((pl.program_id(0) == 0) & (pl.program_id(1) == 0))
    def _():
        ri = lax.broadcasted_iota(jnp.int32, (t_blk, t_blk), 0)
        ci = lax.broadcasted_iota(jnp.int32, (t_blk, t_blk), 1)
        dist = jnp.abs(ri - ci).astype(F32)
        visible = (ci // CHUNK) <= (ri // CHUNK)
        row = lax.broadcasted_iota(jnp.int32, (t_blk, RET_DIM), 0).astype(F32)
        for h in range(RET_HEADS):
            lg = _LOG_GAMMA[h]
            dmat_ref[h] = jnp.where(visible, jnp.exp(lg * dist), 0.0)
            dq_ref[h] = jnp.exp(lg * (row + 1.0))
            dk_ref[h] = jnp.exp(lg * ((t_blk - 1.0) - row))

    @pl.when(pl.program_id(1) == 0)
    def _():
        state_ref[...] = jnp.zeros_like(state_ref)

    cos = cos_ref[...]
    sin = sin_ref[...]
    for h in range(RET_HEADS):
        sl = slice(h * RET_DIM, (h + 1) * RET_DIM)
        q = q_ref[:, sl].astype(F32)
        k = k_ref[:, sl].astype(F32)
        q = (q * cos + pltpu.roll(q, RET_DIM // 2, 1) * sin) * (RET_DIM ** -0.5)
        k = k * cos + pltpu.roll(k, RET_DIM // 2, 1) * sin
        v = v_ref[:, sl]
        scores = _dot_nt(q.astype(BF16), k.astype(BF16)) * dmat_ref[h]
        o = _dot(scores.astype(BF16), v)
        state = state_ref[h]
        o = o + _dot((q * dq_ref[h]).astype(BF16), state.astype(BF16))
        k_dec = (k * dk_ref[h]).astype(BF16)
        state_ref[h] = state * math.exp(_LOG_GAMMA[h] * t_blk) + _dot_tn(k_dec, v)
        mean = jnp.mean(o, axis=-1, keepdims=True)
        cen = o - mean
        var = jnp.mean(cen * cen, axis=-1, keepdims=True)
        normed = cen * lax.rsqrt(var + NORM_EPS) * gn_ref[0, :, sl]
        o_ref[:, sl] = (normed * _silu(g_ref[:, sl].astype(F32))).astype(o_ref.dtype)


def _ret_call(proj, cos_r, sin_r, ret_gn, layer, batch, seq):
    m = proj.shape[0]
    nt = seq // RET_T
    w = RET_WIDTH

    def col(c):
        return pl.BlockSpec((RET_T, w), lambda b, t, c=c: (b * nt + t, c // w))

    tab = pl.BlockSpec((RET_T, RET_DIM), lambda b, t: (b * nt + t, 0))
    return pl.pallas_call(
        _ret_kernel,
        grid=(batch, nt),
        in_specs=[col(COL_RQ), col(COL_RK), col(COL_RV), col(COL_RG), tab, tab,
                  _vec_spec(w, layer, 2)],
        out_specs=pl.BlockSpec((RET_T, w), lambda b, t: (b * nt + t, 0)),
        out_shape=jax.ShapeDtypeStruct((m, w), BF16),
        scratch_shapes=[
            pltpu.VMEM((RET_HEADS, RET_DIM, RET_DIM), F32),
            pltpu.VMEM((RET_HEADS, RET_T, RET_T), F32),
            pltpu.VMEM((RET_HEADS, RET_T, RET_DIM), F32),
            pltpu.VMEM((RET_HEADS, RET_T, RET_DIM), F32),
        ],
        compiler_params=_params("arbitrary", "arbitrary"),
        name="retention",
    )(proj, proj, proj, proj, cos_r, sin_r, _layer_vec(ret_gn))


def _lru_kernel(x_ref, g_ref, cw_ref, cb_ref, wa_ref, ba_ref, wx_ref, bx_ref, lam_ref, o_ref,
                xbuf, hcar, a_s, b_s):
    t_blk = LRU_T
    halo = SUBLANE

    @pl.when(pl.program_id(1) == 0)
    def _():
        xbuf[0:halo, :] = jnp.zeros((halo, LRU_WIDTH), F32)
        hcar[...] = jnp.zeros_like(hcar)

    xbuf[halo:halo + t_blk, :] = x_ref[...].astype(F32)
    xc = cb_ref[0]
    for w in range(CONV_WIDTH):
        off = halo - (CONV_WIDTH - 1) + w
        xc = xc + cw_ref[0, w:w + 1, :] * xbuf[off:off + t_blk, :]
    xbuf[0:halo, :] = xbuf[t_blk:t_blk + halo, :]

    neg_lam = -lam_ref[0]
    softplus = jnp.maximum(neg_lam, 0.0) + jnp.log1p(jnp.exp(-jnp.abs(neg_lam)))
    for n in range(LRU_BLOCKS):
        sl = slice(n * LRU_BLOCK_DIM, (n + 1) * LRU_BLOCK_DIM)
        xn = xc[:, sl]
        xn_b = xn.astype(BF16)
        r = jax.nn.sigmoid(_dot(xn_b, wa_ref[0, n].astype(BF16)) + ba_ref[0, :, sl])
        i = jax.nn.sigmoid(_dot(xn_b, wx_ref[0, n].astype(BF16)) + bx_ref[0, :, sl])
        log_a = (-LRU_C) * r * softplus[:, sl]
        a = jnp.exp(log_a)
        a_s[:, sl] = a
        b_s[:, sl] = jnp.sqrt(jnp.tanh(-log_a) * (a * a + 1.0)) * (i * xn)

    row = lax.broadcasted_iota(jnp.int32, (SUBLANE, LRU_WIDTH), 0)

    def body(i, h):
        r0 = pl.multiple_of(i * SUBLANE, SUBLANE)
        a = a_s[pl.ds(r0, SUBLANE), :]
        b = b_s[pl.ds(r0, SUBLANE), :]
        for s in (1, 2, 4):
            keep = row >= s
            a_prev = jnp.where(keep, pltpu.roll(a, s, 0), 1.0)
            b_prev = jnp.where(keep, pltpu.roll(b, s, 0), 0.0)
            b = a * b_prev + b
            a = a * a_prev
        hrows = a * h + b
        b_s[pl.ds(r0, SUBLANE), :] = hrows
        return jnp.broadcast_to(hrows[SUBLANE - 1:SUBLANE, :], (SUBLANE, LRU_WIDTH))

    hcar[...] = lax.fori_loop(0, t_blk // SUBLANE, body, hcar[...])
    o_ref[...] = (b_s[...] * _silu(g_ref[...].astype(F32))).astype(o_ref.dtype)


def _lru_call(proj, cw, cb, wa, ba, wx, bx, lam, layer, batch, seq):
    m = proj.shape[0]
    nt = seq // LRU_T
    w = LRU_WIDTH

    def col(c):
        return pl.BlockSpec((LRU_T, w), lambda b, t, c=c: (b * nt + t, c // w))

    def vec():
        return _vec_spec(w, layer, 2)

    def blk():
        return pl.BlockSpec((1, LRU_BLOCKS, LRU_BLOCK_DIM, LRU_BLOCK_DIM), lambda b, t: (layer, 0, 0, 0))

    return pl.pallas_call(
        _lru_kernel,
        grid=(batch, nt),
        in_specs=[col(COL_LX), col(COL_LG),
                  pl.BlockSpec((1, CONV_WIDTH, w), lambda b, t: (layer, 0, 0)), vec(),
                  blk(), vec(), blk(), vec(), vec()],
        out_specs=pl.BlockSpec((LRU_T, w), lambda b, t: (b * nt + t, 0)),
        out_shape=jax.ShapeDtypeStruct((m, w), BF16),
        scratch_shapes=[
            pltpu.VMEM((LRU_T + SUBLANE, w), F32),
            pltpu.VMEM((SUBLANE, w), F32),
            pltpu.VMEM((LRU_T, w), F32),
            pltpu.VMEM((LRU_T, w), F32),
        ],
        compiler_params=_params("arbitrary", "arbitrary"),
        name="rg_lru",
    )(proj, proj, cw, _layer_vec(cb), wa, _layer_vec(ba), wx, _layer_vec(bx), _layer_vec(lam))


def _rope64(x, cos_t, sin_t):
    swapped = pltpu.roll(x, MLA_ROPE // 2, 1) + pltpu.roll(x, LANE - MLA_ROPE // 2, 1)
    return x * cos_t + swapped * sin_t


def _mla_kernel(h_ref, wkr_ref, mq_ref, mkv_ref, mg_ref, wq_ref, wkv_ref, qn_ref, kvn_ref,
                cos_ref, sin_ref, o_ref, qlat, kvlat, krope, wq_s, qs, ks, vs, s_scr, *, seq):
    rows = 256
    head = pl.program_id(1)

    @pl.when(head == 0)
    def _():
        wkr = jnp.concatenate([wkr_ref[0], jnp.zeros((LANE - MLA_ROPE, D_MODEL), F32)], axis=0).astype(BF16)
        for i in range(seq // rows):
            rs = slice(i * rows, (i + 1) * rows)
            qlat[rs, :] = _rms(mq_ref[rs, :].astype(F32), qn_ref[0]).astype(BF16)
            kvlat[rs, :] = _rms(mkv_ref[rs, :].astype(F32), kvn_ref[0]).astype(BF16)
            kr = _dot_nt(h_ref[rs, :], wkr)
            krope[rs, :] = _rope64(kr, cos_ref[rs, :], sin_ref[rs, :]).astype(BF16)
        vs[:, MLA_V:] = jnp.ones((seq, MLA_V), BF16)
        zeros = jnp.zeros((MLA_LORA, MLA_QK_PAD - MLA_QK), F32)
        wq_all = wq_ref[0]
        for hh in range(MLA_HEADS):
            w_head = wq_all[:, hh * MLA_QK:(hh + 1) * MLA_QK]
            wq_s[hh] = jnp.concatenate([w_head, zeros], axis=1).astype(BF16)

    scale = (MLA_QK ** -0.5) * math.log2(math.e)
    wq = wq_s[head]
    wkv = wkv_ref[0].astype(BF16)
    for i in range(seq // rows):
        rs = slice(i * rows, (i + 1) * rows)
        q = _dot(qlat[rs, :], wq)
        qs[rs, 0:MLA_NOPE] = (q[:, 0:MLA_NOPE] * scale).astype(BF16)
        qs[rs, MLA_NOPE:] = (_rope64(q[:, MLA_NOPE:], cos_ref[rs, :], sin_ref[rs, :]) * scale).astype(BF16)
        kv = _dot(kvlat[rs, :], wkv)
        ks[rs, 0:MLA_NOPE] = kv[:, 0:MLA_NOPE].astype(BF16)
        ks[rs, MLA_NOPE:] = krope[rs, :]
        vs[rs, 0:MLA_V] = kv[:, MLA_NOPE:].astype(BF16)

    tq = MLA_TQ
    ri = lax.broadcasted_iota(jnp.int32, (tq, tq), 0)
    ci = lax.broadcasted_iota(jnp.int32, (tq, tq), 1)
    visible = (ci // CHUNK) <= (ri // CHUNK)
    n_q = seq // tq

    def scores(i):
        q0, q1 = i * tq, (i + 1) * tq
        s_scr[i % 2, :, 0:q1] = _dot_nt(qs[q0:q1, :], ks[0:q1, :])

    scores(0)
    for i in range(n_q):
        q0, q1 = i * tq, (i + 1) * tq
        if i + 1 < n_q:
            scores(i + 1)
        s_d = jnp.where(visible, s_scr[i % 2, :, q0:q1], -1e30)
        m = jnp.max(s_d, axis=-1, keepdims=True)
        if i > 0:
            m = jnp.maximum(m, jnp.max(s_scr[i % 2, :, 0:q0], axis=-1, keepdims=True))
        acc = _dot(jnp.exp2(s_d - m).astype(BF16), vs[q0:q1, :])
        if i > 0:
            acc = acc + _dot(jnp.exp2(s_scr[i % 2, :, 0:q0] - m).astype(BF16), vs[0:q0, :])
        out = acc[:, 0:MLA_V] / acc[:, MLA_V:]
        o_ref[q0:q1, :] = (out * _silu(mg_ref[q0:q1, :].astype(F32))).astype(o_ref.dtype)


def _mla_call(h, w_in_t, proj, w_uq, w_ukv, q_norm, kv_norm, cos_m, sin_m, layer, batch, seq):
    m, d = h.shape
    lora = MLA_LORA
    return pl.pallas_call(
        functools.partial(_mla_kernel, seq=seq),
        grid=(batch, MLA_HEADS),
        in_specs=[
            pl.BlockSpec((seq, d), lambda b, hd: (b, 0)),
            pl.BlockSpec((1, MLA_ROPE, d), lambda b, hd: (layer, SRC_MKR // MLA_ROPE, 0)),
            pl.BlockSpec((seq, lora), lambda b, hd: (b, COL_MQ // lora)),
            pl.BlockSpec((seq, lora), lambda b, hd: (b, COL_MKV // lora)),
            pl.BlockSpec((seq, MLA_V), lambda b, hd: (b, COL_MG // MLA_V + hd)),
            pl.BlockSpec((1, lora, MLA_HEADS * MLA_QK), lambda b, hd: (layer, 0, 0)),
            pl.BlockSpec((1, lora, MLA_NOPE + MLA_V), lambda b, hd: (layer, 0, hd)),
            _vec_spec(lora, layer, 2),
            _vec_spec(lora, layer, 2),
            pl.BlockSpec((seq, LANE), lambda b, hd: (b, 0)),
            pl.BlockSpec((seq, LANE), lambda b, hd: (b, 0)),
        ],
        out_specs=pl.BlockSpec((seq, MLA_V), lambda b, hd: (b, hd)),
        out_shape=jax.ShapeDtypeStruct((m, MLA_WIDTH), BF16),
        scratch_shapes=[
            pltpu.VMEM((seq, lora), BF16),
            pltpu.VMEM((seq, lora), BF16),
            pltpu.VMEM((seq, LANE), BF16),
            pltpu.VMEM((MLA_HEADS, lora, MLA_QK_PAD), BF16),
            pltpu.VMEM((seq, MLA_QK_PAD), BF16),
            pltpu.VMEM((seq, MLA_QK_PAD), BF16),
            pltpu.VMEM((seq, 2 * MLA_V), BF16),
            pltpu.VMEM((2, MLA_TQ, seq), F32),
        ],
        compiler_params=_params("arbitrary", "arbitrary"),
        name="mla",
    )(h, w_in_t, proj, proj, proj, w_uq, w_ukv, _layer_vec(q_norm), _layer_vec(kv_norm), cos_m, sin_m)


def _merge_kernel(yr_ref, yl_ref, ym_ref, l0_ref, l1_ref, l2_ref, x_ref, res_ref, g_ref,
                  wb_ref, wo_ref, *rest, tiles_per_batch, emit_next):
    b = pl.program_id(0) // tiles_per_batch
    merged = None
    for i, (y_ref, l_ref) in enumerate(((yr_ref, l0_ref), (yl_ref, l1_ref), (ym_ref, l2_ref))):
        z = _dot(y_ref[...], wb_ref[0, i * 1024:(i + 1) * 1024, :])
        z = z * jax.nn.sigmoid(l_ref[...].astype(F32))
        merged = z if merged is None else merged + z
    y = _dot(merged.astype(BF16), wo_ref[0])
    x_new = x_ref[...] + (1.0 + res_ref[0, pl.ds(b, 1), :]) * _rms(y, g_ref[0])
    if emit_next:
        gn_ref, shift_ref, scale_ref, o_ref, h_ref = rest
        h_ref[...] = _modulate(x_new, gn_ref[0], shift_ref, scale_ref, b).astype(h_ref.dtype)
    else:
        (o_ref,) = rest
    o_ref[...] = x_new


def _merge_call(y_ret, y_lru, y_mla, proj, x2, mod_all, norm_post, norm_pre, wb, wo, layer, seq):
    m, d = x2.shape
    tm = MERGE_TM
    depth, mod_rows = mod_all.shape[:2]
    emit_next = layer + 1 < depth

    def rows(width, c=0):
        return pl.BlockSpec((tm, width), lambda i, c=c: (i, c))

    def resident(shape):
        return pl.BlockSpec((1,) + shape[1:], lambda i: (layer, 0, 0), pipeline_mode=pl.Buffered(1))

    def mod_part(lyr, part):
        return pl.BlockSpec((1, mod_rows, d), lambda i: (lyr, 0, part))

    merge_col = COL_MERGE // d
    in_specs = [rows(1024), rows(1024), rows(1024),
                rows(d, merge_col), rows(d, merge_col + 1), rows(d, merge_col + 2),
                rows(d), mod_part(layer, 2), _vec_spec(d, layer, 1),
                resident(wb.shape), resident(wo.shape)]
    args = [y_ret, y_lru, y_mla, proj, proj, proj, x2, mod_all, _layer_vec(norm_post), wb, wo]
    out_specs = rows(d)
    out_shape = jax.ShapeDtypeStruct((m, d), F32)
    if emit_next:
        in_specs += [_vec_spec(d, layer + 1, 1), mod_part(layer + 1, 0), mod_part(layer + 1, 1)]
        args += [_layer_vec(norm_pre), mod_all, mod_all]
        out_specs = (out_specs, rows(d))
        out_shape = (out_shape, jax.ShapeDtypeStruct((m, d), BF16))
    return pl.pallas_call(
        functools.partial(_merge_kernel, tiles_per_batch=seq // tm, emit_next=emit_next),
        grid=(m // tm,),
        in_specs=in_specs,
        out_specs=out_specs,
        out_shape=out_shape,
        compiler_params=_params("arbitrary"),
        name="merge_out",
    )(*args)


def _rope_tables(positions, dim):
    inv_freq = ROPE_BASE ** (-jnp.arange(0, dim, 2, dtype=F32) / dim)
    ang = positions.astype(F32)[:, :, None] * inv_freq
    return jnp.cos(ang), jnp.sin(ang)


def kernel(x, c, positions, ada_w, ada_b, norm_pre, norm_post, w_in, ret_gn, lru_conv_w, lru_conv_b,
           lru_wa, lru_ba, lru_wx, lru_bx, lru_lambda, mla_q_norm, mla_w_uq, mla_kv_norm, mla_w_ukv,
           w_branch, w_out):
    batch, seq, d = x.shape
    depth = w_in.shape[0]
    m = batch * seq
    assert d == D_MODEL and w_in.shape[2] == SRC_WIDTH and batch <= SUBLANE
    assert seq % PROJ_TM == 0 and seq % RET_T == 0 and seq % LRU_T == 0 and seq % MLA_TQ == 0
    assert seq % MERGE_TM == 0 and seq % NORM_TM == 0

    cos_r, sin_r = _rope_tables(positions, RET_DIM)
    cos_r = jnp.concatenate([cos_r, cos_r], axis=-1).reshape(m, RET_DIM)
    sin_r = jnp.concatenate([-sin_r, sin_r], axis=-1).reshape(m, RET_DIM)
    cos_m, sin_m = _rope_tables(positions, MLA_ROPE)
    tail = jnp.zeros((batch, seq, LANE - MLA_ROPE), F32)
    cos_m = jnp.concatenate([cos_m, cos_m, tail], axis=-1).reshape(m, LANE)
    sin_m = jnp.concatenate([-sin_m, sin_m, tail], axis=-1).reshape(m, LANE)

    c_pad = jnp.pad(c, ((0, SUBLANE - batch), (0, 0)))
    mod_all = _ada_call(c_pad, ada_w, ada_b)
    wb_all = w_branch.astype(BF16)
    wo_all = w_out.astype(BF16)

    w_in_t = jnp.swapaxes(w_in, 1, 2)

    x2 = x.reshape(m, d)
    h = _prenorm_call(x2, mod_all, norm_pre, 0, seq)
    for l in range(depth):
        proj = _inproj_call(h, w_in_t, l)
        y_ret = _ret_call(proj, cos_r, sin_r, ret_gn, l, batch, seq)
        y_lru = _lru_call(proj, lru_conv_w, lru_conv_b, lru_wa, lru_ba, lru_wx, lru_bx, lru_lambda,
                          l, batch, seq)
        y_mla = _mla_call(h, w_in_t, proj, mla_w_uq, mla_w_ukv, mla_q_norm, mla_kv_norm, cos_m, sin_m,
                          l, batch, seq)
        out = _merge_call(y_ret, y_lru, y_mla, proj, x2, mod_all, norm_post, norm_pre, wb_all, wo_all,
                          l, seq)
        if l + 1 < depth:
            x2, h = out
        else:
            x2 = out
    return x2.reshape(batch, seq, d)
```

```python
import functools
import math

import jax
import jax.numpy as jnp
from jax import lax
from jax.experimental import pallas as pl
from jax.experimental.pallas import tpu as pltpu

F32 = jnp.float32
BF16 = jnp.bfloat16

D_MODEL = 2048
CHUNK = 64
ROPE_BASE = 10000.0
NORM_EPS = 1e-6

RET_HEADS = 8
RET_DIM = 128
RET_WIDTH = RET_HEADS * RET_DIM

LRU_WIDTH = 1024
LRU_BLOCKS = 8
LRU_BLOCK_DIM = 128
CONV_WIDTH = 4
LRU_C = 8.0

MLA_HEADS = 8
MLA_NOPE = 128
MLA_ROPE = 64
MLA_V = 128
MLA_LORA = 512
MLA_WIDTH = MLA_HEADS * MLA_V
MLA_QK = MLA_NOPE + MLA_ROPE
MLA_QK_PAD = 256

LANE = 128
SUBLANE = 8
VMEM_LIMIT = 58 * 1024 * 1024

SRC_MKR = 4 * 1024 + 2 * 1024 + 2 * MLA_LORA
SRC_WIDTH = SRC_MKR + MLA_ROPE + MLA_WIDTH + 3 * D_MODEL
COL_RQ = 0
COL_RK = 1024
COL_RV = 2048
COL_RG = 3072
COL_LX = 4096
COL_LG = 5120
COL_MQ = 6144
COL_MKV = COL_MQ + MLA_LORA
COL_MG = SRC_MKR
COL_MERGE = COL_MG + MLA_WIDTH
PROJ_WIDTH = COL_MERGE + 3 * D_MODEL
PROJ_TN = 1024
PROJ_TM = 2048
ALIGNED_TILES = SRC_MKR // PROJ_TN
CAST_ROWS = 64
assert SRC_MKR % PROJ_TN == 0 and PROJ_WIDTH % PROJ_TN == 0 and COL_MERGE % D_MODEL == 0

NORM_TM = 512
RET_T = 256
LRU_T = 512
LRU_CLEN = LRU_T // 8
MLA_TQ = 256
MERGE_TM = 256
ADA_TK = 256


def _silu(v):
    return v * jax.nn.sigmoid(v)


def _dot(a, b):
    return jnp.dot(a, b, preferred_element_type=F32)


def _dot_nt(a, b):
    return lax.dot_general(a, b, (((1,), (1,)), ((), ())), preferred_element_type=F32)


def _dot_tn(a, b):
    return lax.dot_general(a, b, (((0,), (0,)), ((), ())), preferred_element_type=F32)


def _rms(x, gain):
    return x * lax.rsqrt(jnp.mean(x * x, axis=-1, keepdims=True) + NORM_EPS) * gain


def _params(*sem):
    return pltpu.CompilerParams(dimension_semantics=sem, vmem_limit_bytes=VMEM_LIMIT)


def _layer_vec(arr):
    return arr.reshape(arr.shape[0], 1, arr.shape[1])


def _vec_spec(width, layer, ngrid):
    if ngrid == 1:
        return pl.BlockSpec((1, 1, width), lambda i: (layer, 0, 0))
    return pl.BlockSpec((1, 1, width), lambda i, j: (layer, 0, 0))


def _ada_kernel(c_ref, w_ref, b_ref, o_ref):
    @pl.when(pl.program_id(1) == 0)
    def _():
        o_ref[0] = jnp.broadcast_to(b_ref[0], o_ref.shape[1:])

    c_act = _silu(c_ref[...]).astype(BF16)
    o_ref[0] += _dot(c_act, w_ref[0].astype(BF16))


def _ada_call(c_pad, ada_w, ada_b):
    depth, d, n = ada_w.shape
    rows = c_pad.shape[0]
    return pl.pallas_call(
        _ada_kernel,
        grid=(depth, d // ADA_TK),
        in_specs=[
            pl.BlockSpec((rows, ADA_TK), lambda l, k: (0, k)),
            pl.BlockSpec((1, ADA_TK, n), lambda l, k: (l, k, 0)),
            pl.BlockSpec((1, 1, n), lambda l, k: (l, 0, 0)),
        ],
        out_specs=pl.BlockSpec((1, rows, n), lambda l, k: (l, 0, 0)),
        out_shape=jax.ShapeDtypeStruct((depth, rows, n), F32),
        compiler_params=_params("arbitrary", "arbitrary"),
        name="ada_mod",
    )(c_pad, ada_w, _layer_vec(ada_b))


def _modulate(x, gain, shift_ref, scale_ref, b):
    shift = shift_ref[0, pl.ds(b, 1), :]
    scale = scale_ref[0, pl.ds(b, 1), :]
    return _rms(x, gain) * (1.0 + scale) + shift


def _prenorm_kernel(x_ref, g_ref, shift_ref, scale_ref, o_ref, *, tiles_per_batch):
    b = pl.program_id(0) // tiles_per_batch
    o_ref[...] = _modulate(x_ref[...], g_ref[0], shift_ref, scale_ref, b).astype(o_ref.dtype)


def _prenorm_call(x2, mod_all, norm_pre, layer, seq):
    m, d = x2.shape
    rows = mod_all.shape[1]
    return pl.pallas_call(
        functools.partial(_prenorm_kernel, tiles_per_batch=seq // NORM_TM),
        grid=(m // NORM_TM,),
        in_specs=[
            pl.BlockSpec((NORM_TM, d), lambda i: (i, 0)),
            _vec_spec(d, layer, 1),
            pl.BlockSpec((1, rows, d), lambda i: (layer, 0, 0)),
            pl.BlockSpec((1, rows, d), lambda i: (layer, 0, 1)),
        ],
        out_specs=pl.BlockSpec((NORM_TM, d), lambda i: (i, 0)),
        out_shape=jax.ShapeDtypeStruct((m, d), BF16),
        compiler_params=_params("arbitrary"),
        name="pre_norm",
    )(x2, _layer_vec(norm_pre), mod_all, mod_all)


def _inproj_kernel(h_ref, w_ref, wx_ref, o_ref, wb_ref):
    j = pl.program_id(0)
    first_row_tile = pl.program_id(1) == 0
    n_chunks = PROJ_TN // CAST_ROWS

    @pl.when(first_row_tile & (j < ALIGNED_TILES))
    def _():
        def body(i, carry):
            r0 = pl.multiple_of(i * CAST_ROWS, CAST_ROWS)
            wb_ref[pl.ds(r0, CAST_ROWS), :] = w_ref[0, pl.ds(r0, CAST_ROWS), :].astype(BF16)
            return carry

        lax.fori_loop(0, n_chunks, body, 0)

    @pl.when(first_row_tile & (j >= ALIGNED_TILES))
    def _():
        def body(i, carry):
            r0 = pl.multiple_of(i * CAST_ROWS, CAST_ROWS)
            wb_ref[pl.ds(r0, CAST_ROWS), :] = w_ref[0, pl.ds(r0 + MLA_ROPE, CAST_ROWS), :].astype(BF16)
            return carry

        lax.fori_loop(0, n_chunks - 1, body, 0)
        wb_ref[PROJ_TN - MLA_ROPE:, :] = wx_ref[0].astype(BF16)

    o_ref[...] = _dot_nt(h_ref[...], wb_ref[...]).astype(o_ref.dtype)


def _inproj_call(h, w_in_t, layer):
    m, d = h.shape
    per_tile = PROJ_TN // MLA_ROPE
    last_block = SRC_WIDTH // MLA_ROPE - 1
    return pl.pallas_call(
        _inproj_kernel,
        grid=(PROJ_WIDTH // PROJ_TN, m // PROJ_TM),
        in_specs=[
            pl.BlockSpec((PROJ_TM, d), lambda j, i: (i, 0)),
            pl.BlockSpec((1, PROJ_TN, d), lambda j, i: (layer, j, 0)),
            pl.BlockSpec((1, MLA_ROPE, d),
                         lambda j, i: (layer, jnp.minimum((j + 1) * per_tile, last_block), 0)),
        ],
        out_specs=pl.BlockSpec((PROJ_TM, PROJ_TN), lambda j, i: (i, j)),
        out_shape=jax.ShapeDtypeStruct((m, PROJ_WIDTH), BF16),
        scratch_shapes=[pltpu.VMEM((PROJ_TN, d), BF16)],
        compiler_params=_params("arbitrary", "arbitrary"),
        name="in_proj",
    )(h, w_in_t, w_in_t)


_LOG_GAMMA = tuple(math.log1p(-(2.0 ** (-5.0 - h))) for h in range(RET_HEADS))


def _ret_kernel(q_ref, k_ref, v_ref, g_ref, cos_ref, sin_ref, gn_ref, o_ref,
                state_ref, dmat_ref, dq_ref, dk_ref):
    t_blk = RET_T

    @pl.when((pl.program_id(0) == 0) & (pl.program_id(1) == 0))
    def _():
        ri = lax.broadcasted_iota(jnp.int32, (t_blk, t_blk), 0)
        ci = lax.broadcasted_iota(jnp.int32, (t_blk, t_blk), 1)
        dist = jnp.abs(ri - ci).astype(F32)
        visible = (ci // CHUNK) <= (ri // CHUNK)
        row = lax.broadcasted_iota(jnp.int32, (t_blk, RET_DIM), 0).astype(F32)
        for h in range(RET_HEADS):
            lg = _LOG_GAMMA[h]
            dmat_ref[h] = jnp.where(visible, jnp.exp(lg * dist), 0.0)
            dq_ref[h] = jnp.exp(lg * (row + 1.0))
            dk_ref[h] = jnp.exp(lg * ((t_blk - 1.0) - row))

    @pl.when(pl.program_id(1) == 0)
    def _():
        state_ref[...] = jnp.zeros_like(state_ref)

    cos = cos_ref[...]
    sin = sin_ref[...]
    for h in range(RET_HEADS):
        sl = slice(h * RET_DIM, (h + 1) * RET_DIM)
        q = q_ref[:, sl].astype(F32)
        k = k_ref[:, sl].astype(F32)
        q = (q * cos + pltpu.roll(q, RET_DIM // 2, 1) * sin) * (RET_DIM ** -0.5)
        k = k * cos + pltpu.roll(k, RET_DIM // 2, 1) * sin
        v = v_ref[:, sl]
        scores = _dot_nt(q.astype(BF16), k.astype(BF16)) * dmat_ref[h]
        o = _dot(scores.astype(BF16), v)
        state = state_ref[h]
        o = o + _dot((q * dq_ref[h]).astype(BF16), state.astype(BF16))
        k_dec = (k * dk_ref[h]).astype(BF16)
        state_ref[h] = state * math.exp(_LOG_GAMMA[h] * t_blk) + _dot_tn(k_dec, v)
        mean = jnp.mean(o, axis=-1, keepdims=True)
        cen = o - mean
        var = jnp.mean(cen * cen, axis=-1, keepdims=True)
        normed = cen * lax.rsqrt(var + NORM_EPS) * gn_ref[0, :, sl]
        o_ref[:, sl] = (normed * _silu(g_ref[:, sl].astype(F32))).astype(o_ref.dtype)


def _ret_call(proj, cos_r, sin_r, ret_gn, layer, batch, seq):
    m = proj.shape[0]
    nt = seq // RET_T
    w = RET_WIDTH

    def col(c):
        return pl.BlockSpec((RET_T, w), lambda b, t, c=c: (b * nt + t, c // w))

    tab = pl.BlockSpec((RET_T, RET_DIM), lambda b, t: (b * nt + t, 0))
    return pl.pallas_call(
        _ret_kernel,
        grid=(batch, nt),
        in_specs=[col(COL_RQ), col(COL_RK), col(COL_RV), col(COL_RG), tab, tab,
                  _vec_spec(w, layer, 2)],
        out_specs=pl.BlockSpec((RET_T, w), lambda b, t: (b * nt + t, 0)),
        out_shape=jax.ShapeDtypeStruct((m, w), BF16),
        scratch_shapes=[
            pltpu.VMEM((RET_HEADS, RET_DIM, RET_DIM), F32),
            pltpu.VMEM((RET_HEADS, RET_T, RET_T), F32),
            pltpu.VMEM((RET_HEADS, RET_T, RET_DIM), F32),
            pltpu.VMEM((RET_HEADS, RET_T, RET_DIM), F32),
        ],
        compiler_params=_params("arbitrary", "arbitrary"),
        name="retention",
    )(proj, proj, proj, proj, cos_r, sin_r, _layer_vec(ret_gn))


def _sublane_scan(a, b, row):
    for s in (1, 2, 4):
        keep = row >= s
        a_prev = jnp.where(keep, pltpu.roll(a, s, 0), 1.0)
        b_prev = jnp.where(keep, pltpu.roll(b, s, 0), 0.0)
        b = a * b_prev + b
        a = a * a_prev
    return a, b


def _lru_kernel(x_ref, g_ref, cw_ref, cb_ref, wa_ref, ba_ref, wx_ref, bx_ref, lam_ref, o_ref,
                perm, perm_t, halo, hcar, a_s, b_s):
    t_blk = LRU_T
    clen = LRU_CLEN
    taps = CONV_WIDTH - 1
    row = lax.broadcasted_iota(jnp.int32, (SUBLANE, LRU_WIDTH), 0)

    @pl.when((pl.program_id(0) == 0) & (pl.program_id(1) == 0))
    def _():
        r = lax.broadcasted_iota(jnp.int32, (t_blk, t_blk), 0)
        c = lax.broadcasted_iota(jnp.int32, (t_blk, t_blk), 1)
        perm[...] = jnp.where(c == (r % SUBLANE) * clen + r // SUBLANE, 1.0, 0.0).astype(BF16)
        perm_t[...] = jnp.where(c == (r % clen) * SUBLANE + r // clen, 1.0, 0.0).astype(BF16)

    @pl.when(pl.program_id(1) == 0)
    def _():
        halo[...] = jnp.zeros_like(halo)
        hcar[...] = jnp.zeros_like(hcar)

    xp = _dot(perm[...], x_ref[...])
    pieces = []
    for k in range(taps):
        cur = xp[t_blk - (taps - k) * SUBLANE:t_blk - (taps - k - 1) * SUBLANE, :]
        prev = halo[k * SUBLANE:(k + 1) * SUBLANE, :]
        pieces.append(jnp.where(row == 0, pltpu.roll(prev, 1, 0), pltpu.roll(cur, 1, 0)))
    halo[...] = xp[t_blk - taps * SUBLANE:, :]
    xext = jnp.concatenate(pieces + [xp], axis=0)
    xc = cb_ref[0]
    for w in range(CONV_WIDTH):
        xc = xc + cw_ref[0, w:w + 1, :] * xext[w * SUBLANE:w * SUBLANE + t_blk, :]

    neg_lam = -lam_ref[0]
    softplus = jnp.maximum(neg_lam, 0.0) + jnp.log1p(jnp.exp(-jnp.abs(neg_lam)))
    neg_half_rate = (-0.5 * LRU_C) * softplus
    for n in range(LRU_BLOCKS):
        sl = slice(n * LRU_BLOCK_DIM, (n + 1) * LRU_BLOCK_DIM)
        xn = xc[:, sl]
        xn_b = xn.astype(BF16)
        t_r = jnp.tanh(0.5 * (_dot(xn_b, wa_ref[0, n].astype(BF16)) + ba_ref[0, :, sl]))
        t_i = jnp.tanh(0.5 * (_dot(xn_b, wx_ref[0, n].astype(BF16)) + bx_ref[0, :, sl]))
        half_rate = neg_half_rate[:, sl]
        log_a = half_rate * t_r + half_rate
        a = jnp.exp(log_a)
        a_s[:, sl] = a
        var = jnp.tanh(-log_a) * (a * a + 1.0)
        std = jnp.where(var > 0.0, var * lax.rsqrt(var), 0.0)
        half_x = 0.5 * xn
        b_s[:, sl] = std * (half_x * t_i + half_x)

    def body(j, carry):
        h, prod = carry
        r0 = pl.multiple_of(j * SUBLANE, SUBLANE)
        a = a_s[pl.ds(r0, SUBLANE), :]
        h = a * h + b_s[pl.ds(r0, SUBLANE), :]
        prod = a * prod
        b_s[pl.ds(r0, SUBLANE), :] = h
        a_s[pl.ds(r0, SUBLANE), :] = prod
        return h, prod

    zeros = jnp.zeros((SUBLANE, LRU_WIDTH), F32)
    h_end, a_end = lax.fori_loop(0, clen, body, (zeros, zeros + 1.0), unroll=4)

    a_inc, h_inc = _sublane_scan(a_end, h_end, row)
    h_prev_tile = hcar[...]
    h_chunk_end = a_inc * h_prev_tile + h_inc
    h_chunk_start = jnp.where(row == 0, h_prev_tile, pltpu.roll(h_chunk_end, 1, 0))
    hcar[...] = jnp.broadcast_to(h_chunk_end[SUBLANE - 1:SUBLANE, :], (SUBLANE, LRU_WIDTH))

    h_true = b_s[...] + a_s[...] * pltpu.repeat(h_chunk_start, clen, axis=0)
    gp = _dot(perm[...], g_ref[...])
    out_p = (h_true * _silu(gp)).astype(BF16)
    o_ref[...] = _dot(perm_t[...], out_p).astype(o_ref.dtype)


def _lru_call(proj, cw, cb, wa, ba, wx, bx, lam, layer, batch, seq):
    m = proj.shape[0]
    nt = seq // LRU_T
    w = LRU_WIDTH

    def col(c):
        return pl.BlockSpec((LRU_T, w), lambda b, t, c=c: (b * nt + t, c // w))

    def vec():
        return _vec_spec(w, layer, 2)

    def blk():
        return pl.BlockSpec((1, LRU_BLOCKS, LRU_BLOCK_DIM, LRU_BLOCK_DIM), lambda b, t: (layer, 0, 0, 0))

    return pl.pallas_call(
        _lru_kernel,
        grid=(batch, nt),
        in_specs=[col(COL_LX), col(COL_LG),
                  pl.BlockSpec((1, CONV_WIDTH, w), lambda b, t: (layer, 0, 0)), vec(),
                  blk(), vec(), blk(), vec(), vec()],
        out_specs=pl.BlockSpec((LRU_T, w), lambda b, t: (b * nt + t, 0)),
        out_shape=jax.ShapeDtypeStruct((m, w), BF16),
        scratch_shapes=[
            pltpu.VMEM((LRU_T, LRU_T), BF16),
            pltpu.VMEM((LRU_T, LRU_T), BF16),
            pltpu.VMEM(((CONV_WIDTH - 1) * SUBLANE, w), F32),
            pltpu.VMEM((SUBLANE, w), F32),
            pltpu.VMEM((LRU_T, w), F32),
            pltpu.VMEM((LRU_T, w), F32),
        ],
        compiler_params=_params("arbitrary", "arbitrary"),
        name="rg_lru",
    )(proj, proj, cw, _layer_vec(cb), wa, _layer_vec(ba), wx, _layer_vec(bx), _layer_vec(lam))


def _rope64(x, cos_t, sin_t):
    swapped = pltpu.roll(x, MLA_ROPE // 2, 1) + pltpu.roll(x, LANE - MLA_ROPE // 2, 1)
    return x * cos_t + swapped * sin_t


def _mla_kernel(h_ref, wkr_ref, mq_ref, mkv_ref, mg_ref, wq_ref, wkv_ref, qn_ref, kvn_ref,
                cos_ref, sin_ref, o_ref, qlat, kvlat, krope, wq_s, qs, ks, vs, s_scr, p_scr, *, seq):
    rows = 256
    head = pl.program_id(1)

    @pl.when(head == 0)
    def _():
        wkr = jnp.concatenate([wkr_ref[0], jnp.zeros((LANE - MLA_ROPE, D_MODEL), F32)], axis=0).astype(BF16)
        for i in range(seq // rows):
            rs = slice(i * rows, (i + 1) * rows)
            qlat[rs, :] = _rms(mq_ref[rs, :].astype(F32), qn_ref[0]).astype(BF16)
            kvlat[rs, :] = _rms(mkv_ref[rs, :].astype(F32), kvn_ref[0]).astype(BF16)
            kr = _dot_nt(h_ref[rs, :], wkr)
            krope[rs, :] = _rope64(kr, cos_ref[rs, :], sin_ref[rs, :]).astype(BF16)
        vs[:, MLA_V:] = jnp.ones((seq, MLA_V), BF16)
        zeros = jnp.zeros((MLA_LORA, MLA_QK_PAD - MLA_QK), F32)
        wq_all = wq_ref[0]
        for hh in range(MLA_HEADS):
            w_head = wq_all[:, hh * MLA_QK:(hh + 1) * MLA_QK]
            wq_s[hh] = jnp.concatenate([w_head, zeros], axis=1).astype(BF16)

    scale = (MLA_QK ** -0.5) * math.log2(math.e)
    wq = wq_s[head]
    wkv = wkv_ref[0].astype(BF16)
    for i in range(seq // rows):
        rs = slice(i * rows, (i + 1) * rows)
        q = _dot(qlat[rs, :], wq)
        qs[rs, 0:MLA_NOPE] = (q[:, 0:MLA_NOPE] * scale).astype(BF16)
        qs[rs, MLA_NOPE:] = (_rope64(q[:, MLA_NOPE:], cos_ref[rs, :], sin_ref[rs, :]) * scale).astype(BF16)
        kv = _dot(kvlat[rs, :], wkv)
        ks[rs, 0:MLA_NOPE] = kv[:, 0:MLA_NOPE].astype(BF16)
        ks[rs, MLA_NOPE:] = krope[rs, :]
        vs[rs, 0:MLA_V] = kv[:, MLA_NOPE:].astype(BF16)

    tq = MLA_TQ
    ri = lax.broadcasted_iota(jnp.int32, (tq, tq), 0)
    ci = lax.broadcasted_iota(jnp.int32, (tq, tq), 1)
    visible = (ci // CHUNK) <= (ri // CHUNK)
    n_q = seq // tq

    def scores(i):
        q0, q1 = i * tq, (i + 1) * tq
        s_scr[i % 2, :, 0:q1] = _dot_nt(qs[q0:q1, :], ks[0:q1, :])

    def probs(i):
        q0, q1 = i * tq, (i + 1) * tq
        s_d = jnp.where(visible, s_scr[i % 2, :, q0:q1], -1e30)
        m = jnp.max(s_d, axis=-1, keepdims=True)
        if i > 0:
            m = jnp.maximum(m, jnp.max(s_scr[i % 2, :, 0:q0], axis=-1, keepdims=True))
            p_scr[i % 2, :, 0:q0] = jnp.exp2(s_scr[i % 2, :, 0:q0] - m).astype(BF16)
        p_scr[i % 2, :, q0:q1] = jnp.exp2(s_d - m).astype(BF16)

    def values(i):
        q0, q1 = i * tq, (i + 1) * tq
        acc = _dot(p_scr[i % 2, :, 0:q1], vs[0:q1, :])
        out = acc[:, 0:MLA_V] / acc[:, MLA_V:]
        o_ref[q0:q1, :] = (out * _silu(mg_ref[q0:q1, :].astype(F32))).astype(o_ref.dtype)

    scores(0)
    for i in range(n_q + 1):
        if i + 1 < n_q:
            scores(i + 1)
        if i < n_q:
            probs(i)
        if i > 0:
            values(i - 1)


def _mla_call(h, w_in_t, proj, w_uq, w_ukv, q_norm, kv_norm, cos_m, sin_m, layer, batch, seq):
    m, d = h.shape
    lora = MLA_LORA
    return pl.pallas_call(
        functools.partial(_mla_kernel, seq=seq),
        grid=(batch, MLA_HEADS),
        in_specs=[
            pl.BlockSpec((seq, d), lambda b, hd: (b, 0)),
            pl.BlockSpec((1, MLA_ROPE, d), lambda b, hd: (layer, SRC_MKR // MLA_ROPE, 0)),
            pl.BlockSpec((seq, lora), lambda b, hd: (b, COL_MQ // lora)),
            pl.BlockSpec((seq, lora), lambda b, hd: (b, COL_MKV // lora)),
            pl.BlockSpec((seq, MLA_V), lambda b, hd: (b, COL_MG // MLA_V + hd)),
            pl.BlockSpec((1, lora, MLA_HEADS * MLA_QK), lambda b, hd: (layer, 0, 0)),
            pl.BlockSpec((1, lora, MLA_NOPE + MLA_V), lambda b, hd: (layer, 0, hd)),
            _vec_spec(lora, layer, 2),
            _vec_spec(lora, layer, 2),
            pl.BlockSpec((seq, LANE), lambda b, hd: (b, 0)),
            pl.BlockSpec((seq, LANE), lambda b, hd: (b, 0)),
        ],
        out_specs=pl.BlockSpec((seq, MLA_V), lambda b, hd: (b, hd)),
        out_shape=jax.ShapeDtypeStruct((m, MLA_WIDTH), BF16),
        scratch_shapes=[
            pltpu.VMEM((seq, lora), BF16),
            pltpu.VMEM((seq, lora), BF16),
            pltpu.VMEM((seq, LANE), BF16),
            pltpu.VMEM((MLA_HEADS, lora, MLA_QK_PAD), BF16),
            pltpu.VMEM((seq, MLA_QK_PAD), BF16),
            pltpu.VMEM((seq, MLA_QK_PAD), BF16),
            pltpu.VMEM((seq, 2 * MLA_V), BF16),
            pltpu.VMEM((2, MLA_TQ, seq), F32),
            pltpu.VMEM((2, MLA_TQ, seq), BF16),
        ],
        compiler_params=_params("arbitrary", "arbitrary"),
        name="mla",
    )(h, w_in_t, proj, proj, proj, w_uq, w_ukv, _layer_vec(q_norm), _layer_vec(kv_norm), cos_m, sin_m)


def _merge_kernel(yr_ref, yl_ref, ym_ref, l0_ref, l1_ref, l2_ref, x_ref, res_ref, g_ref,
                  wb_ref, wo_ref, *rest, tiles_per_batch, emit_next):
    b = pl.program_id(0) // tiles_per_batch
    merged = None
    for i, (y_ref, l_ref) in enumerate(((yr_ref, l0_ref), (yl_ref, l1_ref), (ym_ref, l2_ref))):
        z = _dot(y_ref[...], wb_ref[0, i * 1024:(i + 1) * 1024, :])
        z = z * jax.nn.sigmoid(l_ref[...].astype(F32))
        merged = z if merged is None else merged + z
    y = _dot(merged.astype(BF16), wo_ref[0])
    x_new = x_ref[...] + (1.0 + res_ref[0, pl.ds(b, 1), :]) * _rms(y, g_ref[0])
    if emit_next:
        gn_ref, shift_ref, scale_ref, o_ref, h_ref = rest
        h_ref[...] = _modulate(x_new, gn_ref[0], shift_ref, scale_ref, b).astype(h_ref.dtype)
    else:
        (o_ref,) = rest
    o_ref[...] = x_new


def _merge_call(y_ret, y_lru, y_mla, proj, x2, mod_all, norm_post, norm_pre, wb, wo, layer, seq):
    m, d = x2.shape
    tm = MERGE_TM
    depth, mod_rows = mod_all.shape[:2]
    emit_next = layer + 1 < depth

    def rows(width, c=0):
        return pl.BlockSpec((tm, width), lambda i, c=c: (i, c))

    def resident(shape):
        return pl.BlockSpec((1,) + shape[1:], lambda i: (layer, 0, 0), pipeline_mode=pl.Buffered(1))

    def mod_part(lyr, part):
        return pl.BlockSpec((1, mod_rows, d), lambda i: (lyr, 0, part))

    merge_col = COL_MERGE // d
    in_specs = [rows(1024), rows(1024), rows(1024),
                rows(d, merge_col), rows(d, merge_col + 1), rows(d, merge_col + 2),
                rows(d), mod_part(layer, 2), _vec_spec(d, layer, 1),
                resident(wb.shape), resident(wo.shape)]
    args = [y_ret, y_lru, y_mla, proj, proj, proj, x2, mod_all, _layer_vec(norm_post), wb, wo]
    out_specs = rows(d)
    out_shape = jax.ShapeDtypeStruct((m, d), F32)
    if emit_next:
        in_specs += [_vec_spec(d, layer + 1, 1), mod_part(layer + 1, 0), mod_part(layer + 1, 1)]
        args += [_layer_vec(norm_pre), mod_all, mod_all]
        out_specs = (out_specs, rows(d))
        out_shape = (out_shape, jax.ShapeDtypeStruct((m, d), BF16))
    return pl.pallas_call(
        functools.partial(_merge_kernel, tiles_per_batch=seq // tm, emit_next=emit_next),
        grid=(m // tm,),
        in_specs=in_specs,
        out_specs=out_specs,
        out_shape=out_shape,
        compiler_params=_params("arbitrary"),
        name="merge_out",
    )(*args)


def _rope_tables(positions, dim):
    inv_freq = ROPE_BASE ** (-jnp.arange(0, dim, 2, dtype=F32) / dim)
    ang = positions.astype(F32)[:, :, None] * inv_freq
    return jnp.cos(ang), jnp.sin(ang)


def kernel(x, c, positions, ada_w, ada_b, norm_pre, norm_post, w_in, ret_gn, lru_conv_w, lru_conv_b,
           lru_wa, lru_ba, lru_wx, lru_bx, lru_lambda, mla_q_norm, mla_w_uq, mla_kv_norm, mla_w_ukv,
           w_branch, w_out):
    batch, seq, d = x.shape
    depth = w_in.shape[0]
    m = batch * seq
    assert d == D_MODEL and w_in.shape[2] == SRC_WIDTH and batch <= SUBLANE
    assert seq % PROJ_TM == 0 and seq % RET_T == 0 and seq % LRU_T == 0 and seq % MLA_TQ == 0
    assert seq % MERGE_TM == 0 and seq % NORM_TM == 0

    cos_r, sin_r = _rope_tables(positions, RET_DIM)
    cos_r = jnp.concatenate([cos_r, cos_r], axis=-1).reshape(m, RET_DIM)
    sin_r = jnp.concatenate([-sin_r, sin_r], axis=-1).reshape(m, RET_DIM)
    cos_m, sin_m = _rope_tables(positions, MLA_ROPE)
    tail = jnp.zeros((batch, seq, LANE - MLA_ROPE), F32)
    cos_m = jnp.concatenate([cos_m, cos_m, tail], axis=-1).reshape(m, LANE)
    sin_m = jnp.concatenate([-sin_m, sin_m, tail], axis=-1).reshape(m, LANE)

    c_pad = jnp.pad(c, ((0, SUBLANE - batch), (0, 0)))
    mod_all = _ada_call(c_pad, ada_w, ada_b)
    wb_all = w_branch.astype(BF16)
    wo_all = w_out.astype(BF16)

    w_in_t = jnp.swapaxes(w_in, 1, 2)

    x2 = x.reshape(m, d)
    h = _prenorm_call(x2, mod_all, norm_pre, 0, seq)
    for l in range(depth):
        proj = _inproj_call(h, w_in_t, l)
        y_ret = _ret_call(proj, cos_r, sin_r, ret_gn, l, batch, seq)
        y_lru = _lru_call(proj, lru_conv_w, lru_conv_b, lru_wa, lru_ba, lru_wx, lru_bx, lru_lambda,
                          l, batch, seq)
        y_mla = _mla_call(h, w_in_t, proj, mla_w_uq, mla_w_ukv, mla_q_norm, mla_kv_norm, cos_m, sin_m,
                          l, batch, seq)
        out = _merge_call(y_ret, y_lru, y_mla, proj, x2, mod_all, norm_post, norm_pre, wb_all, wo_all,
                          l, seq)
        if l + 1 < depth:
            x2, h = out
        else:
            x2 = out
    return x2.reshape(batch, seq, d)
```

```python
import functools
import math

import jax
import jax.numpy as jnp
from jax import lax
from jax.experimental import pallas as pl
from jax.experimental.pallas import tpu as pltpu

F32 = jnp.float32
BF16 = jnp.bfloat16

D_MODEL = 2048
CHUNK = 64
ROPE_BASE = 10000.0
NORM_EPS = 1e-6

RET_HEADS = 8
RET_DIM = 128
RET_WIDTH = RET_HEADS * RET_DIM

LRU_WIDTH = 1024
LRU_BLOCKS = 8
LRU_BLOCK_DIM = 128
CONV_WIDTH = 4
LRU_C = 8.0

MLA_HEADS = 8
MLA_NOPE = 128
MLA_ROPE = 64
MLA_V = 128
MLA_LORA = 512
MLA_WIDTH = MLA_HEADS * MLA_V
MLA_QK = MLA_NOPE + MLA_ROPE
MLA_QK_PAD = 256

LANE = 128
SUBLANE = 8
VMEM_LIMIT = 58 * 1024 * 1024

SRC_MKR = 4 * 1024 + 2 * 1024 + 2 * MLA_LORA
SRC_WIDTH = SRC_MKR + MLA_ROPE + MLA_WIDTH + 3 * D_MODEL
COL_RQ = 0
COL_RK = 1024
COL_RV = 2048
COL_RG = 3072
COL_LX = 4096
COL_LG = 5120
COL_MQ = 6144
COL_MKV = COL_MQ + MLA_LORA
COL_MG = SRC_MKR
COL_MERGE = COL_MG + MLA_WIDTH
PROJ_WIDTH = COL_MERGE + 3 * D_MODEL
PROJ_TN = 1024
PROJ_TM = 2048
ALIGNED_TILES = SRC_MKR // PROJ_TN
CAST_ROWS = 64
assert SRC_MKR % PROJ_TN == 0 and PROJ_WIDTH % PROJ_TN == 0 and COL_MERGE % D_MODEL == 0

NORM_TM = 512
RET_T = 256
LRU_T = 512
LRU_CLEN = LRU_T // 8
MLA_TQ = 256
MERGE_TM = 256
ADA_TK = 256


def _silu(v):
    return v * jax.nn.sigmoid(v)


def _dot(a, b):
    return jnp.dot(a, b, preferred_element_type=F32)


def _dot_nt(a, b):
    return lax.dot_general(a, b, (((1,), (1,)), ((), ())), preferred_element_type=F32)


def _dot_tn(a, b):
    return lax.dot_general(a, b, (((0,), (0,)), ((), ())), preferred_element_type=F32)


def _rms(x, gain):
    return x * lax.rsqrt(jnp.mean(x * x, axis=-1, keepdims=True) + NORM_EPS) * gain


def _params(*sem):
    return pltpu.CompilerParams(dimension_semantics=sem, vmem_limit_bytes=VMEM_LIMIT)


def _layer_vec(arr):
    return arr.reshape(arr.shape[0], 1, arr.shape[1])


def _vec_spec(width, layer, ngrid):
    if ngrid == 1:
        return pl.BlockSpec((1, 1, width), lambda i: (layer, 0, 0))
    return pl.BlockSpec((1, 1, width), lambda i, j: (layer, 0, 0))


def _ada_kernel(c_ref, w_ref, b_ref, o_ref):
    @pl.when(pl.program_id(1) == 0)
    def _():
        o_ref[0] = jnp.broadcast_to(b_ref[0], o_ref.shape[1:])

    c_act = _silu(c_ref[...]).astype(BF16)
    o_ref[0] += _dot(c_act, w_ref[0].astype(BF16))


def _ada_call(c_pad, ada_w, ada_b):
    depth, d, n = ada_w.shape
    rows = c_pad.shape[0]
    return pl.pallas_call(
        _ada_kernel,
        grid=(depth, d // ADA_TK),
        in_specs=[
            pl.BlockSpec((rows, ADA_TK), lambda l, k: (0, k)),
            pl.BlockSpec((1, ADA_TK, n), lambda l, k: (l, k, 0)),
            pl.BlockSpec((1, 1, n), lambda l, k: (l, 0, 0)),
        ],
        out_specs=pl.BlockSpec((1, rows, n), lambda l, k: (l, 0, 0)),
        out_shape=jax.ShapeDtypeStruct((depth, rows, n), F32),
        compiler_params=_params("arbitrary", "arbitrary"),
        name="ada_mod",
    )(c_pad, ada_w, _layer_vec(ada_b))


def _modulate(x, gain, shift_ref, scale_ref, b):
    shift = shift_ref[0, pl.ds(b, 1), :]
    scale = scale_ref[0, pl.ds(b, 1), :]
    return _rms(x, gain) * (1.0 + scale) + shift


def _prenorm_kernel(x_ref, g_ref, shift_ref, scale_ref, o_ref, *, tiles_per_batch):
    b = pl.program_id(0) // tiles_per_batch
    o_ref[...] = _modulate(x_ref[...], g_ref[0], shift_ref, scale_ref, b).astype(o_ref.dtype)


def _prenorm_call(x2, mod_all, norm_pre, layer, seq):
    m, d = x2.shape
    rows = mod_all.shape[1]
    return pl.pallas_call(
        functools.partial(_prenorm_kernel, tiles_per_batch=seq // NORM_TM),
        grid=(m // NORM_TM,),
        in_specs=[
            pl.BlockSpec((NORM_TM, d), lambda i: (i, 0)),
            _vec_spec(d, layer, 1),
            pl.BlockSpec((1, rows, d), lambda i: (layer, 0, 0)),
            pl.BlockSpec((1, rows, d), lambda i: (layer, 0, 1)),
        ],
        out_specs=pl.BlockSpec((NORM_TM, d), lambda i: (i, 0)),
        out_shape=jax.ShapeDtypeStruct((m, d), BF16),
        compiler_params=_params("arbitrary"),
        name="pre_norm",
    )(x2, _layer_vec(norm_pre), mod_all, mod_all)


def _inproj_kernel(h_ref, w_ref, wx_ref, o_ref, wb_ref):
    j = pl.program_id(0)
    first_row_tile = pl.program_id(1) == 0
    n_chunks = PROJ_TN // CAST_ROWS

    @pl.when(first_row_tile & (j < ALIGNED_TILES))
    def _():
        def body(i, carry):
            r0 = pl.multiple_of(i * CAST_ROWS, CAST_ROWS)
            wb_ref[pl.ds(r0, CAST_ROWS), :] = w_ref[0, pl.ds(r0, CAST_ROWS), :].astype(BF16)
            return carry

        lax.fori_loop(0, n_chunks, body, 0)

    @pl.when(first_row_tile & (j >= ALIGNED_TILES))
    def _():
        def body(i, carry):
            r0 = pl.multiple_of(i * CAST_ROWS, CAST_ROWS)
            wb_ref[pl.ds(r0, CAST_ROWS), :] = w_ref[0, pl.ds(r0 + MLA_ROPE, CAST_ROWS), :].astype(BF16)
            return carry

        lax.fori_loop(0, n_chunks - 1, body, 0)
        wb_ref[PROJ_TN - MLA_ROPE:, :] = wx_ref[0].astype(BF16)

    o_ref[...] = _dot_nt(h_ref[...], wb_ref[...]).astype(o_ref.dtype)


def _inproj_call(h, w_in_t, layer):
    m, d = h.shape
    per_tile = PROJ_TN // MLA_ROPE
    last_block = SRC_WIDTH // MLA_ROPE - 1
    return pl.pallas_call(
        _inproj_kernel,
        grid=(PROJ_WIDTH // PROJ_TN, m // PROJ_TM),
        in_specs=[
            pl.BlockSpec((PROJ_TM, d), lambda j, i: (i, 0)),
            pl.BlockSpec((1, PROJ_TN, d), lambda j, i: (layer, j, 0)),
            pl.BlockSpec((1, MLA_ROPE, d),
                         lambda j, i: (layer, jnp.minimum((j + 1) * per_tile, last_block), 0)),
        ],
        out_specs=pl.BlockSpec((PROJ_TM, PROJ_TN), lambda j, i: (i, j)),
        out_shape=jax.ShapeDtypeStruct((m, PROJ_WIDTH), BF16),
        scratch_shapes=[pltpu.VMEM((PROJ_TN, d), BF16)],
        compiler_params=_params("arbitrary", "arbitrary"),
        name="in_proj",
    )(h, w_in_t, w_in_t)


_LOG_GAMMA = tuple(math.log1p(-(2.0 ** (-5.0 - h))) for h in range(RET_HEADS))


def _ret_kernel(q_ref, k_ref, v_ref, g_ref, cos_ref, sin_ref, gn_ref, wb_ref, wo_ref,
                o_ref, wb_out, wo_out, state_ref, dmat_ref, dq_ref, dk_ref):
    t_blk = RET_T
    wb_out[...] = wb_ref[0].astype(BF16)
    wo_out[...] = wo_ref[0].astype(BF16)

    @pl.when((pl.program_id(0) == 0) & (pl.program_id(1) == 0))
    def _():
        ri = lax.broadcasted_iota(jnp.int32, (t_blk, t_blk), 0)
        ci = lax.broadcasted_iota(jnp.int32, (t_blk, t_blk), 1)
        dist = jnp.abs(ri - ci).astype(F32)
        visible = (ci // CHUNK) <= (ri // CHUNK)
        row = lax.broadcasted_iota(jnp.int32, (t_blk, RET_DIM), 0).astype(F32)
        for h in range(RET_HEADS):
            lg = _LOG_GAMMA[h]
            dmat_ref[h] = jnp.where(visible, jnp.exp(lg * dist), 0.0)
            dq_ref[h] = jnp.exp(lg * (row + 1.0))
            dk_ref[h] = jnp.exp(lg * ((t_blk - 1.0) - row))

    @pl.when(pl.program_id(1) == 0)
    def _():
        state_ref[...] = jnp.zeros_like(state_ref)

    cos = cos_ref[...]
    sin = sin_ref[...]
    for h in range(RET_HEADS):
        sl = slice(h * RET_DIM, (h + 1) * RET_DIM)
        q = q_ref[:, sl].astype(F32)
        k = k_ref[:, sl].astype(F32)
        q = (q * cos + pltpu.roll(q, RET_DIM // 2, 1) * sin) * (RET_DIM ** -0.5)
        k = k * cos + pltpu.roll(k, RET_DIM // 2, 1) * sin
        v = v_ref[:, sl]
        scores = _dot_nt(q.astype(BF16), k.astype(BF16)) * dmat_ref[h]
        o = _dot(scores.astype(BF16), v)
        state = state_ref[h]
        o = o + _dot((q * dq_ref[h]).astype(BF16), state.astype(BF16))
        k_dec = (k * dk_ref[h]).astype(BF16)
        state_ref[h] = state * math.exp(_LOG_GAMMA[h] * t_blk) + _dot_tn(k_dec, v)
        mean = jnp.mean(o, axis=-1, keepdims=True)
        cen = o - mean
        var = jnp.mean(cen * cen, axis=-1, keepdims=True)
        normed = cen * lax.rsqrt(var + NORM_EPS) * gn_ref[0, :, sl]
        o_ref[:, sl] = (normed * _silu(g_ref[:, sl].astype(F32))).astype(o_ref.dtype)


def _ret_call(proj, cos_r, sin_r, ret_gn, w_branch, w_out, layer, batch, seq):
    m = proj.shape[0]
    nt = seq // RET_T
    w = RET_WIDTH
    steps = batch * nt
    wb_rows, wo_rows = w_branch.shape[1] // steps, w_out.shape[1] // steps
    bf16_rows = 2 * SUBLANE
    assert wb_rows * steps == w_branch.shape[1] and wb_rows % bf16_rows == 0
    assert wo_rows * steps == w_out.shape[1] and wo_rows % bf16_rows == 0
    d = w_branch.shape[2]

    def col(c):
        return pl.BlockSpec((RET_T, w), lambda b, t, c=c: (b * nt + t, c // w))

    tab = pl.BlockSpec((RET_T, RET_DIM), lambda b, t: (b * nt + t, 0))
    return pl.pallas_call(
        _ret_kernel,
        grid=(batch, nt),
        in_specs=[col(COL_RQ), col(COL_RK), col(COL_RV), col(COL_RG), tab, tab,
                  _vec_spec(w, layer, 2),
                  pl.BlockSpec((1, wb_rows, d), lambda b, t: (layer, b * nt + t, 0)),
                  pl.BlockSpec((1, wo_rows, d), lambda b, t: (layer, b * nt + t, 0))],
        out_specs=(pl.BlockSpec((RET_T, w), lambda b, t: (b * nt + t, 0)),
                   pl.BlockSpec((wb_rows, d), lambda b, t: (b * nt + t, 0)),
                   pl.BlockSpec((wo_rows, d), lambda b, t: (b * nt + t, 0))),
        out_shape=(jax.ShapeDtypeStruct((m, w), BF16),
                   jax.ShapeDtypeStruct(w_branch.shape[1:], BF16),
                   jax.ShapeDtypeStruct(w_out.shape[1:], BF16)),
        scratch_shapes=[
            pltpu.VMEM((RET_HEADS, RET_DIM, RET_DIM), F32),
            pltpu.VMEM((RET_HEADS, RET_T, RET_T), F32),
            pltpu.VMEM((RET_HEADS, RET_T, RET_DIM), F32),
            pltpu.VMEM((RET_HEADS, RET_T, RET_DIM), F32),
        ],
        compiler_params=_params("arbitrary", "arbitrary"),
        name="retention",
    )(proj, proj, proj, proj, cos_r, sin_r, _layer_vec(ret_gn), w_branch, w_out)


def _sublane_scan(a, b, row):
    for s in (1, 2, 4):
        keep = row >= s
        a_prev = jnp.where(keep, pltpu.roll(a, s, 0), 1.0)
        b_prev = jnp.where(keep, pltpu.roll(b, s, 0), 0.0)
        b = a * b_prev + b
        a = a * a_prev
    return a, b


def _lru_kernel(x_ref, g_ref, cw_ref, cb_ref, wa_ref, ba_ref, wx_ref, bx_ref, lam_ref, o_ref,
                perm, perm_t, halo, hcar, a_s, b_s):
    t_blk = LRU_T
    clen = LRU_CLEN
    taps = CONV_WIDTH - 1
    row = lax.broadcasted_iota(jnp.int32, (SUBLANE, LRU_WIDTH), 0)

    @pl.when((pl.program_id(0) == 0) & (pl.program_id(1) == 0))
    def _():
        r = lax.broadcasted_iota(jnp.int32, (t_blk, t_blk), 0)
        c = lax.broadcasted_iota(jnp.int32, (t_blk, t_blk), 1)
        perm[...] = jnp.where(c == (r % SUBLANE) * clen + r // SUBLANE, 1.0, 0.0).astype(BF16)
        perm_t[...] = jnp.where(c == (r % clen) * SUBLANE + r // clen, 1.0, 0.0).astype(BF16)

    @pl.when(pl.program_id(1) == 0)
    def _():
        halo[...] = jnp.zeros_like(halo)
        hcar[...] = jnp.zeros_like(hcar)

    xp = _dot(perm[...], x_ref[...])
    pieces = []
    for k in range(taps):
        cur = xp[t_blk - (taps - k) * SUBLANE:t_blk - (taps - k - 1) * SUBLANE, :]
        prev = halo[k * SUBLANE:(k + 1) * SUBLANE, :]
        pieces.append(jnp.where(row == 0, pltpu.roll(prev, 1, 0), pltpu.roll(cur, 1, 0)))
    halo[...] = xp[t_blk - taps * SUBLANE:, :]
    xext = jnp.concatenate(pieces + [xp], axis=0)
    xc = cb_ref[0]
    for w in range(CONV_WIDTH):
        xc = xc + cw_ref[0, w:w + 1, :] * xext[w * SUBLANE:w * SUBLANE + t_blk, :]

    neg_lam = -lam_ref[0]
    softplus = jnp.maximum(neg_lam, 0.0) + jnp.log1p(jnp.exp(-jnp.abs(neg_lam)))
    neg_half_rate = (-0.5 * LRU_C) * softplus
    for n in range(LRU_BLOCKS):
        sl = slice(n * LRU_BLOCK_DIM, (n + 1) * LRU_BLOCK_DIM)
        xn = xc[:, sl]
        xn_b = xn.astype(BF16)
        t_r = jnp.tanh(0.5 * (_dot(xn_b, wa_ref[0, n].astype(BF16)) + ba_ref[0, :, sl]))
        t_i = jnp.tanh(0.5 * (_dot(xn_b, wx_ref[0, n].astype(BF16)) + bx_ref[0, :, sl]))
        half_rate = neg_half_rate[:, sl]
        log_a = half_rate * t_r + half_rate
        a = jnp.exp(log_a)
        a_s[:, sl] = a
        var = jnp.tanh(-log_a) * (a * a + 1.0)
        std = jnp.where(var > 0.0, var * lax.rsqrt(var), 0.0)
        half_x = 0.5 * xn
        b_s[:, sl] = std * (half_x * t_i + half_x)

    def body(j, carry):
        h, prod = carry
        r0 = pl.multiple_of(j * SUBLANE, SUBLANE)
        a = a_s[pl.ds(r0, SUBLANE), :]
        h = a * h + b_s[pl.ds(r0, SUBLANE), :]
        prod = a * prod
        b_s[pl.ds(r0, SUBLANE), :] = h
        a_s[pl.ds(r0, SUBLANE), :] = prod
        return h, prod

    zeros = jnp.zeros((SUBLANE, LRU_WIDTH), F32)
    h_end, a_end = lax.fori_loop(0, clen, body, (zeros, zeros + 1.0), unroll=4)

    a_inc, h_inc = _sublane_scan(a_end, h_end, row)
    h_prev_tile = hcar[...]
    h_chunk_end = a_inc * h_prev_tile + h_inc
    h_chunk_start = jnp.where(row == 0, h_prev_tile, pltpu.roll(h_chunk_end, 1, 0))
    hcar[...] = jnp.broadcast_to(h_chunk_end[SUBLANE - 1:SUBLANE, :], (SUBLANE, LRU_WIDTH))

    h_true = b_s[...] + a_s[...] * pltpu.repeat(h_chunk_start, clen, axis=0)
    gp = _dot(perm[...], g_ref[...])
    out_p = (h_true * _silu(gp)).astype(BF16)
    o_ref[...] = _dot(perm_t[...], out_p).astype(o_ref.dtype)


def _lru_call(proj, cw, cb, wa, ba, wx, bx, lam, layer, batch, seq):
    m = proj.shape[0]
    nt = seq // LRU_T
    w = LRU_WIDTH

    def col(c):
        return pl.BlockSpec((LRU_T, w), lambda b, t, c=c: (b * nt + t, c // w))

    def vec():
        return _vec_spec(w, layer, 2)

    def blk():
        return pl.BlockSpec((1, LRU_BLOCKS, LRU_BLOCK_DIM, LRU_BLOCK_DIM), lambda b, t: (layer, 0, 0, 0))

    return pl.pallas_call(
        _lru_kernel,
        grid=(batch, nt),
        in_specs=[col(COL_LX), col(COL_LG),
                  pl.BlockSpec((1, CONV_WIDTH, w), lambda b, t: (layer, 0, 0)), vec(),
                  blk(), vec(), blk(), vec(), vec()],
        out_specs=pl.BlockSpec((LRU_T, w), lambda b, t: (b * nt + t, 0)),
        out_shape=jax.ShapeDtypeStruct((m, w), BF16),
        scratch_shapes=[
            pltpu.VMEM((LRU_T, LRU_T), BF16),
            pltpu.VMEM((LRU_T, LRU_T), BF16),
            pltpu.VMEM(((CONV_WIDTH - 1) * SUBLANE, w), F32),
            pltpu.VMEM((SUBLANE, w), F32),
            pltpu.VMEM((LRU_T, w), F32),
            pltpu.VMEM((LRU_T, w), F32),
        ],
        compiler_params=_params("arbitrary", "arbitrary"),
        name="rg_lru",
    )(proj, proj, cw, _layer_vec(cb), wa, _layer_vec(ba), wx, _layer_vec(bx), _layer_vec(lam))


def _rope64(x, cos_t, sin_t):
    swapped = pltpu.roll(x, MLA_ROPE // 2, 1) + pltpu.roll(x, LANE - MLA_ROPE // 2, 1)
    return x * cos_t + swapped * sin_t


def _mla_kernel(h_ref, wkr_ref, mq_ref, mkv_ref, mg_ref, wq_ref, wkv_ref, qn_ref, kvn_ref,
                cos_ref, sin_ref, o_ref, qlat, kvlat, krope, wq_s, qs, ks, vs, s_scr, p_scr, *, seq):
    rows = 256
    head = pl.program_id(1)

    @pl.when(head == 0)
    def _():
        wkr = jnp.concatenate([wkr_ref[0], jnp.zeros((LANE - MLA_ROPE, D_MODEL), F32)], axis=0).astype(BF16)
        for i in range(seq // rows):
            rs = slice(i * rows, (i + 1) * rows)
            qlat[rs, :] = _rms(mq_ref[rs, :].astype(F32), qn_ref[0]).astype(BF16)
            kvlat[rs, :] = _rms(mkv_ref[rs, :].astype(F32), kvn_ref[0]).astype(BF16)
            kr = _dot_nt(h_ref[rs, :], wkr)
            krope[rs, :] = _rope64(kr, cos_ref[rs, :], sin_ref[rs, :]).astype(BF16)
        vs[:, MLA_V:] = jnp.ones((seq, MLA_V), BF16)
        zeros = jnp.zeros((MLA_LORA, MLA_QK_PAD - MLA_QK), F32)
        wq_all = wq_ref[0]
        for hh in range(MLA_HEADS):
            w_head = wq_all[:, hh * MLA_QK:(hh + 1) * MLA_QK]
            wq_s[hh] = jnp.concatenate([w_head, zeros], axis=1).astype(BF16)

    scale = (MLA_QK ** -0.5) * math.log2(math.e)
    wq = wq_s[head]
    wkv = wkv_ref[0].astype(BF16)
    for i in range(seq // rows):
        rs = slice(i * rows, (i + 1) * rows)
        q = _dot(qlat[rs, :], wq)
        qs[rs, 0:MLA_NOPE] = (q[:, 0:MLA_NOPE] * scale).astype(BF16)
        qs[rs, MLA_NOPE:] = (_rope64(q[:, MLA_NOPE:], cos_ref[rs, :], sin_ref[rs, :]) * scale).astype(BF16)
        kv = _dot(kvlat[rs, :], wkv)
        ks[rs, 0:MLA_NOPE] = kv[:, 0:MLA_NOPE].astype(BF16)
        ks[rs, MLA_NOPE:] = krope[rs, :]
        vs[rs, 0:MLA_V] = kv[:, MLA_NOPE:].astype(BF16)

    tq = MLA_TQ
    ri = lax.broadcasted_iota(jnp.int32, (tq, tq), 0)
    ci = lax.broadcasted_iota(jnp.int32, (tq, tq), 1)
    visible = (ci // CHUNK) <= (ri // CHUNK)
    n_q = seq // tq

    def scores(i):
        q0, q1 = i * tq, (i + 1) * tq
        s_scr[i % 2, :, 0:q1] = _dot_nt(qs[q0:q1, :], ks[0:q1, :])

    def probs(i):
        q0, q1 = i * tq, (i + 1) * tq
        s_d = jnp.where(visible, s_scr[i % 2, :, q0:q1], -1e30)
        m = jnp.max(s_d, axis=-1, keepdims=True)
        if i > 0:
            m = jnp.maximum(m, jnp.max(s_scr[i % 2, :, 0:q0], axis=-1, keepdims=True))
            p_scr[i % 2, :, 0:q0] = jnp.exp2(s_scr[i % 2, :, 0:q0] - m).astype(BF16)
        p_scr[i % 2, :, q0:q1] = jnp.exp2(s_d - m).astype(BF16)

    def values(i):
        q0, q1 = i * tq, (i + 1) * tq
        acc = _dot(p_scr[i % 2, :, 0:q1], vs[0:q1, :])
        out = acc[:, 0:MLA_V] / acc[:, MLA_V:]
        o_ref[q0:q1, :] = (out * _silu(mg_ref[q0:q1, :].astype(F32))).astype(o_ref.dtype)

    scores(0)
    for i in range(n_q + 1):
        if i + 1 < n_q:
            scores(i + 1)
        if i < n_q:
            probs(i)
        if i > 0:
            values(i - 1)


def _mla_call(h, w_in_t, proj, w_uq, w_ukv, q_norm, kv_norm, cos_m, sin_m, layer, batch, seq):
    m, d = h.shape
    lora = MLA_LORA
    return pl.pallas_call(
        functools.partial(_mla_kernel, seq=seq),
        grid=(batch, MLA_HEADS),
        in_specs=[
            pl.BlockSpec((seq, d), lambda b, hd: (b, 0)),
            pl.BlockSpec((1, MLA_ROPE, d), lambda b, hd: (layer, SRC_MKR // MLA_ROPE, 0)),
            pl.BlockSpec((seq, lora), lambda b, hd: (b, COL_MQ // lora)),
            pl.BlockSpec((seq, lora), lambda b, hd: (b, COL_MKV // lora)),
            pl.BlockSpec((seq, MLA_V), lambda b, hd: (b, COL_MG // MLA_V + hd)),
            pl.BlockSpec((1, lora, MLA_HEADS * MLA_QK), lambda b, hd: (layer, 0, 0)),
            pl.BlockSpec((1, lora, MLA_NOPE + MLA_V), lambda b, hd: (layer, 0, hd)),
            _vec_spec(lora, layer, 2),
            _vec_spec(lora, layer, 2),
            pl.BlockSpec((seq, LANE), lambda b, hd: (b, 0)),
            pl.BlockSpec((seq, LANE), lambda b, hd: (b, 0)),
        ],
        out_specs=pl.BlockSpec((seq, MLA_V), lambda b, hd: (b, hd)),
        out_shape=jax.ShapeDtypeStruct((m, MLA_WIDTH), BF16),
        scratch_shapes=[
            pltpu.VMEM((seq, lora), BF16),
            pltpu.VMEM((seq, lora), BF16),
            pltpu.VMEM((seq, LANE), BF16),
            pltpu.VMEM((MLA_HEADS, lora, MLA_QK_PAD), BF16),
            pltpu.VMEM((seq, MLA_QK_PAD), BF16),
            pltpu.VMEM((seq, MLA_QK_PAD), BF16),
            pltpu.VMEM((seq, 2 * MLA_V), BF16),
            pltpu.VMEM((2, MLA_TQ, seq), F32),
            pltpu.VMEM((2, MLA_TQ, seq), BF16),
        ],
        compiler_params=_params("arbitrary", "arbitrary"),
        name="mla",
    )(h, w_in_t, proj, proj, proj, w_uq, w_ukv, _layer_vec(q_norm), _layer_vec(kv_norm), cos_m, sin_m)


def _merge_kernel(yr_ref, yl_ref, ym_ref, l0_ref, l1_ref, l2_ref, x_ref, res_ref, g_ref,
                  wb_ref, wo_ref, *rest, tiles_per_batch, emit_next):
    b = pl.program_id(0) // tiles_per_batch
    merged = None
    for i, (y_ref, l_ref) in enumerate(((yr_ref, l0_ref), (yl_ref, l1_ref), (ym_ref, l2_ref))):
        z = _dot(y_ref[...], wb_ref[i * 1024:(i + 1) * 1024, :])
        z = z * jax.nn.sigmoid(l_ref[...].astype(F32))
        merged = z if merged is None else merged + z
    y = _dot(merged.astype(BF16), wo_ref[...])
    x_new = x_ref[...] + (1.0 + res_ref[0, pl.ds(b, 1), :]) * _rms(y, g_ref[0])
    if emit_next:
        gn_ref, shift_ref, scale_ref, o_ref, h_ref = rest
        h_ref[...] = _modulate(x_new, gn_ref[0], shift_ref, scale_ref, b).astype(h_ref.dtype)
    else:
        (o_ref,) = rest
    o_ref[...] = x_new


def _merge_call(y_ret, y_lru, y_mla, proj, x2, mod_all, norm_post, norm_pre, wb, wo, layer, seq):
    m, d = x2.shape
    tm = MERGE_TM
    depth, mod_rows = mod_all.shape[:2]
    emit_next = layer + 1 < depth

    def rows(width, c=0):
        return pl.BlockSpec((tm, width), lambda i, c=c: (i, c))

    def resident(shape):
        return pl.BlockSpec(shape, lambda i: (0, 0), pipeline_mode=pl.Buffered(1))

    def mod_part(lyr, part):
        return pl.BlockSpec((1, mod_rows, d), lambda i: (lyr, 0, part))

    merge_col = COL_MERGE // d
    in_specs = [rows(1024), rows(1024), rows(1024),
                rows(d, merge_col), rows(d, merge_col + 1), rows(d, merge_col + 2),
                rows(d), mod_part(layer, 2), _vec_spec(d, layer, 1),
                resident(wb.shape), resident(wo.shape)]
    args = [y_ret, y_lru, y_mla, proj, proj, proj, x2, mod_all, _layer_vec(norm_post), wb, wo]
    out_specs = rows(d)
    out_shape = jax.ShapeDtypeStruct((m, d), F32)
    if emit_next:
        in_specs += [_vec_spec(d, layer + 1, 1), mod_part(layer + 1, 0), mod_part(layer + 1, 1)]
        args += [_layer_vec(norm_pre), mod_all, mod_all]
        out_specs = (out_specs, rows(d))
        out_shape = (out_shape, jax.ShapeDtypeStruct((m, d), BF16))
    return pl.pallas_call(
        functools.partial(_merge_kernel, tiles_per_batch=seq // tm, emit_next=emit_next),
        grid=(m // tm,),
        in_specs=in_specs,
        out_specs=out_specs,
        out_shape=out_shape,
        compiler_params=_params("arbitrary"),
        name="merge_out",
    )(*args)


def _rope_tables(positions, dim):
    inv_freq = ROPE_BASE ** (-jnp.arange(0, dim, 2, dtype=F32) / dim)
    ang = positions.astype(F32)[:, :, None] * inv_freq
    return jnp.cos(ang), jnp.sin(ang)


def kernel(x, c, positions, ada_w, ada_b, norm_pre, norm_post, w_in, ret_gn, lru_conv_w, lru_conv_b,
           lru_wa, lru_ba, lru_wx, lru_bx, lru_lambda, mla_q_norm, mla_w_uq, mla_kv_norm, mla_w_ukv,
           w_branch, w_out):
    batch, seq, d = x.shape
    depth = w_in.shape[0]
    m = batch * seq
    assert d == D_MODEL and w_in.shape[2] == SRC_WIDTH and batch <= SUBLANE
    assert seq % PROJ_TM == 0 and seq % RET_T == 0 and seq % LRU_T == 0 and seq % MLA_TQ == 0
    assert seq % MERGE_TM == 0 and seq % NORM_TM == 0

    cos_r, sin_r = _rope_tables(positions, RET_DIM)
    cos_r = jnp.concatenate([cos_r, cos_r], axis=-1).reshape(m, RET_DIM)
    sin_r = jnp.concatenate([-sin_r, sin_r], axis=-1).reshape(m, RET_DIM)
    cos_m, sin_m = _rope_tables(positions, MLA_ROPE)
    tail = jnp.zeros((batch, seq, LANE - MLA_ROPE), F32)
    cos_m = jnp.concatenate([cos_m, cos_m, tail], axis=-1).reshape(m, LANE)
    sin_m = jnp.concatenate([-sin_m, sin_m, tail], axis=-1).reshape(m, LANE)

    c_pad = jnp.pad(c, ((0, SUBLANE - batch), (0, 0)))
    mod_all = _ada_call(c_pad, ada_w, ada_b)
    w_in_t = jnp.swapaxes(w_in, 1, 2)

    x2 = x.reshape(m, d)
    h = _prenorm_call(x2, mod_all, norm_pre, 0, seq)
    for l in range(depth):
        proj = _inproj_call(h, w_in_t, l)
        y_ret, wb, wo = _ret_call(proj, cos_r, sin_r, ret_gn, w_branch, w_out, l, batch, seq)
        y_lru = _lru_call(proj, lru_conv_w, lru_conv_b, lru_wa, lru_ba, lru_wx, lru_bx, lru_lambda,
                          l, batch, seq)
        y_mla = _mla_call(h, w_in_t, proj, mla_w_uq, mla_w_ukv, mla_q_norm, mla_kv_norm, cos_m, sin_m,
                          l, batch, seq)
        out = _merge_call(y_ret, y_lru, y_mla, proj, x2, mod_all, norm_post, norm_pre, wb, wo, l, seq)
        if l + 1 < depth:
            x2, h = out
        else:
            x2 = out
    return x2.reshape(batch, seq, d)
```

```python
import functools
import math

import jax
import jax.numpy as jnp
from jax import lax
from jax.experimental import pallas as pl
from jax.experimental.pallas import tpu as pltpu

F32 = jnp.float32
BF16 = jnp.bfloat16

D_MODEL = 2048
CHUNK = 64
ROPE_BASE = 10000.0
NORM_EPS = 1e-6

RET_HEADS = 8
RET_DIM = 128
RET_WIDTH = RET_HEADS * RET_DIM

LRU_WIDTH = 1024
LRU_BLOCKS = 8
LRU_BLOCK_DIM = 128
CONV_WIDTH = 4
LRU_C = 8.0

MLA_HEADS = 8
MLA_NOPE = 128
MLA_ROPE = 64
MLA_V = 128
MLA_LORA = 512
MLA_WIDTH = MLA_HEADS * MLA_V
MLA_QK = MLA_NOPE + MLA_ROPE
MLA_QK_PAD = 256

LANE = 128
SUBLANE = 8
VMEM_LIMIT = 58 * 1024 * 1024

SRC_MKR = 4 * 1024 + 2 * 1024 + 2 * MLA_LORA
SRC_WIDTH = SRC_MKR + MLA_ROPE + MLA_WIDTH + 3 * D_MODEL
COL_RQ = 0
COL_RK = 1024
COL_RV = 2048
COL_RG = 3072
COL_LX = 4096
COL_LG = 5120
COL_MQ = 6144
COL_MKV = COL_MQ + MLA_LORA
COL_MG = SRC_MKR
COL_MERGE = COL_MG + MLA_WIDTH
PROJ_WIDTH = COL_MERGE + 3 * D_MODEL
PROJ_TN = 1024
PROJ_TM = 2048
ALIGNED_TILES = SRC_MKR // PROJ_TN
CAST_ROWS = 64
assert SRC_MKR % PROJ_TN == 0 and PROJ_WIDTH % PROJ_TN == 0 and COL_MERGE % D_MODEL == 0

NORM_TM = 512
RET_T = 256
LRU_T = 512
LRU_CLEN = LRU_T // 8
MLA_TQ = 256
MERGE_TM = 256
ADA_TK = 256


def _silu(v):
    return v * jax.nn.sigmoid(v)


def _dot(a, b):
    return jnp.dot(a, b, preferred_element_type=F32)


def _dot_nt(a, b):
    return lax.dot_general(a, b, (((1,), (1,)), ((), ())), preferred_element_type=F32)


def _dot_tn(a, b):
    return lax.dot_general(a, b, (((0,), (0,)), ((), ())), preferred_element_type=F32)


def _rms(x, gain):
    return x * lax.rsqrt(jnp.mean(x * x, axis=-1, keepdims=True) + NORM_EPS) * gain


def _params(*sem):
    return pltpu.CompilerParams(dimension_semantics=sem, vmem_limit_bytes=VMEM_LIMIT)


def _layer_vec(arr):
    return arr.reshape(arr.shape[0], 1, arr.shape[1])


def _vec_spec(width, layer, ngrid):
    if ngrid == 1:
        return pl.BlockSpec((1, 1, width), lambda i: (layer, 0, 0))
    return pl.BlockSpec((1, 1, width), lambda i, j: (layer, 0, 0))


def _ada_accumulate(first, c_ref, w_ref, b_ref, o_ref):
    @pl.when(first)
    def _():
        o_ref[0] = jnp.broadcast_to(b_ref[0], o_ref.shape[1:])

    c_act = _silu(c_ref[0]).astype(BF16)
    o_ref[0] += _dot(c_act, w_ref[0].astype(BF16))


def _ada_kernel(c_ref, w_ref, b_ref, o_ref):
    _ada_accumulate(pl.program_id(0) == 0, c_ref, w_ref, b_ref, o_ref)


def _c_slabs(c_pad, n_slabs):
    rows, d = c_pad.shape
    return c_pad.reshape(rows, n_slabs, d // n_slabs).transpose(1, 0, 2)


def _ada_call(c_pad, ada_w, ada_b, layer):
    _, d, n = ada_w.shape
    rows = c_pad.shape[0]
    n_slabs = d // ADA_TK
    return pl.pallas_call(
        _ada_kernel,
        grid=(n_slabs,),
        in_specs=[
            pl.BlockSpec((1, rows, ADA_TK), lambda k: (k, 0, 0)),
            pl.BlockSpec((1, ADA_TK, n), lambda k: (layer, k, 0)),
            pl.BlockSpec((1, 1, n), lambda k: (layer, 0, 0)),
        ],
        out_specs=pl.BlockSpec((1, rows, n), lambda k: (0, 0, 0)),
        out_shape=jax.ShapeDtypeStruct((1, rows, n), F32),
        compiler_params=_params("arbitrary"),
        name="ada_mod",
    )(_c_slabs(c_pad, n_slabs), ada_w, _layer_vec(ada_b))


def _modulate(x, gain, shift_ref, scale_ref, b):
    shift = shift_ref[0, pl.ds(b, 1), :]
    scale = scale_ref[0, pl.ds(b, 1), :]
    return _rms(x, gain) * (1.0 + scale) + shift


def _prenorm_kernel(x_ref, g_ref, shift_ref, scale_ref, pos_ref, freq_ref,
                    o_ref, cos_r_ref, sin_r_ref, cos_m_ref, sin_m_ref, *, tiles_per_batch):
    b = pl.program_id(0) // tiles_per_batch
    o_ref[...] = _modulate(x_ref[...], g_ref[0], shift_ref, scale_ref, b).astype(o_ref.dtype)

    ang = freq_ref[...] * pos_ref[0].astype(F32)
    cos_t = jnp.cos(ang)
    sin_t = jnp.sin(ang)
    n_r, n_m = RET_DIM // 2, MLA_ROPE // 2
    c_r, s_r = cos_t[0:n_r], sin_t[0:n_r]
    c_m, s_m = cos_t[n_r:n_r + n_m], sin_t[n_r:n_r + n_m]
    pad = jnp.zeros((LANE - MLA_ROPE, ang.shape[1]), F32)
    cos_r_ref[...] = jnp.concatenate([c_r, c_r], axis=0).T
    sin_r_ref[...] = jnp.concatenate([-s_r, s_r], axis=0).T
    cos_m_ref[...] = jnp.concatenate([c_m, c_m, pad], axis=0).T
    sin_m_ref[...] = jnp.concatenate([-s_m, s_m, pad], axis=0).T


def _prenorm_call(x2, mod, norm_pre, positions, layer, seq):
    m, d = x2.shape
    rows = mod.shape[1]
    n_tiles = m // NORM_TM

    def inv_freq(dim):
        return ROPE_BASE ** (-jnp.arange(0, dim, 2, dtype=F32) / dim)

    freqs = jnp.concatenate([inv_freq(RET_DIM), inv_freq(MLA_ROPE)])[:, None]
    table = pl.BlockSpec((NORM_TM, LANE), lambda i: (i, 0))
    return pl.pallas_call(
        functools.partial(_prenorm_kernel, tiles_per_batch=seq // NORM_TM),
        grid=(n_tiles,),
        in_specs=[
            pl.BlockSpec((NORM_TM, d), lambda i: (i, 0)),
            _vec_spec(d, layer, 1),
            pl.BlockSpec((1, rows, d), lambda i: (0, 0, 0)),
            pl.BlockSpec((1, rows, d), lambda i: (0, 0, 1)),
            pl.BlockSpec((1, 1, NORM_TM), lambda i: (i, 0, 0)),
            pl.BlockSpec(freqs.shape, lambda i: (0, 0)),
        ],
        out_specs=(pl.BlockSpec((NORM_TM, d), lambda i: (i, 0)), table, table, table, table),
        out_shape=(jax.ShapeDtypeStruct((m, d), BF16),) + (jax.ShapeDtypeStruct((m, LANE), F32),) * 4,
        compiler_params=_params("arbitrary"),
        name="pre_norm",
    )(x2, _layer_vec(norm_pre), mod, mod, positions.reshape(n_tiles, 1, NORM_TM), freqs)


def _inproj_kernel(h_ref, w_ref, wx_ref, o_ref, wb_ref):
    j = pl.program_id(0)
    first_row_tile = pl.program_id(1) == 0
    n_chunks = PROJ_TN // CAST_ROWS

    @pl.when(first_row_tile & (j < ALIGNED_TILES))
    def _():
        def body(i, carry):
            r0 = pl.multiple_of(i * CAST_ROWS, CAST_ROWS)
            wb_ref[pl.ds(r0, CAST_ROWS), :] = w_ref[0, pl.ds(r0, CAST_ROWS), :].astype(BF16)
            return carry

        lax.fori_loop(0, n_chunks, body, 0)

    @pl.when(first_row_tile & (j >= ALIGNED_TILES))
    def _():
        def body(i, carry):
            r0 = pl.multiple_of(i * CAST_ROWS, CAST_ROWS)
            wb_ref[pl.ds(r0, CAST_ROWS), :] = w_ref[0, pl.ds(r0 + MLA_ROPE, CAST_ROWS), :].astype(BF16)
            return carry

        lax.fori_loop(0, n_chunks - 1, body, 0)
        wb_ref[PROJ_TN - MLA_ROPE:, :] = wx_ref[0].astype(BF16)

    o_ref[...] = _dot_nt(h_ref[...], wb_ref[...]).astype(o_ref.dtype)


def _inproj_call(h, w_in_t, layer):
    m, d = h.shape
    per_tile = PROJ_TN // MLA_ROPE
    last_block = SRC_WIDTH // MLA_ROPE - 1
    return pl.pallas_call(
        _inproj_kernel,
        grid=(PROJ_WIDTH // PROJ_TN, m // PROJ_TM),
        in_specs=[
            pl.BlockSpec((PROJ_TM, d), lambda j, i: (i, 0)),
            pl.BlockSpec((1, PROJ_TN, d), lambda j, i: (layer, j, 0)),
            pl.BlockSpec((1, MLA_ROPE, d),
                         lambda j, i: (layer, jnp.minimum((j + 1) * per_tile, last_block), 0)),
        ],
        out_specs=pl.BlockSpec((PROJ_TM, PROJ_TN), lambda j, i: (i, j)),
        out_shape=jax.ShapeDtypeStruct((m, PROJ_WIDTH), BF16),
        scratch_shapes=[pltpu.VMEM((PROJ_TN, d), BF16)],
        compiler_params=_params("arbitrary", "arbitrary"),
        name="in_proj",
    )(h, w_in_t, w_in_t)


_LOG_GAMMA = tuple(math.log1p(-(2.0 ** (-5.0 - h))) for h in range(RET_HEADS))


def _ret_kernel(q_ref, k_ref, v_ref, g_ref, cos_ref, sin_ref, gn_ref, wb_ref, wo_ref,
                o_ref, wb_out, wo_out, state_ref, dmat_ref, dq_ref, dk_ref):
    t_blk = RET_T
    wb_out[...] = wb_ref[0].astype(BF16)
    wo_out[...] = wo_ref[0].astype(BF16)

    @pl.when((pl.program_id(0) == 0) & (pl.program_id(1) == 0))
    def _():
        ri = lax.broadcasted_iota(jnp.int32, (t_blk, t_blk), 0)
        ci = lax.broadcasted_iota(jnp.int32, (t_blk, t_blk), 1)
        dist = jnp.abs(ri - ci).astype(F32)
        visible = (ci // CHUNK) <= (ri // CHUNK)
        row = lax.broadcasted_iota(jnp.int32, (t_blk, RET_DIM), 0).astype(F32)
        for h in range(RET_HEADS):
            lg = _LOG_GAMMA[h]
            dmat_ref[h] = jnp.where(visible, jnp.exp(lg * dist), 0.0)
            dq_ref[h] = jnp.exp(lg * (row + 1.0))
            dk_ref[h] = jnp.exp(lg * ((t_blk - 1.0) - row))

    @pl.when(pl.program_id(1) == 0)
    def _():
        state_ref[...] = jnp.zeros_like(state_ref)

    cos = cos_ref[...]
    sin = sin_ref[...]
    for h in range(RET_HEADS):
        sl = slice(h * RET_DIM, (h + 1) * RET_DIM)
        q = q_ref[:, sl].astype(F32)
        k = k_ref[:, sl].astype(F32)
        q = (q * cos + pltpu.roll(q, RET_DIM // 2, 1) * sin) * (RET_DIM ** -0.5)
        k = k * cos + pltpu.roll(k, RET_DIM // 2, 1) * sin
        v = v_ref[:, sl]
        scores = _dot_nt(q.astype(BF16), k.astype(BF16)) * dmat_ref[h]
        o = _dot(scores.astype(BF16), v)
        state = state_ref[h]
        o = o + _dot((q * dq_ref[h]).astype(BF16), state.astype(BF16))
        k_dec = (k * dk_ref[h]).astype(BF16)
        state_ref[h] = state * math.exp(_LOG_GAMMA[h] * t_blk) + _dot_tn(k_dec, v)
        mean = jnp.mean(o, axis=-1, keepdims=True)
        cen = o - mean
        var = jnp.mean(cen * cen, axis=-1, keepdims=True)
        normed = cen * lax.rsqrt(var + NORM_EPS) * gn_ref[0, :, sl]
        o_ref[:, sl] = (normed * _silu(g_ref[:, sl].astype(F32))).astype(o_ref.dtype)


def _ret_call(proj, cos_r, sin_r, ret_gn, w_branch, w_out, layer, batch, seq):
    m = proj.shape[0]
    nt = seq // RET_T
    w = RET_WIDTH
    steps = batch * nt
    wb_rows, wo_rows = w_branch.shape[1] // steps, w_out.shape[1] // steps
    bf16_rows = 2 * SUBLANE
    assert wb_rows * steps == w_branch.shape[1] and wb_rows % bf16_rows == 0
    assert wo_rows * steps == w_out.shape[1] and wo_rows % bf16_rows == 0
    d = w_branch.shape[2]

    def col(c):
        return pl.BlockSpec((RET_T, w), lambda b, t, c=c: (b * nt + t, c // w))

    tab = pl.BlockSpec((RET_T, RET_DIM), lambda b, t: (b * nt + t, 0))
    return pl.pallas_call(
        _ret_kernel,
        grid=(batch, nt),
        in_specs=[col(COL_RQ), col(COL_RK), col(COL_RV), col(COL_RG), tab, tab,
                  _vec_spec(w, layer, 2),
                  pl.BlockSpec((1, wb_rows, d), lambda b, t: (layer, b * nt + t, 0)),
                  pl.BlockSpec((1, wo_rows, d), lambda b, t: (layer, b * nt + t, 0))],
        out_specs=(pl.BlockSpec((RET_T, w), lambda b, t: (b * nt + t, 0)),
                   pl.BlockSpec((wb_rows, d), lambda b, t: (b * nt + t, 0)),
                   pl.BlockSpec((wo_rows, d), lambda b, t: (b * nt + t, 0))),
        out_shape=(jax.ShapeDtypeStruct((m, w), BF16),
                   jax.ShapeDtypeStruct(w_branch.shape[1:], BF16),
                   jax.ShapeDtypeStruct(w_out.shape[1:], BF16)),
        scratch_shapes=[
            pltpu.VMEM((RET_HEADS, RET_DIM, RET_DIM), F32),
            pltpu.VMEM((RET_HEADS, RET_T, RET_T), F32),
            pltpu.VMEM((RET_HEADS, RET_T, RET_DIM), F32),
            pltpu.VMEM((RET_HEADS, RET_T, RET_DIM), F32),
        ],
        compiler_params=_params("arbitrary", "arbitrary"),
        name="retention",
    )(proj, proj, proj, proj, cos_r, sin_r, _layer_vec(ret_gn), w_branch, w_out)


def _sublane_scan(a, b, row):
    for s in (1, 2, 4):
        keep = row >= s
        a_prev = jnp.where(keep, pltpu.roll(a, s, 0), 1.0)
        b_prev = jnp.where(keep, pltpu.roll(b, s, 0), 0.0)
        b = a * b_prev + b
        a = a * a_prev
    return a, b


def _lru_kernel(x_ref, g_ref, cw_ref, cb_ref, wa_ref, ba_ref, wx_ref, bx_ref, lam_ref, *rest, with_ada):
    if with_ada:
        c_ref, aw_ref, ab_ref, o_ref, mod_ref, perm, perm_t, halo, hcar, a_s, b_s = rest
        _ada_accumulate((pl.program_id(0) == 0) & (pl.program_id(1) == 0), c_ref, aw_ref, ab_ref, mod_ref)
    else:
        o_ref, perm, perm_t, halo, hcar, a_s, b_s = rest
    t_blk = LRU_T
    clen = LRU_CLEN
    taps = CONV_WIDTH - 1
    row = lax.broadcasted_iota(jnp.int32, (SUBLANE, LRU_WIDTH), 0)

    @pl.when((pl.program_id(0) == 0) & (pl.program_id(1) == 0))
    def _():
        r = lax.broadcasted_iota(jnp.int32, (t_blk, t_blk), 0)
        c = lax.broadcasted_iota(jnp.int32, (t_blk, t_blk), 1)
        perm[...] = jnp.where(c == (r % SUBLANE) * clen + r // SUBLANE, 1.0, 0.0).astype(BF16)
        perm_t[...] = jnp.where(c == (r % clen) * SUBLANE + r // clen, 1.0, 0.0).astype(BF16)

    @pl.when(pl.program_id(1) == 0)
    def _():
        halo[...] = jnp.zeros_like(halo)
        hcar[...] = jnp.zeros_like(hcar)

    xp = _dot(perm[...], x_ref[...])
    pieces = []
    for k in range(taps):
        cur = xp[t_blk - (taps - k) * SUBLANE:t_blk - (taps - k - 1) * SUBLANE, :]
        prev = halo[k * SUBLANE:(k + 1) * SUBLANE, :]
        pieces.append(jnp.where(row == 0, pltpu.roll(prev, 1, 0), pltpu.roll(cur, 1, 0)))
    halo[...] = xp[t_blk - taps * SUBLANE:, :]
    xext = jnp.concatenate(pieces + [xp], axis=0)
    xc = cb_ref[0]
    for w in range(CONV_WIDTH):
        xc = xc + cw_ref[0, w:w + 1, :] * xext[w * SUBLANE:w * SUBLANE + t_blk, :]

    neg_lam = -lam_ref[0]
    softplus = jnp.maximum(neg_lam, 0.0) + jnp.log1p(jnp.exp(-jnp.abs(neg_lam)))
    neg_half_rate = (-0.5 * LRU_C) * softplus
    for n in range(LRU_BLOCKS):
        sl = slice(n * LRU_BLOCK_DIM, (n + 1) * LRU_BLOCK_DIM)
        xn = xc[:, sl]
        xn_b = xn.astype(BF16)
        t_r = jnp.tanh(0.5 * (_dot(xn_b, wa_ref[0, n].astype(BF16)) + ba_ref[0, :, sl]))
        t_i = jnp.tanh(0.5 * (_dot(xn_b, wx_ref[0, n].astype(BF16)) + bx_ref[0, :, sl]))
        half_rate = neg_half_rate[:, sl]
        log_a = half_rate * t_r + half_rate
        a = jnp.exp(log_a)
        a_s[:, sl] = a
        var = jnp.tanh(-log_a) * (a * a + 1.0)
        std = jnp.where(var > 0.0, var * lax.rsqrt(var), 0.0)
        half_x = 0.5 * xn
        b_s[:, sl] = std * (half_x * t_i + half_x)

    def body(j, carry):
        h, prod = carry
        r0 = pl.multiple_of(j * SUBLANE, SUBLANE)
        a = a_s[pl.ds(r0, SUBLANE), :]
        h = a * h + b_s[pl.ds(r0, SUBLANE), :]
        prod = a * prod
        b_s[pl.ds(r0, SUBLANE), :] = h
        a_s[pl.ds(r0, SUBLANE), :] = prod
        return h, prod

    zeros = jnp.zeros((SUBLANE, LRU_WIDTH), F32)
    h_end, a_end = lax.fori_loop(0, clen, body, (zeros, zeros + 1.0), unroll=4)

    a_inc, h_inc = _sublane_scan(a_end, h_end, row)
    h_prev_tile = hcar[...]
    h_chunk_end = a_inc * h_prev_tile + h_inc
    h_chunk_start = jnp.where(row == 0, h_prev_tile, pltpu.roll(h_chunk_end, 1, 0))
    hcar[...] = jnp.broadcast_to(h_chunk_end[SUBLANE - 1:SUBLANE, :], (SUBLANE, LRU_WIDTH))

    h_true = b_s[...] + a_s[...] * jnp.concatenate([h_chunk_start] * clen, axis=0)
    gp = _dot(perm[...], g_ref[...])
    out_p = (h_true * _silu(gp)).astype(BF16)
    o_ref[...] = _dot(perm_t[...], out_p).astype(o_ref.dtype)


def _lru_call(proj, cw, cb, wa, ba, wx, bx, lam, layer, batch, seq, ada=None):
    m = proj.shape[0]
    nt = seq // LRU_T
    w = LRU_WIDTH

    def col(c):
        return pl.BlockSpec((LRU_T, w), lambda b, t, c=c: (b * nt + t, c // w))

    def vec():
        return _vec_spec(w, layer, 2)

    def blk():
        return pl.BlockSpec((1, LRU_BLOCKS, LRU_BLOCK_DIM, LRU_BLOCK_DIM), lambda b, t: (layer, 0, 0, 0))

    in_specs = [col(COL_LX), col(COL_LG),
                pl.BlockSpec((1, CONV_WIDTH, w), lambda b, t: (layer, 0, 0)), vec(),
                blk(), vec(), blk(), vec(), vec()]
    args = [proj, proj, cw, _layer_vec(cb), wa, _layer_vec(ba), wx, _layer_vec(bx), _layer_vec(lam)]
    out_specs = pl.BlockSpec((LRU_T, w), lambda b, t: (b * nt + t, 0))
    out_shape = jax.ShapeDtypeStruct((m, w), BF16)
    if ada is not None:
        c_pad, ada_w, ada_b = ada
        rows = c_pad.shape[0]
        _, d, n = ada_w.shape
        steps = batch * nt
        slab = d // steps
        assert slab * steps == d and slab % LANE == 0
        in_specs += [pl.BlockSpec((1, rows, slab), lambda b, t: (b * nt + t, 0, 0)),
                     pl.BlockSpec((1, slab, n), lambda b, t: (layer + 1, b * nt + t, 0)),
                     pl.BlockSpec((1, 1, n), lambda b, t: (layer + 1, 0, 0))]
        args += [_c_slabs(c_pad, steps), ada_w, _layer_vec(ada_b)]
        out_specs = (out_specs, pl.BlockSpec((1, rows, n), lambda b, t: (0, 0, 0)))
        out_shape = (out_shape, jax.ShapeDtypeStruct((1, rows, n), F32))
    return pl.pallas_call(
        functools.partial(_lru_kernel, with_ada=ada is not None),
        grid=(batch, nt),
        in_specs=in_specs,
        out_specs=out_specs,
        out_shape=out_shape,
        scratch_shapes=[
            pltpu.VMEM((LRU_T, LRU_T), BF16),
            pltpu.VMEM((LRU_T, LRU_T), BF16),
            pltpu.VMEM(((CONV_WIDTH - 1) * SUBLANE, w), F32),
            pltpu.VMEM((SUBLANE, w), F32),
            pltpu.VMEM((LRU_T, w), F32),
            pltpu.VMEM((LRU_T, w), F32),
        ],
        compiler_params=_params("arbitrary", "arbitrary"),
        name="rg_lru",
    )(*args)


def _rope64(x, cos_t, sin_t):
    swapped = pltpu.roll(x, MLA_ROPE // 2, 1) + pltpu.roll(x, LANE - MLA_ROPE // 2, 1)
    return x * cos_t + swapped * sin_t


def _mla_kernel(h_ref, wkr_ref, mq_ref, mkv_ref, mg_ref, wq_ref, wkv_ref, qn_ref, kvn_ref,
                cos_ref, sin_ref, o_ref, qlat, kvlat, krope, wq_s, qs, ks, vs, s_scr, p_scr, *, seq):
    rows = 256
    head = pl.program_id(1)

    @pl.when(head == 0)
    def _():
        wkr = jnp.concatenate([wkr_ref[0], jnp.zeros((LANE - MLA_ROPE, D_MODEL), F32)], axis=0).astype(BF16)
        for i in range(seq // rows):
            rs = slice(i * rows, (i + 1) * rows)
            qlat[rs, :] = _rms(mq_ref[rs, :].astype(F32), qn_ref[0]).astype(BF16)
            kvlat[rs, :] = _rms(mkv_ref[rs, :].astype(F32), kvn_ref[0]).astype(BF16)
            kr = _dot_nt(h_ref[rs, :], wkr)
            krope[rs, :] = _rope64(kr, cos_ref[rs, :], sin_ref[rs, :]).astype(BF16)
        vs[:, MLA_V:] = jnp.ones((seq, MLA_V), BF16)
        zeros = jnp.zeros((MLA_LORA, MLA_QK_PAD - MLA_QK), F32)
        wq_all = wq_ref[0]
        for hh in range(MLA_HEADS):
            w_head = wq_all[:, hh * MLA_QK:(hh + 1) * MLA_QK]
            wq_s[hh] = jnp.concatenate([w_head, zeros], axis=1).astype(BF16)

    scale = (MLA_QK ** -0.5) * math.log2(math.e)
    wq = wq_s[head]
    wkv = wkv_ref[0].astype(BF16)
    for i in range(seq // rows):
        rs = slice(i * rows, (i + 1) * rows)
        q = _dot(qlat[rs, :], wq)
        qs[rs, 0:MLA_NOPE] = (q[:, 0:MLA_NOPE] * scale).astype(BF16)
        qs[rs, MLA_NOPE:] = (_rope64(q[:, MLA_NOPE:], cos_ref[rs, :], sin_ref[rs, :]) * scale).astype(BF16)
        kv = _dot(kvlat[rs, :], wkv)
        ks[rs, 0:MLA_NOPE] = kv[:, 0:MLA_NOPE].astype(BF16)
        ks[rs, MLA_NOPE:] = krope[rs, :]
        vs[rs, 0:MLA_V] = kv[:, MLA_NOPE:].astype(BF16)

    tq = MLA_TQ
    ri = lax.broadcasted_iota(jnp.int32, (tq, tq), 0)
    ci = lax.broadcasted_iota(jnp.int32, (tq, tq), 1)
    visible = (ci // CHUNK) <= (ri // CHUNK)
    n_q = seq // tq

    def scores(i):
        q0, q1 = i * tq, (i + 1) * tq
        s_scr[i % 2, :, 0:q1] = _dot_nt(qs[q0:q1, :], ks[0:q1, :])

    def probs(i):
        q0, q1 = i * tq, (i + 1) * tq
        s_d = jnp.where(visible, s_scr[i % 2, :, q0:q1], -1e30)
        m = jnp.max(s_d, axis=-1, keepdims=True)
        if i > 0:
            m = jnp.maximum(m, jnp.max(s_scr[i % 2, :, 0:q0], axis=-1, keepdims=True))
            p_scr[i % 2, :, 0:q0] = jnp.exp2(s_scr[i % 2, :, 0:q0] - m).astype(BF16)
        p_scr[i % 2, :, q0:q1] = jnp.exp2(s_d - m).astype(BF16)

    def values(i):
        q0, q1 = i * tq, (i + 1) * tq
        acc = _dot(p_scr[i % 2, :, 0:q1], vs[0:q1, :])
        out = acc[:, 0:MLA_V] / acc[:, MLA_V:]
        o_ref[q0:q1, :] = (out * _silu(mg_ref[q0:q1, :].astype(F32))).astype(o_ref.dtype)

    scores(0)
    for i in range(n_q + 1):
        if i + 1 < n_q:
            scores(i + 1)
        if i < n_q:
            probs(i)
        if i > 0:
            values(i - 1)


def _mla_call(h, w_in_t, proj, w_uq, w_ukv, q_norm, kv_norm, cos_m, sin_m, layer, batch, seq):
    m, d = h.shape
    lora = MLA_LORA
    return pl.pallas_call(
        functools.partial(_mla_kernel, seq=seq),
        grid=(batch, MLA_HEADS),
        in_specs=[
            pl.BlockSpec((seq, d), lambda b, hd: (b, 0)),
            pl.BlockSpec((1, MLA_ROPE, d), lambda b, hd: (layer, SRC_MKR // MLA_ROPE, 0)),
            pl.BlockSpec((seq, lora), lambda b, hd: (b, COL_MQ // lora)),
            pl.BlockSpec((seq, lora), lambda b, hd: (b, COL_MKV // lora)),
            pl.BlockSpec((seq, MLA_V), lambda b, hd: (b, COL_MG // MLA_V + hd)),
            pl.BlockSpec((1, lora, MLA_HEADS * MLA_QK), lambda b, hd: (layer, 0, 0)),
            pl.BlockSpec((1, lora, MLA_NOPE + MLA_V), lambda b, hd: (layer, 0, hd)),
            _vec_spec(lora, layer, 2),
            _vec_spec(lora, layer, 2),
            pl.BlockSpec((seq, LANE), lambda b, hd: (b, 0)),
            pl.BlockSpec((seq, LANE), lambda b, hd: (b, 0)),
        ],
        out_specs=pl.BlockSpec((seq, MLA_V), lambda b, hd: (b, hd)),
        out_shape=jax.ShapeDtypeStruct((m, MLA_WIDTH), BF16),
        scratch_shapes=[
            pltpu.VMEM((seq, lora), BF16),
            pltpu.VMEM((seq, lora), BF16),
            pltpu.VMEM((seq, LANE), BF16),
            pltpu.VMEM((MLA_HEADS, lora, MLA_QK_PAD), BF16),
            pltpu.VMEM((seq, MLA_QK_PAD), BF16),
            pltpu.VMEM((seq, MLA_QK_PAD), BF16),
            pltpu.VMEM((seq, 2 * MLA_V), BF16),
            pltpu.VMEM((2, MLA_TQ, seq), F32),
            pltpu.VMEM((2, MLA_TQ, seq), BF16),
        ],
        compiler_params=_params("arbitrary", "arbitrary"),
        name="mla",
    )(h, w_in_t, proj, proj, proj, w_uq, w_ukv, _layer_vec(q_norm), _layer_vec(kv_norm), cos_m, sin_m)


def _merge_kernel(yr_ref, yl_ref, ym_ref, l0_ref, l1_ref, l2_ref, x_ref, res_ref, g_ref,
                  wb_ref, wo_ref, *rest, tiles_per_batch, emit_next):
    b = pl.program_id(0) // tiles_per_batch
    merged = None
    for i, (y_ref, l_ref) in enumerate(((yr_ref, l0_ref), (yl_ref, l1_ref), (ym_ref, l2_ref))):
        z = _dot(y_ref[...], wb_ref[i * 1024:(i + 1) * 1024, :])
        z = z * jax.nn.sigmoid(l_ref[...].astype(F32))
        merged = z if merged is None else merged + z
    y = _dot(merged.astype(BF16), wo_ref[...])
    post = g_ref[0] * (1.0 + res_ref[0, pl.ds(b, 1), :])
    inv = lax.rsqrt(jnp.mean(y * y, axis=-1, keepdims=True) + NORM_EPS)
    x_new = x_ref[...] + (y * inv) * post
    if emit_next:
        gn_ref, shift_ref, scale_ref, o_ref, h_ref = rest
        pre = gn_ref[0] * (1.0 + scale_ref[0, pl.ds(b, 1), :])
        inv_n = lax.rsqrt(jnp.mean(x_new * x_new, axis=-1, keepdims=True) + NORM_EPS)
        h_ref[...] = ((x_new * inv_n) * pre + shift_ref[0, pl.ds(b, 1), :]).astype(h_ref.dtype)
    else:
        (o_ref,) = rest
    o_ref[...] = x_new


def _merge_call(y_ret, y_lru, y_mla, proj, x2, mod, mod_next, norm_post, norm_pre, wb, wo, layer, seq):
    m, d = x2.shape
    tm = MERGE_TM
    mod_rows = mod.shape[1]
    emit_next = mod_next is not None

    def rows(width, c=0):
        return pl.BlockSpec((tm, width), lambda i, c=c: (i, c))

    def resident(shape):
        return pl.BlockSpec(shape, lambda i: (0, 0), pipeline_mode=pl.Buffered(1))

    def mod_part(part):
        return pl.BlockSpec((1, mod_rows, d), lambda i: (0, 0, part))

    merge_col = COL_MERGE // d
    in_specs = [rows(1024), rows(1024), rows(1024),
                rows(d, merge_col), rows(d, merge_col + 1), rows(d, merge_col + 2),
                rows(d), mod_part(2), _vec_spec(d, layer, 1),
                resident(wb.shape), resident(wo.shape)]
    args = [y_ret, y_lru, y_mla, proj, proj, proj, x2, mod, _layer_vec(norm_post), wb, wo]
    out_specs = rows(d)
    out_shape = jax.ShapeDtypeStruct((m, d), F32)
    if emit_next:
        in_specs += [_vec_spec(d, layer + 1, 1), mod_part(0), mod_part(1)]
        args += [_layer_vec(norm_pre), mod_next, mod_next]
        out_specs = (out_specs, rows(d))
        out_shape = (out_shape, jax.ShapeDtypeStruct((m, d), BF16))
    return pl.pallas_call(
        functools.partial(_merge_kernel, tiles_per_batch=seq // tm, emit_next=emit_next),
        grid=(m // tm,),
        in_specs=in_specs,
        out_specs=out_specs,
        out_shape=out_shape,
        compiler_params=_params("arbitrary"),
        name="merge_out",
    )(*args)


def kernel(x, c, positions, ada_w, ada_b, norm_pre, norm_post, w_in, ret_gn, lru_conv_w, lru_conv_b,
           lru_wa, lru_ba, lru_wx, lru_bx, lru_lambda, mla_q_norm, mla_w_uq, mla_kv_norm, mla_w_ukv,
           w_branch, w_out):
    batch, seq, d = x.shape
    depth = w_in.shape[0]
    m = batch * seq
    assert d == D_MODEL and w_in.shape[2] == SRC_WIDTH and batch <= SUBLANE
    assert seq % PROJ_TM == 0 and seq % RET_T == 0 and seq % LRU_T == 0 and seq % MLA_TQ == 0
    assert seq % MERGE_TM == 0 and seq % NORM_TM == 0

    c_pad = jnp.pad(c, ((0, SUBLANE - batch), (0, 0)))
    mod = _ada_call(c_pad, ada_w, ada_b, 0)
    w_in_t = jnp.swapaxes(w_in, 1, 2)

    x2 = x.reshape(m, d)
    h, cos_r, sin_r, cos_m, sin_m = _prenorm_call(x2, mod, norm_pre, positions, 0, seq)
    for l in range(depth):
        proj = _inproj_call(h, w_in_t, l)
        y_ret, wb, wo = _ret_call(proj, cos_r, sin_r, ret_gn, w_branch, w_out, l, batch, seq)
        lru_args = (proj, lru_conv_w, lru_conv_b, lru_wa, lru_ba, lru_wx, lru_bx, lru_lambda, l, batch, seq)
        if l + 1 < depth:
            y_lru, mod_next = _lru_call(*lru_args, ada=(c_pad, ada_w, ada_b))
        else:
            y_lru, mod_next = _lru_call(*lru_args), None
        y_mla = _mla_call(h, w_in_t, proj, mla_w_uq, mla_w_ukv, mla_q_norm, mla_kv_norm, cos_m, sin_m,
                          l, batch, seq)
        out = _merge_call(y_ret, y_lru, y_mla, proj, x2, mod, mod_next, norm_post, norm_pre, wb, wo, l, seq)
        mod = mod_next
        if l + 1 < depth:
            x2, h = out
        else:
            x2 = out
    return x2.reshape(batch, seq, d)
```

```python
import functools
import math

import jax
import jax.numpy as jnp
from jax import lax
from jax.experimental import pallas as pl
from jax.experimental.pallas import tpu as pltpu

F32 = jnp.float32
BF16 = jnp.bfloat16

D_MODEL = 2048
CHUNK = 64
ROPE_BASE = 10000.0
NORM_EPS = 1e-6

RET_HEADS = 8
RET_DIM = 128
RET_WIDTH = RET_HEADS * RET_DIM

LRU_WIDTH = 1024
LRU_BLOCKS = 8
LRU_BLOCK_DIM = 128
CONV_WIDTH = 4
LRU_C = 8.0

MLA_HEADS = 8
MLA_NOPE = 128
MLA_ROPE = 64
MLA_V = 128
MLA_LORA = 512
MLA_WIDTH = MLA_HEADS * MLA_V
MLA_QK = MLA_NOPE + MLA_ROPE
MLA_QK_PAD = 256

LANE = 128
SUBLANE = 8
VMEM_LIMIT = 58 * 1024 * 1024

SRC_MKR = 4 * 1024 + 2 * 1024 + 2 * MLA_LORA
SRC_WIDTH = SRC_MKR + MLA_ROPE + MLA_WIDTH + 3 * D_MODEL
COL_RQ = 0
COL_RK = 1024
COL_RV = 2048
COL_RG = 3072
COL_LX = 4096
COL_LG = 5120
COL_MQ = 6144
COL_MKV = COL_MQ + MLA_LORA
COL_MG = SRC_MKR
COL_MERGE = COL_MG + MLA_WIDTH
PROJ_WIDTH = COL_MERGE + 3 * D_MODEL
PROJ_TN = 1024
PROJ_TM = 2048
ALIGNED_TILES = SRC_MKR // PROJ_TN
CAST_ROWS = 64
assert SRC_MKR % PROJ_TN == 0 and PROJ_WIDTH % PROJ_TN == 0 and COL_MERGE % D_MODEL == 0

NORM_TM = 512
RET_T = 256
RET_SUB = 1
LRU_T = 512
LRU_CLEN = LRU_T // 8
MLA_TQ = 256
MLA_HPS = 2
MERGE_TM = 256
ADA_TK = 256


def _silu(v):
    return v * jax.nn.sigmoid(v)


def _dot(a, b):
    return jnp.dot(a, b, preferred_element_type=F32)


def _dot_nt(a, b):
    return lax.dot_general(a, b, (((1,), (1,)), ((), ())), preferred_element_type=F32)


def _dot_tn(a, b):
    return lax.dot_general(a, b, (((0,), (0,)), ((), ())), preferred_element_type=F32)


def _rms(x, gain):
    return x * lax.rsqrt(jnp.mean(x * x, axis=-1, keepdims=True) + NORM_EPS) * gain


def _params(*sem):
    return pltpu.CompilerParams(dimension_semantics=sem, vmem_limit_bytes=VMEM_LIMIT)


def _layer_vec(arr):
    return arr.reshape(arr.shape[0], 1, arr.shape[1])


def _vec_spec(width, layer, ngrid):
    if ngrid == 1:
        return pl.BlockSpec((1, 1, width), lambda i: (layer, 0, 0))
    return pl.BlockSpec((1, 1, width), lambda i, j: (layer, 0, 0))


def _ada_accumulate(first, c_ref, w_ref, b_ref, o_ref):
    @pl.when(first)
    def _():
        o_ref[0] = jnp.broadcast_to(b_ref[0], o_ref.shape[1:])

    c_act = _silu(c_ref[0]).astype(BF16)
    o_ref[0] += _dot(c_act, w_ref[0].astype(BF16))


def _ada_kernel(c_ref, w_ref, b_ref, o_ref):
    _ada_accumulate(pl.program_id(0) == 0, c_ref, w_ref, b_ref, o_ref)


def _c_slabs(c_pad, n_slabs):
    rows, d = c_pad.shape
    return c_pad.reshape(rows, n_slabs, d // n_slabs).transpose(1, 0, 2)


def _ada_call(c_pad, ada_w, ada_b, layer):
    _, d, n = ada_w.shape
    rows = c_pad.shape[0]
    n_slabs = d // ADA_TK
    return pl.pallas_call(
        _ada_kernel,
        grid=(n_slabs,),
        in_specs=[
            pl.BlockSpec((1, rows, ADA_TK), lambda k: (k, 0, 0)),
            pl.BlockSpec((1, ADA_TK, n), lambda k: (layer, k, 0)),
            pl.BlockSpec((1, 1, n), lambda k: (layer, 0, 0)),
        ],
        out_specs=pl.BlockSpec((1, rows, n), lambda k: (0, 0, 0)),
        out_shape=jax.ShapeDtypeStruct((1, rows, n), F32),
        compiler_params=_params("arbitrary"),
        name="ada_mod",
    )(_c_slabs(c_pad, n_slabs), ada_w, _layer_vec(ada_b))


def _modulate(x, gain, shift_ref, scale_ref, b):
    shift = shift_ref[0, pl.ds(b, 1), :]
    scale = scale_ref[0, pl.ds(b, 1), :]
    return _rms(x, gain) * (1.0 + scale) + shift


def _prenorm_kernel(x_ref, g_ref, shift_ref, scale_ref, pos_ref, freq_ref,
                    o_ref, cos_r_ref, sin_r_ref, cos_m_ref, sin_m_ref, *, tiles_per_batch):
    b = pl.program_id(0) // tiles_per_batch
    o_ref[...] = _modulate(x_ref[...], g_ref[0], shift_ref, scale_ref, b).astype(o_ref.dtype)

    ang = freq_ref[...] * pos_ref[0].astype(F32)
    cos_t = jnp.cos(ang)
    sin_t = jnp.sin(ang)
    n_r, n_m = RET_DIM // 2, MLA_ROPE // 2
    c_r, s_r = cos_t[0:n_r], sin_t[0:n_r]
    c_m, s_m = cos_t[n_r:n_r + n_m], sin_t[n_r:n_r + n_m]
    pad = jnp.zeros((LANE - MLA_ROPE, ang.shape[1]), F32)
    cos_r_ref[...] = jnp.concatenate([c_r, c_r], axis=0).T
    sin_r_ref[...] = jnp.concatenate([-s_r, s_r], axis=0).T
    cos_m_ref[...] = jnp.concatenate([c_m, c_m, pad], axis=0).T
    sin_m_ref[...] = jnp.concatenate([-s_m, s_m, pad], axis=0).T


def _prenorm_call(x2, mod, norm_pre, positions, layer, seq):
    m, d = x2.shape
    rows = mod.shape[1]
    n_tiles = m // NORM_TM

    def inv_freq(dim):
        return ROPE_BASE ** (-jnp.arange(0, dim, 2, dtype=F32) / dim)

    freqs = jnp.concatenate([inv_freq(RET_DIM), inv_freq(MLA_ROPE)])[:, None]
    table = pl.BlockSpec((NORM_TM, LANE), lambda i: (i, 0))
    return pl.pallas_call(
        functools.partial(_prenorm_kernel, tiles_per_batch=seq // NORM_TM),
        grid=(n_tiles,),
        in_specs=[
            pl.BlockSpec((NORM_TM, d), lambda i: (i, 0)),
            _vec_spec(d, layer, 1),
            pl.BlockSpec((1, rows, d), lambda i: (0, 0, 0)),
            pl.BlockSpec((1, rows, d), lambda i: (0, 0, 1)),
            pl.BlockSpec((1, 1, NORM_TM), lambda i: (i, 0, 0)),
            pl.BlockSpec(freqs.shape, lambda i: (0, 0)),
        ],
        out_specs=(pl.BlockSpec((NORM_TM, d), lambda i: (i, 0)), table, table, table, table),
        out_shape=(jax.ShapeDtypeStruct((m, d), BF16),) + (jax.ShapeDtypeStruct((m, LANE), F32),) * 4,
        compiler_params=_params("arbitrary"),
        name="pre_norm",
    )(x2, _layer_vec(norm_pre), mod, mod, positions.reshape(n_tiles, 1, NORM_TM), freqs)


def _inproj_kernel(h_ref, w_ref, wx_ref, o_ref, wb_ref):
    j = pl.program_id(0)
    first_row_tile = pl.program_id(1) == 0
    n_chunks = PROJ_TN // CAST_ROWS

    @pl.when(first_row_tile & (j < ALIGNED_TILES))
    def _():
        def body(i, carry):
            r0 = pl.multiple_of(i * CAST_ROWS, CAST_ROWS)
            wb_ref[pl.ds(r0, CAST_ROWS), :] = w_ref[0, pl.ds(r0, CAST_ROWS), :].astype(BF16)
            return carry

        lax.fori_loop(0, n_chunks, body, 0)

    @pl.when(first_row_tile & (j >= ALIGNED_TILES))
    def _():
        def body(i, carry):
            r0 = pl.multiple_of(i * CAST_ROWS, CAST_ROWS)
            wb_ref[pl.ds(r0, CAST_ROWS), :] = w_ref[0, pl.ds(r0 + MLA_ROPE, CAST_ROWS), :].astype(BF16)
            return carry

        lax.fori_loop(0, n_chunks - 1, body, 0)
        wb_ref[PROJ_TN - MLA_ROPE:, :] = wx_ref[0].astype(BF16)

    o_ref[...] = _dot_nt(h_ref[...], wb_ref[...]).astype(o_ref.dtype)


def _inproj_call(h, w_in_t, layer):
    m, d = h.shape
    per_tile = PROJ_TN // MLA_ROPE
    last_block = SRC_WIDTH // MLA_ROPE - 1
    return pl.pallas_call(
        _inproj_kernel,
        grid=(PROJ_WIDTH // PROJ_TN, m // PROJ_TM),
        in_specs=[
            pl.BlockSpec((PROJ_TM, d), lambda j, i: (i, 0)),
            pl.BlockSpec((1, PROJ_TN, d), lambda j, i: (layer, j, 0)),
            pl.BlockSpec((1, MLA_ROPE, d),
                         lambda j, i: (layer, jnp.minimum((j + 1) * per_tile, last_block), 0)),
        ],
        out_specs=pl.BlockSpec((PROJ_TM, PROJ_TN), lambda j, i: (i, j)),
        out_shape=jax.ShapeDtypeStruct((m, PROJ_WIDTH), BF16),
        scratch_shapes=[pltpu.VMEM((PROJ_TN, d), BF16)],
        compiler_params=_params("arbitrary", "arbitrary"),
        name="in_proj",
    )(h, w_in_t, w_in_t)


_LOG_GAMMA = tuple(math.log1p(-(2.0 ** (-5.0 - h))) for h in range(RET_HEADS))


def _ret_kernel(q_ref, k_ref, v_ref, g_ref, cos_ref, sin_ref, gn_ref, wb_ref, wo_ref,
                o_ref, wb_out, wo_out, state_ref, dmat_ref, dq_ref, dk_ref):
    t_blk = RET_T
    wb_out[...] = wb_ref[0].astype(BF16)
    wo_out[...] = wo_ref[0].astype(BF16)

    @pl.when((pl.program_id(0) == 0) & (pl.program_id(1) == 0))
    def _():
        ri = lax.broadcasted_iota(jnp.int32, (t_blk, t_blk), 0)
        ci = lax.broadcasted_iota(jnp.int32, (t_blk, t_blk), 1)
        dist = jnp.abs(ri - ci).astype(F32)
        visible = (ci // CHUNK) <= (ri // CHUNK)
        row = lax.broadcasted_iota(jnp.int32, (t_blk, RET_DIM), 0).astype(F32)
        for h in range(RET_HEADS):
            lg = _LOG_GAMMA[h]
            dmat_ref[h] = jnp.where(visible, jnp.exp(lg * dist), 0.0)
            dq_ref[h] = jnp.exp(lg * (row + 1.0))
            dk_ref[h] = jnp.exp(lg * ((t_blk - 1.0) - row))

    @pl.when(pl.program_id(1) == 0)
    def _():
        state_ref[...] = jnp.zeros_like(state_ref)

    for sub in range(RET_SUB):
        rs = slice(sub * t_blk, (sub + 1) * t_blk)
        cos = cos_ref[rs, :]
        sin = sin_ref[rs, :]
        for h in range(RET_HEADS):
            sl = slice(h * RET_DIM, (h + 1) * RET_DIM)
            q = q_ref[rs, sl].astype(F32)
            k = k_ref[rs, sl].astype(F32)
            q = (q * cos + pltpu.roll(q, RET_DIM // 2, 1) * sin) * (RET_DIM ** -0.5)
            k = k * cos + pltpu.roll(k, RET_DIM // 2, 1) * sin
            v = v_ref[rs, sl]
            scores = _dot_nt(q.astype(BF16), k.astype(BF16)) * dmat_ref[h]
            o = _dot(scores.astype(BF16), v)
            state = state_ref[h]
            o = o + _dot((q * dq_ref[h]).astype(BF16), state.astype(BF16))
            k_dec = (k * dk_ref[h]).astype(BF16)
            state_ref[h] = state * math.exp(_LOG_GAMMA[h] * t_blk) + _dot_tn(k_dec, v)
            mean = jnp.mean(o, axis=-1, keepdims=True)
            cen = o - mean
            var = jnp.mean(cen * cen, axis=-1, keepdims=True)
            normed = cen * lax.rsqrt(var + NORM_EPS) * gn_ref[0, :, sl]
            o_ref[rs, sl] = (normed * _silu(g_ref[rs, sl].astype(F32))).astype(o_ref.dtype)


def _ret_call(proj, cos_r, sin_r, ret_gn, w_branch, w_out, layer, batch, seq):
    m = proj.shape[0]
    rows = RET_SUB * RET_T
    nt = seq // rows
    w = RET_WIDTH
    steps = batch * nt
    wb_rows, wo_rows = w_branch.shape[1] // steps, w_out.shape[1] // steps
    bf16_rows = 2 * SUBLANE
    assert wb_rows * steps == w_branch.shape[1] and wb_rows % bf16_rows == 0
    assert wo_rows * steps == w_out.shape[1] and wo_rows % bf16_rows == 0
    d = w_branch.shape[2]

    def col(c):
        return pl.BlockSpec((rows, w), lambda b, t, c=c: (b * nt + t, c // w))

    tab = pl.BlockSpec((rows, RET_DIM), lambda b, t: (b * nt + t, 0))
    return pl.pallas_call(
        _ret_kernel,
        grid=(batch, nt),
        in_specs=[col(COL_RQ), col(COL_RK), col(COL_RV), col(COL_RG), tab, tab,
                  _vec_spec(w, layer, 2),
                  pl.BlockSpec((1, wb_rows, d), lambda b, t: (layer, b * nt + t, 0)),
                  pl.BlockSpec((1, wo_rows, d), lambda b, t: (layer, b * nt + t, 0))],
        out_specs=(pl.BlockSpec((rows, w), lambda b, t: (b * nt + t, 0)),
                   pl.BlockSpec((wb_rows, d), lambda b, t: (b * nt + t, 0)),
                   pl.BlockSpec((wo_rows, d), lambda b, t: (b * nt + t, 0))),
        out_shape=(jax.ShapeDtypeStruct((m, w), BF16),
                   jax.ShapeDtypeStruct(w_branch.shape[1:], BF16),
                   jax.ShapeDtypeStruct(w_out.shape[1:], BF16)),
        scratch_shapes=[
            pltpu.VMEM((RET_HEADS, RET_DIM, RET_DIM), F32),
            pltpu.VMEM((RET_HEADS, RET_T, RET_T), F32),
            pltpu.VMEM((RET_HEADS, RET_T, RET_DIM), F32),
            pltpu.VMEM((RET_HEADS, RET_T, RET_DIM), F32),
        ],
        compiler_params=_params("arbitrary", "arbitrary"),
        name="retention",
    )(proj, proj, proj, proj, cos_r, sin_r, _layer_vec(ret_gn), w_branch, w_out)


def _sublane_scan(a, b, row):
    for s in (1, 2, 4):
        keep = row >= s
        a_prev = jnp.where(keep, pltpu.roll(a, s, 0), 1.0)
        b_prev = jnp.where(keep, pltpu.roll(b, s, 0), 0.0)
        b = a * b_prev + b
        a = a * a_prev
    return a, b


def _lru_kernel(x_ref, g_ref, cw_ref, cb_ref, wa_ref, ba_ref, wx_ref, bx_ref, lam_ref, *rest, with_ada):
    if with_ada:
        c_ref, aw_ref, ab_ref, o_ref, mod_ref, perm, perm_t, halo, hcar, a_s, b_s = rest
        _ada_accumulate((pl.program_id(0) == 0) & (pl.program_id(1) == 0), c_ref, aw_ref, ab_ref, mod_ref)
    else:
        o_ref, perm, perm_t, halo, hcar, a_s, b_s = rest
    t_blk = LRU_T
    clen = LRU_CLEN
    taps = CONV_WIDTH - 1
    row = lax.broadcasted_iota(jnp.int32, (SUBLANE, LRU_WIDTH), 0)

    @pl.when((pl.program_id(0) == 0) & (pl.program_id(1) == 0))
    def _():
        r = lax.broadcasted_iota(jnp.int32, (t_blk, t_blk), 0)
        c = lax.broadcasted_iota(jnp.int32, (t_blk, t_blk), 1)
        perm[...] = jnp.where(c == (r % SUBLANE) * clen + r // SUBLANE, 1.0, 0.0).astype(BF16)
        perm_t[...] = jnp.where(c == (r % clen) * SUBLANE + r // clen, 1.0, 0.0).astype(BF16)

    @pl.when(pl.program_id(1) == 0)
    def _():
        halo[...] = jnp.zeros_like(halo)
        hcar[...] = jnp.zeros_like(hcar)

    xp = _dot(perm[...], x_ref[...])
    pieces = []
    for k in range(taps):
        cur = xp[t_blk - (taps - k) * SUBLANE:t_blk - (taps - k - 1) * SUBLANE, :]
        prev = halo[k * SUBLANE:(k + 1) * SUBLANE, :]
        pieces.append(jnp.where(row == 0, pltpu.roll(prev, 1, 0), pltpu.roll(cur, 1, 0)))
    halo[...] = xp[t_blk - taps * SUBLANE:, :]
    xext = jnp.concatenate(pieces + [xp], axis=0)
    xc = cb_ref[0]
    for w in range(CONV_WIDTH):
        xc = xc + cw_ref[0, w:w + 1, :] * xext[w * SUBLANE:w * SUBLANE + t_blk, :]

    neg_lam = -lam_ref[0]
    softplus = jnp.maximum(neg_lam, 0.0) + jnp.log1p(jnp.exp(-jnp.abs(neg_lam)))
    neg_half_rate = (-0.5 * LRU_C) * softplus
    for n in range(LRU_BLOCKS):
        sl = slice(n * LRU_BLOCK_DIM, (n + 1) * LRU_BLOCK_DIM)
        xn = xc[:, sl]
        xn_b = xn.astype(BF16)
        t_r = jnp.tanh(0.5 * (_dot(xn_b, wa_ref[0, n].astype(BF16)) + ba_ref[0, :, sl]))
        t_i = jnp.tanh(0.5 * (_dot(xn_b, wx_ref[0, n].astype(BF16)) + bx_ref[0, :, sl]))
        half_rate = neg_half_rate[:, sl]
        log_a = half_rate * t_r + half_rate
        a = jnp.exp(log_a)
        a_s[:, sl] = a
        var = jnp.tanh(-log_a) * (a * a + 1.0)
        std = jnp.where(var > 0.0, var * lax.rsqrt(var), 0.0)
        half_x = 0.5 * xn
        b_s[:, sl] = std * (half_x * t_i + half_x)

    def body(j, carry):
        h, prod = carry
        r0 = pl.multiple_of(j * SUBLANE, SUBLANE)
        a = a_s[pl.ds(r0, SUBLANE), :]
        h = a * h + b_s[pl.ds(r0, SUBLANE), :]
        prod = a * prod
        b_s[pl.ds(r0, SUBLANE), :] = h
        a_s[pl.ds(r0, SUBLANE), :] = prod
        return h, prod

    zeros = jnp.zeros((SUBLANE, LRU_WIDTH), F32)
    h_end, a_end = lax.fori_loop(0, clen, body, (zeros, zeros + 1.0), unroll=4)

    a_inc, h_inc = _sublane_scan(a_end, h_end, row)
    h_prev_tile = hcar[...]
    h_chunk_end = a_inc * h_prev_tile + h_inc
    h_chunk_start = jnp.where(row == 0, h_prev_tile, pltpu.roll(h_chunk_end, 1, 0))
    hcar[...] = jnp.broadcast_to(h_chunk_end[SUBLANE - 1:SUBLANE, :], (SUBLANE, LRU_WIDTH))

    h_true = b_s[...] + a_s[...] * jnp.concatenate([h_chunk_start] * clen, axis=0)
    gp = _dot(perm[...], g_ref[...])
    out_p = (h_true * _silu(gp)).astype(BF16)
    o_ref[...] = _dot(perm_t[...], out_p).astype(o_ref.dtype)


def _lru_call(proj, cw, cb, wa, ba, wx, bx, lam, layer, batch, seq, ada=None):
    m = proj.shape[0]
    nt = seq // LRU_T
    w = LRU_WIDTH

    def col(c):
        return pl.BlockSpec((LRU_T, w), lambda b, t, c=c: (b * nt + t, c // w))

    def vec():
        return _vec_spec(w, layer, 2)

    def blk():
        return pl.BlockSpec((1, LRU_BLOCKS, LRU_BLOCK_DIM, LRU_BLOCK_DIM), lambda b, t: (layer, 0, 0, 0))

    in_specs = [col(COL_LX), col(COL_LG),
                pl.BlockSpec((1, CONV_WIDTH, w), lambda b, t: (layer, 0, 0)), vec(),
                blk(), vec(), blk(), vec(), vec()]
    args = [proj, proj, cw, _layer_vec(cb), wa, _layer_vec(ba), wx, _layer_vec(bx), _layer_vec(lam)]
    out_specs = pl.BlockSpec((LRU_T, w), lambda b, t: (b * nt + t, 0))
    out_shape = jax.ShapeDtypeStruct((m, w), BF16)
    if ada is not None:
        c_pad, ada_w, ada_b = ada
        rows = c_pad.shape[0]
        _, d, n = ada_w.shape
        steps = batch * nt
        slab = d // steps
        assert slab * steps == d and slab % LANE == 0
        in_specs += [pl.BlockSpec((1, rows, slab), lambda b, t: (b * nt + t, 0, 0)),
                     pl.BlockSpec((1, slab, n), lambda b, t: (layer + 1, b * nt + t, 0)),
                     pl.BlockSpec((1, 1, n), lambda b, t: (layer + 1, 0, 0))]
        args += [_c_slabs(c_pad, steps), ada_w, _layer_vec(ada_b)]
        out_specs = (out_specs, pl.BlockSpec((1, rows, n), lambda b, t: (0, 0, 0)))
        out_shape = (out_shape, jax.ShapeDtypeStruct((1, rows, n), F32))
    return pl.pallas_call(
        functools.partial(_lru_kernel, with_ada=ada is not None),
        grid=(batch, nt),
        in_specs=in_specs,
        out_specs=out_specs,
        out_shape=out_shape,
        scratch_shapes=[
            pltpu.VMEM((LRU_T, LRU_T), BF16),
            pltpu.VMEM((LRU_T, LRU_T), BF16),
            pltpu.VMEM(((CONV_WIDTH - 1) * SUBLANE, w), F32),
            pltpu.VMEM((SUBLANE, w), F32),
            pltpu.VMEM((LRU_T, w), F32),
            pltpu.VMEM((LRU_T, w), F32),
        ],
        compiler_params=_params("arbitrary", "arbitrary"),
        name="rg_lru",
    )(*args)


def _rope64(x, cos_t, sin_t):
    swapped = pltpu.roll(x, MLA_ROPE // 2, 1) + pltpu.roll(x, LANE - MLA_ROPE // 2, 1)
    return x * cos_t + swapped * sin_t


def _mla_kernel(h_ref, wkr_ref, mq_ref, mkv_ref, mg_ref, wq_ref, wkv_ref, qn_ref, kvn_ref,
                cos_ref, sin_ref, o_ref, qlat, kvlat, krope, wq_s, qs, ks, vs, s_scr, p_scr, *, seq):
    rows = 256
    pair = pl.program_id(1)

    @pl.when(pair == 0)
    def _():
        wkr = jnp.concatenate([wkr_ref[0], jnp.zeros((LANE - MLA_ROPE, D_MODEL), F32)], axis=0).astype(BF16)
        for i in range(seq // rows):
            rs = slice(i * rows, (i + 1) * rows)
            qlat[rs, :] = _rms(mq_ref[rs, :].astype(F32), qn_ref[0]).astype(BF16)
            kvlat[rs, :] = _rms(mkv_ref[rs, :].astype(F32), kvn_ref[0]).astype(BF16)
            kr = _dot_nt(h_ref[rs, :], wkr)
            krope[rs, :] = _rope64(kr, cos_ref[rs, :], sin_ref[rs, :]).astype(BF16)
        for hh in range(MLA_HPS):
            vs[hh, :, MLA_V:] = jnp.ones((seq, MLA_V), BF16)
        zeros = jnp.zeros((MLA_LORA, MLA_QK_PAD - MLA_QK), F32)
        wq_all = wq_ref[0]
        for hh in range(MLA_HEADS):
            w_head = wq_all[:, hh * MLA_QK:(hh + 1) * MLA_QK]
            wq_s[hh] = jnp.concatenate([w_head, zeros], axis=1).astype(BF16)

    scale = (MLA_QK ** -0.5) * math.log2(math.e)
    tq = MLA_TQ
    ri = lax.broadcasted_iota(jnp.int32, (tq, tq), 0)
    ci = lax.broadcasted_iota(jnp.int32, (tq, tq), 1)
    visible = (ci // CHUNK) <= (ri // CHUNK)
    n_q = seq // tq
    kv_w = MLA_NOPE + MLA_V

    for hh in range(MLA_HPS):
        wq = wq_s[pair * MLA_HPS + hh]
        wkv = wkv_ref[0, :, hh * kv_w:(hh + 1) * kv_w].astype(BF16)
        for i in range(seq // rows):
            rs = slice(i * rows, (i + 1) * rows)
            q = _dot(qlat[rs, :], wq)
            qs[hh, rs, 0:MLA_NOPE] = (q[:, 0:MLA_NOPE] * scale).astype(BF16)
            qs[hh, rs, MLA_NOPE:] = (_rope64(q[:, MLA_NOPE:], cos_ref[rs, :], sin_ref[rs, :]) * scale).astype(BF16)
            kv = _dot(kvlat[rs, :], wkv)
            ks[hh, rs, 0:MLA_NOPE] = kv[:, 0:MLA_NOPE].astype(BF16)
            ks[hh, rs, MLA_NOPE:] = krope[rs, :]
            vs[hh, rs, 0:MLA_V] = kv[:, MLA_NOPE:].astype(BF16)

    def scores(hh, i):
        q0, q1 = i * tq, (i + 1) * tq
        s_scr[hh, i % 2, :, 0:q1] = _dot_nt(qs[hh, q0:q1, :], ks[hh, 0:q1, :])

    def probs(hh, i):
        q0, q1 = i * tq, (i + 1) * tq
        s_d = jnp.where(visible, s_scr[hh, i % 2, :, q0:q1], -1e30)
        m = jnp.max(s_d, axis=-1, keepdims=True)
        if i > 0:
            m = jnp.maximum(m, jnp.max(s_scr[hh, i % 2, :, 0:q0], axis=-1, keepdims=True))
            p_scr[hh, i % 2, :, 0:q0] = jnp.exp2(s_scr[hh, i % 2, :, 0:q0] - m).astype(BF16)
        p_scr[hh, i % 2, :, q0:q1] = jnp.exp2(s_d - m).astype(BF16)

    def values(hh, i):
        q0, q1 = i * tq, (i + 1) * tq
        cols = slice(hh * MLA_V, (hh + 1) * MLA_V)
        acc = _dot(p_scr[hh, i % 2, :, 0:q1], vs[hh, 0:q1, :])
        out = acc[:, 0:MLA_V] / acc[:, MLA_V:]
        o_ref[q0:q1, cols] = (out * _silu(mg_ref[q0:q1, cols].astype(F32))).astype(o_ref.dtype)

    for hh in range(MLA_HPS):
        scores(hh, 0)
    for i in range(n_q + 1):
        for hh in range(MLA_HPS):
            if i + 1 < n_q:
                scores(hh, i + 1)
            if i < n_q:
                probs(hh, i)
            if i > 0:
                values(hh, i - 1)


def _mla_call(h, w_in_t, proj, w_uq, w_ukv, q_norm, kv_norm, cos_m, sin_m, layer, batch, seq):
    m, d = h.shape
    lora = MLA_LORA
    hps = MLA_HPS
    assert MLA_HEADS % hps == 0 and COL_MG % (hps * MLA_V) == 0
    return pl.pallas_call(
        functools.partial(_mla_kernel, seq=seq),
        grid=(batch, MLA_HEADS // hps),
        in_specs=[
            pl.BlockSpec((seq, d), lambda b, hd: (b, 0)),
            pl.BlockSpec((1, MLA_ROPE, d), lambda b, hd: (layer, SRC_MKR // MLA_ROPE, 0)),
            pl.BlockSpec((seq, lora), lambda b, hd: (b, COL_MQ // lora), pipeline_mode=pl.Buffered(1)),
            pl.BlockSpec((seq, lora), lambda b, hd: (b, COL_MKV // lora), pipeline_mode=pl.Buffered(1)),
            pl.BlockSpec((seq, hps * MLA_V), lambda b, hd: (b, COL_MG // (hps * MLA_V) + hd)),
            pl.BlockSpec((1, lora, MLA_HEADS * MLA_QK), lambda b, hd: (layer, 0, 0),
                         pipeline_mode=pl.Buffered(1)),
            pl.BlockSpec((1, lora, hps * (MLA_NOPE + MLA_V)), lambda b, hd: (layer, 0, hd)),
            _vec_spec(lora, layer, 2),
            _vec_spec(lora, layer, 2),
            pl.BlockSpec((seq, LANE), lambda b, hd: (b, 0), pipeline_mode=pl.Buffered(1)),
            pl.BlockSpec((seq, LANE), lambda b, hd: (b, 0), pipeline_mode=pl.Buffered(1)),
        ],
        out_specs=pl.BlockSpec((seq, hps * MLA_V), lambda b, hd: (b, hd)),
        out_shape=jax.ShapeDtypeStruct((m, MLA_WIDTH), BF16),
        scratch_shapes=[
            pltpu.VMEM((seq, lora), BF16),
            pltpu.VMEM((seq, lora), BF16),
            pltpu.VMEM((seq, LANE), BF16),
            pltpu.VMEM((MLA_HEADS, lora, MLA_QK_PAD), BF16),
            pltpu.VMEM((hps, seq, MLA_QK_PAD), BF16),
            pltpu.VMEM((hps, seq, MLA_QK_PAD), BF16),
            pltpu.VMEM((hps, seq, 2 * MLA_V), BF16),
            pltpu.VMEM((hps, 2, MLA_TQ, seq), F32),
            pltpu.VMEM((hps, 2, MLA_TQ, seq), BF16),
        ],
        compiler_params=_params("arbitrary", "arbitrary"),
        name="mla",
    )(h, w_in_t, proj, proj, proj, w_uq, w_ukv, _layer_vec(q_norm), _layer_vec(kv_norm), cos_m, sin_m)


def _merge_kernel(yr_ref, yl_ref, ym_ref, l0_ref, l1_ref, l2_ref, x_ref, res_ref, g_ref,
                  wb_ref, wo_ref, *rest, tiles_per_batch, emit_next):
    b = pl.program_id(0) // tiles_per_batch
    merged = None
    for i, (y_ref, l_ref) in enumerate(((yr_ref, l0_ref), (yl_ref, l1_ref), (ym_ref, l2_ref))):
        z = _dot(y_ref[...], wb_ref[i * 1024:(i + 1) * 1024, :])
        z = z * jax.nn.sigmoid(l_ref[...].astype(F32))
        merged = z if merged is None else merged + z
    y = _dot(merged.astype(BF16), wo_ref[...])
    post = g_ref[0] * (1.0 + res_ref[0, pl.ds(b, 1), :])
    inv = lax.rsqrt(jnp.mean(y * y, axis=-1, keepdims=True) + NORM_EPS)
    x_new = x_ref[...] + (y * inv) * post
    if emit_next:
        gn_ref, shift_ref, scale_ref, o_ref, h_ref = rest
        pre = gn_ref[0] * (1.0 + scale_ref[0, pl.ds(b, 1), :])
        inv_n = lax.rsqrt(jnp.mean(x_new * x_new, axis=-1, keepdims=True) + NORM_EPS)
        h_ref[...] = ((x_new * inv_n) * pre + shift_ref[0, pl.ds(b, 1), :]).astype(h_ref.dtype)
    else:
        (o_ref,) = rest
    o_ref[...] = x_new


def _merge_call(y_ret, y_lru, y_mla, proj, x2, mod, mod_next, norm_post, norm_pre, wb, wo, layer, seq):
    m, d = x2.shape
    tm = MERGE_TM
    mod_rows = mod.shape[1]
    emit_next = mod_next is not None

    def rows(width, c=0):
        return pl.BlockSpec((tm, width), lambda i, c=c: (i, c))

    def resident(shape):
        return pl.BlockSpec(shape, lambda i: (0, 0), pipeline_mode=pl.Buffered(1))

    def mod_part(part):
        return pl.BlockSpec((1, mod_rows, d), lambda i: (0, 0, part))

    merge_col = COL_MERGE // d
    in_specs = [rows(1024), rows(1024), rows(1024),
                rows(d, merge_col), rows(d, merge_col + 1), rows(d, merge_col + 2),
                rows(d), mod_part(2), _vec_spec(d, layer, 1),
                resident(wb.shape), resident(wo.shape)]
    args = [y_ret, y_lru, y_mla, proj, proj, proj, x2, mod, _layer_vec(norm_post), wb, wo]
    out_specs = rows(d)
    out_shape = jax.ShapeDtypeStruct((m, d), F32)
    if emit_next:
        in_specs += [_vec_spec(d, layer + 1, 1), mod_part(0), mod_part(1)]
        args += [_layer_vec(norm_pre), mod_next, mod_next]
        out_specs = (out_specs, rows(d))
        out_shape = (out_shape, jax.ShapeDtypeStruct((m, d), BF16))
    return pl.pallas_call(
        functools.partial(_merge_kernel, tiles_per_batch=seq // tm, emit_next=emit_next),
        grid=(m // tm,),
        in_specs=in_specs,
        out_specs=out_specs,
        out_shape=out_shape,
        compiler_params=_params("arbitrary"),
        name="merge_out",
    )(*args)


def kernel(x, c, positions, ada_w, ada_b, norm_pre, norm_post, w_in, ret_gn, lru_conv_w, lru_conv_b,
           lru_wa, lru_ba, lru_wx, lru_bx, lru_lambda, mla_q_norm, mla_w_uq, mla_kv_norm, mla_w_ukv,
           w_branch, w_out):
    batch, seq, d = x.shape
    depth = w_in.shape[0]
    m = batch * seq
    assert d == D_MODEL and w_in.shape[2] == SRC_WIDTH and batch <= SUBLANE
    assert seq % PROJ_TM == 0 and seq % (RET_SUB * RET_T) == 0 and seq % LRU_T == 0 and seq % MLA_TQ == 0
    assert seq % MERGE_TM == 0 and seq % NORM_TM == 0

    c_pad = jnp.pad(c, ((0, SUBLANE - batch), (0, 0)))
    mod = _ada_call(c_pad, ada_w, ada_b, 0)
    w_in_t = jnp.swapaxes(w_in, 1, 2)

    x2 = x.reshape(m, d)
    h, cos_r, sin_r, cos_m, sin_m = _prenorm_call(x2, mod, norm_pre, positions, 0, seq)
    for l in range(depth):
        proj = _inproj_call(h, w_in_t, l)
        y_ret, wb, wo = _ret_call(proj, cos_r, sin_r, ret_gn, w_branch, w_out, l, batch, seq)
        lru_args = (proj, lru_conv_w, lru_conv_b, lru_wa, lru_ba, lru_wx, lru_bx, lru_lambda, l, batch, seq)
        if l + 1 < depth:
            y_lru, mod_next = _lru_call(*lru_args, ada=(c_pad, ada_w, ada_b))
        else:
            y_lru, mod_next = _lru_call(*lru_args), None
        y_mla = _mla_call(h, w_in_t, proj, mla_w_uq, mla_w_ukv, mla_q_norm, mla_kv_norm, cos_m, sin_m,
                          l, batch, seq)
        out = _merge_call(y_ret, y_lru, y_mla, proj, x2, mod, mod_next, norm_post, norm_pre, wb, wo, l, seq)
        mod = mod_next
        if l + 1 < depth:
            x2, h = out
        else:
            x2 = out
    return x2.reshape(batch, seq, d)
```

```python
import functools
import math

import jax
import jax.numpy as jnp
from jax import lax
from jax.experimental import pallas as pl
from jax.experimental.pallas import tpu as pltpu

F32 = jnp.float32
BF16 = jnp.bfloat16

D_MODEL = 2048
CHUNK = 64
ROPE_BASE = 10000.0
NORM_EPS = 1e-6

RET_HEADS = 8
RET_DIM = 128
RET_WIDTH = RET_HEADS * RET_DIM

LRU_WIDTH = 1024
LRU_BLOCKS = 8
LRU_BLOCK_DIM = 128
CONV_WIDTH = 4
LRU_C = 8.0

MLA_HEADS = 8
MLA_NOPE = 128
MLA_ROPE = 64
MLA_V = 128
MLA_LORA = 512
MLA_WIDTH = MLA_HEADS * MLA_V
MLA_QK = MLA_NOPE + MLA_ROPE
MLA_QK_PAD = 256

LANE = 128
SUBLANE = 8
VMEM_LIMIT = 58 * 1024 * 1024

SRC_MKR = 4 * 1024 + 2 * 1024 + 2 * MLA_LORA
SRC_WIDTH = SRC_MKR + MLA_ROPE + MLA_WIDTH + 3 * D_MODEL
COL_RQ = 0
COL_RK = 1024
COL_RV = 2048
COL_RG = 3072
COL_LX = 4096
COL_LG = 5120
COL_MQ = 6144
COL_MKV = COL_MQ + MLA_LORA
COL_MG = SRC_MKR
COL_MERGE = COL_MG + MLA_WIDTH
PROJ_WIDTH = COL_MERGE + 3 * D_MODEL
PROJ_TN = 1024
PROJ_TM = 2048
ALIGNED_TILES = SRC_MKR // PROJ_TN
CAST_ROWS = 64
assert SRC_MKR % PROJ_TN == 0 and PROJ_WIDTH % PROJ_TN == 0 and COL_MERGE % D_MODEL == 0

NORM_TM = 512
RET_T = 256
RET_SUB = 1
LRU_T = 512
LRU_CLEN = LRU_T // 8
MLA_TQ = 256
MLA_HPS = 2
MERGE_TM = 256
ADA_TK = 256


def _silu(v):
    return v * jax.nn.sigmoid(v)


def _dot(a, b):
    return jnp.dot(a, b, preferred_element_type=F32)


def _dot_nt(a, b):
    return lax.dot_general(a, b, (((1,), (1,)), ((), ())), preferred_element_type=F32)


def _dot_tn(a, b):
    return lax.dot_general(a, b, (((0,), (0,)), ((), ())), preferred_element_type=F32)


def _rms(x, gain):
    return x * lax.rsqrt(jnp.mean(x * x, axis=-1, keepdims=True) + NORM_EPS) * gain


def _params(*sem):
    return pltpu.CompilerParams(dimension_semantics=sem, vmem_limit_bytes=VMEM_LIMIT)


VEC_ADA_B = 0
VEC_NORM_PRE = 3 * D_MODEL
VEC_NORM_POST = VEC_NORM_PRE + D_MODEL
VEC_RET_GN = VEC_NORM_POST + D_MODEL
VEC_CONV_B = VEC_RET_GN + RET_WIDTH
VEC_BA = VEC_CONV_B + LRU_WIDTH
VEC_BX = VEC_BA + LRU_WIDTH
VEC_LAM = VEC_BX + LRU_WIDTH
VEC_Q_NORM = VEC_LAM + LRU_WIDTH
VEC_KV_NORM = VEC_Q_NORM + MLA_LORA


def _pack_vectors(ada_b, norm_pre, norm_post, ret_gn, conv_b, ba, bx, lam, q_norm, kv_norm):
    packed = jnp.concatenate([ada_b, norm_pre, norm_post, ret_gn, conv_b, ba, bx, lam, q_norm, kv_norm], axis=1)
    return packed[:, None, :]


def _vec_spec(width, layer, ngrid, col):
    assert col % width == 0
    if ngrid == 1:
        return pl.BlockSpec((1, 1, width), lambda i: (layer, 0, col // width))
    return pl.BlockSpec((1, 1, width), lambda i, j: (layer, 0, col // width))


def _ada_accumulate(first, c_ref, w_ref, b_ref, o_ref):
    @pl.when(first)
    def _():
        o_ref[0] = jnp.broadcast_to(b_ref[0], o_ref.shape[1:])

    c_act = _silu(c_ref[0]).astype(BF16)
    o_ref[0] += _dot(c_act, w_ref[0].astype(BF16))


def _ada_kernel(c_ref, w_ref, b_ref, o_ref):
    _ada_accumulate(pl.program_id(0) == 0, c_ref, w_ref, b_ref, o_ref)


def _c_slabs(c_pad, n_slabs):
    rows, d = c_pad.shape
    return c_pad.reshape(rows, n_slabs, d // n_slabs).transpose(1, 0, 2)


def _ada_call(c_pad, ada_w, vecs, layer):
    _, d, n = ada_w.shape
    rows = c_pad.shape[0]
    n_slabs = d // ADA_TK
    return pl.pallas_call(
        _ada_kernel,
        grid=(n_slabs,),
        in_specs=[
            pl.BlockSpec((1, rows, ADA_TK), lambda k: (k, 0, 0)),
            pl.BlockSpec((1, ADA_TK, n), lambda k: (layer, k, 0)),
            _vec_spec(n, layer, 1, VEC_ADA_B),
        ],
        out_specs=pl.BlockSpec((1, rows, n), lambda k: (0, 0, 0)),
        out_shape=jax.ShapeDtypeStruct((1, rows, n), F32),
        compiler_params=_params("arbitrary"),
        name="ada_mod",
    )(_c_slabs(c_pad, n_slabs), ada_w, vecs)


def _modulate(x, gain, shift_ref, scale_ref, b):
    shift = shift_ref[0, pl.ds(b, 1), :]
    scale = scale_ref[0, pl.ds(b, 1), :]
    return _rms(x, gain) * (1.0 + scale) + shift


def _prenorm_kernel(x_ref, g_ref, shift_ref, scale_ref, pos_ref, freq_ref,
                    o_ref, cos_r_ref, sin_r_ref, cos_m_ref, sin_m_ref, *, tiles_per_batch):
    b = pl.program_id(0) // tiles_per_batch
    o_ref[...] = _modulate(x_ref[...], g_ref[0], shift_ref, scale_ref, b).astype(o_ref.dtype)

    ang = freq_ref[...] * pos_ref[0].astype(F32)
    cos_t = jnp.cos(ang)
    sin_t = jnp.sin(ang)
    n_r, n_m = RET_DIM // 2, MLA_ROPE // 2
    c_r, s_r = cos_t[0:n_r], sin_t[0:n_r]
    c_m, s_m = cos_t[n_r:n_r + n_m], sin_t[n_r:n_r + n_m]
    pad = jnp.zeros((LANE - MLA_ROPE, ang.shape[1]), F32)
    cos_r_ref[...] = jnp.concatenate([c_r, c_r], axis=0).T
    sin_r_ref[...] = jnp.concatenate([-s_r, s_r], axis=0).T
    cos_m_ref[...] = jnp.concatenate([c_m, c_m, pad], axis=0).T
    sin_m_ref[...] = jnp.concatenate([-s_m, s_m, pad], axis=0).T


def _prenorm_call(x2, mod, vecs, positions, layer, seq):
    m, d = x2.shape
    rows = mod.shape[1]
    n_tiles = m // NORM_TM

    def inv_freq(dim):
        return ROPE_BASE ** (-jnp.arange(0, dim, 2, dtype=F32) / dim)

    freqs = jnp.concatenate([inv_freq(RET_DIM), inv_freq(MLA_ROPE)])[:, None]
    table = pl.BlockSpec((NORM_TM, LANE), lambda i: (i, 0))
    return pl.pallas_call(
        functools.partial(_prenorm_kernel, tiles_per_batch=seq // NORM_TM),
        grid=(n_tiles,),
        in_specs=[
            pl.BlockSpec((NORM_TM, d), lambda i: (i, 0)),
            _vec_spec(d, layer, 1, VEC_NORM_PRE),
            pl.BlockSpec((1, rows, d), lambda i: (0, 0, 0)),
            pl.BlockSpec((1, rows, d), lambda i: (0, 0, 1)),
            pl.BlockSpec((1, 1, NORM_TM), lambda i: (i, 0, 0)),
            pl.BlockSpec(freqs.shape, lambda i: (0, 0)),
        ],
        out_specs=(pl.BlockSpec((NORM_TM, d), lambda i: (i, 0)), table, table, table, table),
        out_shape=(jax.ShapeDtypeStruct((m, d), BF16),) + (jax.ShapeDtypeStruct((m, LANE), F32),) * 4,
        compiler_params=_params("arbitrary"),
        name="pre_norm",
    )(x2, vecs, mod, mod, positions.reshape(n_tiles, 1, NORM_TM), freqs)


def _inproj_kernel(h_ref, w_ref, wx_ref, o_ref, wb_ref):
    j = pl.program_id(0)
    first_row_tile = pl.program_id(1) == 0
    n_chunks = PROJ_TN // CAST_ROWS

    @pl.when(first_row_tile & (j < ALIGNED_TILES))
    def _():
        def body(i, carry):
            r0 = pl.multiple_of(i * CAST_ROWS, CAST_ROWS)
            wb_ref[pl.ds(r0, CAST_ROWS), :] = w_ref[0, pl.ds(r0, CAST_ROWS), :].astype(BF16)
            return carry

        lax.fori_loop(0, n_chunks, body, 0)

    @pl.when(first_row_tile & (j >= ALIGNED_TILES))
    def _():
        def body(i, carry):
            r0 = pl.multiple_of(i * CAST_ROWS, CAST_ROWS)
            wb_ref[pl.ds(r0, CAST_ROWS), :] = w_ref[0, pl.ds(r0 + MLA_ROPE, CAST_ROWS), :].astype(BF16)
            return carry

        lax.fori_loop(0, n_chunks - 1, body, 0)
        wb_ref[PROJ_TN - MLA_ROPE:, :] = wx_ref[0].astype(BF16)

    o_ref[...] = _dot_nt(h_ref[...], wb_ref[...]).astype(o_ref.dtype)


def _inproj_call(h, w_in_t, layer):
    m, d = h.shape
    per_tile = PROJ_TN // MLA_ROPE
    last_block = SRC_WIDTH // MLA_ROPE - 1
    return pl.pallas_call(
        _inproj_kernel,
        grid=(PROJ_WIDTH // PROJ_TN, m // PROJ_TM),
        in_specs=[
            pl.BlockSpec((PROJ_TM, d), lambda j, i: (i, 0)),
            pl.BlockSpec((1, PROJ_TN, d), lambda j, i: (layer, j, 0)),
            pl.BlockSpec((1, MLA_ROPE, d),
                         lambda j, i: (layer, jnp.minimum((j + 1) * per_tile, last_block), 0)),
        ],
        out_specs=pl.BlockSpec((PROJ_TM, PROJ_TN), lambda j, i: (i, j)),
        out_shape=jax.ShapeDtypeStruct((m, PROJ_WIDTH), BF16),
        scratch_shapes=[pltpu.VMEM((PROJ_TN, d), BF16)],
        compiler_params=_params("arbitrary", "arbitrary"),
        name="in_proj",
    )(h, w_in_t, w_in_t)


_LOG_GAMMA = tuple(math.log1p(-(2.0 ** (-5.0 - h))) for h in range(RET_HEADS))


def _ret_kernel(q_ref, k_ref, v_ref, g_ref, cos_ref, sin_ref, gn_ref, wb_ref, wo_ref,
                o_ref, wb_out, wo_out, state_ref, dmat_ref, dq_ref, dk_ref):
    t_blk = RET_T
    wb_out[...] = wb_ref[0].astype(BF16)
    wo_out[...] = wo_ref[0].astype(BF16)

    @pl.when((pl.program_id(0) == 0) & (pl.program_id(1) == 0))
    def _():
        ri = lax.broadcasted_iota(jnp.int32, (t_blk, t_blk), 0)
        ci = lax.broadcasted_iota(jnp.int32, (t_blk, t_blk), 1)
        dist = jnp.abs(ri - ci).astype(F32)
        visible = (ci // CHUNK) <= (ri // CHUNK)
        row = lax.broadcasted_iota(jnp.int32, (t_blk, RET_DIM), 0).astype(F32)
        for h in range(RET_HEADS):
            lg = _LOG_GAMMA[h]
            dmat_ref[h] = jnp.where(visible, jnp.exp(lg * dist), 0.0)
            dq_ref[h] = jnp.exp(lg * (row + 1.0))
            dk_ref[h] = jnp.exp(lg * ((t_blk - 1.0) - row))

    @pl.when(pl.program_id(1) == 0)
    def _():
        state_ref[...] = jnp.zeros_like(state_ref)

    for sub in range(RET_SUB):
        rs = slice(sub * t_blk, (sub + 1) * t_blk)
        cos = cos_ref[rs, :]
        sin = sin_ref[rs, :]
        for h in range(RET_HEADS):
            sl = slice(h * RET_DIM, (h + 1) * RET_DIM)
            q = q_ref[rs, sl].astype(F32)
            k = k_ref[rs, sl].astype(F32)
            q = (q * cos + pltpu.roll(q, RET_DIM // 2, 1) * sin) * (RET_DIM ** -0.5)
            k = k * cos + pltpu.roll(k, RET_DIM // 2, 1) * sin
            v = v_ref[rs, sl]
            scores = _dot_nt(q.astype(BF16), k.astype(BF16)) * dmat_ref[h]
            o = _dot(scores.astype(BF16), v)
            state = state_ref[h]
            o = o + _dot((q * dq_ref[h]).astype(BF16), state.astype(BF16))
            k_dec = (k * dk_ref[h]).astype(BF16)
            state_ref[h] = state * math.exp(_LOG_GAMMA[h] * t_blk) + _dot_tn(k_dec, v)
            mean = jnp.mean(o, axis=-1, keepdims=True)
            cen = o - mean
            var = jnp.mean(cen * cen, axis=-1, keepdims=True)
            normed = cen * lax.rsqrt(var + NORM_EPS) * gn_ref[0, :, sl]
            o_ref[rs, sl] = (normed * _silu(g_ref[rs, sl].astype(F32))).astype(o_ref.dtype)


def _ret_call(proj, cos_r, sin_r, vecs, w_branch, w_out, layer, batch, seq):
    m = proj.shape[0]
    rows = RET_SUB * RET_T
    nt = seq // rows
    w = RET_WIDTH
    steps = batch * nt
    wb_rows, wo_rows = w_branch.shape[1] // steps, w_out.shape[1] // steps
    bf16_rows = 2 * SUBLANE
    assert wb_rows * steps == w_branch.shape[1] and wb_rows % bf16_rows == 0
    assert wo_rows * steps == w_out.shape[1] and wo_rows % bf16_rows == 0
    d = w_branch.shape[2]

    def col(c):
        return pl.BlockSpec((rows, w), lambda b, t, c=c: (b * nt + t, c // w))

    tab = pl.BlockSpec((rows, RET_DIM), lambda b, t: (b * nt + t, 0))
    return pl.pallas_call(
        _ret_kernel,
        grid=(batch, nt),
        in_specs=[col(COL_RQ), col(COL_RK), col(COL_RV), col(COL_RG), tab, tab,
                  _vec_spec(w, layer, 2, VEC_RET_GN),
                  pl.BlockSpec((1, wb_rows, d), lambda b, t: (layer, b * nt + t, 0)),
                  pl.BlockSpec((1, wo_rows, d), lambda b, t: (layer, b * nt + t, 0))],
        out_specs=(pl.BlockSpec((rows, w), lambda b, t: (b * nt + t, 0)),
                   pl.BlockSpec((wb_rows, d), lambda b, t: (b * nt + t, 0)),
                   pl.BlockSpec((wo_rows, d), lambda b, t: (b * nt + t, 0))),
        out_shape=(jax.ShapeDtypeStruct((m, w), BF16),
                   jax.ShapeDtypeStruct(w_branch.shape[1:], BF16),
                   jax.ShapeDtypeStruct(w_out.shape[1:], BF16)),
        scratch_shapes=[
            pltpu.VMEM((RET_HEADS, RET_DIM, RET_DIM), F32),
            pltpu.VMEM((RET_HEADS, RET_T, RET_T), F32),
            pltpu.VMEM((RET_HEADS, RET_T, RET_DIM), F32),
            pltpu.VMEM((RET_HEADS, RET_T, RET_DIM), F32),
        ],
        compiler_params=_params("arbitrary", "arbitrary"),
        name="retention",
    )(proj, proj, proj, proj, cos_r, sin_r, vecs, w_branch, w_out)


def _sublane_scan(a, b, row):
    for s in (1, 2, 4):
        keep = row >= s
        a_prev = jnp.where(keep, pltpu.roll(a, s, 0), 1.0)
        b_prev = jnp.where(keep, pltpu.roll(b, s, 0), 0.0)
        b = a * b_prev + b
        a = a * a_prev
    return a, b


def _lru_kernel(x_ref, g_ref, cw_ref, cb_ref, wa_ref, ba_ref, wx_ref, bx_ref, lam_ref, *rest, with_ada):
    if with_ada:
        c_ref, aw_ref, ab_ref, o_ref, mod_ref, perm, perm_t, halo, hcar, a_s, b_s = rest
        _ada_accumulate((pl.program_id(0) == 0) & (pl.program_id(1) == 0), c_ref, aw_ref, ab_ref, mod_ref)
    else:
        o_ref, perm, perm_t, halo, hcar, a_s, b_s = rest
    t_blk = LRU_T
    clen = LRU_CLEN
    taps = CONV_WIDTH - 1
    row = lax.broadcasted_iota(jnp.int32, (SUBLANE, LRU_WIDTH), 0)

    @pl.when((pl.program_id(0) == 0) & (pl.program_id(1) == 0))
    def _():
        r = lax.broadcasted_iota(jnp.int32, (t_blk, t_blk), 0)
        c = lax.broadcasted_iota(jnp.int32, (t_blk, t_blk), 1)
        perm[...] = jnp.where(c == (r % SUBLANE) * clen + r // SUBLANE, 1.0, 0.0).astype(BF16)
        perm_t[...] = jnp.where(c == (r % clen) * SUBLANE + r // clen, 1.0, 0.0).astype(BF16)

    @pl.when(pl.program_id(1) == 0)
    def _():
        halo[...] = jnp.zeros_like(halo)
        hcar[...] = jnp.zeros_like(hcar)

    xp = _dot(perm[...], x_ref[...])
    pieces = []
    for k in range(taps):
        cur = xp[t_blk - (taps - k) * SUBLANE:t_blk - (taps - k - 1) * SUBLANE, :]
        prev = halo[k * SUBLANE:(k + 1) * SUBLANE, :]
        pieces.append(jnp.where(row == 0, pltpu.roll(prev, 1, 0), pltpu.roll(cur, 1, 0)))
    halo[...] = xp[t_blk - taps * SUBLANE:, :]
    xext = jnp.concatenate(pieces + [xp], axis=0)
    xc = cb_ref[0]
    for w in range(CONV_WIDTH):
        xc = xc + cw_ref[0, w:w + 1, :] * xext[w * SUBLANE:w * SUBLANE + t_blk, :]

    neg_lam = -lam_ref[0]
    softplus = jnp.maximum(neg_lam, 0.0) + jnp.log1p(jnp.exp(-jnp.abs(neg_lam)))
    neg_half_rate = (-0.5 * LRU_C) * softplus
    for n in range(LRU_BLOCKS):
        sl = slice(n * LRU_BLOCK_DIM, (n + 1) * LRU_BLOCK_DIM)
        xn = xc[:, sl]
        xn_b = xn.astype(BF16)
        t_r = jnp.tanh(0.5 * (_dot(xn_b, wa_ref[0, n].astype(BF16)) + ba_ref[0, :, sl]))
        t_i = jnp.tanh(0.5 * (_dot(xn_b, wx_ref[0, n].astype(BF16)) + bx_ref[0, :, sl]))
        half_rate = neg_half_rate[:, sl]
        log_a = half_rate * t_r + half_rate
        a = jnp.exp(log_a)
        a_s[:, sl] = a
        var = jnp.tanh(-log_a) * (a * a + 1.0)
        std = jnp.where(var > 0.0, var * lax.rsqrt(var), 0.0)
        half_x = 0.5 * xn
        b_s[:, sl] = std * (half_x * t_i + half_x)

    def body(j, carry):
        h, prod = carry
        r0 = pl.multiple_of(j * SUBLANE, SUBLANE)
        a = a_s[pl.ds(r0, SUBLANE), :]
        h = a * h + b_s[pl.ds(r0, SUBLANE), :]
        prod = a * prod
        b_s[pl.ds(r0, SUBLANE), :] = h
        a_s[pl.ds(r0, SUBLANE), :] = prod
        return h, prod

    zeros = jnp.zeros((SUBLANE, LRU_WIDTH), F32)
    h_end, a_end = lax.fori_loop(0, clen, body, (zeros, zeros + 1.0), unroll=4)

    a_inc, h_inc = _sublane_scan(a_end, h_end, row)
    h_prev_tile = hcar[...]
    h_chunk_end = a_inc * h_prev_tile + h_inc
    h_chunk_start = jnp.where(row == 0, h_prev_tile, pltpu.roll(h_chunk_end, 1, 0))
    hcar[...] = jnp.broadcast_to(h_chunk_end[SUBLANE - 1:SUBLANE, :], (SUBLANE, LRU_WIDTH))

    h_true = b_s[...] + a_s[...] * jnp.concatenate([h_chunk_start] * clen, axis=0)
    gp = _dot(perm[...], g_ref[...])
    out_p = (h_true * _silu(gp)).astype(BF16)
    o_ref[...] = _dot(perm_t[...], out_p).astype(o_ref.dtype)


def _lru_call(proj, cw, wa, wx, vecs, layer, batch, seq, ada=None):
    m = proj.shape[0]
    nt = seq // LRU_T
    w = LRU_WIDTH

    def col(c):
        return pl.BlockSpec((LRU_T, w), lambda b, t, c=c: (b * nt + t, c // w))

    def vec(offset):
        return _vec_spec(w, layer, 2, offset)

    def blk():
        return pl.BlockSpec((1, LRU_BLOCKS, LRU_BLOCK_DIM, LRU_BLOCK_DIM), lambda b, t: (layer, 0, 0, 0))

    in_specs = [col(COL_LX), col(COL_LG),
                pl.BlockSpec((1, CONV_WIDTH, w), lambda b, t: (layer, 0, 0)), vec(VEC_CONV_B),
                blk(), vec(VEC_BA), blk(), vec(VEC_BX), vec(VEC_LAM)]
    args = [proj, proj, cw, vecs, wa, vecs, wx, vecs, vecs]
    out_specs = pl.BlockSpec((LRU_T, w), lambda b, t: (b * nt + t, 0))
    out_shape = jax.ShapeDtypeStruct((m, w), BF16)
    if ada is not None:
        c_pad, ada_w = ada
        rows = c_pad.shape[0]
        _, d, n = ada_w.shape
        steps = batch * nt
        slab = d // steps
        assert slab * steps == d and slab % LANE == 0
        in_specs += [pl.BlockSpec((1, rows, slab), lambda b, t: (b * nt + t, 0, 0)),
                     pl.BlockSpec((1, slab, n), lambda b, t: (layer + 1, b * nt + t, 0)),
                     _vec_spec(n, layer + 1, 2, VEC_ADA_B)]
        args += [_c_slabs(c_pad, steps), ada_w, vecs]
        out_specs = (out_specs, pl.BlockSpec((1, rows, n), lambda b, t: (0, 0, 0)))
        out_shape = (out_shape, jax.ShapeDtypeStruct((1, rows, n), F32))
    return pl.pallas_call(
        functools.partial(_lru_kernel, with_ada=ada is not None),
        grid=(batch, nt),
        in_specs=in_specs,
        out_specs=out_specs,
        out_shape=out_shape,
        scratch_shapes=[
            pltpu.VMEM((LRU_T, LRU_T), BF16),
            pltpu.VMEM((LRU_T, LRU_T), BF16),
            pltpu.VMEM(((CONV_WIDTH - 1) * SUBLANE, w), F32),
            pltpu.VMEM((SUBLANE, w), F32),
            pltpu.VMEM((LRU_T, w), F32),
            pltpu.VMEM((LRU_T, w), F32),
        ],
        compiler_params=_params("arbitrary", "arbitrary"),
        name="rg_lru",
    )(*args)


def _rope64(x, cos_t, sin_t):
    swapped = pltpu.roll(x, MLA_ROPE // 2, 1) + pltpu.roll(x, LANE - MLA_ROPE // 2, 1)
    return x * cos_t + swapped * sin_t


def _mla_kernel(h_ref, wkr_ref, mq_ref, mkv_ref, mg_ref, wq_ref, wkv_ref, qn_ref, kvn_ref,
                cos_ref, sin_ref, o_ref, qlat, kvlat, krope, wq_s, qs, ks, vs, s_scr, p_scr, *, seq):
    rows = 256
    pair = pl.program_id(1)

    @pl.when(pair == 0)
    def _():
        wkr = jnp.concatenate([wkr_ref[0], jnp.zeros((LANE - MLA_ROPE, D_MODEL), F32)], axis=0).astype(BF16)
        for i in range(seq // rows):
            rs = slice(i * rows, (i + 1) * rows)
            qlat[rs, :] = _rms(mq_ref[rs, :].astype(F32), qn_ref[0]).astype(BF16)
            kvlat[rs, :] = _rms(mkv_ref[rs, :].astype(F32), kvn_ref[0]).astype(BF16)
            kr = _dot_nt(h_ref[rs, :], wkr)
            krope[rs, :] = _rope64(kr, cos_ref[rs, :], sin_ref[rs, :]).astype(BF16)
        for hh in range(MLA_HPS):
            vs[hh, :, MLA_V:] = jnp.ones((seq, MLA_V), BF16)
        zeros = jnp.zeros((MLA_LORA, MLA_QK_PAD - MLA_QK), F32)
        wq_all = wq_ref[0]
        for hh in range(MLA_HEADS):
            w_head = wq_all[:, hh * MLA_QK:(hh + 1) * MLA_QK]
            wq_s[hh] = jnp.concatenate([w_head, zeros], axis=1).astype(BF16)

    scale = (MLA_QK ** -0.5) * math.log2(math.e)
    tq = MLA_TQ
    ri = lax.broadcasted_iota(jnp.int32, (tq, tq), 0)
    ci = lax.broadcasted_iota(jnp.int32, (tq, tq), 1)
    visible = (ci // CHUNK) <= (ri // CHUNK)
    n_q = seq // tq
    kv_w = MLA_NOPE + MLA_V

    for hh in range(MLA_HPS):
        wq = wq_s[pair * MLA_HPS + hh]
        wkv = wkv_ref[0, :, hh * kv_w:(hh + 1) * kv_w].astype(BF16)
        for i in range(seq // rows):
            rs = slice(i * rows, (i + 1) * rows)
            q = _dot(qlat[rs, :], wq)
            qs[hh, rs, 0:MLA_NOPE] = (q[:, 0:MLA_NOPE] * scale).astype(BF16)
            qs[hh, rs, MLA_NOPE:] = (_rope64(q[:, MLA_NOPE:], cos_ref[rs, :], sin_ref[rs, :]) * scale).astype(BF16)
            kv = _dot(kvlat[rs, :], wkv)
            ks[hh, rs, 0:MLA_NOPE] = kv[:, 0:MLA_NOPE].astype(BF16)
            ks[hh, rs, MLA_NOPE:] = krope[rs, :]
            vs[hh, rs, 0:MLA_V] = kv[:, MLA_NOPE:].astype(BF16)

    def scores(hh, i):
        q0, q1 = i * tq, (i + 1) * tq
        s_scr[hh, i % 2, :, 0:q1] = _dot_nt(qs[hh, q0:q1, :], ks[hh, 0:q1, :])

    def probs(hh, i):
        q0, q1 = i * tq, (i + 1) * tq
        s_d = jnp.where(visible, s_scr[hh, i % 2, :, q0:q1], -1e30)
        m = jnp.max(s_d, axis=-1, keepdims=True)
        if i > 0:
            m = jnp.maximum(m, jnp.max(s_scr[hh, i % 2, :, 0:q0], axis=-1, keepdims=True))
            p_scr[hh, i % 2, :, 0:q0] = jnp.exp2(s_scr[hh, i % 2, :, 0:q0] - m).astype(BF16)
        p_scr[hh, i % 2, :, q0:q1] = jnp.exp2(s_d - m).astype(BF16)

    def values(hh, i):
        q0, q1 = i * tq, (i + 1) * tq
        cols = slice(hh * MLA_V, (hh + 1) * MLA_V)
        acc = _dot(p_scr[hh, i % 2, :, 0:q1], vs[hh, 0:q1, :])
        out = acc[:, 0:MLA_V] / acc[:, MLA_V:]
        o_ref[q0:q1, cols] = (out * _silu(mg_ref[q0:q1, cols].astype(F32))).astype(o_ref.dtype)

    for hh in range(MLA_HPS):
        scores(hh, 0)
    for i in range(n_q + 1):
        for hh in range(MLA_HPS):
            if i + 1 < n_q:
                scores(hh, i + 1)
            if i < n_q:
                probs(hh, i)
            if i > 0:
                values(hh, i - 1)


def _mla_call(h, w_in_t, proj, w_uq, w_ukv, vecs, cos_m, sin_m, layer, batch, seq):
    m, d = h.shape
    lora = MLA_LORA
    hps = MLA_HPS
    assert MLA_HEADS % hps == 0 and COL_MG % (hps * MLA_V) == 0
    return pl.pallas_call(
        functools.partial(_mla_kernel, seq=seq),
        grid=(batch, MLA_HEADS // hps),
        in_specs=[
            pl.BlockSpec((seq, d), lambda b, hd: (b, 0)),
            pl.BlockSpec((1, MLA_ROPE, d), lambda b, hd: (layer, SRC_MKR // MLA_ROPE, 0)),
            pl.BlockSpec((seq, lora), lambda b, hd: (b, COL_MQ // lora), pipeline_mode=pl.Buffered(1)),
            pl.BlockSpec((seq, lora), lambda b, hd: (b, COL_MKV // lora), pipeline_mode=pl.Buffered(1)),
            pl.BlockSpec((seq, hps * MLA_V), lambda b, hd: (b, COL_MG // (hps * MLA_V) + hd)),
            pl.BlockSpec((1, lora, MLA_HEADS * MLA_QK), lambda b, hd: (layer, 0, 0),
                         pipeline_mode=pl.Buffered(1)),
            pl.BlockSpec((1, lora, hps * (MLA_NOPE + MLA_V)), lambda b, hd: (layer, 0, hd)),
            _vec_spec(lora, layer, 2, VEC_Q_NORM),
            _vec_spec(lora, layer, 2, VEC_KV_NORM),
            pl.BlockSpec((seq, LANE), lambda b, hd: (b, 0), pipeline_mode=pl.Buffered(1)),
            pl.BlockSpec((seq, LANE), lambda b, hd: (b, 0), pipeline_mode=pl.Buffered(1)),
        ],
        out_specs=pl.BlockSpec((seq, hps * MLA_V), lambda b, hd: (b, hd)),
        out_shape=jax.ShapeDtypeStruct((m, MLA_WIDTH), BF16),
        scratch_shapes=[
            pltpu.VMEM((seq, lora), BF16),
            pltpu.VMEM((seq, lora), BF16),
            pltpu.VMEM((seq, LANE), BF16),
            pltpu.VMEM((MLA_HEADS, lora, MLA_QK_PAD), BF16),
            pltpu.VMEM((hps, seq, MLA_QK_PAD), BF16),
            pltpu.VMEM((hps, seq, MLA_QK_PAD), BF16),
            pltpu.VMEM((hps, seq, 2 * MLA_V), BF16),
            pltpu.VMEM((hps, 2, MLA_TQ, seq), F32),
            pltpu.VMEM((hps, 2, MLA_TQ, seq), BF16),
        ],
        compiler_params=_params("arbitrary", "arbitrary"),
        name="mla",
    )(h, w_in_t, proj, proj, proj, w_uq, w_ukv, vecs, vecs, cos_m, sin_m)


def _merge_kernel(yr_ref, yl_ref, ym_ref, l0_ref, l1_ref, l2_ref, x_ref, res_ref, g_ref,
                  wb_ref, wo_ref, *rest, tiles_per_batch, emit_next):
    b = pl.program_id(0) // tiles_per_batch
    merged = None
    for i, (y_ref, l_ref) in enumerate(((yr_ref, l0_ref), (yl_ref, l1_ref), (ym_ref, l2_ref))):
        z = _dot(y_ref[...], wb_ref[i * 1024:(i + 1) * 1024, :])
        z = z * jax.nn.sigmoid(l_ref[...].astype(F32))
        merged = z if merged is None else merged + z
    y = _dot(merged.astype(BF16), wo_ref[...])
    post = g_ref[0] * (1.0 + res_ref[0, pl.ds(b, 1), :])
    inv = lax.rsqrt(jnp.mean(y * y, axis=-1, keepdims=True) + NORM_EPS)
    x_new = x_ref[...] + (y * inv) * post
    if emit_next:
        gn_ref, shift_ref, scale_ref, o_ref, h_ref = rest
        pre = gn_ref[0] * (1.0 + scale_ref[0, pl.ds(b, 1), :])
        inv_n = lax.rsqrt(jnp.mean(x_new * x_new, axis=-1, keepdims=True) + NORM_EPS)
        h_ref[...] = ((x_new * inv_n) * pre + shift_ref[0, pl.ds(b, 1), :]).astype(h_ref.dtype)
    else:
        (o_ref,) = rest
    o_ref[...] = x_new


def _merge_call(y_ret, y_lru, y_mla, proj, x2, mod, mod_next, vecs, wb, wo, layer, seq):
    m, d = x2.shape
    tm = MERGE_TM
    mod_rows = mod.shape[1]
    emit_next = mod_next is not None

    def rows(width, c=0):
        return pl.BlockSpec((tm, width), lambda i, c=c: (i, c))

    def resident(shape):
        return pl.BlockSpec(shape, lambda i: (0, 0), pipeline_mode=pl.Buffered(1))

    def mod_part(part):
        return pl.BlockSpec((1, mod_rows, d), lambda i: (0, 0, part))

    merge_col = COL_MERGE // d
    in_specs = [rows(1024), rows(1024), rows(1024),
                rows(d, merge_col), rows(d, merge_col + 1), rows(d, merge_col + 2),
                rows(d), mod_part(2), _vec_spec(d, layer, 1, VEC_NORM_POST),
                resident(wb.shape), resident(wo.shape)]
    args = [y_ret, y_lru, y_mla, proj, proj, proj, x2, mod, vecs, wb, wo]
    out_specs = rows(d)
    out_shape = jax.ShapeDtypeStruct((m, d), F32)
    if emit_next:
        in_specs += [_vec_spec(d, layer + 1, 1, VEC_NORM_PRE), mod_part(0), mod_part(1)]
        args += [vecs, mod_next, mod_next]
        out_specs = (out_specs, rows(d))
        out_shape = (out_shape, jax.ShapeDtypeStruct((m, d), BF16))
    return pl.pallas_call(
        functools.partial(_merge_kernel, tiles_per_batch=seq // tm, emit_next=emit_next),
        grid=(m // tm,),
        in_specs=in_specs,
        out_specs=out_specs,
        out_shape=out_shape,
        compiler_params=_params("arbitrary"),
        name="merge_out",
    )(*args)


def kernel(x, c, positions, ada_w, ada_b, norm_pre, norm_post, w_in, ret_gn, lru_conv_w, lru_conv_b,
           lru_wa, lru_ba, lru_wx, lru_bx, lru_lambda, mla_q_norm, mla_w_uq, mla_kv_norm, mla_w_ukv,
           w_branch, w_out):
    batch, seq, d = x.shape
    depth = w_in.shape[0]
    m = batch * seq
    assert d == D_MODEL and w_in.shape[2] == SRC_WIDTH and batch <= SUBLANE
    assert seq % PROJ_TM == 0 and seq % (RET_SUB * RET_T) == 0 and seq % LRU_T == 0 and seq % MLA_TQ == 0
    assert seq % MERGE_TM == 0 and seq % NORM_TM == 0

    c_pad = jnp.pad(c, ((0, SUBLANE - batch), (0, 0)))
    vecs = _pack_vectors(ada_b, norm_pre, norm_post, ret_gn, lru_conv_b, lru_ba, lru_bx, lru_lambda,
                         mla_q_norm, mla_kv_norm)
    mod = _ada_call(c_pad, ada_w, vecs, 0)
    w_in_t = jnp.swapaxes(w_in, 1, 2)

    x2 = x.reshape(m, d)
    h, cos_r, sin_r, cos_m, sin_m = _prenorm_call(x2, mod, vecs, positions, 0, seq)
    for l in range(depth):
        proj = _inproj_call(h, w_in_t, l)
        y_ret, wb, wo = _ret_call(proj, cos_r, sin_r, vecs, w_branch, w_out, l, batch, seq)
        lru_args = (proj, lru_conv_w, lru_wa, lru_wx, vecs, l, batch, seq)
        if l + 1 < depth:
            y_lru, mod_next = _lru_call(*lru_args, ada=(c_pad, ada_w))
        else:
            y_lru, mod_next = _lru_call(*lru_args), None
        y_mla = _mla_call(h, w_in_t, proj, mla_w_uq, mla_w_ukv, vecs, cos_m, sin_m, l, batch, seq)
        out = _merge_call(y_ret, y_lru, y_mla, proj, x2, mod, mod_next, vecs, wb, wo, l, seq)
        mod = mod_next
        if l + 1 < depth:
            x2, h = out
        else:
            x2 = out
    return x2.reshape(batch, seq, d)
```

```python
import functools
import math

import jax
import jax.numpy as jnp
from jax import lax
from jax.experimental import pallas as pl
from jax.experimental.pallas import tpu as pltpu

F32 = jnp.float32
BF16 = jnp.bfloat16

D_MODEL = 2048
CHUNK = 64
ROPE_BASE = 10000.0
NORM_EPS = 1e-6

RET_HEADS = 8
RET_DIM = 128
RET_WIDTH = RET_HEADS * RET_DIM

LRU_WIDTH = 1024
LRU_BLOCKS = 8
LRU_BLOCK_DIM = 128
CONV_WIDTH = 4
LRU_C = 8.0

MLA_HEADS = 8
MLA_NOPE = 128
MLA_ROPE = 64
MLA_V = 128
MLA_LORA = 512
MLA_WIDTH = MLA_HEADS * MLA_V
MLA_QK = MLA_NOPE + MLA_ROPE
MLA_QK_PAD = 256

LANE = 128
SUBLANE = 8
VMEM_LIMIT = 58 * 1024 * 1024

SRC_MKR = 4 * 1024 + 2 * 1024 + 2 * MLA_LORA
SRC_WIDTH = SRC_MKR + MLA_ROPE + MLA_WIDTH + 3 * D_MODEL
COL_RQ = 0
COL_RK = 1024
COL_RV = 2048
COL_RG = 3072
COL_LX = 4096
COL_LG = 5120
COL_MQ = 6144
COL_MKV = COL_MQ + MLA_LORA
COL_MG = SRC_MKR
COL_MERGE = COL_MG + MLA_WIDTH
PROJ_WIDTH = COL_MERGE + 3 * D_MODEL
PROJ_TN = 1024
PROJ_TM = 2048
ALIGNED_TILES = SRC_MKR // PROJ_TN
PROJ_NB = 256
assert SRC_MKR % PROJ_TN == 0 and PROJ_WIDTH % PROJ_TN == 0 and COL_MERGE % D_MODEL == 0

NORM_TM = 512
RET_T = 256
RET_SUB = 1
LRU_T = 512
LRU_CLEN = LRU_T // 8
MLA_TQ = 256
MLA_HPS = 2
MERGE_TM = 256
ADA_TK = 256


def _silu(v):
    return v * jax.nn.sigmoid(v)


def _dot(a, b):
    return jnp.dot(a, b, preferred_element_type=F32)


def _dot_nt(a, b):
    return lax.dot_general(a, b, (((1,), (1,)), ((), ())), preferred_element_type=F32)


def _dot_tn(a, b):
    return lax.dot_general(a, b, (((0,), (0,)), ((), ())), preferred_element_type=F32)


def _rms(x, gain):
    return x * lax.rsqrt(jnp.mean(x * x, axis=-1, keepdims=True) + NORM_EPS) * gain


def _params(*sem):
    return pltpu.CompilerParams(dimension_semantics=sem, vmem_limit_bytes=VMEM_LIMIT)


VEC_ADA_B = 0
VEC_NORM_PRE = 3 * D_MODEL
VEC_NORM_POST = VEC_NORM_PRE + D_MODEL
VEC_RET_GN = VEC_NORM_POST + D_MODEL
VEC_CONV_B = VEC_RET_GN + RET_WIDTH
VEC_BA = VEC_CONV_B + LRU_WIDTH
VEC_BX = VEC_BA + LRU_WIDTH
VEC_LAM = VEC_BX + LRU_WIDTH
VEC_Q_NORM = VEC_LAM + LRU_WIDTH
VEC_KV_NORM = VEC_Q_NORM + MLA_LORA


def _pack_vectors(ada_b, norm_pre, norm_post, ret_gn, conv_b, ba, bx, lam, q_norm, kv_norm):
    packed = jnp.concatenate([ada_b, norm_pre, norm_post, ret_gn, conv_b, ba, bx, lam, q_norm, kv_norm], axis=1)
    return packed[:, None, :]


def _vec_spec(width, layer, ngrid, col):
    assert col % width == 0
    if ngrid == 1:
        return pl.BlockSpec((1, 1, width), lambda i: (layer, 0, col // width))
    return pl.BlockSpec((1, 1, width), lambda i, j: (layer, 0, col // width))


def _ada_accumulate(first, c_ref, w_ref, b_ref, o_ref):
    @pl.when(first)
    def _():
        o_ref[0] = jnp.broadcast_to(b_ref[0], o_ref.shape[1:])

    c_act = _silu(c_ref[0]).astype(BF16)
    o_ref[0] += _dot(c_act, w_ref[0].astype(BF16))


def _ada_kernel(c_ref, w_ref, b_ref, o_ref):
    _ada_accumulate(pl.program_id(0) == 0, c_ref, w_ref, b_ref, o_ref)


def _c_slabs(c_pad, n_slabs):
    rows, d = c_pad.shape
    return c_pad.reshape(rows, n_slabs, d // n_slabs).transpose(1, 0, 2)


def _ada_call(c_pad, ada_w, vecs, layer):
    _, d, n = ada_w.shape
    rows = c_pad.shape[0]
    n_slabs = d // ADA_TK
    return pl.pallas_call(
        _ada_kernel,
        grid=(n_slabs,),
        in_specs=[
            pl.BlockSpec((1, rows, ADA_TK), lambda k: (k, 0, 0)),
            pl.BlockSpec((1, ADA_TK, n), lambda k: (layer, k, 0)),
            _vec_spec(n, layer, 1, VEC_ADA_B),
        ],
        out_specs=pl.BlockSpec((1, rows, n), lambda k: (0, 0, 0)),
        out_shape=jax.ShapeDtypeStruct((1, rows, n), F32),
        compiler_params=_params("arbitrary"),
        name="ada_mod",
    )(_c_slabs(c_pad, n_slabs), ada_w, vecs)


def _modulate(x, gain, shift_ref, scale_ref, b):
    shift = shift_ref[0, pl.ds(b, 1), :]
    scale = scale_ref[0, pl.ds(b, 1), :]
    return _rms(x, gain) * (1.0 + scale) + shift


def _prenorm_kernel(x_ref, g_ref, shift_ref, scale_ref, pos_ref, freq_ref,
                    o_ref, cos_r_ref, sin_r_ref, cos_m_ref, sin_m_ref, *, tiles_per_batch):
    b = pl.program_id(0) // tiles_per_batch
    o_ref[...] = _modulate(x_ref[...], g_ref[0], shift_ref, scale_ref, b).astype(o_ref.dtype)

    ang = freq_ref[...] * pos_ref[0].astype(F32)
    cos_t = jnp.cos(ang)
    sin_t = jnp.sin(ang)
    n_r, n_m = RET_DIM // 2, MLA_ROPE // 2
    c_r, s_r = cos_t[0:n_r], sin_t[0:n_r]
    c_m, s_m = cos_t[n_r:n_r + n_m], sin_t[n_r:n_r + n_m]
    pad = jnp.zeros((LANE - MLA_ROPE, ang.shape[1]), F32)
    cos_r_ref[...] = jnp.concatenate([c_r, c_r], axis=0).T
    sin_r_ref[...] = jnp.concatenate([-s_r, s_r], axis=0).T
    cos_m_ref[...] = jnp.concatenate([c_m, c_m, pad], axis=0).T
    sin_m_ref[...] = jnp.concatenate([-s_m, s_m, pad], axis=0).T


def _prenorm_call(x2, mod, vecs, positions, layer, seq):
    m, d = x2.shape
    rows = mod.shape[1]
    n_tiles = m // NORM_TM

    def inv_freq(dim):
        return ROPE_BASE ** (-jnp.arange(0, dim, 2, dtype=F32) / dim)

    freqs = jnp.concatenate([inv_freq(RET_DIM), inv_freq(MLA_ROPE)])[:, None]
    table = pl.BlockSpec((NORM_TM, LANE), lambda i: (i, 0))
    return pl.pallas_call(
        functools.partial(_prenorm_kernel, tiles_per_batch=seq // NORM_TM),
        grid=(n_tiles,),
        in_specs=[
            pl.BlockSpec((NORM_TM, d), lambda i: (i, 0)),
            _vec_spec(d, layer, 1, VEC_NORM_PRE),
            pl.BlockSpec((1, rows, d), lambda i: (0, 0, 0)),
            pl.BlockSpec((1, rows, d), lambda i: (0, 0, 1)),
            pl.BlockSpec((1, 1, NORM_TM), lambda i: (i, 0, 0)),
            pl.BlockSpec(freqs.shape, lambda i: (0, 0)),
        ],
        out_specs=(pl.BlockSpec((NORM_TM, d), lambda i: (i, 0)), table, table, table, table),
        out_shape=(jax.ShapeDtypeStruct((m, d), BF16),) + (jax.ShapeDtypeStruct((m, LANE), F32),) * 4,
        compiler_params=_params("arbitrary"),
        name="pre_norm",
    )(x2, vecs, mod, mod, positions.reshape(n_tiles, 1, NORM_TM), freqs)


def _inproj_kernel(h_ref, w_ref, wx_ref, o_ref, wb_ref):
    j = pl.program_id(0)
    first_row_tile = pl.program_id(1) == 0
    nb = PROJ_TN // PROJ_NB

    def cast_and_dot(offset):
        for k in range(nb):
            r0 = k * PROJ_NB
            if offset and k == nb - 1:
                wb_ref[r0:r0 + PROJ_NB - offset, :] = w_ref[0, r0 + offset:PROJ_TN, :].astype(BF16)
                wb_ref[PROJ_TN - offset:, :] = wx_ref[0].astype(BF16)
            else:
                wb_ref[r0:r0 + PROJ_NB, :] = w_ref[0, r0 + offset:r0 + offset + PROJ_NB, :].astype(BF16)
            o_ref[:, r0:r0 + PROJ_NB] = _dot_nt(h_ref[...], wb_ref[r0:r0 + PROJ_NB, :]).astype(o_ref.dtype)

    @pl.when(first_row_tile & (j < ALIGNED_TILES))
    def _():
        cast_and_dot(0)

    @pl.when(first_row_tile & (j >= ALIGNED_TILES))
    def _():
        cast_and_dot(MLA_ROPE)

    @pl.when(jnp.logical_not(first_row_tile))
    def _():
        o_ref[...] = _dot_nt(h_ref[...], wb_ref[...]).astype(o_ref.dtype)


def _inproj_call(h, w_in_t, layer):
    m, d = h.shape
    per_tile = PROJ_TN // MLA_ROPE
    last_block = SRC_WIDTH // MLA_ROPE - 1
    return pl.pallas_call(
        _inproj_kernel,
        grid=(PROJ_WIDTH // PROJ_TN, m // PROJ_TM),
        in_specs=[
            pl.BlockSpec((PROJ_TM, d), lambda j, i: (i, 0)),
            pl.BlockSpec((1, PROJ_TN, d), lambda j, i: (layer, j, 0)),
            pl.BlockSpec((1, MLA_ROPE, d),
                         lambda j, i: (layer, jnp.minimum((j + 1) * per_tile, last_block), 0)),
        ],
        out_specs=pl.BlockSpec((PROJ_TM, PROJ_TN), lambda j, i: (i, j)),
        out_shape=jax.ShapeDtypeStruct((m, PROJ_WIDTH), BF16),
        scratch_shapes=[pltpu.VMEM((PROJ_TN, d), BF16)],
        compiler_params=_params("arbitrary", "arbitrary"),
        name="in_proj",
    )(h, w_in_t, w_in_t)


_LOG_GAMMA = tuple(math.log1p(-(2.0 ** (-5.0 - h))) for h in range(RET_HEADS))


def _ret_kernel(q_ref, k_ref, v_ref, g_ref, cos_ref, sin_ref, gn_ref, wb_ref, wo_ref,
                o_ref, wb_out, wo_out, state_ref, dmat_ref, dq_ref, dk_ref):
    t_blk = RET_T
    wb_out[...] = wb_ref[0].astype(BF16)
    wo_out[...] = wo_ref[0].astype(BF16)

    @pl.when((pl.program_id(0) == 0) & (pl.program_id(1) == 0))
    def _():
        ri = lax.broadcasted_iota(jnp.int32, (t_blk, t_blk), 0)
        ci = lax.broadcasted_iota(jnp.int32, (t_blk, t_blk), 1)
        dist = jnp.abs(ri - ci).astype(F32)
        visible = (ci // CHUNK) <= (ri // CHUNK)
        row = lax.broadcasted_iota(jnp.int32, (t_blk, RET_DIM), 0).astype(F32)
        for h in range(RET_HEADS):
            lg = _LOG_GAMMA[h]
            dmat_ref[h] = jnp.where(visible, jnp.exp(lg * dist), 0.0)
            dq_ref[h] = jnp.exp(lg * (row + 1.0))
            dk_ref[h] = jnp.exp(lg * ((t_blk - 1.0) - row))

    @pl.when(pl.program_id(1) == 0)
    def _():
        state_ref[...] = jnp.zeros_like(state_ref)

    for sub in range(RET_SUB):
        rs = slice(sub * t_blk, (sub + 1) * t_blk)
        cos = cos_ref[rs, :]
        sin = sin_ref[rs, :]
        for h in range(RET_HEADS):
            sl = slice(h * RET_DIM, (h + 1) * RET_DIM)
            q = q_ref[rs, sl].astype(F32)
            k = k_ref[rs, sl].astype(F32)
            q = (q * cos + pltpu.roll(q, RET_DIM // 2, 1) * sin) * (RET_DIM ** -0.5)
            k = k * cos + pltpu.roll(k, RET_DIM // 2, 1) * sin
            v = v_ref[rs, sl]
            scores = _dot_nt(q.astype(BF16), k.astype(BF16)) * dmat_ref[h]
            o = _dot(scores.astype(BF16), v)
            state = state_ref[h]
            o = o + _dot((q * dq_ref[h]).astype(BF16), state.astype(BF16))
            k_dec = (k * dk_ref[h]).astype(BF16)
            state_ref[h] = state * math.exp(_LOG_GAMMA[h] * t_blk) + _dot_tn(k_dec, v)
            mean = jnp.mean(o, axis=-1, keepdims=True)
            cen = o - mean
            var = jnp.mean(cen * cen, axis=-1, keepdims=True)
            normed = cen * lax.rsqrt(var + NORM_EPS) * gn_ref[0, :, sl]
            o_ref[rs, sl] = (normed * _silu(g_ref[rs, sl].astype(F32))).astype(o_ref.dtype)


def _ret_call(proj, cos_r, sin_r, vecs, w_branch, w_out, layer, batch, seq):
    m = proj.shape[0]
    rows = RET_SUB * RET_T
    nt = seq // rows
    w = RET_WIDTH
    steps = batch * nt
    wb_rows, wo_rows = w_branch.shape[1] // steps, w_out.shape[1] // steps
    bf16_rows = 2 * SUBLANE
    assert wb_rows * steps == w_branch.shape[1] and wb_rows % bf16_rows == 0
    assert wo_rows * steps == w_out.shape[1] and wo_rows % bf16_rows == 0
    d = w_branch.shape[2]

    def col(c):
        return pl.BlockSpec((rows, w), lambda b, t, c=c: (b * nt + t, c // w))

    tab = pl.BlockSpec((rows, RET_DIM), lambda b, t: (b * nt + t, 0))
    return pl.pallas_call(
        _ret_kernel,
        grid=(batch, nt),
        in_specs=[col(COL_RQ), col(COL_RK), col(COL_RV), col(COL_RG), tab, tab,
                  _vec_spec(w, layer, 2, VEC_RET_GN),
                  pl.BlockSpec((1, wb_rows, d), lambda b, t: (layer, b * nt + t, 0)),
                  pl.BlockSpec((1, wo_rows, d), lambda b, t: (layer, b * nt + t, 0))],
        out_specs=(pl.BlockSpec((rows, w), lambda b, t: (b * nt + t, 0)),
                   pl.BlockSpec((wb_rows, d), lambda b, t: (b * nt + t, 0)),
                   pl.BlockSpec((wo_rows, d), lambda b, t: (b * nt + t, 0))),
        out_shape=(jax.ShapeDtypeStruct((m, w), BF16),
                   jax.ShapeDtypeStruct(w_branch.shape[1:], BF16),
                   jax.ShapeDtypeStruct(w_out.shape[1:], BF16)),
        scratch_shapes=[
            pltpu.VMEM((RET_HEADS, RET_DIM, RET_DIM), F32),
            pltpu.VMEM((RET_HEADS, RET_T, RET_T), F32),
            pltpu.VMEM((RET_HEADS, RET_T, RET_DIM), F32),
            pltpu.VMEM((RET_HEADS, RET_T, RET_DIM), F32),
        ],
        compiler_params=_params("arbitrary", "arbitrary"),
        name="retention",
    )(proj, proj, proj, proj, cos_r, sin_r, vecs, w_branch, w_out)


def _sublane_scan(a, b, row):
    for s in (1, 2, 4):
        keep = row >= s
        a_prev = jnp.where(keep, pltpu.roll(a, s, 0), 1.0)
        b_prev = jnp.where(keep, pltpu.roll(b, s, 0), 0.0)
        b = a * b_prev + b
        a = a * a_prev
    return a, b


def _lru_kernel(x_ref, g_ref, cw_ref, cb_ref, wa_ref, ba_ref, wx_ref, bx_ref, lam_ref, *rest, with_ada):
    if with_ada:
        c_ref, aw_ref, ab_ref, o_ref, mod_ref, perm, perm_t, halo, hcar, a_s, b_s = rest
        _ada_accumulate((pl.program_id(0) == 0) & (pl.program_id(1) == 0), c_ref, aw_ref, ab_ref, mod_ref)
    else:
        o_ref, perm, perm_t, halo, hcar, a_s, b_s = rest
    t_blk = LRU_T
    clen = LRU_CLEN
    taps = CONV_WIDTH - 1
    row = lax.broadcasted_iota(jnp.int32, (SUBLANE, LRU_WIDTH), 0)

    @pl.when((pl.program_id(0) == 0) & (pl.program_id(1) == 0))
    def _():
        r = lax.broadcasted_iota(jnp.int32, (t_blk, t_blk), 0)
        c = lax.broadcasted_iota(jnp.int32, (t_blk, t_blk), 1)
        perm[...] = jnp.where(c == (r % SUBLANE) * clen + r // SUBLANE, 1.0, 0.0).astype(BF16)
        perm_t[...] = jnp.where(c == (r % clen) * SUBLANE + r // clen, 1.0, 0.0).astype(BF16)

    @pl.when(pl.program_id(1) == 0)
    def _():
        halo[...] = jnp.zeros_like(halo)
        hcar[...] = jnp.zeros_like(hcar)

    xp = _dot(perm[...], x_ref[...])
    pieces = []
    for k in range(taps):
        cur = xp[t_blk - (taps - k) * SUBLANE:t_blk - (taps - k - 1) * SUBLANE, :]
        prev = halo[k * SUBLANE:(k + 1) * SUBLANE, :]
        pieces.append(jnp.where(row == 0, pltpu.roll(prev, 1, 0), pltpu.roll(cur, 1, 0)))
    halo[...] = xp[t_blk - taps * SUBLANE:, :]
    xext = jnp.concatenate(pieces + [xp], axis=0)
    xc = cb_ref[0]
    for w in range(CONV_WIDTH):
        xc = xc + cw_ref[0, w:w + 1, :] * xext[w * SUBLANE:w * SUBLANE + t_blk, :]

    neg_lam = -lam_ref[0]
    softplus = jnp.maximum(neg_lam, 0.0) + jnp.log1p(jnp.exp(-jnp.abs(neg_lam)))
    neg_half_rate = (-0.5 * LRU_C) * softplus
    for n in range(LRU_BLOCKS):
        sl = slice(n * LRU_BLOCK_DIM, (n + 1) * LRU_BLOCK_DIM)
        xn = xc[:, sl]
        xn_b = xn.astype(BF16)
        t_r = jnp.tanh(0.5 * (_dot(xn_b, wa_ref[0, n].astype(BF16)) + ba_ref[0, :, sl]))
        t_i = jnp.tanh(0.5 * (_dot(xn_b, wx_ref[0, n].astype(BF16)) + bx_ref[0, :, sl]))
        half_rate = neg_half_rate[:, sl]
        log_a = half_rate * t_r + half_rate
        a = jnp.exp(log_a)
        a_s[:, sl] = a
        var = jnp.tanh(-log_a) * (a * a + 1.0)
        std = jnp.where(var > 0.0, var * lax.rsqrt(var), 0.0)
        half_x = 0.5 * xn
        b_s[:, sl] = std * (half_x * t_i + half_x)

    def body(j, carry):
        h, prod = carry
        r0 = j * SUBLANE
        a = a_s[pl.ds(r0, SUBLANE), :]
        h = a * h + b_s[pl.ds(r0, SUBLANE), :]
        prod = a * prod
        b_s[pl.ds(r0, SUBLANE), :] = h
        a_s[pl.ds(r0, SUBLANE), :] = prod
        return h, prod

    zeros = jnp.zeros((SUBLANE, LRU_WIDTH), F32)
    carry = (zeros, zeros + 1.0)
    for j in range(clen):
        carry = body(j, carry)
    h_end, a_end = carry

    a_inc, h_inc = _sublane_scan(a_end, h_end, row)
    h_prev_tile = hcar[...]
    h_chunk_end = a_inc * h_prev_tile + h_inc
    h_chunk_start = jnp.where(row == 0, h_prev_tile, pltpu.roll(h_chunk_end, 1, 0))
    hcar[...] = jnp.broadcast_to(h_chunk_end[SUBLANE - 1:SUBLANE, :], (SUBLANE, LRU_WIDTH))

    h_true = b_s[...] + a_s[...] * jnp.concatenate([h_chunk_start] * clen, axis=0)
    gp = _dot(perm[...], g_ref[...])
    out_p = (h_true * _silu(gp)).astype(BF16)
    o_ref[...] = _dot(perm_t[...], out_p).astype(o_ref.dtype)


def _lru_call(proj, cw, wa, wx, vecs, layer, batch, seq, ada=None):
    m = proj.shape[0]
    nt = seq // LRU_T
    w = LRU_WIDTH

    def col(c):
        return pl.BlockSpec((LRU_T, w), lambda b, t, c=c: (b * nt + t, c // w))

    def vec(offset):
        return _vec_spec(w, layer, 2, offset)

    def blk():
        return pl.BlockSpec((1, LRU_BLOCKS, LRU_BLOCK_DIM, LRU_BLOCK_DIM), lambda b, t: (layer, 0, 0, 0))

    in_specs = [col(COL_LX), col(COL_LG),
                pl.BlockSpec((1, CONV_WIDTH, w), lambda b, t: (layer, 0, 0)), vec(VEC_CONV_B),
                blk(), vec(VEC_BA), blk(), vec(VEC_BX), vec(VEC_LAM)]
    args = [proj, proj, cw, vecs, wa, vecs, wx, vecs, vecs]
    out_specs = pl.BlockSpec((LRU_T, w), lambda b, t: (b * nt + t, 0))
    out_shape = jax.ShapeDtypeStruct((m, w), BF16)
    if ada is not None:
        c_pad, ada_w = ada
        rows = c_pad.shape[0]
        _, d, n = ada_w.shape
        steps = batch * nt
        slab = d // steps
        assert slab * steps == d and slab % LANE == 0
        in_specs += [pl.BlockSpec((1, rows, slab), lambda b, t: (b * nt + t, 0, 0)),
                     pl.BlockSpec((1, slab, n), lambda b, t: (layer + 1, b * nt + t, 0)),
                     _vec_spec(n, layer + 1, 2, VEC_ADA_B)]
        args += [_c_slabs(c_pad, steps), ada_w, vecs]
        out_specs = (out_specs, pl.BlockSpec((1, rows, n), lambda b, t: (0, 0, 0)))
        out_shape = (out_shape, jax.ShapeDtypeStruct((1, rows, n), F32))
    return pl.pallas_call(
        functools.partial(_lru_kernel, with_ada=ada is not None),
        grid=(batch, nt),
        in_specs=in_specs,
        out_specs=out_specs,
        out_shape=out_shape,
        scratch_shapes=[
            pltpu.VMEM((LRU_T, LRU_T), BF16),
            pltpu.VMEM((LRU_T, LRU_T), BF16),
            pltpu.VMEM(((CONV_WIDTH - 1) * SUBLANE, w), F32),
            pltpu.VMEM((SUBLANE, w), F32),
            pltpu.VMEM((LRU_T, w), F32),
            pltpu.VMEM((LRU_T, w), F32),
        ],
        compiler_params=_params("arbitrary", "arbitrary"),
        name="rg_lru",
    )(*args)


def _rope64(x, cos_t, sin_t):
    swapped = pltpu.roll(x, MLA_ROPE // 2, 1) + pltpu.roll(x, LANE - MLA_ROPE // 2, 1)
    return x * cos_t + swapped * sin_t


def _mla_kernel(h_ref, wkr_ref, mq_ref, mkv_ref, mg_ref, wq_ref, wkv_ref, qn_ref, kvn_ref,
                cos_ref, sin_ref, o_ref, qlat, kvlat, krope, wq_s, qs, ks, vs, s_scr, p_scr, *, seq):
    rows = 256
    pair = pl.program_id(1)

    @pl.when(pair == 0)
    def _():
        wkr = jnp.concatenate([wkr_ref[0], jnp.zeros((LANE - MLA_ROPE, D_MODEL), F32)], axis=0).astype(BF16)
        for i in range(seq // rows):
            rs = slice(i * rows, (i + 1) * rows)
            qlat[rs, :] = _rms(mq_ref[rs, :].astype(F32), qn_ref[0]).astype(BF16)
            kvlat[rs, :] = _rms(mkv_ref[rs, :].astype(F32), kvn_ref[0]).astype(BF16)
            kr = _dot_nt(h_ref[rs, :], wkr)
            krope[rs, :] = _rope64(kr, cos_ref[rs, :], sin_ref[rs, :]).astype(BF16)
        for hh in range(MLA_HPS):
            vs[hh, :, MLA_V:] = jnp.ones((seq, MLA_V), BF16)
        zeros = jnp.zeros((MLA_LORA, MLA_QK_PAD - MLA_QK), F32)
        wq_all = wq_ref[0]
        for hh in range(MLA_HEADS):
            w_head = wq_all[:, hh * MLA_QK:(hh + 1) * MLA_QK]
            wq_s[hh] = jnp.concatenate([w_head, zeros], axis=1).astype(BF16)

    scale = (MLA_QK ** -0.5) * math.log2(math.e)
    tq = MLA_TQ
    ri = lax.broadcasted_iota(jnp.int32, (tq, tq), 0)
    ci = lax.broadcasted_iota(jnp.int32, (tq, tq), 1)
    visible = (ci // CHUNK) <= (ri // CHUNK)
    n_q = seq // tq
    kv_w = MLA_NOPE + MLA_V

    for hh in range(MLA_HPS):
        wq = wq_s[pair * MLA_HPS + hh]
        wkv = wkv_ref[0, :, hh * kv_w:(hh + 1) * kv_w].astype(BF16)
        for i in range(seq // rows):
            rs = slice(i * rows, (i + 1) * rows)
            q = _dot(qlat[rs, :], wq)
            qs[hh, rs, 0:MLA_NOPE] = (q[:, 0:MLA_NOPE] * scale).astype(BF16)
            qs[hh, rs, MLA_NOPE:] = (_rope64(q[:, MLA_NOPE:], cos_ref[rs, :], sin_ref[rs, :]) * scale).astype(BF16)
            kv = _dot(kvlat[rs, :], wkv)
            ks[hh, rs, 0:MLA_NOPE] = kv[:, 0:MLA_NOPE].astype(BF16)
            ks[hh, rs, MLA_NOPE:] = krope[rs, :]
            vs[hh, rs, 0:MLA_V] = kv[:, MLA_NOPE:].astype(BF16)

    def scores(hh, i):
        q0, q1 = i * tq, (i + 1) * tq
        s_scr[hh, i % 2, :, 0:q1] = _dot_nt(qs[hh, q0:q1, :], ks[hh, 0:q1, :])

    def probs(hh, i):
        q0, q1 = i * tq, (i + 1) * tq
        s_d = jnp.where(visible, s_scr[hh, i % 2, :, q0:q1], -1e30)
        m = jnp.max(s_d, axis=-1, keepdims=True)
        if i > 0:
            m = jnp.maximum(m, jnp.max(s_scr[hh, i % 2, :, 0:q0], axis=-1, keepdims=True))
            p_scr[hh, i % 2, :, 0:q0] = jnp.exp2(s_scr[hh, i % 2, :, 0:q0] - m).astype(BF16)
        p_scr[hh, i % 2, :, q0:q1] = jnp.exp2(s_d - m).astype(BF16)

    def values(hh, i):
        q0, q1 = i * tq, (i + 1) * tq
        cols = slice(hh * MLA_V, (hh + 1) * MLA_V)
        acc = _dot(p_scr[hh, i % 2, :, 0:q1], vs[hh, 0:q1, :])
        out = acc[:, 0:MLA_V] / acc[:, MLA_V:]
        o_ref[q0:q1, cols] = (out * _silu(mg_ref[q0:q1, cols].astype(F32))).astype(o_ref.dtype)

    for hh in range(MLA_HPS):
        scores(hh, 0)
    for i in range(n_q + 1):
        for hh in range(MLA_HPS):
            if i + 1 < n_q:
                scores(hh, i + 1)
            if i < n_q:
                probs(hh, i)
            if i > 0:
                values(hh, i - 1)


def _mla_call(h, w_in_t, proj, w_uq, w_ukv, vecs, cos_m, sin_m, layer, batch, seq):
    m, d = h.shape
    lora = MLA_LORA
    hps = MLA_HPS
    assert MLA_HEADS % hps == 0 and COL_MG % (hps * MLA_V) == 0
    return pl.pallas_call(
        functools.partial(_mla_kernel, seq=seq),
        grid=(batch, MLA_HEADS // hps),
        in_specs=[
            pl.BlockSpec((seq, d), lambda b, hd: (b, 0)),
            pl.BlockSpec((1, MLA_ROPE, d), lambda b, hd: (layer, SRC_MKR // MLA_ROPE, 0)),
            pl.BlockSpec((seq, lora), lambda b, hd: (b, COL_MQ // lora), pipeline_mode=pl.Buffered(1)),
            pl.BlockSpec((seq, lora), lambda b, hd: (b, COL_MKV // lora), pipeline_mode=pl.Buffered(1)),
            pl.BlockSpec((seq, hps * MLA_V), lambda b, hd: (b, COL_MG // (hps * MLA_V) + hd)),
            pl.BlockSpec((1, lora, MLA_HEADS * MLA_QK), lambda b, hd: (layer, 0, 0),
                         pipeline_mode=pl.Buffered(1)),
            pl.BlockSpec((1, lora, hps * (MLA_NOPE + MLA_V)), lambda b, hd: (layer, 0, hd)),
            _vec_spec(lora, layer, 2, VEC_Q_NORM),
            _vec_spec(lora, layer, 2, VEC_KV_NORM),
            pl.BlockSpec((seq, LANE), lambda b, hd: (b, 0), pipeline_mode=pl.Buffered(1)),
            pl.BlockSpec((seq, LANE), lambda b, hd: (b, 0), pipeline_mode=pl.Buffered(1)),
        ],
        out_specs=pl.BlockSpec((seq, hps * MLA_V), lambda b, hd: (b, hd)),
        out_shape=jax.ShapeDtypeStruct((m, MLA_WIDTH), BF16),
        scratch_shapes=[
            pltpu.VMEM((seq, lora), BF16),
            pltpu.VMEM((seq, lora), BF16),
            pltpu.VMEM((seq, LANE), BF16),
            pltpu.VMEM((MLA_HEADS, lora, MLA_QK_PAD), BF16),
            pltpu.VMEM((hps, seq, MLA_QK_PAD), BF16),
            pltpu.VMEM((hps, seq, MLA_QK_PAD), BF16),
            pltpu.VMEM((hps, seq, 2 * MLA_V), BF16),
            pltpu.VMEM((hps, 2, MLA_TQ, seq), F32),
            pltpu.VMEM((hps, 2, MLA_TQ, seq), BF16),
        ],
        compiler_params=_params("arbitrary", "arbitrary"),
        name="mla",
    )(h, w_in_t, proj, proj, proj, w_uq, w_ukv, vecs, vecs, cos_m, sin_m)


def _merge_kernel(yr_ref, yl_ref, ym_ref, l0_ref, l1_ref, l2_ref, x_ref, res_ref, g_ref,
                  wb_ref, wo_ref, *rest, tiles_per_batch, emit_next):
    b = pl.program_id(0) // tiles_per_batch
    merged = None
    for i, (y_ref, l_ref) in enumerate(((yr_ref, l0_ref), (yl_ref, l1_ref), (ym_ref, l2_ref))):
        z = _dot(y_ref[...], wb_ref[i * 1024:(i + 1) * 1024, :])
        z = z * jax.nn.sigmoid(l_ref[...].astype(F32))
        merged = z if merged is None else merged + z
    y = _dot(merged.astype(BF16), wo_ref[...])
    post = g_ref[0] * (1.0 + res_ref[0, pl.ds(b, 1), :])
    inv = lax.rsqrt(jnp.mean(y * y, axis=-1, keepdims=True) + NORM_EPS)
    x_new = x_ref[...] + (y * inv) * post
    if emit_next:
        gn_ref, shift_ref, scale_ref, o_ref, h_ref = rest
        pre = gn_ref[0] * (1.0 + scale_ref[0, pl.ds(b, 1), :])
        inv_n = lax.rsqrt(jnp.mean(x_new * x_new, axis=-1, keepdims=True) + NORM_EPS)
        h_ref[...] = ((x_new * inv_n) * pre + shift_ref[0, pl.ds(b, 1), :]).astype(h_ref.dtype)
    else:
        (o_ref,) = rest
    o_ref[...] = x_new


def _merge_call(y_ret, y_lru, y_mla, proj, x2, mod, mod_next, vecs, wb, wo, layer, seq):
    m, d = x2.shape
    tm = MERGE_TM
    mod_rows = mod.shape[1]
    emit_next = mod_next is not None

    def rows(width, c=0):
        return pl.BlockSpec((tm, width), lambda i, c=c: (i, c))

    def resident(shape):
        return pl.BlockSpec(shape, lambda i: (0, 0), pipeline_mode=pl.Buffered(1))

    def mod_part(part):
        return pl.BlockSpec((1, mod_rows, d), lambda i: (0, 0, part))

    merge_col = COL_MERGE // d
    in_specs = [rows(1024), rows(1024), rows(1024),
                rows(d, merge_col), rows(d, merge_col + 1), rows(d, merge_col + 2),
                rows(d), mod_part(2), _vec_spec(d, layer, 1, VEC_NORM_POST),
                resident(wb.shape), resident(wo.shape)]
    args = [y_ret, y_lru, y_mla, proj, proj, proj, x2, mod, vecs, wb, wo]
    out_specs = rows(d)
    out_shape = jax.ShapeDtypeStruct((m, d), F32)
    if emit_next:
        in_specs += [_vec_spec(d, layer + 1, 1, VEC_NORM_PRE), mod_part(0), mod_part(1)]
        args += [vecs, mod_next, mod_next]
        out_specs = (out_specs, rows(d))
        out_shape = (out_shape, jax.ShapeDtypeStruct((m, d), BF16))
    return pl.pallas_call(
        functools.partial(_merge_kernel, tiles_per_batch=seq // tm, emit_next=emit_next),
        grid=(m // tm,),
        in_specs=in_specs,
        out_specs=out_specs,
        out_shape=out_shape,
        compiler_params=_params("arbitrary"),
        name="merge_out",
    )(*args)


def kernel(x, c, positions, ada_w, ada_b, norm_pre, norm_post, w_in, ret_gn, lru_conv_w, lru_conv_b,
           lru_wa, lru_ba, lru_wx, lru_bx, lru_lambda, mla_q_norm, mla_w_uq, mla_kv_norm, mla_w_ukv,
           w_branch, w_out):
    batch, seq, d = x.shape
    depth = w_in.shape[0]
    m = batch * seq
    assert d == D_MODEL and w_in.shape[2] == SRC_WIDTH and batch <= SUBLANE
    assert seq % PROJ_TM == 0 and seq % (RET_SUB * RET_T) == 0 and seq % LRU_T == 0 and seq % MLA_TQ == 0
    assert seq % MERGE_TM == 0 and seq % NORM_TM == 0

    c_pad = jnp.pad(c, ((0, SUBLANE - batch), (0, 0)))
    vecs = _pack_vectors(ada_b, norm_pre, norm_post, ret_gn, lru_conv_b, lru_ba, lru_bx, lru_lambda,
                         mla_q_norm, mla_kv_norm)
    mod = _ada_call(c_pad, ada_w, vecs, 0)
    w_in_t = jnp.swapaxes(w_in, 1, 2)

    x2 = x.reshape(m, d)
    h, cos_r, sin_r, cos_m, sin_m = _prenorm_call(x2, mod, vecs, positions, 0, seq)
    for l in range(depth):
        proj = _inproj_call(h, w_in_t, l)
        y_ret, wb, wo = _ret_call(proj, cos_r, sin_r, vecs, w_branch, w_out, l, batch, seq)
        lru_args = (proj, lru_conv_w, lru_wa, lru_wx, vecs, l, batch, seq)
        if l + 1 < depth:
            y_lru, mod_next = _lru_call(*lru_args, ada=(c_pad, ada_w))
        else:
            y_lru, mod_next = _lru_call(*lru_args), None
        y_mla = _mla_call(h, w_in_t, proj, mla_w_uq, mla_w_ukv, vecs, cos_m, sin_m, l, batch, seq)
        out = _merge_call(y_ret, y_lru, y_mla, proj, x2, mod, mod_next, vecs, wb, wo, l, seq)
        mod = mod_next
        if l + 1 < depth:
            x2, h = out
        else:
            x2 = out
    return x2.reshape(batch, seq, d)
```

```python
import functools
import math

import jax
import jax.numpy as jnp
from jax import lax
from jax.experimental import pallas as pl
from jax.experimental.pallas import tpu as pltpu

F32 = jnp.float32
BF16 = jnp.bfloat16

D_MODEL = 2048
CHUNK = 64
ROPE_BASE = 10000.0
NORM_EPS = 1e-6

RET_HEADS = 8
RET_DIM = 128
RET_WIDTH = RET_HEADS * RET_DIM

LRU_WIDTH = 1024
LRU_BLOCKS = 8
LRU_BLOCK_DIM = 128
CONV_WIDTH = 4
LRU_C = 8.0

MLA_HEADS = 8
MLA_NOPE = 128
MLA_ROPE = 64
MLA_V = 128
MLA_LORA = 512
MLA_WIDTH = MLA_HEADS * MLA_V
MLA_QK = MLA_NOPE + MLA_ROPE
MLA_QK_PAD = 256

LANE = 128
SUBLANE = 8
VMEM_LIMIT = 58 * 1024 * 1024

SRC_MKR = 4 * 1024 + 2 * 1024 + 2 * MLA_LORA
SRC_WIDTH = SRC_MKR + MLA_ROPE + MLA_WIDTH + 3 * D_MODEL
COL_RQ = 0
COL_RK = 1024
COL_RV = 2048
COL_RG = 3072
COL_LX = 4096
COL_LG = 5120
COL_MQ = 6144
COL_MKV = COL_MQ + MLA_LORA
COL_MG = SRC_MKR
COL_MERGE = COL_MG + MLA_WIDTH
PROJ_WIDTH = COL_MERGE + 3 * D_MODEL
PROJ_TN = 1024
PROJ_TM = 2048
ALIGNED_TILES = SRC_MKR // PROJ_TN
CAST_ROWS = 64
assert SRC_MKR % PROJ_TN == 0 and PROJ_WIDTH % PROJ_TN == 0 and COL_MERGE % D_MODEL == 0

NORM_TM = 512
RET_T = 256
RET_SUB = 1
LRU_T = 512
LRU_CLEN = LRU_T // 8
MLA_TQ = 256
MLA_HPS = 2
MERGE_TM = 256
ADA_TK = 256


def _silu(v):
    return v * jax.nn.sigmoid(v)


def _dot(a, b):
    return jnp.dot(a, b, preferred_element_type=F32)


def _dot_nt(a, b):
    return lax.dot_general(a, b, (((1,), (1,)), ((), ())), preferred_element_type=F32)


def _dot_tn(a, b):
    return lax.dot_general(a, b, (((0,), (0,)), ((), ())), preferred_element_type=F32)


def _rms(x, gain):
    return x * lax.rsqrt(jnp.mean(x * x, axis=-1, keepdims=True) + NORM_EPS) * gain


def _params(*sem):
    return pltpu.CompilerParams(dimension_semantics=sem, vmem_limit_bytes=VMEM_LIMIT)


VEC_ADA_B = 0
VEC_NORM_PRE = 3 * D_MODEL
VEC_NORM_POST = VEC_NORM_PRE + D_MODEL
VEC_RET_GN = VEC_NORM_POST + D_MODEL
VEC_CONV_B = VEC_RET_GN + RET_WIDTH
VEC_BA = VEC_CONV_B + LRU_WIDTH
VEC_BX = VEC_BA + LRU_WIDTH
VEC_LAM = VEC_BX + LRU_WIDTH
VEC_Q_NORM = VEC_LAM + LRU_WIDTH
VEC_KV_NORM = VEC_Q_NORM + MLA_LORA


def _pack_vectors(ada_b, norm_pre, norm_post, ret_gn, conv_b, ba, bx, lam, q_norm, kv_norm):
    packed = jnp.concatenate([ada_b, norm_pre, norm_post, ret_gn, conv_b, ba, bx, lam, q_norm, kv_norm], axis=1)
    return packed[:, None, :]


def _vec_spec(width, layer, ngrid, col):
    assert col % width == 0
    if ngrid == 1:
        return pl.BlockSpec((1, 1, width), lambda i: (layer, 0, col // width))
    return pl.BlockSpec((1, 1, width), lambda i, j: (layer, 0, col // width))


def _ada_accumulate(first, c_ref, w_ref, b_ref, o_ref):
    @pl.when(first)
    def _():
        o_ref[0] = jnp.broadcast_to(b_ref[0], o_ref.shape[1:])

    c_act = _silu(c_ref[0]).astype(BF16)
    o_ref[0] += _dot(c_act, w_ref[0].astype(BF16))


def _ada_kernel(c_ref, w_ref, b_ref, o_ref):
    _ada_accumulate(pl.program_id(0) == 0, c_ref, w_ref, b_ref, o_ref)


def _c_slabs(c_pad, n_slabs):
    rows, d = c_pad.shape
    return c_pad.reshape(rows, n_slabs, d // n_slabs).transpose(1, 0, 2)


def _ada_call(c_pad, ada_w, vecs, layer):
    _, d, n = ada_w.shape
    rows = c_pad.shape[0]
    n_slabs = d // ADA_TK
    return pl.pallas_call(
        _ada_kernel,
        grid=(n_slabs,),
        in_specs=[
            pl.BlockSpec((1, rows, ADA_TK), lambda k: (k, 0, 0)),
            pl.BlockSpec((1, ADA_TK, n), lambda k: (layer, k, 0)),
            _vec_spec(n, layer, 1, VEC_ADA_B),
        ],
        out_specs=pl.BlockSpec((1, rows, n), lambda k: (0, 0, 0)),
        out_shape=jax.ShapeDtypeStruct((1, rows, n), F32),
        compiler_params=_params("arbitrary"),
        name="ada_mod",
    )(_c_slabs(c_pad, n_slabs), ada_w, vecs)


def _modulate(x, gain, shift_ref, scale_ref, b):
    shift = shift_ref[0, pl.ds(b, 1), :]
    scale = scale_ref[0, pl.ds(b, 1), :]
    return _rms(x, gain) * (1.0 + scale) + shift


def _prenorm_kernel(x_ref, g_ref, shift_ref, scale_ref, pos_ref, freq_ref,
                    o_ref, cos_r_ref, sin_r_ref, cos_m_ref, sin_m_ref, *, tiles_per_batch):
    b = pl.program_id(0) // tiles_per_batch
    o_ref[...] = _modulate(x_ref[...], g_ref[0], shift_ref, scale_ref, b).astype(o_ref.dtype)

    ang = freq_ref[...] * pos_ref[0].astype(F32)
    cos_t = jnp.cos(ang)
    sin_t = jnp.sin(ang)
    n_r, n_m = RET_DIM // 2, MLA_ROPE // 2
    c_r, s_r = cos_t[0:n_r], sin_t[0:n_r]
    c_m, s_m = cos_t[n_r:n_r + n_m], sin_t[n_r:n_r + n_m]
    pad = jnp.zeros((LANE - MLA_ROPE, ang.shape[1]), F32)
    cos_r_ref[...] = jnp.concatenate([c_r, c_r], axis=0).T
    sin_r_ref[...] = jnp.concatenate([-s_r, s_r], axis=0).T
    cos_m_ref[...] = jnp.concatenate([c_m, c_m, pad], axis=0).T
    sin_m_ref[...] = jnp.concatenate([-s_m, s_m, pad], axis=0).T


def _prenorm_call(x2, mod, vecs, positions, layer, seq):
    m, d = x2.shape
    rows = mod.shape[1]
    n_tiles = m // NORM_TM

    def inv_freq(dim):
        return ROPE_BASE ** (-jnp.arange(0, dim, 2, dtype=F32) / dim)

    freqs = jnp.concatenate([inv_freq(RET_DIM), inv_freq(MLA_ROPE)])[:, None]
    table = pl.BlockSpec((NORM_TM, LANE), lambda i: (i, 0))
    return pl.pallas_call(
        functools.partial(_prenorm_kernel, tiles_per_batch=seq // NORM_TM),
        grid=(n_tiles,),
        in_specs=[
            pl.BlockSpec((NORM_TM, d), lambda i: (i, 0)),
            _vec_spec(d, layer, 1, VEC_NORM_PRE),
            pl.BlockSpec((1, rows, d), lambda i: (0, 0, 0)),
            pl.BlockSpec((1, rows, d), lambda i: (0, 0, 1)),
            pl.BlockSpec((1, 1, NORM_TM), lambda i: (i, 0, 0)),
            pl.BlockSpec(freqs.shape, lambda i: (0, 0)),
        ],
        out_specs=(pl.BlockSpec((NORM_TM, d), lambda i: (i, 0)), table, table, table, table),
        out_shape=(jax.ShapeDtypeStruct((m, d), BF16),) + (jax.ShapeDtypeStruct((m, LANE), F32),) * 4,
        compiler_params=_params("arbitrary"),
        name="pre_norm",
    )(x2, vecs, mod, mod, positions.reshape(n_tiles, 1, NORM_TM), freqs)


def _inproj_kernel(h_ref, w_ref, wx_ref, o_ref, wb_ref):
    j = pl.program_id(0)
    first_row_tile = pl.program_id(1) == 0
    n_chunks = PROJ_TN // CAST_ROWS

    @pl.when(first_row_tile & (j < ALIGNED_TILES))
    def _():
        def body(i, carry):
            r0 = pl.multiple_of(i * CAST_ROWS, CAST_ROWS)
            wb_ref[pl.ds(r0, CAST_ROWS), :] = w_ref[0, pl.ds(r0, CAST_ROWS), :].astype(BF16)
            return carry

        lax.fori_loop(0, n_chunks, body, 0)

    @pl.when(first_row_tile & (j >= ALIGNED_TILES))
    def _():
        def body(i, carry):
            r0 = pl.multiple_of(i * CAST_ROWS, CAST_ROWS)
            wb_ref[pl.ds(r0, CAST_ROWS), :] = w_ref[0, pl.ds(r0 + MLA_ROPE, CAST_ROWS), :].astype(BF16)
            return carry

        lax.fori_loop(0, n_chunks - 1, body, 0)
        wb_ref[PROJ_TN - MLA_ROPE:, :] = wx_ref[0].astype(BF16)

    o_ref[...] = _dot_nt(h_ref[...], wb_ref[...]).astype(o_ref.dtype)


def _inproj_call(h, w_in_t, layer):
    m, d = h.shape
    per_tile = PROJ_TN // MLA_ROPE
    last_block = SRC_WIDTH // MLA_ROPE - 1
    return pl.pallas_call(
        _inproj_kernel,
        grid=(PROJ_WIDTH // PROJ_TN, m // PROJ_TM),
        in_specs=[
            pl.BlockSpec((PROJ_TM, d), lambda j, i: (i, 0)),
            pl.BlockSpec((1, PROJ_TN, d), lambda j, i: (layer, j, 0)),
            pl.BlockSpec((1, MLA_ROPE, d),
                         lambda j, i: (layer, jnp.minimum((j + 1) * per_tile, last_block), 0)),
        ],
        out_specs=pl.BlockSpec((PROJ_TM, PROJ_TN), lambda j, i: (i, j)),
        out_shape=jax.ShapeDtypeStruct((m, PROJ_WIDTH), BF16),
        scratch_shapes=[pltpu.VMEM((PROJ_TN, d), BF16)],
        compiler_params=_params("arbitrary", "arbitrary"),
        name="in_proj",
    )(h, w_in_t, w_in_t)


_LOG_GAMMA = tuple(math.log1p(-(2.0 ** (-5.0 - h))) for h in range(RET_HEADS))


def _ret_kernel(q_ref, k_ref, v_ref, g_ref, cos_ref, sin_ref, gn_ref, wb_ref, wo_ref,
                o_ref, wb_out, wo_out, state_ref, dmat_ref, dq_ref, dk_ref):
    t_blk = RET_T
    wb_out[...] = wb_ref[0].astype(BF16)
    wo_out[...] = wo_ref[0].astype(BF16)

    @pl.when((pl.program_id(0) == 0) & (pl.program_id(1) == 0))
    def _():
        ri = lax.broadcasted_iota(jnp.int32, (t_blk, t_blk), 0)
        ci = lax.broadcasted_iota(jnp.int32, (t_blk, t_blk), 1)
        dist = jnp.abs(ri - ci).astype(F32)
        visible = (ci // CHUNK) <= (ri // CHUNK)
        row = lax.broadcasted_iota(jnp.int32, (t_blk, RET_DIM), 0).astype(F32)
        for h in range(RET_HEADS):
            lg = _LOG_GAMMA[h]
            dmat_ref[h] = jnp.where(visible, jnp.exp(lg * dist), 0.0)
            dq_ref[h] = jnp.exp(lg * (row + 1.0))
            dk_ref[h] = jnp.exp(lg * ((t_blk - 1.0) - row))

    @pl.when(pl.program_id(1) == 0)
    def _():
        state_ref[...] = jnp.zeros_like(state_ref)

    for sub in range(RET_SUB):
        rs = slice(sub * t_blk, (sub + 1) * t_blk)
        cos = cos_ref[rs, :]
        sin = sin_ref[rs, :]
        for h in range(RET_HEADS):
            sl = slice(h * RET_DIM, (h + 1) * RET_DIM)
            q = q_ref[rs, sl].astype(F32)
            k = k_ref[rs, sl].astype(F32)
            q = (q * cos + pltpu.roll(q, RET_DIM // 2, 1) * sin) * (RET_DIM ** -0.5)
            k = k * cos + pltpu.roll(k, RET_DIM // 2, 1) * sin
            v = v_ref[rs, sl]
            scores = _dot_nt(q.astype(BF16), k.astype(BF16)) * dmat_ref[h]
            o = _dot(scores.astype(BF16), v)
            state = state_ref[h]
            o = o + _dot((q * dq_ref[h]).astype(BF16), state.astype(BF16))
            k_dec = (k * dk_ref[h]).astype(BF16)
            state_ref[h] = state * math.exp(_LOG_GAMMA[h] * t_blk) + _dot_tn(k_dec, v)
            mean = jnp.mean(o, axis=-1, keepdims=True)
            cen = o - mean
            var = jnp.mean(cen * cen, axis=-1, keepdims=True)
            normed = cen * lax.rsqrt(var + NORM_EPS) * gn_ref[0, :, sl]
            o_ref[rs, sl] = (normed * _silu(g_ref[rs, sl].astype(F32))).astype(o_ref.dtype)


def _ret_call(proj, cos_r, sin_r, vecs, w_branch, w_out, layer, batch, seq):
    m = proj.shape[0]
    rows = RET_SUB * RET_T
    nt = seq // rows
    w = RET_WIDTH
    steps = batch * nt
    wb_rows, wo_rows = w_branch.shape[1] // steps, w_out.shape[1] // steps
    bf16_rows = 2 * SUBLANE
    assert wb_rows * steps == w_branch.shape[1] and wb_rows % bf16_rows == 0
    assert wo_rows * steps == w_out.shape[1] and wo_rows % bf16_rows == 0
    d = w_branch.shape[2]

    def col(c):
        return pl.BlockSpec((rows, w), lambda b, t, c=c: (b * nt + t, c // w))

    tab = pl.BlockSpec((rows, RET_DIM), lambda b, t: (b * nt + t, 0))
    return pl.pallas_call(
        _ret_kernel,
        grid=(batch, nt),
        in_specs=[col(COL_RQ), col(COL_RK), col(COL_RV), col(COL_RG), tab, tab,
                  _vec_spec(w, layer, 2, VEC_RET_GN),
                  pl.BlockSpec((1, wb_rows, d), lambda b, t: (layer, b * nt + t, 0)),
                  pl.BlockSpec((1, wo_rows, d), lambda b, t: (layer, b * nt + t, 0))],
        out_specs=(pl.BlockSpec((rows, w), lambda b, t: (b * nt + t, 0)),
                   pl.BlockSpec((wb_rows, d), lambda b, t: (b * nt + t, 0)),
                   pl.BlockSpec((wo_rows, d), lambda b, t: (b * nt + t, 0))),
        out_shape=(jax.ShapeDtypeStruct((m, w), BF16),
                   jax.ShapeDtypeStruct(w_branch.shape[1:], BF16),
                   jax.ShapeDtypeStruct(w_out.shape[1:], BF16)),
        scratch_shapes=[
            pltpu.VMEM((RET_HEADS, RET_DIM, RET_DIM), F32),
            pltpu.VMEM((RET_HEADS, RET_T, RET_T), F32),
            pltpu.VMEM((RET_HEADS, RET_T, RET_DIM), F32),
            pltpu.VMEM((RET_HEADS, RET_T, RET_DIM), F32),
        ],
        compiler_params=_params("arbitrary", "arbitrary"),
        name="retention",
    )(proj, proj, proj, proj, cos_r, sin_r, vecs, w_branch, w_out)


def _sublane_scan(a, b, row):
    for s in (1, 2, 4):
        keep = row >= s
        a_prev = jnp.where(keep, pltpu.roll(a, s, 0), 1.0)
        b_prev = jnp.where(keep, pltpu.roll(b, s, 0), 0.0)
        b = a * b_prev + b
        a = a * a_prev
    return a, b


def _lru_kernel(x_ref, g_ref, cw_ref, cb_ref, wa_ref, ba_ref, wx_ref, bx_ref, lam_ref, *rest, with_ada):
    if with_ada:
        c_ref, aw_ref, ab_ref, o_ref, mod_ref, perm, perm_t, halo, hcar, a_s, b_s = rest
        _ada_accumulate((pl.program_id(0) == 0) & (pl.program_id(1) == 0), c_ref, aw_ref, ab_ref, mod_ref)
    else:
        o_ref, perm, perm_t, halo, hcar, a_s, b_s = rest
    t_blk = LRU_T
    clen = LRU_CLEN
    taps = CONV_WIDTH - 1
    row = lax.broadcasted_iota(jnp.int32, (SUBLANE, LRU_WIDTH), 0)

    @pl.when((pl.program_id(0) == 0) & (pl.program_id(1) == 0))
    def _():
        r = lax.broadcasted_iota(jnp.int32, (t_blk, t_blk), 0)
        c = lax.broadcasted_iota(jnp.int32, (t_blk, t_blk), 1)
        perm[...] = jnp.where(c == (r % SUBLANE) * clen + r // SUBLANE, 1.0, 0.0).astype(BF16)
        perm_t[...] = jnp.where(c == (r % clen) * SUBLANE + r // clen, 1.0, 0.0).astype(BF16)

    @pl.when(pl.program_id(1) == 0)
    def _():
        halo[...] = jnp.zeros_like(halo)
        hcar[...] = jnp.zeros_like(hcar)

    xp = _dot(perm[...], x_ref[...])
    pieces = []
    for k in range(taps):
        cur = xp[t_blk - (taps - k) * SUBLANE:t_blk - (taps - k - 1) * SUBLANE, :]
        prev = halo[k * SUBLANE:(k + 1) * SUBLANE, :]
        pieces.append(jnp.where(row == 0, pltpu.roll(prev, 1, 0), pltpu.roll(cur, 1, 0)))
    halo[...] = xp[t_blk - taps * SUBLANE:, :]
    xext = jnp.concatenate(pieces + [xp], axis=0)
    xc = cb_ref[0]
    for w in range(CONV_WIDTH):
        xc = xc + cw_ref[0, w:w + 1, :] * xext[w * SUBLANE:w * SUBLANE + t_blk, :]

    neg_lam = -lam_ref[0]
    softplus = jnp.maximum(neg_lam, 0.0) + jnp.log1p(jnp.exp(-jnp.abs(neg_lam)))
    neg_half_rate = (-0.5 * LRU_C) * softplus
    for n in range(LRU_BLOCKS):
        sl = slice(n * LRU_BLOCK_DIM, (n + 1) * LRU_BLOCK_DIM)
        xn = xc[:, sl]
        xn_b = xn.astype(BF16)
        t_r = jnp.tanh(0.5 * (_dot(xn_b, wa_ref[0, n].astype(BF16)) + ba_ref[0, :, sl]))
        t_i = jnp.tanh(0.5 * (_dot(xn_b, wx_ref[0, n].astype(BF16)) + bx_ref[0, :, sl]))
        half_rate = neg_half_rate[:, sl]
        log_a = half_rate * t_r + half_rate
        a = jnp.exp(log_a)
        a_s[:, sl] = a
        var = jnp.tanh(-log_a) * (a * a + 1.0)
        std = jnp.where(var > 0.0, var * lax.rsqrt(var), 0.0)
        half_x = 0.5 * xn
        b_s[:, sl] = std * (half_x * t_i + half_x)

    def body(j, carry):
        h, prod = carry
        r0 = j * SUBLANE
        a = a_s[pl.ds(r0, SUBLANE), :]
        h = a * h + b_s[pl.ds(r0, SUBLANE), :]
        prod = a * prod
        b_s[pl.ds(r0, SUBLANE), :] = h
        a_s[pl.ds(r0, SUBLANE), :] = prod
        return h, prod

    zeros = jnp.zeros((SUBLANE, LRU_WIDTH), F32)
    carry = (zeros, zeros + 1.0)
    for j in range(clen):
        carry = body(j, carry)
    h_end, a_end = carry

    a_inc, h_inc = _sublane_scan(a_end, h_end, row)
    h_prev_tile = hcar[...]
    h_chunk_end = a_inc * h_prev_tile + h_inc
    h_chunk_start = jnp.where(row == 0, h_prev_tile, pltpu.roll(h_chunk_end, 1, 0))
    hcar[...] = jnp.broadcast_to(h_chunk_end[SUBLANE - 1:SUBLANE, :], (SUBLANE, LRU_WIDTH))

    h_true = b_s[...] + a_s[...] * jnp.concatenate([h_chunk_start] * clen, axis=0)
    gp = _dot(perm[...], g_ref[...])
    out_p = (h_true * _silu(gp)).astype(BF16)
    o_ref[...] = _dot(perm_t[...], out_p).astype(o_ref.dtype)


def _lru_call(proj, cw, wa, wx, vecs, layer, batch, seq, ada=None):
    m = proj.shape[0]
    nt = seq // LRU_T
    w = LRU_WIDTH

    def col(c):
        return pl.BlockSpec((LRU_T, w), lambda b, t, c=c: (b * nt + t, c // w))

    def vec(offset):
        return _vec_spec(w, layer, 2, offset)

    def blk():
        return pl.BlockSpec((1, LRU_BLOCKS, LRU_BLOCK_DIM, LRU_BLOCK_DIM), lambda b, t: (layer, 0, 0, 0))

    in_specs = [col(COL_LX), col(COL_LG),
                pl.BlockSpec((1, CONV_WIDTH, w), lambda b, t: (layer, 0, 0)), vec(VEC_CONV_B),
                blk(), vec(VEC_BA), blk(), vec(VEC_BX), vec(VEC_LAM)]
    args = [proj, proj, cw, vecs, wa, vecs, wx, vecs, vecs]
    out_specs = pl.BlockSpec((LRU_T, w), lambda b, t: (b * nt + t, 0))
    out_shape = jax.ShapeDtypeStruct((m, w), BF16)
    if ada is not None:
        c_pad, ada_w = ada
        rows = c_pad.shape[0]
        _, d, n = ada_w.shape
        steps = batch * nt
        slab = d // steps
        assert slab * steps == d and slab % LANE == 0
        in_specs += [pl.BlockSpec((1, rows, slab), lambda b, t: (b * nt + t, 0, 0)),
                     pl.BlockSpec((1, slab, n), lambda b, t: (layer + 1, b * nt + t, 0)),
                     _vec_spec(n, layer + 1, 2, VEC_ADA_B)]
        args += [_c_slabs(c_pad, steps), ada_w, vecs]
        out_specs = (out_specs, pl.BlockSpec((1, rows, n), lambda b, t: (0, 0, 0)))
        out_shape = (out_shape, jax.ShapeDtypeStruct((1, rows, n), F32))
    return pl.pallas_call(
        functools.partial(_lru_kernel, with_ada=ada is not None),
        grid=(batch, nt),
        in_specs=in_specs,
        out_specs=out_specs,
        out_shape=out_shape,
        scratch_shapes=[
            pltpu.VMEM((LRU_T, LRU_T), BF16),
            pltpu.VMEM((LRU_T, LRU_T), BF16),
            pltpu.VMEM(((CONV_WIDTH - 1) * SUBLANE, w), F32),
            pltpu.VMEM((SUBLANE, w), F32),
            pltpu.VMEM((LRU_T, w), F32),
            pltpu.VMEM((LRU_T, w), F32),
        ],
        compiler_params=_params("arbitrary", "arbitrary"),
        name="rg_lru",
    )(*args)


def _rope64(x, cos_t, sin_t):
    swapped = pltpu.roll(x, MLA_ROPE // 2, 1) + pltpu.roll(x, LANE - MLA_ROPE // 2, 1)
    return x * cos_t + swapped * sin_t


def _mla_kernel(h_ref, wkr_ref, mq_ref, mkv_ref, mg_ref, wq_ref, wkv_ref, qn_ref, kvn_ref,
                cos_ref, sin_ref, o_ref, qlat, kvlat, krope, wq_s, qs, ks, vs, s_scr, p_scr, *, seq):
    rows = 256
    pair = pl.program_id(1)

    @pl.when(pair == 0)
    def _():
        wkr = jnp.concatenate([wkr_ref[0], jnp.zeros((LANE - MLA_ROPE, D_MODEL), F32)], axis=0).astype(BF16)
        for i in range(seq // rows):
            rs = slice(i * rows, (i + 1) * rows)
            qlat[rs, :] = _rms(mq_ref[rs, :].astype(F32), qn_ref[0]).astype(BF16)
            kvlat[rs, :] = _rms(mkv_ref[rs, :].astype(F32), kvn_ref[0]).astype(BF16)
            kr = _dot_nt(h_ref[rs, :], wkr)
            krope[rs, :] = _rope64(kr, cos_ref[rs, :], sin_ref[rs, :]).astype(BF16)
        for hh in range(MLA_HPS):
            vs[hh, :, MLA_V:] = jnp.ones((seq, MLA_V), BF16)
        zeros = jnp.zeros((MLA_LORA, MLA_QK_PAD - MLA_QK), F32)
        wq_all = wq_ref[0]
        for hh in range(MLA_HEADS):
            w_head = wq_all[:, hh * MLA_QK:(hh + 1) * MLA_QK]
            wq_s[hh] = jnp.concatenate([w_head, zeros], axis=1).astype(BF16)

    scale = (MLA_QK ** -0.5) * math.log2(math.e)
    tq = MLA_TQ
    ri = lax.broadcasted_iota(jnp.int32, (tq, tq), 0)
    ci = lax.broadcasted_iota(jnp.int32, (tq, tq), 1)
    visible = (ci // CHUNK) <= (ri // CHUNK)
    n_q = seq // tq
    kv_w = MLA_NOPE + MLA_V

    for hh in range(MLA_HPS):
        wq = wq_s[pair * MLA_HPS + hh]
        wkv = wkv_ref[0, :, hh * kv_w:(hh + 1) * kv_w].astype(BF16)
        for i in range(seq // rows):
            rs = slice(i * rows, (i + 1) * rows)
            q = _dot(qlat[rs, :], wq)
            qs[hh, rs, 0:MLA_NOPE] = (q[:, 0:MLA_NOPE] * scale).astype(BF16)
            qs[hh, rs, MLA_NOPE:] = (_rope64(q[:, MLA_NOPE:], cos_ref[rs, :], sin_ref[rs, :]) * scale).astype(BF16)
            kv = _dot(kvlat[rs, :], wkv)
            ks[hh, rs, 0:MLA_NOPE] = kv[:, 0:MLA_NOPE].astype(BF16)
            ks[hh, rs, MLA_NOPE:] = krope[rs, :]
            vs[hh, rs, 0:MLA_V] = kv[:, MLA_NOPE:].astype(BF16)

    def scores(hh, i):
        q0, q1 = i * tq, (i + 1) * tq
        s_scr[hh, i % 2, :, 0:q1] = _dot_nt(qs[hh, q0:q1, :], ks[hh, 0:q1, :])

    def probs(hh, i):
        q0, q1 = i * tq, (i + 1) * tq
        s_d = jnp.where(visible, s_scr[hh, i % 2, :, q0:q1], -1e30)
        m = jnp.max(s_d, axis=-1, keepdims=True)
        if i > 0:
            m = jnp.maximum(m, jnp.max(s_scr[hh, i % 2, :, 0:q0], axis=-1, keepdims=True))
            p_scr[hh, i % 2, :, 0:q0] = jnp.exp2(s_scr[hh, i % 2, :, 0:q0] - m).astype(BF16)
        p_scr[hh, i % 2, :, q0:q1] = jnp.exp2(s_d - m).astype(BF16)

    def values(hh, i):
        q0, q1 = i * tq, (i + 1) * tq
        cols = slice(hh * MLA_V, (hh + 1) * MLA_V)
        acc = _dot(p_scr[hh, i % 2, :, 0:q1], vs[hh, 0:q1, :])
        out = acc[:, 0:MLA_V] / acc[:, MLA_V:]
        o_ref[q0:q1, cols] = (out * _silu(mg_ref[q0:q1, cols].astype(F32))).astype(o_ref.dtype)

    for hh in range(MLA_HPS):
        scores(hh, 0)
    for i in range(n_q + 1):
        for hh in range(MLA_HPS):
            if i + 1 < n_q:
                scores(hh, i + 1)
            if i < n_q:
                probs(hh, i)
            if i > 0:
                values(hh, i - 1)


def _mla_call(h, w_in_t, proj, w_uq, w_ukv, vecs, cos_m, sin_m, layer, batch, seq):
    m, d = h.shape
    lora = MLA_LORA
    hps = MLA_HPS
    assert MLA_HEADS % hps == 0 and COL_MG % (hps * MLA_V) == 0
    return pl.pallas_call(
        functools.partial(_mla_kernel, seq=seq),
        grid=(batch, MLA_HEADS // hps),
        in_specs=[
            pl.BlockSpec((seq, d), lambda b, hd: (b, 0)),
            pl.BlockSpec((1, MLA_ROPE, d), lambda b, hd: (layer, SRC_MKR // MLA_ROPE, 0)),
            pl.BlockSpec((seq, lora), lambda b, hd: (b, COL_MQ // lora)),
            pl.BlockSpec((seq, lora), lambda b, hd: (b, COL_MKV // lora), pipeline_mode=pl.Buffered(1)),
            pl.BlockSpec((seq, hps * MLA_V), lambda b, hd: (b, COL_MG // (hps * MLA_V) + hd)),
            pl.BlockSpec((1, lora, MLA_HEADS * MLA_QK), lambda b, hd: (layer, 0, 0),
                         pipeline_mode=pl.Buffered(1)),
            pl.BlockSpec((1, lora, hps * (MLA_NOPE + MLA_V)), lambda b, hd: (layer, 0, hd)),
            _vec_spec(lora, layer, 2, VEC_Q_NORM),
            _vec_spec(lora, layer, 2, VEC_KV_NORM),
            pl.BlockSpec((seq, LANE), lambda b, hd: (b, 0), pipeline_mode=pl.Buffered(1)),
            pl.BlockSpec((seq, LANE), lambda b, hd: (b, 0), pipeline_mode=pl.Buffered(1)),
        ],
        out_specs=pl.BlockSpec((seq, hps * MLA_V), lambda b, hd: (b, hd)),
        out_shape=jax.ShapeDtypeStruct((m, MLA_WIDTH), BF16),
        scratch_shapes=[
            pltpu.VMEM((seq, lora), BF16),
            pltpu.VMEM((seq, lora), BF16),
            pltpu.VMEM((seq, LANE), BF16),
            pltpu.VMEM((MLA_HEADS, lora, MLA_QK_PAD), BF16),
            pltpu.VMEM((hps, seq, MLA_QK_PAD), BF16),
            pltpu.VMEM((hps, seq, MLA_QK_PAD), BF16),
            pltpu.VMEM((hps, seq, 2 * MLA_V), BF16),
            pltpu.VMEM((hps, 2, MLA_TQ, seq), F32),
            pltpu.VMEM((hps, 2, MLA_TQ, seq), BF16),
        ],
        compiler_params=_params("arbitrary", "arbitrary"),
        name="mla",
    )(h, w_in_t, proj, proj, proj, w_uq, w_ukv, vecs, vecs, cos_m, sin_m)


def _merge_kernel(yr_ref, yl_ref, ym_ref, l0_ref, l1_ref, l2_ref, x_ref, res_ref, g_ref,
                  wb_ref, wo_ref, *rest, tiles_per_batch, emit_next):
    b = pl.program_id(0) // tiles_per_batch
    merged = None
    for i, (y_ref, l_ref) in enumerate(((yr_ref, l0_ref), (yl_ref, l1_ref), (ym_ref, l2_ref))):
        z = _dot(y_ref[...], wb_ref[i * 1024:(i + 1) * 1024, :])
        z = z * jax.nn.sigmoid(l_ref[...].astype(F32))
        merged = z if merged is None else merged + z
    y = _dot(merged.astype(BF16), wo_ref[...])
    post = g_ref[0] * (1.0 + res_ref[0, pl.ds(b, 1), :])
    inv = lax.rsqrt(jnp.mean(y * y, axis=-1, keepdims=True) + NORM_EPS)
    x_new = x_ref[...] + (y * inv) * post
    if emit_next:
        gn_ref, shift_ref, scale_ref, o_ref, h_ref = rest
        pre = gn_ref[0] * (1.0 + scale_ref[0, pl.ds(b, 1), :])
        inv_n = lax.rsqrt(jnp.mean(x_new * x_new, axis=-1, keepdims=True) + NORM_EPS)
        h_ref[...] = ((x_new * inv_n) * pre + shift_ref[0, pl.ds(b, 1), :]).astype(h_ref.dtype)
    else:
        (o_ref,) = rest
    o_ref[...] = x_new


def _merge_call(y_ret, y_lru, y_mla, proj, x2, mod, mod_next, vecs, wb, wo, layer, seq):
    m, d = x2.shape
    tm = MERGE_TM
    mod_rows = mod.shape[1]
    emit_next = mod_next is not None

    def rows(width, c=0):
        return pl.BlockSpec((tm, width), lambda i, c=c: (i, c))

    def resident(shape):
        return pl.BlockSpec(shape, lambda i: (0, 0), pipeline_mode=pl.Buffered(1))

    def mod_part(part):
        return pl.BlockSpec((1, mod_rows, d), lambda i: (0, 0, part))

    merge_col = COL_MERGE // d
    in_specs = [rows(1024), rows(1024), rows(1024),
                rows(d, merge_col), rows(d, merge_col + 1), rows(d, merge_col + 2),
                rows(d), mod_part(2), _vec_spec(d, layer, 1, VEC_NORM_POST),
                resident(wb.shape), resident(wo.shape)]
    args = [y_ret, y_lru, y_mla, proj, proj, proj, x2, mod, vecs, wb, wo]
    out_specs = rows(d)
    out_shape = jax.ShapeDtypeStruct((m, d), F32)
    if emit_next:
        in_specs += [_vec_spec(d, layer + 1, 1, VEC_NORM_PRE), mod_part(0), mod_part(1)]
        args += [vecs, mod_next, mod_next]
        out_specs = (out_specs, rows(d))
        out_shape = (out_shape, jax.ShapeDtypeStruct((m, d), BF16))
    return pl.pallas_call(
        functools.partial(_merge_kernel, tiles_per_batch=seq // tm, emit_next=emit_next),
        grid=(m // tm,),
        in_specs=in_specs,
        out_specs=out_specs,
        out_shape=out_shape,
        compiler_params=_params("arbitrary"),
        name="merge_out",
    )(*args)


def kernel(x, c, positions, ada_w, ada_b, norm_pre, norm_post, w_in, ret_gn, lru_conv_w, lru_conv_b,
           lru_wa, lru_ba, lru_wx, lru_bx, lru_lambda, mla_q_norm, mla_w_uq, mla_kv_norm, mla_w_ukv,
           w_branch, w_out):
    batch, seq, d = x.shape
    depth = w_in.shape[0]
    m = batch * seq
    assert d == D_MODEL and w_in.shape[2] == SRC_WIDTH and batch <= SUBLANE
    assert seq % PROJ_TM == 0 and seq % (RET_SUB * RET_T) == 0 and seq % LRU_T == 0 and seq % MLA_TQ == 0
    assert seq % MERGE_TM == 0 and seq % NORM_TM == 0

    c_pad = jnp.pad(c, ((0, SUBLANE - batch), (0, 0)))
    vecs = _pack_vectors(ada_b, norm_pre, norm_post, ret_gn, lru_conv_b, lru_ba, lru_bx, lru_lambda,
                         mla_q_norm, mla_kv_norm)
    mod = _ada_call(c_pad, ada_w, vecs, 0)
    w_in_t = jnp.swapaxes(w_in, 1, 2)

    x2 = x.reshape(m, d)
    h, cos_r, sin_r, cos_m, sin_m = _prenorm_call(x2, mod, vecs, positions, 0, seq)
    for l in range(depth):
        proj = _inproj_call(h, w_in_t, l)
        y_ret, wb, wo = _ret_call(proj, cos_r, sin_r, vecs, w_branch, w_out, l, batch, seq)
        lru_args = (proj, lru_conv_w, lru_wa, lru_wx, vecs, l, batch, seq)
        if l + 1 < depth:
            y_lru, mod_next = _lru_call(*lru_args, ada=(c_pad, ada_w))
        else:
            y_lru, mod_next = _lru_call(*lru_args), None
        y_mla = _mla_call(h, w_in_t, proj, mla_w_uq, mla_w_ukv, vecs, cos_m, sin_m, l, batch, seq)
        out = _merge_call(y_ret, y_lru, y_mla, proj, x2, mod, mod_next, vecs, wb, wo, l, seq)
        mod = mod_next
        if l + 1 < depth:
            x2, h = out
        else:
            x2 = out
    return x2.reshape(batch, seq, d)
```

```python
import functools
import math

import jax
import jax.numpy as jnp
from jax import lax
from jax.experimental import pallas as pl
from jax.experimental.pallas import tpu as pltpu

F32 = jnp.float32
BF16 = jnp.bfloat16

D_MODEL = 2048
CHUNK = 64
ROPE_BASE = 10000.0
NORM_EPS = 1e-6

RET_HEADS = 8
RET_DIM = 128
RET_WIDTH = RET_HEADS * RET_DIM

LRU_WIDTH = 1024
LRU_BLOCKS = 8
LRU_BLOCK_DIM = 128
CONV_WIDTH = 4
LRU_C = 8.0

MLA_HEADS = 8
MLA_NOPE = 128
MLA_ROPE = 64
MLA_V = 128
MLA_LORA = 512
MLA_WIDTH = MLA_HEADS * MLA_V
MLA_QK = MLA_NOPE + MLA_ROPE
MLA_QK_PAD = 256

LANE = 128
SUBLANE = 8
VMEM_LIMIT = 58 * 1024 * 1024

BRANCH_WIDTH = 1024
assert RET_WIDTH == LRU_WIDTH == MLA_WIDTH == BRANCH_WIDTH

COL_RQ = 0
COL_RK = COL_RQ + RET_WIDTH
COL_RV = COL_RK + RET_WIDTH
COL_RG = COL_RV + RET_WIDTH
COL_LX = COL_RG + RET_WIDTH
COL_LG = COL_LX + LRU_WIDTH
COL_MQ = COL_LG + LRU_WIDTH
COL_MKV = COL_MQ + MLA_LORA
COL_MG = COL_MKV + MLA_LORA
COL_MERGE = COL_MG + MLA_WIDTH
PROJ_WIDTH = COL_MERGE + 3 * D_MODEL
SRC_MKR = COL_MG
SRC_WIDTH = PROJ_WIDTH + MLA_ROPE
PROJ_TN = 1024
PROJ_TM = 2048
ALIGNED_TILES = SRC_MKR // PROJ_TN
CAST_ROWS = 64
assert SRC_MKR % PROJ_TN == 0 and PROJ_WIDTH % PROJ_TN == 0 and COL_MERGE % D_MODEL == 0

NORM_TM = 1024
RET_T = 256
LRU_T = 512
LRU_CLEN = LRU_T // SUBLANE
MLA_ROWS = 256
MLA_TQ = 256
MLA_HPS = 2
MERGE_TM = 256
ADA_TK = 512


def _silu(v):
    return v * jax.nn.sigmoid(v)


def _dot(a, b):
    return jnp.dot(a, b, preferred_element_type=F32)


def _dot_nt(a, b):
    return lax.dot_general(a, b, (((1,), (1,)), ((), ())), preferred_element_type=F32)


def _dot_tn(a, b):
    return lax.dot_general(a, b, (((0,), (0,)), ((), ())), preferred_element_type=F32)


def _rms(x, gain):
    return x * lax.rsqrt(jnp.mean(x * x, axis=-1, keepdims=True) + NORM_EPS) * gain


def _params(*sem):
    return pltpu.CompilerParams(dimension_semantics=sem, vmem_limit_bytes=VMEM_LIMIT)


VEC_ADA_B = 0
VEC_NORM_PRE = 3 * D_MODEL
VEC_NORM_POST = VEC_NORM_PRE + D_MODEL
VEC_RET_GN = VEC_NORM_POST + D_MODEL
VEC_CONV_B = VEC_RET_GN + RET_WIDTH
VEC_BA = VEC_CONV_B + LRU_WIDTH
VEC_BX = VEC_BA + LRU_WIDTH
VEC_LAM = VEC_BX + LRU_WIDTH
VEC_Q_NORM = VEC_LAM + LRU_WIDTH
VEC_KV_NORM = VEC_Q_NORM + MLA_LORA


def _pack_vectors(ada_b, norm_pre, norm_post, ret_gn, conv_b, ba, bx, lam, q_norm, kv_norm):
    packed = jnp.concatenate([ada_b, norm_pre, norm_post, ret_gn, conv_b, ba, bx, lam, q_norm, kv_norm], axis=1)
    return packed[:, None, :]


def _vec_spec(width, layer, ngrid, col):
    assert col % width == 0
    if ngrid == 1:
        return pl.BlockSpec((1, 1, width), lambda i: (layer, 0, col // width))
    return pl.BlockSpec((1, 1, width), lambda i, j: (layer, 0, col // width))


def _ada_accumulate(first, c_ref, w_ref, b_ref, o_ref):
    @pl.when(first)
    def _():
        o_ref[0] = jnp.broadcast_to(b_ref[0], o_ref.shape[1:])

    c_act = _silu(c_ref[0]).astype(BF16)
    o_ref[0] += _dot(c_act, w_ref[0].astype(BF16))


def _ada_kernel(c_ref, w_ref, b_ref, o_ref):
    _ada_accumulate(pl.program_id(0) == 0, c_ref, w_ref, b_ref, o_ref)


def _c_slabs(c_pad, n_slabs):
    rows, d = c_pad.shape
    return c_pad.reshape(rows, n_slabs, d // n_slabs).transpose(1, 0, 2)


def _ada_call(c_pad, ada_w, vecs, layer):
    _, d, n = ada_w.shape
    rows = c_pad.shape[0]
    n_slabs = d // ADA_TK
    return pl.pallas_call(
        _ada_kernel,
        grid=(n_slabs,),
        in_specs=[
            pl.BlockSpec((1, rows, ADA_TK), lambda k: (k, 0, 0)),
            pl.BlockSpec((1, ADA_TK, n), lambda k: (layer, k, 0)),
            _vec_spec(n, layer, 1, VEC_ADA_B),
        ],
        out_specs=pl.BlockSpec((1, rows, n), lambda k: (0, 0, 0)),
        out_shape=jax.ShapeDtypeStruct((1, rows, n), F32),
        compiler_params=_params("arbitrary"),
        name="ada_mod",
    )(_c_slabs(c_pad, n_slabs), ada_w, vecs)


def _modulate(x, gain, shift_ref, scale_ref, b):
    shift = shift_ref[0, pl.ds(b, 1), :]
    scale = scale_ref[0, pl.ds(b, 1), :]
    return _rms(x, gain) * (1.0 + scale) + shift


def _prenorm_kernel(x_ref, g_ref, shift_ref, scale_ref, pos_ref, freq_ref,
                    o_ref, cos_r_ref, sin_r_ref, cos_m_ref, sin_m_ref, *, tiles_per_batch):
    b = pl.program_id(0) // tiles_per_batch
    o_ref[...] = _modulate(x_ref[...], g_ref[0], shift_ref, scale_ref, b).astype(o_ref.dtype)

    ang = freq_ref[...] * pos_ref[0].astype(F32)
    cos_t = jnp.cos(ang)
    sin_t = jnp.sin(ang)
    n_r, n_m = RET_DIM // 2, MLA_ROPE // 2
    c_r, s_r = cos_t[0:n_r], sin_t[0:n_r]
    c_m, s_m = cos_t[n_r:n_r + n_m], sin_t[n_r:n_r + n_m]
    pad = jnp.zeros((LANE - MLA_ROPE, ang.shape[1]), F32)
    cos_r_ref[...] = jnp.concatenate([c_r, c_r], axis=0).T
    sin_r_ref[...] = jnp.concatenate([-s_r, s_r], axis=0).T
    cos_m_ref[...] = jnp.concatenate([c_m, c_m, pad], axis=0).T
    sin_m_ref[...] = jnp.concatenate([-s_m, s_m, pad], axis=0).T


def _prenorm_call(x2, mod, vecs, positions, layer, seq):
    m, d = x2.shape
    rows = mod.shape[1]
    n_tiles = m // NORM_TM

    def inv_freq(dim):
        return ROPE_BASE ** (-jnp.arange(0, dim, 2, dtype=F32) / dim)

    freqs = jnp.concatenate([inv_freq(RET_DIM), inv_freq(MLA_ROPE)])[:, None]
    table = pl.BlockSpec((NORM_TM, LANE), lambda i: (i, 0))
    return pl.pallas_call(
        functools.partial(_prenorm_kernel, tiles_per_batch=seq // NORM_TM),
        grid=(n_tiles,),
        in_specs=[
            pl.BlockSpec((NORM_TM, d), lambda i: (i, 0)),
            _vec_spec(d, layer, 1, VEC_NORM_PRE),
            pl.BlockSpec((1, rows, d), lambda i: (0, 0, 0)),
            pl.BlockSpec((1, rows, d), lambda i: (0, 0, 1)),
            pl.BlockSpec((1, 1, NORM_TM), lambda i: (i, 0, 0)),
            pl.BlockSpec(freqs.shape, lambda i: (0, 0)),
        ],
        out_specs=(pl.BlockSpec((NORM_TM, d), lambda i: (i, 0)), table, table, table, table),
        out_shape=(jax.ShapeDtypeStruct((m, d), BF16),) + (jax.ShapeDtypeStruct((m, LANE), F32),) * 4,
        compiler_params=_params("arbitrary"),
        name="pre_norm",
    )(x2, vecs, mod, mod, positions.reshape(n_tiles, 1, NORM_TM), freqs)


def _inproj_kernel(h_ref, w_ref, wx_ref, o_ref, wb_ref):
    j = pl.program_id(0)
    first_row_tile = pl.program_id(1) == 0
    n_chunks = PROJ_TN // CAST_ROWS

    @pl.when(first_row_tile & (j < ALIGNED_TILES))
    def _():
        def body(i, carry):
            r0 = pl.multiple_of(i * CAST_ROWS, CAST_ROWS)
            wb_ref[pl.ds(r0, CAST_ROWS), :] = w_ref[0, pl.ds(r0, CAST_ROWS), :].astype(BF16)
            return carry

        lax.fori_loop(0, n_chunks, body, 0)

    @pl.when(first_row_tile & (j >= ALIGNED_TILES))
    def _():
        def body(i, carry):
            r0 = pl.multiple_of(i * CAST_ROWS, CAST_ROWS)
            wb_ref[pl.ds(r0, CAST_ROWS), :] = w_ref[0, pl.ds(r0 + MLA_ROPE, CAST_ROWS), :].astype(BF16)
            return carry

        lax.fori_loop(0, n_chunks - 1, body, 0)
        wb_ref[PROJ_TN - MLA_ROPE:, :] = wx_ref[0].astype(BF16)

    o_ref[...] = _dot_nt(h_ref[...], wb_ref[...]).astype(o_ref.dtype)


def _inproj_call(h, w_in_t, layer):
    m, d = h.shape
    per_tile = PROJ_TN // MLA_ROPE
    last_block = SRC_WIDTH // MLA_ROPE - 1
    return pl.pallas_call(
        _inproj_kernel,
        grid=(PROJ_WIDTH // PROJ_TN, m // PROJ_TM),
        in_specs=[
            pl.BlockSpec((PROJ_TM, d), lambda j, i: (i, 0)),
            pl.BlockSpec((1, PROJ_TN, d), lambda j, i: (layer, j, 0)),
            pl.BlockSpec((1, MLA_ROPE, d),
                         lambda j, i: (layer, jnp.minimum((j + 1) * per_tile, last_block), 0)),
        ],
        out_specs=pl.BlockSpec((PROJ_TM, PROJ_TN), lambda j, i: (i, j)),
        out_shape=jax.ShapeDtypeStruct((m, PROJ_WIDTH), BF16),
        scratch_shapes=[pltpu.VMEM((PROJ_TN, d), BF16)],
        compiler_params=_params("arbitrary", "arbitrary"),
        name="in_proj",
    )(h, w_in_t, w_in_t)


_LOG_GAMMA = tuple(math.log1p(-(2.0 ** (-5.0 - h))) for h in range(RET_HEADS))


def _ret_kernel(q_ref, k_ref, v_ref, g_ref, cos_ref, sin_ref, gn_ref, wb_ref, wo_ref,
                o_ref, wb_out, wo_out, state_ref, dmat_ref, dq_ref, dk_ref):
    t_blk = RET_T
    wb_out[...] = wb_ref[0].astype(BF16)
    wo_out[...] = wo_ref[0].astype(BF16)

    @pl.when((pl.program_id(0) == 0) & (pl.program_id(1) == 0))
    def _():
        ri = lax.broadcasted_iota(jnp.int32, (t_blk, t_blk), 0)
        ci = lax.broadcasted_iota(jnp.int32, (t_blk, t_blk), 1)
        dist = jnp.abs(ri - ci).astype(F32)
        visible = (ci // CHUNK) <= (ri // CHUNK)
        row = lax.broadcasted_iota(jnp.int32, (t_blk, RET_DIM), 0).astype(F32)
        for h in range(RET_HEADS):
            lg = _LOG_GAMMA[h]
            dmat_ref[h] = jnp.where(visible, jnp.exp(lg * dist), 0.0)
            dq_ref[h] = jnp.exp(lg * (row + 1.0))
            dk_ref[h] = jnp.exp(lg * ((t_blk - 1.0) - row))

    @pl.when(pl.program_id(1) == 0)
    def _():
        state_ref[...] = jnp.zeros_like(state_ref)

    cos = cos_ref[...]
    sin = sin_ref[...]
    for h in range(RET_HEADS):
        sl = slice(h * RET_DIM, (h + 1) * RET_DIM)
        q = q_ref[:, sl].astype(F32)
        k = k_ref[:, sl].astype(F32)
        q = (q * cos + pltpu.roll(q, RET_DIM // 2, 1) * sin) * (RET_DIM ** -0.5)
        k = k * cos + pltpu.roll(k, RET_DIM // 2, 1) * sin
        v = v_ref[:, sl]
        scores = _dot_nt(q.astype(BF16), k.astype(BF16)) * dmat_ref[h]
        o = _dot(scores.astype(BF16), v)
        state = state_ref[h]
        o = o + _dot((q * dq_ref[h]).astype(BF16), state.astype(BF16))
        k_dec = (k * dk_ref[h]).astype(BF16)
        state_ref[h] = state * math.exp(_LOG_GAMMA[h] * t_blk) + _dot_tn(k_dec, v)
        mean = jnp.mean(o, axis=-1, keepdims=True)
        cen = o - mean
        var = jnp.mean(cen * cen, axis=-1, keepdims=True)
        normed = cen * lax.rsqrt(var + NORM_EPS) * gn_ref[0, :, sl]
        o_ref[:, sl] = (normed * _silu(g_ref[:, sl].astype(F32))).astype(o_ref.dtype)


def _ret_call(proj, cos_r, sin_r, vecs, w_branch, w_out, layer, batch, seq):
    m = proj.shape[0]
    rows = RET_T
    nt = seq // rows
    w = RET_WIDTH
    steps = batch * nt
    wb_rows, wo_rows = w_branch.shape[1] // steps, w_out.shape[1] // steps
    bf16_rows = 2 * SUBLANE
    assert wb_rows * steps == w_branch.shape[1] and wb_rows % bf16_rows == 0
    assert wo_rows * steps == w_out.shape[1] and wo_rows % bf16_rows == 0
    d = w_branch.shape[2]

    def col(c):
        return pl.BlockSpec((rows, w), lambda b, t, c=c: (b * nt + t, c // w))

    tab = pl.BlockSpec((rows, RET_DIM), lambda b, t: (b * nt + t, 0))
    return pl.pallas_call(
        _ret_kernel,
        grid=(batch, nt),
        in_specs=[col(COL_RQ), col(COL_RK), col(COL_RV), col(COL_RG), tab, tab,
                  _vec_spec(w, layer, 2, VEC_RET_GN),
                  pl.BlockSpec((1, wb_rows, d), lambda b, t: (layer, b * nt + t, 0)),
                  pl.BlockSpec((1, wo_rows, d), lambda b, t: (layer, b * nt + t, 0))],
        out_specs=(pl.BlockSpec((rows, w), lambda b, t: (b * nt + t, 0)),
                   pl.BlockSpec((wb_rows, d), lambda b, t: (b * nt + t, 0)),
                   pl.BlockSpec((wo_rows, d), lambda b, t: (b * nt + t, 0))),
        out_shape=(jax.ShapeDtypeStruct((m, w), BF16),
                   jax.ShapeDtypeStruct(w_branch.shape[1:], BF16),
                   jax.ShapeDtypeStruct(w_out.shape[1:], BF16)),
        scratch_shapes=[
            pltpu.VMEM((RET_HEADS, RET_DIM, RET_DIM), F32),
            pltpu.VMEM((RET_HEADS, RET_T, RET_T), F32),
            pltpu.VMEM((RET_HEADS, RET_T, RET_DIM), F32),
            pltpu.VMEM((RET_HEADS, RET_T, RET_DIM), F32),
        ],
        compiler_params=_params("arbitrary", "arbitrary"),
        name="retention",
    )(proj, proj, proj, proj, cos_r, sin_r, vecs, w_branch, w_out)


def _sublane_scan(a, b, row):
    for s in (1, 2, 4):
        keep = row >= s
        a_prev = jnp.where(keep, pltpu.roll(a, s, 0), 1.0)
        b_prev = jnp.where(keep, pltpu.roll(b, s, 0), 0.0)
        b = a * b_prev + b
        a = a * a_prev
    return a, b


def _lru_kernel(x_ref, g_ref, cw_ref, cb_ref, wa_ref, ba_ref, wx_ref, bx_ref, lam_ref, *rest, with_ada):
    if with_ada:
        c_ref, aw_ref, ab_ref, o_ref, mod_ref, perm, perm_t, halo, hcar, a_s, b_s = rest
        _ada_accumulate((pl.program_id(0) == 0) & (pl.program_id(1) == 0), c_ref, aw_ref, ab_ref, mod_ref)
    else:
        o_ref, perm, perm_t, halo, hcar, a_s, b_s = rest
    t_blk = LRU_T
    clen = LRU_CLEN
    taps = CONV_WIDTH - 1
    row = lax.broadcasted_iota(jnp.int32, (SUBLANE, LRU_WIDTH), 0)

    @pl.when((pl.program_id(0) == 0) & (pl.program_id(1) == 0))
    def _():
        r = lax.broadcasted_iota(jnp.int32, (t_blk, t_blk), 0)
        c = lax.broadcasted_iota(jnp.int32, (t_blk, t_blk), 1)
        perm[...] = jnp.where(c == (r % SUBLANE) * clen + r // SUBLANE, 1.0, 0.0).astype(BF16)
        perm_t[...] = jnp.where(c == (r % clen) * SUBLANE + r // clen, 1.0, 0.0).astype(BF16)

    @pl.when(pl.program_id(1) == 0)
    def _():
        halo[...] = jnp.zeros_like(halo)
        hcar[...] = jnp.zeros_like(hcar)

    xp = _dot(perm[...], x_ref[...])
    pieces = []
    for k in range(taps):
        cur = xp[t_blk - (taps - k) * SUBLANE:t_blk - (taps - k - 1) * SUBLANE, :]
        prev = halo[k * SUBLANE:(k + 1) * SUBLANE, :]
        pieces.append(jnp.where(row == 0, pltpu.roll(prev, 1, 0), pltpu.roll(cur, 1, 0)))
    halo[...] = xp[t_blk - taps * SUBLANE:, :]
    xext = jnp.concatenate(pieces + [xp], axis=0)
    xc = cb_ref[0]
    for w in range(CONV_WIDTH):
        xc = xc + cw_ref[0, w:w + 1, :] * xext[w * SUBLANE:w * SUBLANE + t_blk, :]

    neg_lam = -lam_ref[0]
    softplus = jnp.maximum(neg_lam, 0.0) + jnp.log1p(jnp.exp(-jnp.abs(neg_lam)))
    neg_half_rate = (-0.5 * LRU_C) * softplus
    for n in range(LRU_BLOCKS):
        sl = slice(n * LRU_BLOCK_DIM, (n + 1) * LRU_BLOCK_DIM)
        xn = xc[:, sl]
        xn_b = xn.astype(BF16)
        t_r = jnp.tanh(0.5 * (_dot(xn_b, wa_ref[0, n].astype(BF16)) + ba_ref[0, :, sl]))
        t_i = jnp.tanh(0.5 * (_dot(xn_b, wx_ref[0, n].astype(BF16)) + bx_ref[0, :, sl]))
        half_rate = neg_half_rate[:, sl]
        log_a = half_rate * t_r + half_rate
        a = jnp.exp(log_a)
        a_s[:, sl] = a
        var = jnp.tanh(-log_a) * (a * a + 1.0)
        std = jnp.where(var > 0.0, var * lax.rsqrt(var), 0.0)
        half_x = 0.5 * xn
        b_s[:, sl] = std * (half_x * t_i + half_x)

    def body(j, carry):
        h, prod = carry
        r0 = j * SUBLANE
        a = a_s[pl.ds(r0, SUBLANE), :]
        h = a * h + b_s[pl.ds(r0, SUBLANE), :]
        prod = a * prod
        b_s[pl.ds(r0, SUBLANE), :] = h
        a_s[pl.ds(r0, SUBLANE), :] = prod
        return h, prod

    zeros = jnp.zeros((SUBLANE, LRU_WIDTH), F32)
    carry = (zeros, zeros + 1.0)
    for j in range(clen):
        carry = body(j, carry)
    h_end, a_end = carry

    a_inc, h_inc = _sublane_scan(a_end, h_end, row)
    h_prev_tile = hcar[...]
    h_chunk_end = a_inc * h_prev_tile + h_inc
    h_chunk_start = jnp.where(row == 0, h_prev_tile, pltpu.roll(h_chunk_end, 1, 0))
    hcar[...] = jnp.broadcast_to(h_chunk_end[SUBLANE - 1:SUBLANE, :], (SUBLANE, LRU_WIDTH))

    h_true = b_s[...] + a_s[...] * jnp.concatenate([h_chunk_start] * clen, axis=0)
    gp = _dot(perm[...], g_ref[...])
    out_p = (h_true * _silu(gp)).astype(BF16)
    o_ref[...] = _dot(perm_t[...], out_p).astype(o_ref.dtype)


def _lru_call(proj, cw, wa, wx, vecs, layer, batch, seq, ada=None):
    m = proj.shape[0]
    nt = seq // LRU_T
    w = LRU_WIDTH

    def col(c):
        return pl.BlockSpec((LRU_T, w), lambda b, t, c=c: (b * nt + t, c // w))

    def vec(offset):
        return _vec_spec(w, layer, 2, offset)

    def blk():
        return pl.BlockSpec((1, LRU_BLOCKS, LRU_BLOCK_DIM, LRU_BLOCK_DIM), lambda b, t: (layer, 0, 0, 0))

    in_specs = [col(COL_LX), col(COL_LG),
                pl.BlockSpec((1, CONV_WIDTH, w), lambda b, t: (layer, 0, 0)), vec(VEC_CONV_B),
                blk(), vec(VEC_BA), blk(), vec(VEC_BX), vec(VEC_LAM)]
    args = [proj, proj, cw, vecs, wa, vecs, wx, vecs, vecs]
    out_specs = pl.BlockSpec((LRU_T, w), lambda b, t: (b * nt + t, 0))
    out_shape = jax.ShapeDtypeStruct((m, w), BF16)
    if ada is not None:
        c_pad, ada_w = ada
        rows = c_pad.shape[0]
        _, d, n = ada_w.shape
        steps = batch * nt
        slab = d // steps
        assert slab * steps == d and slab % LANE == 0
        in_specs += [pl.BlockSpec((1, rows, slab), lambda b, t: (b * nt + t, 0, 0)),
                     pl.BlockSpec((1, slab, n), lambda b, t: (layer + 1, b * nt + t, 0)),
                     _vec_spec(n, layer + 1, 2, VEC_ADA_B)]
        args += [_c_slabs(c_pad, steps), ada_w, vecs]
        out_specs = (out_specs, pl.BlockSpec((1, rows, n), lambda b, t: (0, 0, 0)))
        out_shape = (out_shape, jax.ShapeDtypeStruct((1, rows, n), F32))
    return pl.pallas_call(
        functools.partial(_lru_kernel, with_ada=ada is not None),
        grid=(batch, nt),
        in_specs=in_specs,
        out_specs=out_specs,
        out_shape=out_shape,
        scratch_shapes=[
            pltpu.VMEM((LRU_T, LRU_T), BF16),
            pltpu.VMEM((LRU_T, LRU_T), BF16),
            pltpu.VMEM(((CONV_WIDTH - 1) * SUBLANE, w), F32),
            pltpu.VMEM((SUBLANE, w), F32),
            pltpu.VMEM((LRU_T, w), F32),
            pltpu.VMEM((LRU_T, w), F32),
        ],
        compiler_params=_params("arbitrary", "arbitrary"),
        name="rg_lru",
    )(*args)


def _rope64(x, cos_t, sin_t):
    swapped = pltpu.roll(x, MLA_ROPE // 2, 1) + pltpu.roll(x, LANE - MLA_ROPE // 2, 1)
    return x * cos_t + swapped * sin_t


def _mla_kernel(h_ref, wkr_ref, mq_ref, mkv_ref, mg_ref, wq_ref, wkv_ref, qn_ref, kvn_ref,
                cos_ref, sin_ref, o_ref, qlat, kvlat, krope, wq_s, qs, ks, vs, s_scr, p_scr, *, seq):
    rows = MLA_ROWS
    pair = pl.program_id(1)

    @pl.when(pair == 0)
    def _():
        wkr = jnp.concatenate([wkr_ref[0], jnp.zeros((LANE - MLA_ROPE, D_MODEL), F32)], axis=0).astype(BF16)
        for i in range(seq // rows):
            rs = slice(i * rows, (i + 1) * rows)
            qlat[rs, :] = _rms(mq_ref[rs, :].astype(F32), qn_ref[0]).astype(BF16)
            kvlat[rs, :] = _rms(mkv_ref[rs, :].astype(F32), kvn_ref[0]).astype(BF16)
            kr = _dot_nt(h_ref[rs, :], wkr)
            krope[rs, :] = _rope64(kr, cos_ref[rs, :], sin_ref[rs, :]).astype(BF16)
        for hh in range(MLA_HPS):
            vs[hh, :, MLA_V:] = jnp.ones((seq, MLA_V), BF16)
        zeros = jnp.zeros((MLA_LORA, MLA_QK_PAD - MLA_QK), F32)
        wq_all = wq_ref[0]
        for hh in range(MLA_HEADS):
            w_head = wq_all[:, hh * MLA_QK:(hh + 1) * MLA_QK]
            wq_s[hh] = jnp.concatenate([w_head, zeros], axis=1).astype(BF16)

    scale = (MLA_QK ** -0.5) * math.log2(math.e)
    tq = MLA_TQ
    ri = lax.broadcasted_iota(jnp.int32, (tq, tq), 0)
    ci = lax.broadcasted_iota(jnp.int32, (tq, tq), 1)
    visible = (ci // CHUNK) <= (ri // CHUNK)
    n_q = seq // tq
    kv_w = MLA_NOPE + MLA_V

    for hh in range(MLA_HPS):
        wq = wq_s[pair * MLA_HPS + hh]
        wkv = wkv_ref[0, :, hh * kv_w:(hh + 1) * kv_w].astype(BF16)
        for i in range(seq // rows):
            rs = slice(i * rows, (i + 1) * rows)
            q = _dot(qlat[rs, :], wq)
            qs[hh, rs, 0:MLA_NOPE] = (q[:, 0:MLA_NOPE] * scale).astype(BF16)
            qs[hh, rs, MLA_NOPE:] = (_rope64(q[:, MLA_NOPE:], cos_ref[rs, :], sin_ref[rs, :]) * scale).astype(BF16)
            kv = _dot(kvlat[rs, :], wkv)
            ks[hh, rs, 0:MLA_NOPE] = kv[:, 0:MLA_NOPE].astype(BF16)
            ks[hh, rs, MLA_NOPE:] = krope[rs, :]
            vs[hh, rs, 0:MLA_V] = kv[:, MLA_NOPE:].astype(BF16)

    def scores(hh, i):
        q0, q1 = i * tq, (i + 1) * tq
        s_scr[hh, i % 2, :, 0:q1] = _dot_nt(qs[hh, q0:q1, :], ks[hh, 0:q1, :])

    def probs(hh, i):
        q0, q1 = i * tq, (i + 1) * tq
        s_d = jnp.where(visible, s_scr[hh, i % 2, :, q0:q1], -1e30)
        m = jnp.max(s_d, axis=-1, keepdims=True)
        if i > 0:
            m = jnp.maximum(m, jnp.max(s_scr[hh, i % 2, :, 0:q0], axis=-1, keepdims=True))
            p_scr[hh, i % 2, :, 0:q0] = jnp.exp2(s_scr[hh, i % 2, :, 0:q0] - m).astype(BF16)
        p_scr[hh, i % 2, :, q0:q1] = jnp.exp2(s_d - m).astype(BF16)

    def values(hh, i):
        q0, q1 = i * tq, (i + 1) * tq
        cols = slice(hh * MLA_V, (hh + 1) * MLA_V)
        acc = _dot(p_scr[hh, i % 2, :, 0:q1], vs[hh, 0:q1, :])
        out = acc[:, 0:MLA_V] / acc[:, MLA_V:]
        o_ref[q0:q1, cols] = (out * _silu(mg_ref[q0:q1, cols].astype(F32))).astype(o_ref.dtype)

    for hh in range(MLA_HPS):
        scores(hh, 0)
    for i in range(n_q + 1):
        for hh in range(MLA_HPS):
            if i + 1 < n_q:
                scores(hh, i + 1)
            if i < n_q:
                probs(hh, i)
            if i > 0:
                values(hh, i - 1)


def _mla_call(h, w_in_t, proj, w_uq, w_ukv, vecs, cos_m, sin_m, layer, batch, seq):
    m, d = h.shape
    lora = MLA_LORA
    hps = MLA_HPS
    assert MLA_HEADS % hps == 0 and COL_MG % (hps * MLA_V) == 0
    return pl.pallas_call(
        functools.partial(_mla_kernel, seq=seq),
        grid=(batch, MLA_HEADS // hps),
        in_specs=[
            pl.BlockSpec((seq, d), lambda b, hd: (b, 0)),
            pl.BlockSpec((1, MLA_ROPE, d), lambda b, hd: (layer, SRC_MKR // MLA_ROPE, 0)),
            pl.BlockSpec((seq, lora), lambda b, hd: (b, COL_MQ // lora)),
            pl.BlockSpec((seq, lora), lambda b, hd: (b, COL_MKV // lora), pipeline_mode=pl.Buffered(1)),
            pl.BlockSpec((seq, hps * MLA_V), lambda b, hd: (b, COL_MG // (hps * MLA_V) + hd)),
            pl.BlockSpec((1, lora, MLA_HEADS * MLA_QK), lambda b, hd: (layer, 0, 0),
                         pipeline_mode=pl.Buffered(1)),
            pl.BlockSpec((1, lora, hps * (MLA_NOPE + MLA_V)), lambda b, hd: (layer, 0, hd)),
            _vec_spec(lora, layer, 2, VEC_Q_NORM),
            _vec_spec(lora, layer, 2, VEC_KV_NORM),
            pl.BlockSpec((seq, LANE), lambda b, hd: (b, 0), pipeline_mode=pl.Buffered(1)),
            pl.BlockSpec((seq, LANE), lambda b, hd: (b, 0), pipeline_mode=pl.Buffered(1)),
        ],
        out_specs=pl.BlockSpec((seq, hps * MLA_V), lambda b, hd: (b, hd)),
        out_shape=jax.ShapeDtypeStruct((m, MLA_WIDTH), BF16),
        scratch_shapes=[
            pltpu.VMEM((seq, lora), BF16),
            pltpu.VMEM((seq, lora), BF16),
            pltpu.VMEM((seq, LANE), BF16),
            pltpu.VMEM((MLA_HEADS, lora, MLA_QK_PAD), BF16),
            pltpu.VMEM((hps, seq, MLA_QK_PAD), BF16),
            pltpu.VMEM((hps, seq, MLA_QK_PAD), BF16),
            pltpu.VMEM((hps, seq, 2 * MLA_V), BF16),
            pltpu.VMEM((hps, 2, MLA_TQ, seq), F32),
            pltpu.VMEM((hps, 2, MLA_TQ, seq), BF16),
        ],
        compiler_params=_params("arbitrary", "arbitrary"),
        name="mla",
    )(h, w_in_t, proj, proj, proj, w_uq, w_ukv, vecs, vecs, cos_m, sin_m)


def _merge_kernel(yr_ref, yl_ref, ym_ref, l0_ref, l1_ref, l2_ref, x_ref, res_ref, g_ref,
                  wb_ref, wo_ref, *rest, tiles_per_batch, emit_next):
    b = pl.program_id(0) // tiles_per_batch
    merged = None
    for i, (y_ref, l_ref) in enumerate(((yr_ref, l0_ref), (yl_ref, l1_ref), (ym_ref, l2_ref))):
        z = _dot(y_ref[...], wb_ref[i * BRANCH_WIDTH:(i + 1) * BRANCH_WIDTH, :])
        z = z * jax.nn.sigmoid(l_ref[...].astype(F32))
        merged = z if merged is None else merged + z
    y = _dot(merged.astype(BF16), wo_ref[...])
    post = g_ref[0] * (1.0 + res_ref[0, pl.ds(b, 1), :])
    inv = lax.rsqrt(jnp.mean(y * y, axis=-1, keepdims=True) + NORM_EPS)
    x_new = x_ref[...] + (y * inv) * post
    if emit_next:
        gn_ref, shift_ref, scale_ref, o_ref, h_ref = rest
        pre = gn_ref[0] * (1.0 + scale_ref[0, pl.ds(b, 1), :])
        inv_n = lax.rsqrt(jnp.mean(x_new * x_new, axis=-1, keepdims=True) + NORM_EPS)
        h_ref[...] = ((x_new * inv_n) * pre + shift_ref[0, pl.ds(b, 1), :]).astype(h_ref.dtype)
    else:
        (o_ref,) = rest
    o_ref[...] = x_new


def _merge_call(y_ret, y_lru, y_mla, proj, x2, mod, mod_next, vecs, wb, wo, layer, seq):
    m, d = x2.shape
    tm = MERGE_TM
    mod_rows = mod.shape[1]
    emit_next = mod_next is not None

    def rows(width, c=0):
        return pl.BlockSpec((tm, width), lambda i, c=c: (i, c))

    def resident(shape):
        return pl.BlockSpec(shape, lambda i: (0, 0), pipeline_mode=pl.Buffered(1))

    def mod_part(part):
        return pl.BlockSpec((1, mod_rows, d), lambda i: (0, 0, part))

    merge_col = COL_MERGE // d
    in_specs = [rows(BRANCH_WIDTH), rows(BRANCH_WIDTH), rows(BRANCH_WIDTH),
                rows(d, merge_col), rows(d, merge_col + 1), rows(d, merge_col + 2),
                rows(d), mod_part(2), _vec_spec(d, layer, 1, VEC_NORM_POST),
                resident(wb.shape), resident(wo.shape)]
    args = [y_ret, y_lru, y_mla, proj, proj, proj, x2, mod, vecs, wb, wo]
    out_specs = rows(d)
    out_shape = jax.ShapeDtypeStruct((m, d), F32)
    if emit_next:
        in_specs += [_vec_spec(d, layer + 1, 1, VEC_NORM_PRE), mod_part(0), mod_part(1)]
        args += [vecs, mod_next, mod_next]
        out_specs = (out_specs, rows(d))
        out_shape = (out_shape, jax.ShapeDtypeStruct((m, d), BF16))
    return pl.pallas_call(
        functools.partial(_merge_kernel, tiles_per_batch=seq // tm, emit_next=emit_next),
        grid=(m // tm,),
        in_specs=in_specs,
        out_specs=out_specs,
        out_shape=out_shape,
        compiler_params=_params("arbitrary"),
        name="merge_out",
    )(*args)


def kernel(x, c, positions, ada_w, ada_b, norm_pre, norm_post, w_in, ret_gn, lru_conv_w, lru_conv_b,
           lru_wa, lru_ba, lru_wx, lru_bx, lru_lambda, mla_q_norm, mla_w_uq, mla_kv_norm, mla_w_ukv,
           w_branch, w_out):
    batch, seq, d = x.shape
    depth = w_in.shape[0]
    m = batch * seq
    assert d == D_MODEL and w_in.shape[2] == SRC_WIDTH and batch <= SUBLANE
    assert seq % PROJ_TM == 0 and seq % RET_T == 0 and seq % LRU_T == 0 and seq % MLA_TQ == 0
    assert seq % MERGE_TM == 0 and seq % NORM_TM == 0

    c_pad = jnp.pad(c, ((0, SUBLANE - batch), (0, 0)))
    vecs = _pack_vectors(ada_b, norm_pre, norm_post, ret_gn, lru_conv_b, lru_ba, lru_bx, lru_lambda,
                         mla_q_norm, mla_kv_norm)
    mod = _ada_call(c_pad, ada_w, vecs, 0)
    w_in_t = jnp.swapaxes(w_in, 1, 2)

    x2 = x.reshape(m, d)
    h, cos_r, sin_r, cos_m, sin_m = _prenorm_call(x2, mod, vecs, positions, 0, seq)
    for l in range(depth):
        proj = _inproj_call(h, w_in_t, l)
        y_ret, wb, wo = _ret_call(proj, cos_r, sin_r, vecs, w_branch, w_out, l, batch, seq)
        lru_args = (proj, lru_conv_w, lru_wa, lru_wx, vecs, l, batch, seq)
        if l + 1 < depth:
            y_lru, mod_next = _lru_call(*lru_args, ada=(c_pad, ada_w))
        else:
            y_lru, mod_next = _lru_call(*lru_args), None
        y_mla = _mla_call(h, w_in_t, proj, mla_w_uq, mla_w_ukv, vecs, cos_m, sin_m, l, batch, seq)
        out = _merge_call(y_ret, y_lru, y_mla, proj, x2, mod, mod_next, vecs, wb, wo, l, seq)
        mod = mod_next
        if l + 1 < depth:
            x2, h = out
        else:
            x2 = out
    return x2.reshape(batch, seq, d)
```

```python
import functools
import math

import jax
import jax.numpy as jnp
from jax import lax
from jax.experimental import pallas as pl
from jax.experimental.pallas import tpu as pltpu

F32 = jnp.float32
BF16 = jnp.bfloat16

D_MODEL = 2048
CHUNK = 64
ROPE_BASE = 10000.0
NORM_EPS = 1e-6

RET_HEADS = 8
RET_DIM = 128
RET_WIDTH = RET_HEADS * RET_DIM

LRU_WIDTH = 1024
LRU_BLOCKS = 8
LRU_BLOCK_DIM = 128
CONV_WIDTH = 4
LRU_C = 8.0

MLA_HEADS = 8
MLA_NOPE = 128
MLA_ROPE = 64
MLA_V = 128
MLA_LORA = 512
MLA_WIDTH = MLA_HEADS * MLA_V
MLA_QK = MLA_NOPE + MLA_ROPE
MLA_QK_PAD = 256

LANE = 128
SUBLANE = 8
VMEM_LIMIT = 58 * 1024 * 1024

BRANCH_WIDTH = 1024
assert RET_WIDTH == LRU_WIDTH == MLA_WIDTH == BRANCH_WIDTH

COL_RQ = 0
COL_RK = COL_RQ + RET_WIDTH
COL_RV = COL_RK + RET_WIDTH
COL_RG = COL_RV + RET_WIDTH
COL_LX = COL_RG + RET_WIDTH
COL_LG = COL_LX + LRU_WIDTH
COL_MQ = COL_LG + LRU_WIDTH
COL_MKV = COL_MQ + MLA_LORA
COL_MG = COL_MKV + MLA_LORA
COL_MERGE = COL_MG + MLA_WIDTH
PROJ_WIDTH = COL_MERGE + 3 * D_MODEL
SRC_MKR = COL_MG
SRC_WIDTH = PROJ_WIDTH + MLA_ROPE
PROJ_TN = 1024
PROJ_TM = 2048
ALIGNED_TILES = SRC_MKR // PROJ_TN
CAST_ROWS = 64
assert SRC_MKR % PROJ_TN == 0 and PROJ_WIDTH % PROJ_TN == 0 and COL_MERGE % D_MODEL == 0

NORM_TM = 1024
RET_T = 256
LRU_T = 512
LRU_CLEN = LRU_T // SUBLANE
MLA_ROWS = 256
MLA_TQ = 256
MLA_HPS = 2
MERGE_TM = 256
ADA_TK = 512


def _silu(v):
    return v * jax.nn.sigmoid(v)


def _dot(a, b):
    return jnp.dot(a, b, preferred_element_type=F32)


def _dot_nt(a, b):
    return lax.dot_general(a, b, (((1,), (1,)), ((), ())), preferred_element_type=F32)


def _dot_tn(a, b):
    return lax.dot_general(a, b, (((0,), (0,)), ((), ())), preferred_element_type=F32)


def _rms(x, gain):
    return x * lax.rsqrt(jnp.mean(x * x, axis=-1, keepdims=True) + NORM_EPS) * gain


def _params(*sem):
    return pltpu.CompilerParams(dimension_semantics=sem, vmem_limit_bytes=VMEM_LIMIT)


VEC_ADA_B = 0
VEC_NORM_PRE = 3 * D_MODEL
VEC_NORM_POST = VEC_NORM_PRE + D_MODEL
VEC_RET_GN = VEC_NORM_POST + D_MODEL
VEC_CONV_B = VEC_RET_GN + RET_WIDTH
VEC_BA = VEC_CONV_B + LRU_WIDTH
VEC_BX = VEC_BA + LRU_WIDTH
VEC_LAM = VEC_BX + LRU_WIDTH
VEC_Q_NORM = VEC_LAM + LRU_WIDTH
VEC_KV_NORM = VEC_Q_NORM + MLA_LORA


def _pack_vectors(ada_b, norm_pre, norm_post, ret_gn, conv_b, ba, bx, lam, q_norm, kv_norm):
    packed = jnp.concatenate([ada_b, norm_pre, norm_post, ret_gn, conv_b, ba, bx, lam, q_norm, kv_norm], axis=1)
    return packed[:, None, :]


def _vec_spec(width, layer, ngrid, col):
    assert col % width == 0
    if ngrid == 1:
        return pl.BlockSpec((1, 1, width), lambda i: (layer, 0, col // width))
    return pl.BlockSpec((1, 1, width), lambda i, j: (layer, 0, col // width))


def _ada_accumulate(first, c_ref, w_ref, b_ref, o_ref):
    @pl.when(first)
    def _():
        o_ref[0] = jnp.broadcast_to(b_ref[0], o_ref.shape[1:])

    c_act = _silu(c_ref[0]).astype(BF16)
    o_ref[0] += _dot(c_act, w_ref[0].astype(BF16))


def _ada_kernel(c_ref, w_ref, b_ref, o_ref):
    _ada_accumulate(pl.program_id(0) == 0, c_ref, w_ref, b_ref, o_ref)


def _c_slabs(c_pad, n_slabs):
    rows, d = c_pad.shape
    return c_pad.reshape(rows, n_slabs, d // n_slabs).transpose(1, 0, 2)


def _ada_call(c_pad, ada_w, vecs, layer):
    _, d, n = ada_w.shape
    rows = c_pad.shape[0]
    n_slabs = d // ADA_TK
    return pl.pallas_call(
        _ada_kernel,
        grid=(n_slabs,),
        in_specs=[
            pl.BlockSpec((1, rows, ADA_TK), lambda k: (k, 0, 0)),
            pl.BlockSpec((1, ADA_TK, n), lambda k: (layer, k, 0)),
            _vec_spec(n, layer, 1, VEC_ADA_B),
        ],
        out_specs=pl.BlockSpec((1, rows, n), lambda k: (0, 0, 0)),
        out_shape=jax.ShapeDtypeStruct((1, rows, n), F32),
        compiler_params=_params("arbitrary"),
        name="ada_mod",
    )(_c_slabs(c_pad, n_slabs), ada_w, vecs)


def _modulate(x, gain, shift_ref, scale_ref, b):
    shift = shift_ref[0, pl.ds(b, 1), :]
    scale = scale_ref[0, pl.ds(b, 1), :]
    return _rms(x, gain) * (1.0 + scale) + shift


def _prenorm_kernel(x_ref, g_ref, shift_ref, scale_ref, pos_ref, freq_ref,
                    o_ref, cos_r_ref, sin_r_ref, cos_m_ref, sin_m_ref, *, tiles_per_batch):
    b = pl.program_id(0) // tiles_per_batch
    o_ref[...] = _modulate(x_ref[...], g_ref[0], shift_ref, scale_ref, b).astype(o_ref.dtype)

    ang = freq_ref[...] * pos_ref[0].astype(F32)
    cos_t = jnp.cos(ang)
    sin_t = jnp.sin(ang)
    n_r, n_m = RET_DIM // 2, MLA_ROPE // 2
    c_r, s_r = cos_t[0:n_r], sin_t[0:n_r]
    c_m, s_m = cos_t[n_r:n_r + n_m], sin_t[n_r:n_r + n_m]
    pad = jnp.zeros((LANE - MLA_ROPE, ang.shape[1]), F32)
    cos_r_ref[...] = jnp.concatenate([c_r, c_r], axis=0).T
    sin_r_ref[...] = jnp.concatenate([-s_r, s_r], axis=0).T
    cos_m_ref[...] = jnp.concatenate([c_m, c_m, pad], axis=0).T
    sin_m_ref[...] = jnp.concatenate([-s_m, s_m, pad], axis=0).T


def _prenorm_call(x2, mod, vecs, positions, layer, seq):
    m, d = x2.shape
    rows = mod.shape[1]
    n_tiles = m // NORM_TM

    def inv_freq(dim):
        return ROPE_BASE ** (-jnp.arange(0, dim, 2, dtype=F32) / dim)

    freqs = jnp.concatenate([inv_freq(RET_DIM), inv_freq(MLA_ROPE)])[:, None]
    table = pl.BlockSpec((NORM_TM, LANE), lambda i: (i, 0))
    return pl.pallas_call(
        functools.partial(_prenorm_kernel, tiles_per_batch=seq // NORM_TM),
        grid=(n_tiles,),
        in_specs=[
            pl.BlockSpec((NORM_TM, d), lambda i: (i, 0)),
            _vec_spec(d, layer, 1, VEC_NORM_PRE),
            pl.BlockSpec((1, rows, d), lambda i: (0, 0, 0)),
            pl.BlockSpec((1, rows, d), lambda i: (0, 0, 1)),
            pl.BlockSpec((1, 1, NORM_TM), lambda i: (i, 0, 0)),
            pl.BlockSpec(freqs.shape, lambda i: (0, 0)),
        ],
        out_specs=(pl.BlockSpec((NORM_TM, d), lambda i: (i, 0)), table, table, table, table),
        out_shape=(jax.ShapeDtypeStruct((m, d), BF16),) + (jax.ShapeDtypeStruct((m, LANE), F32),) * 4,
        compiler_params=_params("arbitrary"),
        name="pre_norm",
    )(x2, vecs, mod, mod, positions.reshape(n_tiles, 1, NORM_TM), freqs)


def _inproj_kernel(h_ref, w_ref, wx_ref, o_ref, wb_ref):
    j = pl.program_id(0)
    first_row_tile = pl.program_id(1) == 0
    n_chunks = PROJ_TN // CAST_ROWS

    @pl.when(first_row_tile & (j < ALIGNED_TILES))
    def _():
        def body(i, carry):
            r0 = pl.multiple_of(i * CAST_ROWS, CAST_ROWS)
            wb_ref[pl.ds(r0, CAST_ROWS), :] = w_ref[0, pl.ds(r0, CAST_ROWS), :].astype(BF16)
            return carry

        lax.fori_loop(0, n_chunks, body, 0)

    @pl.when(first_row_tile & (j >= ALIGNED_TILES))
    def _():
        def body(i, carry):
            r0 = pl.multiple_of(i * CAST_ROWS, CAST_ROWS)
            wb_ref[pl.ds(r0, CAST_ROWS), :] = w_ref[0, pl.ds(r0 + MLA_ROPE, CAST_ROWS), :].astype(BF16)
            return carry

        lax.fori_loop(0, n_chunks - 1, body, 0)
        wb_ref[PROJ_TN - MLA_ROPE:, :] = wx_ref[0].astype(BF16)

    o_ref[...] = _dot_nt(h_ref[...], wb_ref[...]).astype(o_ref.dtype)


def _inproj_call(h, w_in_t, layer):
    m, d = h.shape
    per_tile = PROJ_TN // MLA_ROPE
    last_block = SRC_WIDTH // MLA_ROPE - 1
    return pl.pallas_call(
        _inproj_kernel,
        grid=(PROJ_WIDTH // PROJ_TN, m // PROJ_TM),
        in_specs=[
            pl.BlockSpec((PROJ_TM, d), lambda j, i: (i, 0)),
            pl.BlockSpec((1, PROJ_TN, d), lambda j, i: (layer, j, 0)),
            pl.BlockSpec((1, MLA_ROPE, d),
                         lambda j, i: (layer, jnp.minimum((j + 1) * per_tile, last_block), 0)),
        ],
        out_specs=pl.BlockSpec((PROJ_TM, PROJ_TN), lambda j, i: (i, j)),
        out_shape=jax.ShapeDtypeStruct((m, PROJ_WIDTH), BF16),
        scratch_shapes=[pltpu.VMEM((PROJ_TN, d), BF16)],
        compiler_params=_params("arbitrary", "arbitrary"),
        name="in_proj",
    )(h, w_in_t, w_in_t)


_LOG_GAMMA = tuple(math.log1p(-(2.0 ** (-5.0 - h))) for h in range(RET_HEADS))


def _ret_kernel(q_ref, k_ref, v_ref, g_ref, cos_ref, sin_ref, gn_ref, wb_ref, wo_ref,
                o_ref, wb_out, wo_out, state_ref, dmat_ref, dq_ref, dk_ref, qk_s, p_s):
    t_blk = RET_T
    wb_out[...] = wb_ref[0].astype(BF16)
    wo_out[...] = wo_ref[0].astype(BF16)

    @pl.when((pl.program_id(0) == 0) & (pl.program_id(1) == 0))
    def _():
        ri = lax.broadcasted_iota(jnp.int32, (t_blk, t_blk), 0)
        ci = lax.broadcasted_iota(jnp.int32, (t_blk, t_blk), 1)
        dist = jnp.abs(ri - ci).astype(F32)
        visible = (ci // CHUNK) <= (ri // CHUNK)
        row = lax.broadcasted_iota(jnp.int32, (t_blk, RET_DIM), 0).astype(F32)
        for h in range(RET_HEADS):
            lg = _LOG_GAMMA[h]
            dmat_ref[h] = jnp.where(visible, jnp.exp(lg * dist), 0.0)
            dq_ref[h] = jnp.exp(lg * (row + 1.0))
            dk_ref[h] = jnp.exp(lg * ((t_blk - 1.0) - row))

    @pl.when(pl.program_id(1) == 0)
    def _():
        state_ref[...] = jnp.zeros_like(state_ref)

    cos = cos_ref[...]
    sin = sin_ref[...]

    def rope_stage(h):
        sl = slice(h * RET_DIM, (h + 1) * RET_DIM)
        q = q_ref[:, sl].astype(F32)
        k = k_ref[:, sl].astype(F32)
        q = (q * cos + pltpu.roll(q, RET_DIM // 2, 1) * sin) * (RET_DIM ** -0.5)
        k = k * cos + pltpu.roll(k, RET_DIM // 2, 1) * sin
        qk_s[0, :, sl] = q.astype(BF16)
        qk_s[1, :, sl] = k.astype(BF16)
        qk_s[2, :, sl] = (q * dq_ref[h]).astype(BF16)
        qk_s[3, :, sl] = (k * dk_ref[h]).astype(BF16)

    def score_stage(h):
        sl = slice(h * RET_DIM, (h + 1) * RET_DIM)
        p_s[h % 2] = (_dot_nt(qk_s[0, :, sl], qk_s[1, :, sl]) * dmat_ref[h]).astype(BF16)

    def out_stage(h):
        sl = slice(h * RET_DIM, (h + 1) * RET_DIM)
        v = v_ref[:, sl]
        state = state_ref[h]
        o = _dot(p_s[h % 2], v) + _dot(qk_s[2, :, sl], state.astype(BF16))
        state_ref[h] = state * math.exp(_LOG_GAMMA[h] * t_blk) + _dot_tn(qk_s[3, :, sl], v)
        mean = jnp.mean(o, axis=-1, keepdims=True)
        cen = o - mean
        var = jnp.mean(cen * cen, axis=-1, keepdims=True)
        normed = cen * lax.rsqrt(var + NORM_EPS) * gn_ref[0, :, sl]
        o_ref[:, sl] = (normed * _silu(g_ref[:, sl].astype(F32))).astype(o_ref.dtype)

    rope_stage(0)
    for h in range(RET_HEADS + 1):
        if h + 1 < RET_HEADS:
            rope_stage(h + 1)
        if h < RET_HEADS:
            score_stage(h)
        if h > 0:
            out_stage(h - 1)


def _ret_call(proj, cos_r, sin_r, vecs, w_branch, w_out, layer, batch, seq):
    m = proj.shape[0]
    rows = RET_T
    nt = seq // rows
    w = RET_WIDTH
    steps = batch * nt
    wb_rows, wo_rows = w_branch.shape[1] // steps, w_out.shape[1] // steps
    bf16_rows = 2 * SUBLANE
    assert wb_rows * steps == w_branch.shape[1] and wb_rows % bf16_rows == 0
    assert wo_rows * steps == w_out.shape[1] and wo_rows % bf16_rows == 0
    d = w_branch.shape[2]

    def col(c):
        return pl.BlockSpec((rows, w), lambda b, t, c=c: (b * nt + t, c // w))

    tab = pl.BlockSpec((rows, RET_DIM), lambda b, t: (b * nt + t, 0))
    return pl.pallas_call(
        _ret_kernel,
        grid=(batch, nt),
        in_specs=[col(COL_RQ), col(COL_RK), col(COL_RV), col(COL_RG), tab, tab,
                  _vec_spec(w, layer, 2, VEC_RET_GN),
                  pl.BlockSpec((1, wb_rows, d), lambda b, t: (layer, b * nt + t, 0)),
                  pl.BlockSpec((1, wo_rows, d), lambda b, t: (layer, b * nt + t, 0))],
        out_specs=(pl.BlockSpec((rows, w), lambda b, t: (b * nt + t, 0)),
                   pl.BlockSpec((wb_rows, d), lambda b, t: (b * nt + t, 0)),
                   pl.BlockSpec((wo_rows, d), lambda b, t: (b * nt + t, 0))),
        out_shape=(jax.ShapeDtypeStruct((m, w), BF16),
                   jax.ShapeDtypeStruct(w_branch.shape[1:], BF16),
                   jax.ShapeDtypeStruct(w_out.shape[1:], BF16)),
        scratch_shapes=[
            pltpu.VMEM((RET_HEADS, RET_DIM, RET_DIM), F32),
            pltpu.VMEM((RET_HEADS, RET_T, RET_T), F32),
            pltpu.VMEM((RET_HEADS, RET_T, RET_DIM), F32),
            pltpu.VMEM((RET_HEADS, RET_T, RET_DIM), F32),
            pltpu.VMEM((4, RET_T, RET_WIDTH), BF16),
            pltpu.VMEM((2, RET_T, RET_T), BF16),
        ],
        compiler_params=_params("arbitrary", "arbitrary"),
        name="retention",
    )(proj, proj, proj, proj, cos_r, sin_r, vecs, w_branch, w_out)


def _sublane_scan(a, b, row):
    for s in (1, 2, 4):
        keep = row >= s
        a_prev = jnp.where(keep, pltpu.roll(a, s, 0), 1.0)
        b_prev = jnp.where(keep, pltpu.roll(b, s, 0), 0.0)
        b = a * b_prev + b
        a = a * a_prev
    return a, b


def _lru_kernel(x_ref, g_ref, cw_ref, cb_ref, wa_ref, ba_ref, wx_ref, bx_ref, lam_ref, *rest, with_ada):
    if with_ada:
        c_ref, aw_ref, ab_ref, o_ref, mod_ref, perm, perm_t, halo, hcar, a_s, b_s = rest
        _ada_accumulate((pl.program_id(0) == 0) & (pl.program_id(1) == 0), c_ref, aw_ref, ab_ref, mod_ref)
    else:
        o_ref, perm, perm_t, halo, hcar, a_s, b_s = rest
    t_blk = LRU_T
    clen = LRU_CLEN
    taps = CONV_WIDTH - 1
    row = lax.broadcasted_iota(jnp.int32, (SUBLANE, LRU_WIDTH), 0)

    @pl.when((pl.program_id(0) == 0) & (pl.program_id(1) == 0))
    def _():
        r = lax.broadcasted_iota(jnp.int32, (t_blk, t_blk), 0)
        c = lax.broadcasted_iota(jnp.int32, (t_blk, t_blk), 1)
        perm[...] = jnp.where(c == (r % SUBLANE) * clen + r // SUBLANE, 1.0, 0.0).astype(BF16)
        perm_t[...] = jnp.where(c == (r % clen) * SUBLANE + r // clen, 1.0, 0.0).astype(BF16)

    @pl.when(pl.program_id(1) == 0)
    def _():
        halo[...] = jnp.zeros_like(halo)
        hcar[...] = jnp.zeros_like(hcar)

    xp = _dot(perm[...], x_ref[...])
    pieces = []
    for k in range(taps):
        cur = xp[t_blk - (taps - k) * SUBLANE:t_blk - (taps - k - 1) * SUBLANE, :]
        prev = halo[k * SUBLANE:(k + 1) * SUBLANE, :]
        pieces.append(jnp.where(row == 0, pltpu.roll(prev, 1, 0), pltpu.roll(cur, 1, 0)))
    halo[...] = xp[t_blk - taps * SUBLANE:, :]
    xext = jnp.concatenate(pieces + [xp], axis=0)
    xc = cb_ref[0]
    for w in range(CONV_WIDTH):
        xc = xc + cw_ref[0, w:w + 1, :] * xext[w * SUBLANE:w * SUBLANE + t_blk, :]

    neg_lam = -lam_ref[0]
    softplus = jnp.maximum(neg_lam, 0.0) + jnp.log1p(jnp.exp(-jnp.abs(neg_lam)))
    neg_half_rate = (-0.5 * LRU_C) * softplus
    for n in range(LRU_BLOCKS):
        sl = slice(n * LRU_BLOCK_DIM, (n + 1) * LRU_BLOCK_DIM)
        xn = xc[:, sl]
        xn_b = xn.astype(BF16)
        t_r = jnp.tanh(0.5 * (_dot(xn_b, wa_ref[0, n].astype(BF16)) + ba_ref[0, :, sl]))
        t_i = jnp.tanh(0.5 * (_dot(xn_b, wx_ref[0, n].astype(BF16)) + bx_ref[0, :, sl]))
        half_rate = neg_half_rate[:, sl]
        log_a = half_rate * t_r + half_rate
        a = jnp.exp(log_a)
        a_s[:, sl] = a
        var = jnp.tanh(-log_a) * (a * a + 1.0)
        std = jnp.where(var > 0.0, var * lax.rsqrt(var), 0.0)
        half_x = 0.5 * xn
        b_s[:, sl] = std * (half_x * t_i + half_x)

    def body(j, carry):
        h, prod = carry
        r0 = j * SUBLANE
        a = a_s[pl.ds(r0, SUBLANE), :]
        h = a * h + b_s[pl.ds(r0, SUBLANE), :]
        prod = a * prod
        b_s[pl.ds(r0, SUBLANE), :] = h
        a_s[pl.ds(r0, SUBLANE), :] = prod
        return h, prod

    zeros = jnp.zeros((SUBLANE, LRU_WIDTH), F32)
    carry = (zeros, zeros + 1.0)
    for j in range(clen):
        carry = body(j, carry)
    h_end, a_end = carry

    a_inc, h_inc = _sublane_scan(a_end, h_end, row)
    h_prev_tile = hcar[...]
    h_chunk_end = a_inc * h_prev_tile + h_inc
    h_chunk_start = jnp.where(row == 0, h_prev_tile, pltpu.roll(h_chunk_end, 1, 0))
    hcar[...] = jnp.broadcast_to(h_chunk_end[SUBLANE - 1:SUBLANE, :], (SUBLANE, LRU_WIDTH))

    h_true = b_s[...] + a_s[...] * jnp.concatenate([h_chunk_start] * clen, axis=0)
    gp = _dot(perm[...], g_ref[...])
    out_p = (h_true * _silu(gp)).astype(BF16)
    o_ref[...] = _dot(perm_t[...], out_p).astype(o_ref.dtype)


def _lru_call(proj, cw, wa, wx, vecs, layer, batch, seq, ada=None):
    m = proj.shape[0]
    nt = seq // LRU_T
    w = LRU_WIDTH

    def col(c):
        return pl.BlockSpec((LRU_T, w), lambda b, t, c=c: (b * nt + t, c // w))

    def vec(offset):
        return _vec_spec(w, layer, 2, offset)

    def blk():
        return pl.BlockSpec((1, LRU_BLOCKS, LRU_BLOCK_DIM, LRU_BLOCK_DIM), lambda b, t: (layer, 0, 0, 0))

    in_specs = [col(COL_LX), col(COL_LG),
                pl.BlockSpec((1, CONV_WIDTH, w), lambda b, t: (layer, 0, 0)), vec(VEC_CONV_B),
                blk(), vec(VEC_BA), blk(), vec(VEC_BX), vec(VEC_LAM)]
    args = [proj, proj, cw, vecs, wa, vecs, wx, vecs, vecs]
    out_specs = pl.BlockSpec((LRU_T, w), lambda b, t: (b * nt + t, 0))
    out_shape = jax.ShapeDtypeStruct((m, w), BF16)
    if ada is not None:
        c_pad, ada_w = ada
        rows = c_pad.shape[0]
        _, d, n = ada_w.shape
        steps = batch * nt
        slab = d // steps
        assert slab * steps == d and slab % LANE == 0
        in_specs += [pl.BlockSpec((1, rows, slab), lambda b, t: (b * nt + t, 0, 0)),
                     pl.BlockSpec((1, slab, n), lambda b, t: (layer + 1, b * nt + t, 0)),
                     _vec_spec(n, layer + 1, 2, VEC_ADA_B)]
        args += [_c_slabs(c_pad, steps), ada_w, vecs]
        out_specs = (out_specs, pl.BlockSpec((1, rows, n), lambda b, t: (0, 0, 0)))
        out_shape = (out_shape, jax.ShapeDtypeStruct((1, rows, n), F32))
    return pl.pallas_call(
        functools.partial(_lru_kernel, with_ada=ada is not None),
        grid=(batch, nt),
        in_specs=in_specs,
        out_specs=out_specs,
        out_shape=out_shape,
        scratch_shapes=[
            pltpu.VMEM((LRU_T, LRU_T), BF16),
            pltpu.VMEM((LRU_T, LRU_T), BF16),
            pltpu.VMEM(((CONV_WIDTH - 1) * SUBLANE, w), F32),
            pltpu.VMEM((SUBLANE, w), F32),
            pltpu.VMEM((LRU_T, w), F32),
            pltpu.VMEM((LRU_T, w), F32),
        ],
        compiler_params=_params("arbitrary", "arbitrary"),
        name="rg_lru",
    )(*args)


def _rope64(x, cos_t, sin_t):
    swapped = pltpu.roll(x, MLA_ROPE // 2, 1) + pltpu.roll(x, LANE - MLA_ROPE // 2, 1)
    return x * cos_t + swapped * sin_t


def _mla_kernel(h_ref, wkr_ref, mq_ref, mkv_ref, mg_ref, wq_ref, wkv_ref, qn_ref, kvn_ref,
                cos_ref, sin_ref, o_ref, qlat, kvlat, krope, wq_s, qs, ks, vs, s_scr, p_scr, *, seq):
    rows = MLA_ROWS
    pair = pl.program_id(1)

    @pl.when(pair == 0)
    def _():
        wkr = jnp.concatenate([wkr_ref[0], jnp.zeros((LANE - MLA_ROPE, D_MODEL), F32)], axis=0).astype(BF16)
        for i in range(seq // rows):
            rs = slice(i * rows, (i + 1) * rows)
            qlat[rs, :] = _rms(mq_ref[rs, :].astype(F32), qn_ref[0]).astype(BF16)
            kvlat[rs, :] = _rms(mkv_ref[rs, :].astype(F32), kvn_ref[0]).astype(BF16)
            kr = _dot_nt(h_ref[rs, :], wkr)
            krope[rs, :] = _rope64(kr, cos_ref[rs, :], sin_ref[rs, :]).astype(BF16)
        for hh in range(MLA_HPS):
            vs[hh, :, MLA_V:] = jnp.ones((seq, MLA_V), BF16)
        zeros = jnp.zeros((MLA_LORA, MLA_QK_PAD - MLA_QK), F32)
        wq_all = wq_ref[0]
        for hh in range(MLA_HEADS):
            w_head = wq_all[:, hh * MLA_QK:(hh + 1) * MLA_QK]
            wq_s[hh] = jnp.concatenate([w_head, zeros], axis=1).astype(BF16)

    scale = (MLA_QK ** -0.5) * math.log2(math.e)
    tq = MLA_TQ
    ri = lax.broadcasted_iota(jnp.int32, (tq, tq), 0)
    ci = lax.broadcasted_iota(jnp.int32, (tq, tq), 1)
    visible = (ci // CHUNK) <= (ri // CHUNK)
    n_q = seq // tq
    kv_w = MLA_NOPE + MLA_V

    for hh in range(MLA_HPS):
        wq = wq_s[pair * MLA_HPS + hh]
        wkv = wkv_ref[0, :, hh * kv_w:(hh + 1) * kv_w].astype(BF16)
        for i in range(seq // rows):
            rs = slice(i * rows, (i + 1) * rows)
            q = _dot(qlat[rs, :], wq)
            qs[hh, rs, 0:MLA_NOPE] = (q[:, 0:MLA_NOPE] * scale).astype(BF16)
            qs[hh, rs, MLA_NOPE:] = (_rope64(q[:, MLA_NOPE:], cos_ref[rs, :], sin_ref[rs, :]) * scale).astype(BF16)
            kv = _dot(kvlat[rs, :], wkv)
            ks[hh, rs, 0:MLA_NOPE] = kv[:, 0:MLA_NOPE].astype(BF16)
            ks[hh, rs, MLA_NOPE:] = krope[rs, :]
            vs[hh, rs, 0:MLA_V] = kv[:, MLA_NOPE:].astype(BF16)

    def scores(hh, i):
        q0, q1 = i * tq, (i + 1) * tq
        s_scr[hh, i % 2, :, 0:q1] = _dot_nt(qs[hh, q0:q1, :], ks[hh, 0:q1, :])

    def probs(hh, i):
        q0, q1 = i * tq, (i + 1) * tq
        s_d = jnp.where(visible, s_scr[hh, i % 2, :, q0:q1], -1e30)
        m = jnp.max(s_d, axis=-1, keepdims=True)
        if i > 0:
            m = jnp.maximum(m, jnp.max(s_scr[hh, i % 2, :, 0:q0], axis=-1, keepdims=True))
            p_scr[hh, i % 2, :, 0:q0] = jnp.exp2(s_scr[hh, i % 2, :, 0:q0] - m).astype(BF16)
        p_scr[hh, i % 2, :, q0:q1] = jnp.exp2(s_d - m).astype(BF16)

    def values(hh, i):
        q0, q1 = i * tq, (i + 1) * tq
        cols = slice(hh * MLA_V, (hh + 1) * MLA_V)
        acc = _dot(p_scr[hh, i % 2, :, 0:q1], vs[hh, 0:q1, :])
        out = acc[:, 0:MLA_V] / acc[:, MLA_V:]
        o_ref[q0:q1, cols] = (out * _silu(mg_ref[q0:q1, cols].astype(F32))).astype(o_ref.dtype)

    for hh in range(MLA_HPS):
        scores(hh, 0)
    for i in range(n_q + 1):
        for hh in range(MLA_HPS):
            if i + 1 < n_q:
                scores(hh, i + 1)
            if i < n_q:
                probs(hh, i)
            if i > 0:
                values(hh, i - 1)


def _mla_call(h, w_in_t, proj, w_uq, w_ukv, vecs, cos_m, sin_m, layer, batch, seq):
    m, d = h.shape
    lora = MLA_LORA
    hps = MLA_HPS
    assert MLA_HEADS % hps == 0 and COL_MG % (hps * MLA_V) == 0
    return pl.pallas_call(
        functools.partial(_mla_kernel, seq=seq),
        grid=(batch, MLA_HEADS // hps),
        in_specs=[
            pl.BlockSpec((seq, d), lambda b, hd: (b, 0)),
            pl.BlockSpec((1, MLA_ROPE, d), lambda b, hd: (layer, SRC_MKR // MLA_ROPE, 0)),
            pl.BlockSpec((seq, lora), lambda b, hd: (b, COL_MQ // lora)),
            pl.BlockSpec((seq, lora), lambda b, hd: (b, COL_MKV // lora), pipeline_mode=pl.Buffered(1)),
            pl.BlockSpec((seq, hps * MLA_V), lambda b, hd: (b, COL_MG // (hps * MLA_V) + hd)),
            pl.BlockSpec((1, lora, MLA_HEADS * MLA_QK), lambda b, hd: (layer, 0, 0),
                         pipeline_mode=pl.Buffered(1)),
            pl.BlockSpec((1, lora, hps * (MLA_NOPE + MLA_V)), lambda b, hd: (layer, 0, hd)),
            _vec_spec(lora, layer, 2, VEC_Q_NORM),
            _vec_spec(lora, layer, 2, VEC_KV_NORM),
            pl.BlockSpec((seq, LANE), lambda b, hd: (b, 0), pipeline_mode=pl.Buffered(1)),
            pl.BlockSpec((seq, LANE), lambda b, hd: (b, 0), pipeline_mode=pl.Buffered(1)),
        ],
        out_specs=pl.BlockSpec((seq, hps * MLA_V), lambda b, hd: (b, hd)),
        out_shape=jax.ShapeDtypeStruct((m, MLA_WIDTH), BF16),
        scratch_shapes=[
            pltpu.VMEM((seq, lora), BF16),
            pltpu.VMEM((seq, lora), BF16),
            pltpu.VMEM((seq, LANE), BF16),
            pltpu.VMEM((MLA_HEADS, lora, MLA_QK_PAD), BF16),
            pltpu.VMEM((hps, seq, MLA_QK_PAD), BF16),
            pltpu.VMEM((hps, seq, MLA_QK_PAD), BF16),
            pltpu.VMEM((hps, seq, 2 * MLA_V), BF16),
            pltpu.VMEM((hps, 2, MLA_TQ, seq), F32),
            pltpu.VMEM((hps, 2, MLA_TQ, seq), BF16),
        ],
        compiler_params=_params("arbitrary", "arbitrary"),
        name="mla",
    )(h, w_in_t, proj, proj, proj, w_uq, w_ukv, vecs, vecs, cos_m, sin_m)


def _merge_kernel(yr_ref, yl_ref, ym_ref, l0_ref, l1_ref, l2_ref, x_ref, res_ref, g_ref,
                  wb_ref, wo_ref, *rest, tiles_per_batch, emit_next):
    b = pl.program_id(0) // tiles_per_batch
    merged = None
    for i, (y_ref, l_ref) in enumerate(((yr_ref, l0_ref), (yl_ref, l1_ref), (ym_ref, l2_ref))):
        z = _dot(y_ref[...], wb_ref[i * BRANCH_WIDTH:(i + 1) * BRANCH_WIDTH, :])
        z = z * jax.nn.sigmoid(l_ref[...].astype(F32))
        merged = z if merged is None else merged + z
    y = _dot(merged.astype(BF16), wo_ref[...])
    post = g_ref[0] * (1.0 + res_ref[0, pl.ds(b, 1), :])
    inv = lax.rsqrt(jnp.mean(y * y, axis=-1, keepdims=True) + NORM_EPS)
    x_new = x_ref[...] + (y * inv) * post
    if emit_next:
        gn_ref, shift_ref, scale_ref, o_ref, h_ref = rest
        pre = gn_ref[0] * (1.0 + scale_ref[0, pl.ds(b, 1), :])
        inv_n = lax.rsqrt(jnp.mean(x_new * x_new, axis=-1, keepdims=True) + NORM_EPS)
        h_ref[...] = ((x_new * inv_n) * pre + shift_ref[0, pl.ds(b, 1), :]).astype(h_ref.dtype)
    else:
        (o_ref,) = rest
    o_ref[...] = x_new


def _merge_call(y_ret, y_lru, y_mla, proj, x2, mod, mod_next, vecs, wb, wo, layer, seq):
    m, d = x2.shape
    tm = MERGE_TM
    mod_rows = mod.shape[1]
    emit_next = mod_next is not None

    def rows(width, c=0):
        return pl.BlockSpec((tm, width), lambda i, c=c: (i, c))

    def resident(shape):
        return pl.BlockSpec(shape, lambda i: (0, 0), pipeline_mode=pl.Buffered(1))

    def mod_part(part):
        return pl.BlockSpec((1, mod_rows, d), lambda i: (0, 0, part))

    merge_col = COL_MERGE // d
    in_specs = [rows(BRANCH_WIDTH), rows(BRANCH_WIDTH), rows(BRANCH_WIDTH),
                rows(d, merge_col), rows(d, merge_col + 1), rows(d, merge_col + 2),
                rows(d), mod_part(2), _vec_spec(d, layer, 1, VEC_NORM_POST),
                resident(wb.shape), resident(wo.shape)]
    args = [y_ret, y_lru, y_mla, proj, proj, proj, x2, mod, vecs, wb, wo]
    out_specs = rows(d)
    out_shape = jax.ShapeDtypeStruct((m, d), F32)
    if emit_next:
        in_specs += [_vec_spec(d, layer + 1, 1, VEC_NORM_PRE), mod_part(0), mod_part(1)]
        args += [vecs, mod_next, mod_next]
        out_specs = (out_specs, rows(d))
        out_shape = (out_shape, jax.ShapeDtypeStruct((m, d), BF16))
    return pl.pallas_call(
        functools.partial(_merge_kernel, tiles_per_batch=seq // tm, emit_next=emit_next),
        grid=(m // tm,),
        in_specs=in_specs,
        out_specs=out_specs,
        out_shape=out_shape,
        compiler_params=_params("arbitrary"),
        name="merge_out",
    )(*args)


def kernel(x, c, positions, ada_w, ada_b, norm_pre, norm_post, w_in, ret_gn, lru_conv_w, lru_conv_b,
           lru_wa, lru_ba, lru_wx, lru_bx, lru_lambda, mla_q_norm, mla_w_uq, mla_kv_norm, mla_w_ukv,
           w_branch, w_out):
    batch, seq, d = x.shape
    depth = w_in.shape[0]
    m = batch * seq
    assert d == D_MODEL and w_in.shape[2] == SRC_WIDTH and batch <= SUBLANE
    assert seq % PROJ_TM == 0 and seq % RET_T == 0 and seq % LRU_T == 0 and seq % MLA_TQ == 0
    assert seq % MERGE_TM == 0 and seq % NORM_TM == 0

    c_pad = jnp.pad(c, ((0, SUBLANE - batch), (0, 0)))
    vecs = _pack_vectors(ada_b, norm_pre, norm_post, ret_gn, lru_conv_b, lru_ba, lru_bx, lru_lambda,
                         mla_q_norm, mla_kv_norm)
    mod = _ada_call(c_pad, ada_w, vecs, 0)
    w_in_t = jnp.swapaxes(w_in, 1, 2)

    x2 = x.reshape(m, d)
    h, cos_r, sin_r, cos_m, sin_m = _prenorm_call(x2, mod, vecs, positions, 0, seq)
    for l in range(depth):
        proj = _inproj_call(h, w_in_t, l)
        y_ret, wb, wo = _ret_call(proj, cos_r, sin_r, vecs, w_branch, w_out, l, batch, seq)
        lru_args = (proj, lru_conv_w, lru_wa, lru_wx, vecs, l, batch, seq)
        if l + 1 < depth:
            y_lru, mod_next = _lru_call(*lru_args, ada=(c_pad, ada_w))
        else:
            y_lru, mod_next = _lru_call(*lru_args), None
        y_mla = _mla_call(h, w_in_t, proj, mla_w_uq, mla_w_ukv, vecs, cos_m, sin_m, l, batch, seq)
        out = _merge_call(y_ret, y_lru, y_mla, proj, x2, mod, mod_next, vecs, wb, wo, l, seq)
        mod = mod_next
        if l + 1 < depth:
            x2, h = out
        else:
            x2 = out
    return x2.reshape(batch, seq, d)
```

```python
import functools
import math

import jax
import jax.numpy as jnp
from jax import lax
from jax.experimental import pallas as pl
from jax.experimental.pallas import tpu as pltpu

F32 = jnp.float32
BF16 = jnp.bfloat16

D_MODEL = 2048
CHUNK = 64
ROPE_BASE = 10000.0
NORM_EPS = 1e-6

RET_HEADS = 8
RET_DIM = 128
RET_WIDTH = RET_HEADS * RET_DIM

LRU_WIDTH = 1024
LRU_BLOCKS = 8
LRU_BLOCK_DIM = 128
CONV_WIDTH = 4
LRU_C = 8.0

MLA_HEADS = 8
MLA_NOPE = 128
MLA_ROPE = 64
MLA_V = 128
MLA_LORA = 512
MLA_WIDTH = MLA_HEADS * MLA_V
MLA_QK = MLA_NOPE + MLA_ROPE
MLA_QK_PAD = 256

LANE = 128
SUBLANE = 8
VMEM_LIMIT = 58 * 1024 * 1024

BRANCH_WIDTH = 1024
assert RET_WIDTH == LRU_WIDTH == MLA_WIDTH == BRANCH_WIDTH

COL_RQ = 0
COL_RK = COL_RQ + RET_WIDTH
COL_RV = COL_RK + RET_WIDTH
COL_RG = COL_RV + RET_WIDTH
COL_LX = COL_RG + RET_WIDTH
COL_LG = COL_LX + LRU_WIDTH
COL_MQ = COL_LG + LRU_WIDTH
COL_MKV = COL_MQ + MLA_LORA
COL_MG = COL_MKV + MLA_LORA
COL_MERGE = COL_MG + MLA_WIDTH
PROJ_WIDTH = COL_MERGE + 3 * D_MODEL
SRC_MKR = COL_MG
SRC_WIDTH = PROJ_WIDTH + MLA_ROPE
PROJ_TN = 1024
PROJ_TM = 2048
ALIGNED_TILES = SRC_MKR // PROJ_TN
CAST_ROWS = 64
assert SRC_MKR % PROJ_TN == 0 and PROJ_WIDTH % PROJ_TN == 0 and COL_MERGE % D_MODEL == 0

NORM_TM = 1024
RET_T = 256
LRU_T = 512
LRU_CLEN = LRU_T // SUBLANE
MLA_ROWS = 256
MLA_TQ = 256
MLA_HPS = 2
MERGE_TM = 256
ADA_TK = 512


def _silu(v):
    return v * jax.nn.sigmoid(v)


def _dot(a, b):
    return jnp.dot(a, b, preferred_element_type=F32)


def _dot_nt(a, b):
    return lax.dot_general(a, b, (((1,), (1,)), ((), ())), preferred_element_type=F32)


def _dot_tn(a, b):
    return lax.dot_general(a, b, (((0,), (0,)), ((), ())), preferred_element_type=F32)


def _rms(x, gain):
    return x * lax.rsqrt(jnp.mean(x * x, axis=-1, keepdims=True) + NORM_EPS) * gain


def _params(*sem):
    return pltpu.CompilerParams(dimension_semantics=sem, vmem_limit_bytes=VMEM_LIMIT)


VEC_ADA_B = 0
VEC_NORM_PRE = 3 * D_MODEL
VEC_NORM_POST = VEC_NORM_PRE + D_MODEL
VEC_RET_GN = VEC_NORM_POST + D_MODEL
VEC_CONV_B = VEC_RET_GN + RET_WIDTH
VEC_BA = VEC_CONV_B + LRU_WIDTH
VEC_BX = VEC_BA + LRU_WIDTH
VEC_LAM = VEC_BX + LRU_WIDTH
VEC_Q_NORM = VEC_LAM + LRU_WIDTH
VEC_KV_NORM = VEC_Q_NORM + MLA_LORA


def _pack_vectors(ada_b, norm_pre, norm_post, ret_gn, conv_b, ba, bx, lam, q_norm, kv_norm):
    packed = jnp.concatenate([ada_b, norm_pre, norm_post, ret_gn, conv_b, ba, bx, lam, q_norm, kv_norm], axis=1)
    return packed[:, None, :]


def _vec_spec(width, layer, ngrid, col):
    assert col % width == 0
    if ngrid == 1:
        return pl.BlockSpec((1, 1, width), lambda i: (layer, 0, col // width))
    return pl.BlockSpec((1, 1, width), lambda i, j: (layer, 0, col // width))


def _ada_accumulate(first, c_ref, w_ref, b_ref, o_ref):
    @pl.when(first)
    def _():
        o_ref[0] = jnp.broadcast_to(b_ref[0], o_ref.shape[1:])

    c_act = _silu(c_ref[0]).astype(BF16)
    o_ref[0] += _dot(c_act, w_ref[0].astype(BF16))


def _ada_kernel(c_ref, w_ref, b_ref, o_ref):
    _ada_accumulate(pl.program_id(0) == 0, c_ref, w_ref, b_ref, o_ref)


def _c_slabs(c_pad, n_slabs):
    rows, d = c_pad.shape
    return c_pad.reshape(rows, n_slabs, d // n_slabs).transpose(1, 0, 2)


def _ada_call(c_pad, ada_w, vecs, layer):
    _, d, n = ada_w.shape
    rows = c_pad.shape[0]
    n_slabs = d // ADA_TK
    return pl.pallas_call(
        _ada_kernel,
        grid=(n_slabs,),
        in_specs=[
            pl.BlockSpec((1, rows, ADA_TK), lambda k: (k, 0, 0)),
            pl.BlockSpec((1, ADA_TK, n), lambda k: (layer, k, 0)),
            _vec_spec(n, layer, 1, VEC_ADA_B),
        ],
        out_specs=pl.BlockSpec((1, rows, n), lambda k: (0, 0, 0)),
        out_shape=jax.ShapeDtypeStruct((1, rows, n), F32),
        compiler_params=_params("arbitrary"),
        name="ada_mod",
    )(_c_slabs(c_pad, n_slabs), ada_w, vecs)


def _modulate(x, gain, shift_ref, scale_ref, b):
    shift = shift_ref[0, pl.ds(b, 1), :]
    scale = scale_ref[0, pl.ds(b, 1), :]
    return _rms(x, gain) * (1.0 + scale) + shift


def _prenorm_kernel(x_ref, g_ref, shift_ref, scale_ref, pos_ref, freq_ref,
                    o_ref, cos_r_ref, sin_r_ref, cos_m_ref, sin_m_ref, *, tiles_per_batch):
    b = pl.program_id(0) // tiles_per_batch
    o_ref[...] = _modulate(x_ref[...], g_ref[0], shift_ref, scale_ref, b).astype(o_ref.dtype)

    ang = freq_ref[...] * pos_ref[0].astype(F32)
    cos_t = jnp.cos(ang)
    sin_t = jnp.sin(ang)
    n_r, n_m = RET_DIM // 2, MLA_ROPE // 2
    c_r, s_r = cos_t[0:n_r], sin_t[0:n_r]
    c_m, s_m = cos_t[n_r:n_r + n_m], sin_t[n_r:n_r + n_m]
    pad = jnp.zeros((LANE - MLA_ROPE, ang.shape[1]), F32)
    cos_r_ref[...] = jnp.concatenate([c_r, c_r], axis=0).T
    sin_r_ref[...] = jnp.concatenate([-s_r, s_r], axis=0).T
    cos_m_ref[...] = jnp.concatenate([c_m, c_m, pad], axis=0).T
    sin_m_ref[...] = jnp.concatenate([-s_m, s_m, pad], axis=0).T


def _prenorm_call(x2, mod, vecs, positions, layer, seq):
    m, d = x2.shape
    rows = mod.shape[1]
    n_tiles = m // NORM_TM

    def inv_freq(dim):
        return ROPE_BASE ** (-jnp.arange(0, dim, 2, dtype=F32) / dim)

    freqs = jnp.concatenate([inv_freq(RET_DIM), inv_freq(MLA_ROPE)])[:, None]
    table = pl.BlockSpec((NORM_TM, LANE), lambda i: (i, 0))
    return pl.pallas_call(
        functools.partial(_prenorm_kernel, tiles_per_batch=seq // NORM_TM),
        grid=(n_tiles,),
        in_specs=[
            pl.BlockSpec((NORM_TM, d), lambda i: (i, 0)),
            _vec_spec(d, layer, 1, VEC_NORM_PRE),
            pl.BlockSpec((1, rows, d), lambda i: (0, 0, 0)),
            pl.BlockSpec((1, rows, d), lambda i: (0, 0, 1)),
            pl.BlockSpec((1, 1, NORM_TM), lambda i: (i, 0, 0)),
            pl.BlockSpec(freqs.shape, lambda i: (0, 0)),
        ],
        out_specs=(pl.BlockSpec((NORM_TM, d), lambda i: (i, 0)), table, table, table, table),
        out_shape=(jax.ShapeDtypeStruct((m, d), BF16),) + (jax.ShapeDtypeStruct((m, LANE), F32),) * 4,
        compiler_params=_params("arbitrary"),
        name="pre_norm",
    )(x2, vecs, mod, mod, positions.reshape(n_tiles, 1, NORM_TM), freqs)


def _inproj_kernel(h_ref, w_ref, wx_ref, o_ref, wb_ref):
    j = pl.program_id(0)
    first_row_tile = pl.program_id(1) == 0
    n_chunks = PROJ_TN // CAST_ROWS

    @pl.when(first_row_tile & (j < ALIGNED_TILES))
    def _():
        def body(i, carry):
            r0 = pl.multiple_of(i * CAST_ROWS, CAST_ROWS)
            wb_ref[pl.ds(r0, CAST_ROWS), :] = w_ref[0, pl.ds(r0, CAST_ROWS), :].astype(BF16)
            return carry

        lax.fori_loop(0, n_chunks, body, 0)

    @pl.when(first_row_tile & (j >= ALIGNED_TILES))
    def _():
        def body(i, carry):
            r0 = pl.multiple_of(i * CAST_ROWS, CAST_ROWS)
            wb_ref[pl.ds(r0, CAST_ROWS), :] = w_ref[0, pl.ds(r0 + MLA_ROPE, CAST_ROWS), :].astype(BF16)
            return carry

        lax.fori_loop(0, n_chunks - 1, body, 0)
        wb_ref[PROJ_TN - MLA_ROPE:, :] = wx_ref[0].astype(BF16)

    o_ref[...] = _dot_nt(h_ref[...], wb_ref[...]).astype(o_ref.dtype)


def _inproj_call(h, w_in_t, layer):
    m, d = h.shape
    per_tile = PROJ_TN // MLA_ROPE
    last_block = SRC_WIDTH // MLA_ROPE - 1
    return pl.pallas_call(
        _inproj_kernel,
        grid=(PROJ_WIDTH // PROJ_TN, m // PROJ_TM),
        in_specs=[
            pl.BlockSpec((PROJ_TM, d), lambda j, i: (i, 0)),
            pl.BlockSpec((1, PROJ_TN, d), lambda j, i: (layer, j, 0)),
            pl.BlockSpec((1, MLA_ROPE, d),
                         lambda j, i: (layer, jnp.minimum((j + 1) * per_tile, last_block), 0)),
        ],
        out_specs=pl.BlockSpec((PROJ_TM, PROJ_TN), lambda j, i: (i, j)),
        out_shape=jax.ShapeDtypeStruct((m, PROJ_WIDTH), BF16),
        scratch_shapes=[pltpu.VMEM((PROJ_TN, d), BF16)],
        compiler_params=_params("arbitrary", "arbitrary"),
        name="in_proj",
    )(h, w_in_t, w_in_t)


_LOG_GAMMA = tuple(math.log1p(-(2.0 ** (-5.0 - h))) for h in range(RET_HEADS))


def _ret_kernel(q_ref, k_ref, v_ref, g_ref, cos_ref, sin_ref, gn_ref, wb_ref, wo_ref,
                h_ref, wkr_ref, cos_m_ref, sin_m_ref,
                o_ref, wb_out, wo_out, kr_out, state_ref, dmat_ref, dq_ref, dk_ref, wkr_s):
    t_blk = RET_T

    @pl.when((pl.program_id(0) == 0) & (pl.program_id(1) == 0))
    def _():
        wkr = jnp.concatenate([wkr_ref[0], jnp.zeros((LANE - MLA_ROPE, D_MODEL), F32)], axis=0)
        wkr_s[...] = wkr.T.astype(BF16)
        ri = lax.broadcasted_iota(jnp.int32, (t_blk, t_blk), 0)
        ci = lax.broadcasted_iota(jnp.int32, (t_blk, t_blk), 1)
        dist = jnp.abs(ri - ci).astype(F32)
        visible = (ci // CHUNK) <= (ri // CHUNK)
        row = lax.broadcasted_iota(jnp.int32, (t_blk, RET_DIM), 0).astype(F32)
        for h in range(RET_HEADS):
            lg = _LOG_GAMMA[h]
            dmat_ref[h] = jnp.where(visible, jnp.exp(lg * dist), 0.0)
            dq_ref[h] = jnp.exp(lg * (row + 1.0))
            dk_ref[h] = jnp.exp(lg * ((t_blk - 1.0) - row))

    @pl.when(pl.program_id(1) == 0)
    def _():
        state_ref[...] = jnp.zeros_like(state_ref)

    wb_out[...] = wb_ref[0].astype(BF16)
    wo_out[...] = wo_ref[0].astype(BF16)
    kr_out[...] = _rope64(_dot(h_ref[...], wkr_s[...]), cos_m_ref[...], sin_m_ref[...]).astype(BF16)

    cos = cos_ref[...]
    sin = sin_ref[...]
    for h in range(RET_HEADS):
        sl = slice(h * RET_DIM, (h + 1) * RET_DIM)
        q = q_ref[:, sl].astype(F32)
        k = k_ref[:, sl].astype(F32)
        q = (q * cos + pltpu.roll(q, RET_DIM // 2, 1) * sin) * (RET_DIM ** -0.5)
        k = k * cos + pltpu.roll(k, RET_DIM // 2, 1) * sin
        v = v_ref[:, sl]
        scores = _dot_nt(q.astype(BF16), k.astype(BF16)) * dmat_ref[h]
        o = _dot(scores.astype(BF16), v)
        state = state_ref[h]
        o = o + _dot((q * dq_ref[h]).astype(BF16), state.astype(BF16))
        k_dec = (k * dk_ref[h]).astype(BF16)
        state_ref[h] = state * math.exp(_LOG_GAMMA[h] * t_blk) + _dot_tn(k_dec, v)
        mean = jnp.mean(o, axis=-1, keepdims=True)
        cen = o - mean
        var = jnp.mean(cen * cen, axis=-1, keepdims=True)
        normed = cen * lax.rsqrt(var + NORM_EPS) * gn_ref[0, :, sl]
        o_ref[:, sl] = (normed * _silu(g_ref[:, sl].astype(F32))).astype(o_ref.dtype)


def _ret_call(proj, cos_r, sin_r, vecs, w_branch, w_out, h, w_in_t, cos_m, sin_m, layer, batch, seq):
    m = proj.shape[0]
    rows = RET_T
    nt = seq // rows
    w = RET_WIDTH
    steps = batch * nt
    wb_rows, wo_rows = w_branch.shape[1] // steps, w_out.shape[1] // steps
    bf16_rows = 2 * SUBLANE
    assert wb_rows * steps == w_branch.shape[1] and wb_rows % bf16_rows == 0
    assert wo_rows * steps == w_out.shape[1] and wo_rows % bf16_rows == 0
    d = w_branch.shape[2]

    def col(c):
        return pl.BlockSpec((rows, w), lambda b, t, c=c: (b * nt + t, c // w))

    tab = pl.BlockSpec((rows, RET_DIM), lambda b, t: (b * nt + t, 0))
    return pl.pallas_call(
        _ret_kernel,
        grid=(batch, nt),
        in_specs=[col(COL_RQ), col(COL_RK), col(COL_RV), col(COL_RG), tab, tab,
                  _vec_spec(w, layer, 2, VEC_RET_GN),
                  pl.BlockSpec((1, wb_rows, d), lambda b, t: (layer, b * nt + t, 0)),
                  pl.BlockSpec((1, wo_rows, d), lambda b, t: (layer, b * nt + t, 0)),
                  pl.BlockSpec((rows, d), lambda b, t: (b * nt + t, 0)),
                  pl.BlockSpec((1, MLA_ROPE, d), lambda b, t: (layer, SRC_MKR // MLA_ROPE, 0)),
                  tab, tab],
        out_specs=(pl.BlockSpec((rows, w), lambda b, t: (b * nt + t, 0)),
                   pl.BlockSpec((wb_rows, d), lambda b, t: (b * nt + t, 0)),
                   pl.BlockSpec((wo_rows, d), lambda b, t: (b * nt + t, 0)),
                   tab),
        out_shape=(jax.ShapeDtypeStruct((m, w), BF16),
                   jax.ShapeDtypeStruct(w_branch.shape[1:], BF16),
                   jax.ShapeDtypeStruct(w_out.shape[1:], BF16),
                   jax.ShapeDtypeStruct((m, LANE), BF16)),
        scratch_shapes=[
            pltpu.VMEM((RET_HEADS, RET_DIM, RET_DIM), F32),
            pltpu.VMEM((RET_HEADS, RET_T, RET_T), F32),
            pltpu.VMEM((RET_HEADS, RET_T, RET_DIM), F32),
            pltpu.VMEM((RET_HEADS, RET_T, RET_DIM), F32),
            pltpu.VMEM((D_MODEL, LANE), BF16),
        ],
        compiler_params=_params("arbitrary", "arbitrary"),
        name="retention",
    )(proj, proj, proj, proj, cos_r, sin_r, vecs, w_branch, w_out, h, w_in_t, cos_m, sin_m)


def _sublane_scan(a, b, row):
    for s in (1, 2, 4):
        keep = row >= s
        a_prev = jnp.where(keep, pltpu.roll(a, s, 0), 1.0)
        b_prev = jnp.where(keep, pltpu.roll(b, s, 0), 0.0)
        b = a * b_prev + b
        a = a * a_prev
    return a, b


def _lru_kernel(x_ref, g_ref, cw_ref, cb_ref, wa_ref, ba_ref, wx_ref, bx_ref, lam_ref, *rest, with_ada):
    if with_ada:
        c_ref, aw_ref, ab_ref, o_ref, mod_ref, perm, perm_t, halo, hcar, a_s, b_s = rest
        _ada_accumulate((pl.program_id(0) == 0) & (pl.program_id(1) == 0), c_ref, aw_ref, ab_ref, mod_ref)
    else:
        o_ref, perm, perm_t, halo, hcar, a_s, b_s = rest
    t_blk = LRU_T
    clen = LRU_CLEN
    taps = CONV_WIDTH - 1
    row = lax.broadcasted_iota(jnp.int32, (SUBLANE, LRU_WIDTH), 0)

    @pl.when((pl.program_id(0) == 0) & (pl.program_id(1) == 0))
    def _():
        r = lax.broadcasted_iota(jnp.int32, (t_blk, t_blk), 0)
        c = lax.broadcasted_iota(jnp.int32, (t_blk, t_blk), 1)
        perm[...] = jnp.where(c == (r % SUBLANE) * clen + r // SUBLANE, 1.0, 0.0).astype(BF16)
        perm_t[...] = jnp.where(c == (r % clen) * SUBLANE + r // clen, 1.0, 0.0).astype(BF16)

    @pl.when(pl.program_id(1) == 0)
    def _():
        halo[...] = jnp.zeros_like(halo)
        hcar[...] = jnp.zeros_like(hcar)

    xp = _dot(perm[...], x_ref[...])
    pieces = []
    for k in range(taps):
        cur = xp[t_blk - (taps - k) * SUBLANE:t_blk - (taps - k - 1) * SUBLANE, :]
        prev = halo[k * SUBLANE:(k + 1) * SUBLANE, :]
        pieces.append(jnp.where(row == 0, pltpu.roll(prev, 1, 0), pltpu.roll(cur, 1, 0)))
    halo[...] = xp[t_blk - taps * SUBLANE:, :]
    xext = jnp.concatenate(pieces + [xp], axis=0)
    xc = cb_ref[0]
    for w in range(CONV_WIDTH):
        xc = xc + cw_ref[0, w:w + 1, :] * xext[w * SUBLANE:w * SUBLANE + t_blk, :]

    neg_lam = -lam_ref[0]
    softplus = jnp.maximum(neg_lam, 0.0) + jnp.log1p(jnp.exp(-jnp.abs(neg_lam)))
    neg_half_rate = (-0.5 * LRU_C) * softplus
    for n in range(LRU_BLOCKS):
        sl = slice(n * LRU_BLOCK_DIM, (n + 1) * LRU_BLOCK_DIM)
        xn = xc[:, sl]
        xn_b = xn.astype(BF16)
        t_r = jnp.tanh(0.5 * (_dot(xn_b, wa_ref[0, n].astype(BF16)) + ba_ref[0, :, sl]))
        t_i = jnp.tanh(0.5 * (_dot(xn_b, wx_ref[0, n].astype(BF16)) + bx_ref[0, :, sl]))
        half_rate = neg_half_rate[:, sl]
        log_a = half_rate * t_r + half_rate
        a = jnp.exp(log_a)
        a_s[:, sl] = a
        var = jnp.tanh(-log_a) * (a * a + 1.0)
        std = jnp.where(var > 0.0, var * lax.rsqrt(var), 0.0)
        half_x = 0.5 * xn
        b_s[:, sl] = std * (half_x * t_i + half_x)

    def body(j, carry):
        h, prod = carry
        r0 = j * SUBLANE
        a = a_s[pl.ds(r0, SUBLANE), :]
        h = a * h + b_s[pl.ds(r0, SUBLANE), :]
        prod = a * prod
        b_s[pl.ds(r0, SUBLANE), :] = h
        a_s[pl.ds(r0, SUBLANE), :] = prod
        return h, prod

    zeros = jnp.zeros((SUBLANE, LRU_WIDTH), F32)
    carry = (zeros, zeros + 1.0)
    for j in range(clen):
        carry = body(j, carry)
    h_end, a_end = carry

    a_inc, h_inc = _sublane_scan(a_end, h_end, row)
    h_prev_tile = hcar[...]
    h_chunk_end = a_inc * h_prev_tile + h_inc
    h_chunk_start = jnp.where(row == 0, h_prev_tile, pltpu.roll(h_chunk_end, 1, 0))
    hcar[...] = jnp.broadcast_to(h_chunk_end[SUBLANE - 1:SUBLANE, :], (SUBLANE, LRU_WIDTH))

    h_true = b_s[...] + a_s[...] * jnp.concatenate([h_chunk_start] * clen, axis=0)
    gp = _dot(perm[...], g_ref[...])
    out_p = (h_true * _silu(gp)).astype(BF16)
    o_ref[...] = _dot(perm_t[...], out_p).astype(o_ref.dtype)


def _lru_call(proj, cw, wa, wx, vecs, layer, batch, seq, ada=None):
    m = proj.shape[0]
    nt = seq // LRU_T
    w = LRU_WIDTH

    def col(c):
        return pl.BlockSpec((LRU_T, w), lambda b, t, c=c: (b * nt + t, c // w))

    def vec(offset):
        return _vec_spec(w, layer, 2, offset)

    def blk():
        return pl.BlockSpec((1, LRU_BLOCKS, LRU_BLOCK_DIM, LRU_BLOCK_DIM), lambda b, t: (layer, 0, 0, 0))

    in_specs = [col(COL_LX), col(COL_LG),
                pl.BlockSpec((1, CONV_WIDTH, w), lambda b, t: (layer, 0, 0)), vec(VEC_CONV_B),
                blk(), vec(VEC_BA), blk(), vec(VEC_BX), vec(VEC_LAM)]
    args = [proj, proj, cw, vecs, wa, vecs, wx, vecs, vecs]
    out_specs = pl.BlockSpec((LRU_T, w), lambda b, t: (b * nt + t, 0))
    out_shape = jax.ShapeDtypeStruct((m, w), BF16)
    if ada is not None:
        c_pad, ada_w = ada
        rows = c_pad.shape[0]
        _, d, n = ada_w.shape
        steps = batch * nt
        slab = d // steps
        assert slab * steps == d and slab % LANE == 0
        in_specs += [pl.BlockSpec((1, rows, slab), lambda b, t: (b * nt + t, 0, 0)),
                     pl.BlockSpec((1, slab, n), lambda b, t: (layer + 1, b * nt + t, 0)),
                     _vec_spec(n, layer + 1, 2, VEC_ADA_B)]
        args += [_c_slabs(c_pad, steps), ada_w, vecs]
        out_specs = (out_specs, pl.BlockSpec((1, rows, n), lambda b, t: (0, 0, 0)))
        out_shape = (out_shape, jax.ShapeDtypeStruct((1, rows, n), F32))
    return pl.pallas_call(
        functools.partial(_lru_kernel, with_ada=ada is not None),
        grid=(batch, nt),
        in_specs=in_specs,
        out_specs=out_specs,
        out_shape=out_shape,
        scratch_shapes=[
            pltpu.VMEM((LRU_T, LRU_T), BF16),
            pltpu.VMEM((LRU_T, LRU_T), BF16),
            pltpu.VMEM(((CONV_WIDTH - 1) * SUBLANE, w), F32),
            pltpu.VMEM((SUBLANE, w), F32),
            pltpu.VMEM((LRU_T, w), F32),
            pltpu.VMEM((LRU_T, w), F32),
        ],
        compiler_params=_params("arbitrary", "arbitrary"),
        name="rg_lru",
    )(*args)


def _rope64(x, cos_t, sin_t):
    swapped = pltpu.roll(x, MLA_ROPE // 2, 1) + pltpu.roll(x, LANE - MLA_ROPE // 2, 1)
    return x * cos_t + swapped * sin_t


def _mla_kernel(krope, mq_ref, mkv_ref, mg_ref, wq_ref, wkv_ref, qn_ref, kvn_ref,
                cos_ref, sin_ref, o_ref, qlat, kvlat, wq_s, qs, ks, vs, s_scr, p_scr, *, seq):
    rows = MLA_ROWS
    pair = pl.program_id(1)

    @pl.when(pair == 0)
    def _():
        for i in range(seq // rows):
            rs = slice(i * rows, (i + 1) * rows)
            qlat[rs, :] = _rms(mq_ref[rs, :].astype(F32), qn_ref[0]).astype(BF16)
            kvlat[rs, :] = _rms(mkv_ref[rs, :].astype(F32), kvn_ref[0]).astype(BF16)
        for hh in range(MLA_HPS):
            vs[hh, :, MLA_V:] = jnp.ones((seq, MLA_V), BF16)
        zeros = jnp.zeros((MLA_LORA, MLA_QK_PAD - MLA_QK), F32)
        wq_all = wq_ref[0]
        for hh in range(MLA_HEADS):
            w_head = wq_all[:, hh * MLA_QK:(hh + 1) * MLA_QK]
            wq_s[hh] = jnp.concatenate([w_head, zeros], axis=1).astype(BF16)

    scale = (MLA_QK ** -0.5) * math.log2(math.e)
    tq = MLA_TQ
    ri = lax.broadcasted_iota(jnp.int32, (tq, tq), 0)
    ci = lax.broadcasted_iota(jnp.int32, (tq, tq), 1)
    visible = (ci // CHUNK) <= (ri // CHUNK)
    n_q = seq // tq
    kv_w = MLA_NOPE + MLA_V

    for hh in range(MLA_HPS):
        wq = wq_s[pair * MLA_HPS + hh]
        wkv = wkv_ref[0, :, hh * kv_w:(hh + 1) * kv_w].astype(BF16)
        for i in range(seq // rows):
            rs = slice(i * rows, (i + 1) * rows)
            q = _dot(qlat[rs, :], wq)
            qs[hh, rs, 0:MLA_NOPE] = (q[:, 0:MLA_NOPE] * scale).astype(BF16)
            qs[hh, rs, MLA_NOPE:] = (_rope64(q[:, MLA_NOPE:], cos_ref[rs, :], sin_ref[rs, :]) * scale).astype(BF16)
            kv = _dot(kvlat[rs, :], wkv)
            ks[hh, rs, 0:MLA_NOPE] = kv[:, 0:MLA_NOPE].astype(BF16)
            ks[hh, rs, MLA_NOPE:] = krope[rs, :]
            vs[hh, rs, 0:MLA_V] = kv[:, MLA_NOPE:].astype(BF16)

    def scores(hh, i):
        q0, q1 = i * tq, (i + 1) * tq
        s_scr[hh, i % 2, :, 0:q1] = _dot_nt(qs[hh, q0:q1, :], ks[hh, 0:q1, :])

    def probs(hh, i):
        q0, q1 = i * tq, (i + 1) * tq
        s_d = jnp.where(visible, s_scr[hh, i % 2, :, q0:q1], -1e30)
        m = jnp.max(s_d, axis=-1, keepdims=True)
        if i > 0:
            m = jnp.maximum(m, jnp.max(s_scr[hh, i % 2, :, 0:q0], axis=-1, keepdims=True))
            p_scr[hh, i % 2, :, 0:q0] = jnp.exp2(s_scr[hh, i % 2, :, 0:q0] - m).astype(BF16)
        p_scr[hh, i % 2, :, q0:q1] = jnp.exp2(s_d - m).astype(BF16)

    def values(hh, i):
        q0, q1 = i * tq, (i + 1) * tq
        cols = slice(hh * MLA_V, (hh + 1) * MLA_V)
        acc = _dot(p_scr[hh, i % 2, :, 0:q1], vs[hh, 0:q1, :])
        out = acc[:, 0:MLA_V] / acc[:, MLA_V:]
        o_ref[q0:q1, cols] = (out * _silu(mg_ref[q0:q1, cols].astype(F32))).astype(o_ref.dtype)

    for hh in range(MLA_HPS):
        scores(hh, 0)
    for i in range(n_q + 1):
        for hh in range(MLA_HPS):
            if i + 1 < n_q:
                scores(hh, i + 1)
            if i < n_q:
                probs(hh, i)
            if i > 0:
                values(hh, i - 1)


def _mla_call(k_rope, proj, w_uq, w_ukv, vecs, cos_m, sin_m, layer, batch, seq):
    m = proj.shape[0]
    lora = MLA_LORA
    hps = MLA_HPS
    assert MLA_HEADS % hps == 0 and COL_MG % (hps * MLA_V) == 0
    return pl.pallas_call(
        functools.partial(_mla_kernel, seq=seq),
        grid=(batch, MLA_HEADS // hps),
        in_specs=[
            pl.BlockSpec((seq, LANE), lambda b, hd: (b, 0)),
            pl.BlockSpec((seq, lora), lambda b, hd: (b, COL_MQ // lora)),
            pl.BlockSpec((seq, lora), lambda b, hd: (b, COL_MKV // lora)),
            pl.BlockSpec((seq, hps * MLA_V), lambda b, hd: (b, COL_MG // (hps * MLA_V) + hd)),
            pl.BlockSpec((1, lora, MLA_HEADS * MLA_QK), lambda b, hd: (layer, 0, 0),
                         pipeline_mode=pl.Buffered(1)),
            pl.BlockSpec((1, lora, hps * (MLA_NOPE + MLA_V)), lambda b, hd: (layer, 0, hd)),
            _vec_spec(lora, layer, 2, VEC_Q_NORM),
            _vec_spec(lora, layer, 2, VEC_KV_NORM),
            pl.BlockSpec((seq, LANE), lambda b, hd: (b, 0)),
            pl.BlockSpec((seq, LANE), lambda b, hd: (b, 0)),
        ],
        out_specs=pl.BlockSpec((seq, hps * MLA_V), lambda b, hd: (b, hd)),
        out_shape=jax.ShapeDtypeStruct((m, MLA_WIDTH), BF16),
        scratch_shapes=[
            pltpu.VMEM((seq, lora), BF16),
            pltpu.VMEM((seq, lora), BF16),
            pltpu.VMEM((MLA_HEADS, lora, MLA_QK_PAD), BF16),
            pltpu.VMEM((hps, seq, MLA_QK_PAD), BF16),
            pltpu.VMEM((hps, seq, MLA_QK_PAD), BF16),
            pltpu.VMEM((hps, seq, 2 * MLA_V), BF16),
            pltpu.VMEM((hps, 2, MLA_TQ, seq), F32),
            pltpu.VMEM((hps, 2, MLA_TQ, seq), BF16),
        ],
        compiler_params=_params("arbitrary", "arbitrary"),
        name="mla",
    )(k_rope, proj, proj, proj, w_uq, w_ukv, vecs, vecs, cos_m, sin_m)


def _merge_kernel(yr_ref, yl_ref, ym_ref, l0_ref, l1_ref, l2_ref, x_ref, res_ref, g_ref,
                  wb_ref, wo_ref, *rest, tiles_per_batch, emit_next):
    b = pl.program_id(0) // tiles_per_batch
    merged = None
    for i, (y_ref, l_ref) in enumerate(((yr_ref, l0_ref), (yl_ref, l1_ref), (ym_ref, l2_ref))):
        z = _dot(y_ref[...], wb_ref[i * BRANCH_WIDTH:(i + 1) * BRANCH_WIDTH, :])
        z = z * jax.nn.sigmoid(l_ref[...].astype(F32))
        merged = z if merged is None else merged + z
    y = _dot(merged.astype(BF16), wo_ref[...])
    post = g_ref[0] * (1.0 + res_ref[0, pl.ds(b, 1), :])
    inv = lax.rsqrt(jnp.mean(y * y, axis=-1, keepdims=True) + NORM_EPS)
    x_new = x_ref[...] + (y * inv) * post
    if emit_next:
        gn_ref, shift_ref, scale_ref, o_ref, h_ref = rest
        pre = gn_ref[0] * (1.0 + scale_ref[0, pl.ds(b, 1), :])
        inv_n = lax.rsqrt(jnp.mean(x_new * x_new, axis=-1, keepdims=True) + NORM_EPS)
        h_ref[...] = ((x_new * inv_n) * pre + shift_ref[0, pl.ds(b, 1), :]).astype(h_ref.dtype)
    else:
        (o_ref,) = rest
    o_ref[...] = x_new


def _merge_call(y_ret, y_lru, y_mla, proj, x2, mod, mod_next, vecs, wb, wo, layer, seq):
    m, d = x2.shape
    tm = MERGE_TM
    mod_rows = mod.shape[1]
    emit_next = mod_next is not None

    def rows(width, c=0):
        return pl.BlockSpec((tm, width), lambda i, c=c: (i, c))

    def resident(shape):
        return pl.BlockSpec(shape, lambda i: (0, 0), pipeline_mode=pl.Buffered(1))

    def mod_part(part):
        return pl.BlockSpec((1, mod_rows, d), lambda i: (0, 0, part))

    merge_col = COL_MERGE // d
    in_specs = [rows(BRANCH_WIDTH), rows(BRANCH_WIDTH), rows(BRANCH_WIDTH),
                rows(d, merge_col), rows(d, merge_col + 1), rows(d, merge_col + 2),
                rows(d), mod_part(2), _vec_spec(d, layer, 1, VEC_NORM_POST),
                resident(wb.shape), resident(wo.shape)]
    args = [y_ret, y_lru, y_mla, proj, proj, proj, x2, mod, vecs, wb, wo]
    out_specs = rows(d)
    out_shape = jax.ShapeDtypeStruct((m, d), F32)
    if emit_next:
        in_specs += [_vec_spec(d, layer + 1, 1, VEC_NORM_PRE), mod_part(0), mod_part(1)]
        args += [vecs, mod_next, mod_next]
        out_specs = (out_specs, rows(d))
        out_shape = (out_shape, jax.ShapeDtypeStruct((m, d), BF16))
    return pl.pallas_call(
        functools.partial(_merge_kernel, tiles_per_batch=seq // tm, emit_next=emit_next),
        grid=(m // tm,),
        in_specs=in_specs,
        out_specs=out_specs,
        out_shape=out_shape,
        compiler_params=_params("arbitrary"),
        name="merge_out",
    )(*args)


def kernel(x, c, positions, ada_w, ada_b, norm_pre, norm_post, w_in, ret_gn, lru_conv_w, lru_conv_b,
           lru_wa, lru_ba, lru_wx, lru_bx, lru_lambda, mla_q_norm, mla_w_uq, mla_kv_norm, mla_w_ukv,
           w_branch, w_out):
    batch, seq, d = x.shape
    depth = w_in.shape[0]
    m = batch * seq
    assert d == D_MODEL and w_in.shape[2] == SRC_WIDTH and batch <= SUBLANE
    assert seq % PROJ_TM == 0 and seq % RET_T == 0 and seq % LRU_T == 0 and seq % MLA_TQ == 0
    assert seq % MERGE_TM == 0 and seq % NORM_TM == 0

    c_pad = jnp.pad(c, ((0, SUBLANE - batch), (0, 0)))
    vecs = _pack_vectors(ada_b, norm_pre, norm_post, ret_gn, lru_conv_b, lru_ba, lru_bx, lru_lambda,
                         mla_q_norm, mla_kv_norm)
    mod = _ada_call(c_pad, ada_w, vecs, 0)
    w_in_t = jnp.swapaxes(w_in, 1, 2)

    x2 = x.reshape(m, d)
    h, cos_r, sin_r, cos_m, sin_m = _prenorm_call(x2, mod, vecs, positions, 0, seq)
    for l in range(depth):
        proj = _inproj_call(h, w_in_t, l)
        y_ret, wb, wo, k_rope = _ret_call(proj, cos_r, sin_r, vecs, w_branch, w_out, h, w_in_t, cos_m, sin_m,
                                          l, batch, seq)
        lru_args = (proj, lru_conv_w, lru_wa, lru_wx, vecs, l, batch, seq)
        if l + 1 < depth:
            y_lru, mod_next = _lru_call(*lru_args, ada=(c_pad, ada_w))
        else:
            y_lru, mod_next = _lru_call(*lru_args), None
        y_mla = _mla_call(k_rope, proj, mla_w_uq, mla_w_ukv, vecs, cos_m, sin_m, l, batch, seq)
        out = _merge_call(y_ret, y_lru, y_mla, proj, x2, mod, mod_next, vecs, wb, wo, l, seq)
        mod = mod_next
        if l + 1 < depth:
            x2, h = out
        else:
            x2 = out
    return x2.reshape(batch, seq, d)
```

```python
import functools
import math

import jax
import jax.numpy as jnp
from jax import lax
from jax.experimental import pallas as pl
from jax.experimental.pallas import tpu as pltpu

F32 = jnp.float32
BF16 = jnp.bfloat16

D_MODEL = 2048
CHUNK = 64
ROPE_BASE = 10000.0
NORM_EPS = 1e-6

RET_HEADS = 8
RET_DIM = 128
RET_WIDTH = RET_HEADS * RET_DIM

LRU_WIDTH = 1024
LRU_BLOCKS = 8
LRU_BLOCK_DIM = 128
CONV_WIDTH = 4
LRU_C = 8.0

MLA_HEADS = 8
MLA_NOPE = 128
MLA_ROPE = 64
MLA_V = 128
MLA_LORA = 512
MLA_WIDTH = MLA_HEADS * MLA_V
MLA_QK = MLA_NOPE + MLA_ROPE
MLA_QK_PAD = 256

LANE = 128
SUBLANE = 8
VMEM_LIMIT = 58 * 1024 * 1024

BRANCH_WIDTH = 1024
assert RET_WIDTH == LRU_WIDTH == MLA_WIDTH == BRANCH_WIDTH

COL_RQ = 0
COL_RK = COL_RQ + RET_WIDTH
COL_RV = COL_RK + RET_WIDTH
COL_RG = COL_RV + RET_WIDTH
COL_LX = COL_RG + RET_WIDTH
COL_LG = COL_LX + LRU_WIDTH
COL_MQ = COL_LG + LRU_WIDTH
COL_MKV = COL_MQ + MLA_LORA
COL_MG = COL_MKV + MLA_LORA
COL_MERGE = COL_MG + MLA_WIDTH
PROJ_WIDTH = COL_MERGE + 3 * D_MODEL
SRC_MKR = COL_MG
SRC_WIDTH = PROJ_WIDTH + MLA_ROPE
PROJ_TN = 1024
PROJ_TM = 2048
ALIGNED_TILES = SRC_MKR // PROJ_TN
CAST_ROWS = 64
assert SRC_MKR % PROJ_TN == 0 and PROJ_WIDTH % PROJ_TN == 0 and COL_MERGE % D_MODEL == 0

NORM_TM = 1024
RET_T = 256
LRU_T = 512
LRU_CLEN = LRU_T // SUBLANE
MLA_ROWS = 256
MLA_TQ = 256
MLA_HPS = 2
MERGE_TM = 256
ADA_TK = 512


def _silu(v):
    return v * jax.nn.sigmoid(v)


def _dot(a, b):
    return jnp.dot(a, b, preferred_element_type=F32)


def _dot_nt(a, b):
    return lax.dot_general(a, b, (((1,), (1,)), ((), ())), preferred_element_type=F32)


def _dot_tn(a, b):
    return lax.dot_general(a, b, (((0,), (0,)), ((), ())), preferred_element_type=F32)


def _rms(x, gain):
    return x * lax.rsqrt(jnp.mean(x * x, axis=-1, keepdims=True) + NORM_EPS) * gain


def _params(*sem):
    return pltpu.CompilerParams(dimension_semantics=sem, vmem_limit_bytes=VMEM_LIMIT)


VEC_ADA_B = 0
VEC_NORM_PRE = 3 * D_MODEL
VEC_NORM_POST = VEC_NORM_PRE + D_MODEL
VEC_RET_GN = VEC_NORM_POST + D_MODEL
VEC_CONV_B = VEC_RET_GN + RET_WIDTH
VEC_BA = VEC_CONV_B + LRU_WIDTH
VEC_BX = VEC_BA + LRU_WIDTH
VEC_LAM = VEC_BX + LRU_WIDTH
VEC_Q_NORM = VEC_LAM + LRU_WIDTH
VEC_KV_NORM = VEC_Q_NORM + MLA_LORA


def _pack_vectors(ada_b, norm_pre, norm_post, ret_gn, conv_b, ba, bx, lam, q_norm, kv_norm):
    packed = jnp.concatenate([ada_b, norm_pre, norm_post, ret_gn, conv_b, ba, bx, lam, q_norm, kv_norm], axis=1)
    return packed[:, None, :]


def _vec_spec(width, layer, ngrid, col):
    assert col % width == 0
    if ngrid == 1:
        return pl.BlockSpec((1, 1, width), lambda i: (layer, 0, col // width))
    return pl.BlockSpec((1, 1, width), lambda i, j: (layer, 0, col // width))


def _ada_accumulate(first, c_ref, w_ref, b_ref, o_ref):
    @pl.when(first)
    def _():
        o_ref[0] = jnp.broadcast_to(b_ref[0], o_ref.shape[1:])

    c_act = _silu(c_ref[0]).astype(BF16)
    o_ref[0] += _dot(c_act, w_ref[0].astype(BF16))


def _ada_kernel(c_ref, w_ref, b_ref, o_ref):
    _ada_accumulate(pl.program_id(0) == 0, c_ref, w_ref, b_ref, o_ref)


def _c_slabs(c_pad, n_slabs):
    rows, d = c_pad.shape
    return c_pad.reshape(rows, n_slabs, d // n_slabs).transpose(1, 0, 2)


def _ada_call(c_pad, ada_w, vecs, layer):
    _, d, n = ada_w.shape
    rows = c_pad.shape[0]
    n_slabs = d // ADA_TK
    return pl.pallas_call(
        _ada_kernel,
        grid=(n_slabs,),
        in_specs=[
            pl.BlockSpec((1, rows, ADA_TK), lambda k: (k, 0, 0)),
            pl.BlockSpec((1, ADA_TK, n), lambda k: (layer, k, 0)),
            _vec_spec(n, layer, 1, VEC_ADA_B),
        ],
        out_specs=pl.BlockSpec((1, rows, n), lambda k: (0, 0, 0)),
        out_shape=jax.ShapeDtypeStruct((1, rows, n), F32),
        compiler_params=_params("arbitrary"),
        name="ada_mod",
    )(_c_slabs(c_pad, n_slabs), ada_w, vecs)


def _modulate(x, gain, shift_ref, scale_ref, b):
    shift = shift_ref[0, pl.ds(b, 1), :]
    scale = scale_ref[0, pl.ds(b, 1), :]
    return _rms(x, gain) * (1.0 + scale) + shift


def _prenorm_kernel(x_ref, g_ref, shift_ref, scale_ref, pos_ref, freq_ref,
                    o_ref, cos_r_ref, sin_r_ref, cos_m_ref, sin_m_ref, *, tiles_per_batch):
    b = pl.program_id(0) // tiles_per_batch
    o_ref[...] = _modulate(x_ref[...], g_ref[0], shift_ref, scale_ref, b).astype(o_ref.dtype)

    ang = freq_ref[...] * pos_ref[0].astype(F32)
    cos_t = jnp.cos(ang)
    sin_t = jnp.sin(ang)
    n_r, n_m = RET_DIM // 2, MLA_ROPE // 2
    c_r, s_r = cos_t[0:n_r], sin_t[0:n_r]
    c_m, s_m = cos_t[n_r:n_r + n_m], sin_t[n_r:n_r + n_m]
    pad = jnp.zeros((LANE - MLA_ROPE, ang.shape[1]), F32)
    cos_r_ref[...] = jnp.concatenate([c_r, c_r], axis=0).T
    sin_r_ref[...] = jnp.concatenate([-s_r, s_r], axis=0).T
    cos_m_ref[...] = jnp.concatenate([c_m, c_m, pad], axis=0).T
    sin_m_ref[...] = jnp.concatenate([-s_m, s_m, pad], axis=0).T


def _prenorm_call(x2, mod, vecs, positions, layer, seq):
    m, d = x2.shape
    rows = mod.shape[1]
    n_tiles = m // NORM_TM

    def inv_freq(dim):
        return ROPE_BASE ** (-jnp.arange(0, dim, 2, dtype=F32) / dim)

    freqs = jnp.concatenate([inv_freq(RET_DIM), inv_freq(MLA_ROPE)])[:, None]
    table = pl.BlockSpec((NORM_TM, LANE), lambda i: (i, 0))
    return pl.pallas_call(
        functools.partial(_prenorm_kernel, tiles_per_batch=seq // NORM_TM),
        grid=(n_tiles,),
        in_specs=[
            pl.BlockSpec((NORM_TM, d), lambda i: (i, 0)),
            _vec_spec(d, layer, 1, VEC_NORM_PRE),
            pl.BlockSpec((1, rows, d), lambda i: (0, 0, 0)),
            pl.BlockSpec((1, rows, d), lambda i: (0, 0, 1)),
            pl.BlockSpec((1, 1, NORM_TM), lambda i: (i, 0, 0)),
            pl.BlockSpec(freqs.shape, lambda i: (0, 0)),
        ],
        out_specs=(pl.BlockSpec((NORM_TM, d), lambda i: (i, 0)), table, table, table, table),
        out_shape=(jax.ShapeDtypeStruct((m, d), BF16),) + (jax.ShapeDtypeStruct((m, LANE), F32),) * 4,
        compiler_params=_params("arbitrary"),
        name="pre_norm",
    )(x2, vecs, mod, mod, positions.reshape(n_tiles, 1, NORM_TM), freqs)


def _inproj_kernel(h_ref, w_ref, wx_ref, o_ref, wb_ref):
    j = pl.program_id(0)
    first_row_tile = pl.program_id(1) == 0
    n_chunks = PROJ_TN // CAST_ROWS

    @pl.when(first_row_tile & (j < ALIGNED_TILES))
    def _():
        def body(i, carry):
            r0 = pl.multiple_of(i * CAST_ROWS, CAST_ROWS)
            wb_ref[pl.ds(r0, CAST_ROWS), :] = w_ref[0, pl.ds(r0, CAST_ROWS), :].astype(BF16)
            return carry

        lax.fori_loop(0, n_chunks, body, 0)

    @pl.when(first_row_tile & (j >= ALIGNED_TILES))
    def _():
        def body(i, carry):
            r0 = pl.multiple_of(i * CAST_ROWS, CAST_ROWS)
            wb_ref[pl.ds(r0, CAST_ROWS), :] = w_ref[0, pl.ds(r0 + MLA_ROPE, CAST_ROWS), :].astype(BF16)
            return carry

        lax.fori_loop(0, n_chunks - 1, body, 0)
        wb_ref[PROJ_TN - MLA_ROPE:, :] = wx_ref[0].astype(BF16)

    o_ref[...] = _dot_nt(h_ref[...], wb_ref[...]).astype(o_ref.dtype)


def _inproj_call(h, w_in_t, layer):
    m, d = h.shape
    per_tile = PROJ_TN // MLA_ROPE
    last_block = SRC_WIDTH // MLA_ROPE - 1
    return pl.pallas_call(
        _inproj_kernel,
        grid=(PROJ_WIDTH // PROJ_TN, m // PROJ_TM),
        in_specs=[
            pl.BlockSpec((PROJ_TM, d), lambda j, i: (i, 0)),
            pl.BlockSpec((1, PROJ_TN, d), lambda j, i: (layer, j, 0)),
            pl.BlockSpec((1, MLA_ROPE, d),
                         lambda j, i: (layer, jnp.minimum((j + 1) * per_tile, last_block), 0)),
        ],
        out_specs=pl.BlockSpec((PROJ_TM, PROJ_TN), lambda j, i: (i, j)),
        out_shape=jax.ShapeDtypeStruct((m, PROJ_WIDTH), BF16),
        scratch_shapes=[pltpu.VMEM((PROJ_TN, d), BF16)],
        compiler_params=_params("arbitrary", "arbitrary"),
        name="in_proj",
    )(h, w_in_t, w_in_t)


_LOG_GAMMA = tuple(math.log1p(-(2.0 ** (-5.0 - h))) for h in range(RET_HEADS))


def _ret_kernel(q_ref, k_ref, v_ref, g_ref, cos_ref, sin_ref, gn_ref, wb_ref, wo_ref,
                h_ref, wkr_ref, cos_m_ref, sin_m_ref, mkv_ref, kvn_ref, wkv_ref,
                o_ref, wb_out, wo_out, kr_out, kv_out,
                state_ref, dmat_ref, dq_ref, dk_ref, wkr_s, wkv_s):
    t_blk = RET_T

    @pl.when((pl.program_id(0) == 0) & (pl.program_id(1) == 0))
    def _():
        wkr = jnp.concatenate([wkr_ref[0], jnp.zeros((LANE - MLA_ROPE, D_MODEL), F32)], axis=0)
        wkr_s[...] = wkr.T.astype(BF16)
        wkv_s[...] = wkv_ref[0].astype(BF16)
        ri = lax.broadcasted_iota(jnp.int32, (t_blk, t_blk), 0)
        ci = lax.broadcasted_iota(jnp.int32, (t_blk, t_blk), 1)
        dist = jnp.abs(ri - ci).astype(F32)
        visible = (ci // CHUNK) <= (ri // CHUNK)
        row = lax.broadcasted_iota(jnp.int32, (t_blk, RET_DIM), 0).astype(F32)
        for h in range(RET_HEADS):
            lg = _LOG_GAMMA[h]
            dmat_ref[h] = jnp.where(visible, jnp.exp(lg * dist), 0.0)
            dq_ref[h] = jnp.exp(lg * (row + 1.0))
            dk_ref[h] = jnp.exp(lg * ((t_blk - 1.0) - row))

    @pl.when(pl.program_id(1) == 0)
    def _():
        state_ref[...] = jnp.zeros_like(state_ref)

    wb_out[...] = wb_ref[0].astype(BF16)
    wo_out[...] = wo_ref[0].astype(BF16)
    kr_out[...] = _rope64(_dot(h_ref[...], wkr_s[...]), cos_m_ref[...], sin_m_ref[...]).astype(BF16)
    kv_lat = _rms(mkv_ref[...].astype(F32), kvn_ref[0]).astype(BF16)
    kv_out[...] = _dot(kv_lat, wkv_s[...]).astype(BF16)

    cos = cos_ref[...]
    sin = sin_ref[...]
    for h in range(RET_HEADS):
        sl = slice(h * RET_DIM, (h + 1) * RET_DIM)
        q = q_ref[:, sl].astype(F32)
        k = k_ref[:, sl].astype(F32)
        q = (q * cos + pltpu.roll(q, RET_DIM // 2, 1) * sin) * (RET_DIM ** -0.5)
        k = k * cos + pltpu.roll(k, RET_DIM // 2, 1) * sin
        v = v_ref[:, sl]
        scores = _dot_nt(q.astype(BF16), k.astype(BF16)) * dmat_ref[h]
        o = _dot(scores.astype(BF16), v)
        state = state_ref[h]
        o = o + _dot((q * dq_ref[h]).astype(BF16), state.astype(BF16))
        k_dec = (k * dk_ref[h]).astype(BF16)
        state_ref[h] = state * math.exp(_LOG_GAMMA[h] * t_blk) + _dot_tn(k_dec, v)
        mean = jnp.mean(o, axis=-1, keepdims=True)
        cen = o - mean
        var = jnp.mean(cen * cen, axis=-1, keepdims=True)
        normed = cen * lax.rsqrt(var + NORM_EPS) * gn_ref[0, :, sl]
        o_ref[:, sl] = (normed * _silu(g_ref[:, sl].astype(F32))).astype(o_ref.dtype)


def _ret_call(proj, cos_r, sin_r, vecs, w_branch, w_out, h, w_in_t, cos_m, sin_m, w_ukv, layer, batch, seq):
    m = proj.shape[0]
    lora, kv_width = w_ukv.shape[1:]
    rows = RET_T
    nt = seq // rows
    w = RET_WIDTH
    steps = batch * nt
    wb_rows, wo_rows = w_branch.shape[1] // steps, w_out.shape[1] // steps
    bf16_rows = 2 * SUBLANE
    assert wb_rows * steps == w_branch.shape[1] and wb_rows % bf16_rows == 0
    assert wo_rows * steps == w_out.shape[1] and wo_rows % bf16_rows == 0
    d = w_branch.shape[2]

    def col(c):
        return pl.BlockSpec((rows, w), lambda b, t, c=c: (b * nt + t, c // w))

    tab = pl.BlockSpec((rows, RET_DIM), lambda b, t: (b * nt + t, 0))
    return pl.pallas_call(
        _ret_kernel,
        grid=(batch, nt),
        in_specs=[col(COL_RQ), col(COL_RK), col(COL_RV), col(COL_RG), tab, tab,
                  _vec_spec(w, layer, 2, VEC_RET_GN),
                  pl.BlockSpec((1, wb_rows, d), lambda b, t: (layer, b * nt + t, 0)),
                  pl.BlockSpec((1, wo_rows, d), lambda b, t: (layer, b * nt + t, 0)),
                  pl.BlockSpec((rows, d), lambda b, t: (b * nt + t, 0)),
                  pl.BlockSpec((1, MLA_ROPE, d), lambda b, t: (layer, SRC_MKR // MLA_ROPE, 0)),
                  tab, tab,
                  pl.BlockSpec((rows, lora), lambda b, t: (b * nt + t, COL_MKV // lora)),
                  _vec_spec(lora, layer, 2, VEC_KV_NORM),
                  pl.BlockSpec((1, lora, kv_width), lambda b, t: (layer, 0, 0))],
        out_specs=(pl.BlockSpec((rows, w), lambda b, t: (b * nt + t, 0)),
                   pl.BlockSpec((wb_rows, d), lambda b, t: (b * nt + t, 0)),
                   pl.BlockSpec((wo_rows, d), lambda b, t: (b * nt + t, 0)),
                   tab,
                   pl.BlockSpec((rows, kv_width), lambda b, t: (b * nt + t, 0))),
        out_shape=(jax.ShapeDtypeStruct((m, w), BF16),
                   jax.ShapeDtypeStruct(w_branch.shape[1:], BF16),
                   jax.ShapeDtypeStruct(w_out.shape[1:], BF16),
                   jax.ShapeDtypeStruct((m, LANE), BF16),
                   jax.ShapeDtypeStruct((m, kv_width), BF16)),
        scratch_shapes=[
            pltpu.VMEM((RET_HEADS, RET_DIM, RET_DIM), F32),
            pltpu.VMEM((RET_HEADS, RET_T, RET_T), F32),
            pltpu.VMEM((RET_HEADS, RET_T, RET_DIM), F32),
            pltpu.VMEM((RET_HEADS, RET_T, RET_DIM), F32),
            pltpu.VMEM((D_MODEL, LANE), BF16),
            pltpu.VMEM((lora, kv_width), BF16),
        ],
        compiler_params=_params("arbitrary", "arbitrary"),
        name="retention",
    )(proj, proj, proj, proj, cos_r, sin_r, vecs, w_branch, w_out, h, w_in_t, cos_m, sin_m,
      proj, vecs, w_ukv)


def _sublane_scan(a, b, row):
    for s in (1, 2, 4):
        keep = row >= s
        a_prev = jnp.where(keep, pltpu.roll(a, s, 0), 1.0)
        b_prev = jnp.where(keep, pltpu.roll(b, s, 0), 0.0)
        b = a * b_prev + b
        a = a * a_prev
    return a, b


def _lru_kernel(x_ref, g_ref, cw_ref, cb_ref, wa_ref, ba_ref, wx_ref, bx_ref, lam_ref, *rest, with_ada):
    if with_ada:
        c_ref, aw_ref, ab_ref, o_ref, mod_ref, perm, perm_t, halo, hcar, a_s, b_s = rest
        _ada_accumulate((pl.program_id(0) == 0) & (pl.program_id(1) == 0), c_ref, aw_ref, ab_ref, mod_ref)
    else:
        o_ref, perm, perm_t, halo, hcar, a_s, b_s = rest
    t_blk = LRU_T
    clen = LRU_CLEN
    taps = CONV_WIDTH - 1
    row = lax.broadcasted_iota(jnp.int32, (SUBLANE, LRU_WIDTH), 0)

    @pl.when((pl.program_id(0) == 0) & (pl.program_id(1) == 0))
    def _():
        r = lax.broadcasted_iota(jnp.int32, (t_blk, t_blk), 0)
        c = lax.broadcasted_iota(jnp.int32, (t_blk, t_blk), 1)
        perm[...] = jnp.where(c == (r % SUBLANE) * clen + r // SUBLANE, 1.0, 0.0).astype(BF16)
        perm_t[...] = jnp.where(c == (r % clen) * SUBLANE + r // clen, 1.0, 0.0).astype(BF16)

    @pl.when(pl.program_id(1) == 0)
    def _():
        halo[...] = jnp.zeros_like(halo)
        hcar[...] = jnp.zeros_like(hcar)

    xp = _dot(perm[...], x_ref[...])
    pieces = []
    for k in range(taps):
        cur = xp[t_blk - (taps - k) * SUBLANE:t_blk - (taps - k - 1) * SUBLANE, :]
        prev = halo[k * SUBLANE:(k + 1) * SUBLANE, :]
        pieces.append(jnp.where(row == 0, pltpu.roll(prev, 1, 0), pltpu.roll(cur, 1, 0)))
    halo[...] = xp[t_blk - taps * SUBLANE:, :]
    xext = jnp.concatenate(pieces + [xp], axis=0)
    xc = cb_ref[0]
    for w in range(CONV_WIDTH):
        xc = xc + cw_ref[0, w:w + 1, :] * xext[w * SUBLANE:w * SUBLANE + t_blk, :]

    neg_lam = -lam_ref[0]
    softplus = jnp.maximum(neg_lam, 0.0) + jnp.log1p(jnp.exp(-jnp.abs(neg_lam)))
    neg_half_rate = (-0.5 * LRU_C) * softplus
    for n in range(LRU_BLOCKS):
        sl = slice(n * LRU_BLOCK_DIM, (n + 1) * LRU_BLOCK_DIM)
        xn = xc[:, sl]
        xn_b = xn.astype(BF16)
        t_r = jnp.tanh(0.5 * (_dot(xn_b, wa_ref[0, n].astype(BF16)) + ba_ref[0, :, sl]))
        t_i = jnp.tanh(0.5 * (_dot(xn_b, wx_ref[0, n].astype(BF16)) + bx_ref[0, :, sl]))
        half_rate = neg_half_rate[:, sl]
        log_a = half_rate * t_r + half_rate
        a = jnp.exp(log_a)
        a_s[:, sl] = a
        var = jnp.tanh(-log_a) * (a * a + 1.0)
        std = jnp.where(var > 0.0, var * lax.rsqrt(var), 0.0)
        half_x = 0.5 * xn
        b_s[:, sl] = std * (half_x * t_i + half_x)

    def body(j, carry):
        h, prod = carry
        r0 = j * SUBLANE
        a = a_s[pl.ds(r0, SUBLANE), :]
        h = a * h + b_s[pl.ds(r0, SUBLANE), :]
        prod = a * prod
        b_s[pl.ds(r0, SUBLANE), :] = h
        a_s[pl.ds(r0, SUBLANE), :] = prod
        return h, prod

    zeros = jnp.zeros((SUBLANE, LRU_WIDTH), F32)
    carry = (zeros, zeros + 1.0)
    for j in range(clen):
        carry = body(j, carry)
    h_end, a_end = carry

    a_inc, h_inc = _sublane_scan(a_end, h_end, row)
    h_prev_tile = hcar[...]
    h_chunk_end = a_inc * h_prev_tile + h_inc
    h_chunk_start = jnp.where(row == 0, h_prev_tile, pltpu.roll(h_chunk_end, 1, 0))
    hcar[...] = jnp.broadcast_to(h_chunk_end[SUBLANE - 1:SUBLANE, :], (SUBLANE, LRU_WIDTH))

    h_true = b_s[...] + a_s[...] * jnp.concatenate([h_chunk_start] * clen, axis=0)
    gp = _dot(perm[...], g_ref[...])
    out_p = (h_true * _silu(gp)).astype(BF16)
    o_ref[...] = _dot(perm_t[...], out_p).astype(o_ref.dtype)


def _lru_call(proj, cw, wa, wx, vecs, layer, batch, seq, ada=None):
    m = proj.shape[0]
    nt = seq // LRU_T
    w = LRU_WIDTH

    def col(c):
        return pl.BlockSpec((LRU_T, w), lambda b, t, c=c: (b * nt + t, c // w))

    def vec(offset):
        return _vec_spec(w, layer, 2, offset)

    def blk():
        return pl.BlockSpec((1, LRU_BLOCKS, LRU_BLOCK_DIM, LRU_BLOCK_DIM), lambda b, t: (layer, 0, 0, 0))

    in_specs = [col(COL_LX), col(COL_LG),
                pl.BlockSpec((1, CONV_WIDTH, w), lambda b, t: (layer, 0, 0)), vec(VEC_CONV_B),
                blk(), vec(VEC_BA), blk(), vec(VEC_BX), vec(VEC_LAM)]
    args = [proj, proj, cw, vecs, wa, vecs, wx, vecs, vecs]
    out_specs = pl.BlockSpec((LRU_T, w), lambda b, t: (b * nt + t, 0))
    out_shape = jax.ShapeDtypeStruct((m, w), BF16)
    if ada is not None:
        c_pad, ada_w = ada
        rows = c_pad.shape[0]
        _, d, n = ada_w.shape
        steps = batch * nt
        slab = d // steps
        assert slab * steps == d and slab % LANE == 0
        in_specs += [pl.BlockSpec((1, rows, slab), lambda b, t: (b * nt + t, 0, 0)),
                     pl.BlockSpec((1, slab, n), lambda b, t: (layer + 1, b * nt + t, 0)),
                     _vec_spec(n, layer + 1, 2, VEC_ADA_B)]
        args += [_c_slabs(c_pad, steps), ada_w, vecs]
        out_specs = (out_specs, pl.BlockSpec((1, rows, n), lambda b, t: (0, 0, 0)))
        out_shape = (out_shape, jax.ShapeDtypeStruct((1, rows, n), F32))
    return pl.pallas_call(
        functools.partial(_lru_kernel, with_ada=ada is not None),
        grid=(batch, nt),
        in_specs=in_specs,
        out_specs=out_specs,
        out_shape=out_shape,
        scratch_shapes=[
            pltpu.VMEM((LRU_T, LRU_T), BF16),
            pltpu.VMEM((LRU_T, LRU_T), BF16),
            pltpu.VMEM(((CONV_WIDTH - 1) * SUBLANE, w), F32),
            pltpu.VMEM((SUBLANE, w), F32),
            pltpu.VMEM((LRU_T, w), F32),
            pltpu.VMEM((LRU_T, w), F32),
        ],
        compiler_params=_params("arbitrary", "arbitrary"),
        name="rg_lru",
    )(*args)


def _rope64(x, cos_t, sin_t):
    swapped = pltpu.roll(x, MLA_ROPE // 2, 1) + pltpu.roll(x, LANE - MLA_ROPE // 2, 1)
    return x * cos_t + swapped * sin_t


def _mla_kernel(krope, kv_ref, mq_ref, mg_ref, wq_ref, qn_ref,
                cos_ref, sin_ref, o_ref, qlat, wq_s, qs, ks, vs, s_scr, p_scr, *, seq):
    rows = MLA_ROWS
    pair = pl.program_id(1)

    @pl.when(pair == 0)
    def _():
        for i in range(seq // rows):
            rs = slice(i * rows, (i + 1) * rows)
            qlat[rs, :] = _rms(mq_ref[rs, :].astype(F32), qn_ref[0]).astype(BF16)
        for hh in range(MLA_HPS):
            vs[hh, :, MLA_V:] = jnp.ones((seq, MLA_V), BF16)
        zeros = jnp.zeros((MLA_LORA, MLA_QK_PAD - MLA_QK), F32)
        wq_all = wq_ref[0]
        for hh in range(MLA_HEADS):
            w_head = wq_all[:, hh * MLA_QK:(hh + 1) * MLA_QK]
            wq_s[hh] = jnp.concatenate([w_head, zeros], axis=1).astype(BF16)

    scale = (MLA_QK ** -0.5) * math.log2(math.e)
    tq = MLA_TQ
    ri = lax.broadcasted_iota(jnp.int32, (tq, tq), 0)
    ci = lax.broadcasted_iota(jnp.int32, (tq, tq), 1)
    visible = (ci // CHUNK) <= (ri // CHUNK)
    n_q = seq // tq
    kv_w = MLA_NOPE + MLA_V

    for hh in range(MLA_HPS):
        wq = wq_s[pair * MLA_HPS + hh]
        ks[hh, :, 0:MLA_NOPE] = kv_ref[:, hh * kv_w:hh * kv_w + MLA_NOPE]
        ks[hh, :, MLA_NOPE:] = krope[...]
        vs[hh, :, 0:MLA_V] = kv_ref[:, hh * kv_w + MLA_NOPE:(hh + 1) * kv_w]
        for i in range(seq // rows):
            rs = slice(i * rows, (i + 1) * rows)
            q = _dot(qlat[rs, :], wq)
            qs[hh, rs, 0:MLA_NOPE] = (q[:, 0:MLA_NOPE] * scale).astype(BF16)
            qs[hh, rs, MLA_NOPE:] = (_rope64(q[:, MLA_NOPE:], cos_ref[rs, :], sin_ref[rs, :]) * scale).astype(BF16)

    def scores(hh, i):
        q0, q1 = i * tq, (i + 1) * tq
        s_scr[hh, i % 2, :, 0:q1] = _dot_nt(qs[hh, q0:q1, :], ks[hh, 0:q1, :])

    def probs(hh, i):
        q0, q1 = i * tq, (i + 1) * tq
        s_d = jnp.where(visible, s_scr[hh, i % 2, :, q0:q1], -1e30)
        m = jnp.max(s_d, axis=-1, keepdims=True)
        if i > 0:
            m = jnp.maximum(m, jnp.max(s_scr[hh, i % 2, :, 0:q0], axis=-1, keepdims=True))
            p_scr[hh, i % 2, :, 0:q0] = jnp.exp2(s_scr[hh, i % 2, :, 0:q0] - m).astype(BF16)
        p_scr[hh, i % 2, :, q0:q1] = jnp.exp2(s_d - m).astype(BF16)

    def values(hh, i):
        q0, q1 = i * tq, (i + 1) * tq
        cols = slice(hh * MLA_V, (hh + 1) * MLA_V)
        acc = _dot(p_scr[hh, i % 2, :, 0:q1], vs[hh, 0:q1, :])
        out = acc[:, 0:MLA_V] / acc[:, MLA_V:]
        o_ref[q0:q1, cols] = (out * _silu(mg_ref[q0:q1, cols].astype(F32))).astype(o_ref.dtype)

    for hh in range(MLA_HPS):
        scores(hh, 0)
    for i in range(n_q + 1):
        for hh in range(MLA_HPS):
            if i + 1 < n_q:
                scores(hh, i + 1)
            if i < n_q:
                probs(hh, i)
            if i > 0:
                values(hh, i - 1)


def _mla_call(k_rope, kv_all, proj, w_uq, vecs, cos_m, sin_m, layer, batch, seq):
    m = proj.shape[0]
    lora = MLA_LORA
    hps = MLA_HPS
    assert MLA_HEADS % hps == 0 and COL_MG % (hps * MLA_V) == 0
    return pl.pallas_call(
        functools.partial(_mla_kernel, seq=seq),
        grid=(batch, MLA_HEADS // hps),
        in_specs=[
            pl.BlockSpec((seq, LANE), lambda b, hd: (b, 0)),
            pl.BlockSpec((seq, hps * (MLA_NOPE + MLA_V)), lambda b, hd: (b, hd)),
            pl.BlockSpec((seq, lora), lambda b, hd: (b, COL_MQ // lora)),
            pl.BlockSpec((seq, hps * MLA_V), lambda b, hd: (b, COL_MG // (hps * MLA_V) + hd)),
            pl.BlockSpec((1, lora, MLA_HEADS * MLA_QK), lambda b, hd: (layer, 0, 0),
                         pipeline_mode=pl.Buffered(1)),
            _vec_spec(lora, layer, 2, VEC_Q_NORM),
            pl.BlockSpec((seq, LANE), lambda b, hd: (b, 0)),
            pl.BlockSpec((seq, LANE), lambda b, hd: (b, 0)),
        ],
        out_specs=pl.BlockSpec((seq, hps * MLA_V), lambda b, hd: (b, hd)),
        out_shape=jax.ShapeDtypeStruct((m, MLA_WIDTH), BF16),
        scratch_shapes=[
            pltpu.VMEM((seq, lora), BF16),
            pltpu.VMEM((MLA_HEADS, lora, MLA_QK_PAD), BF16),
            pltpu.VMEM((hps, seq, MLA_QK_PAD), BF16),
            pltpu.VMEM((hps, seq, MLA_QK_PAD), BF16),
            pltpu.VMEM((hps, seq, 2 * MLA_V), BF16),
            pltpu.VMEM((hps, 2, MLA_TQ, seq), F32),
            pltpu.VMEM((hps, 2, MLA_TQ, seq), BF16),
        ],
        compiler_params=_params("arbitrary", "arbitrary"),
        name="mla",
    )(k_rope, kv_all, proj, proj, w_uq, vecs, cos_m, sin_m)


def _merge_kernel(yr_ref, yl_ref, ym_ref, l0_ref, l1_ref, l2_ref, x_ref, res_ref, g_ref,
                  wb_ref, wo_ref, *rest, tiles_per_batch, emit_next):
    b = pl.program_id(0) // tiles_per_batch
    merged = None
    for i, (y_ref, l_ref) in enumerate(((yr_ref, l0_ref), (yl_ref, l1_ref), (ym_ref, l2_ref))):
        z = _dot(y_ref[...], wb_ref[i * BRANCH_WIDTH:(i + 1) * BRANCH_WIDTH, :])
        z = z * jax.nn.sigmoid(l_ref[...].astype(F32))
        merged = z if merged is None else merged + z
    y = _dot(merged.astype(BF16), wo_ref[...])
    post = g_ref[0] * (1.0 + res_ref[0, pl.ds(b, 1), :])
    inv = lax.rsqrt(jnp.mean(y * y, axis=-1, keepdims=True) + NORM_EPS)
    x_new = x_ref[...] + (y * inv) * post
    if emit_next:
        gn_ref, shift_ref, scale_ref, o_ref, h_ref = rest
        pre = gn_ref[0] * (1.0 + scale_ref[0, pl.ds(b, 1), :])
        inv_n = lax.rsqrt(jnp.mean(x_new * x_new, axis=-1, keepdims=True) + NORM_EPS)
        h_ref[...] = ((x_new * inv_n) * pre + shift_ref[0, pl.ds(b, 1), :]).astype(h_ref.dtype)
    else:
        (o_ref,) = rest
    o_ref[...] = x_new


def _merge_call(y_ret, y_lru, y_mla, proj, x2, mod, mod_next, vecs, wb, wo, layer, seq):
    m, d = x2.shape
    tm = MERGE_TM
    mod_rows = mod.shape[1]
    emit_next = mod_next is not None

    def rows(width, c=0):
        return pl.BlockSpec((tm, width), lambda i, c=c: (i, c))

    def resident(shape):
        return pl.BlockSpec(shape, lambda i: (0, 0), pipeline_mode=pl.Buffered(1))

    def mod_part(part):
        return pl.BlockSpec((1, mod_rows, d), lambda i: (0, 0, part))

    merge_col = COL_MERGE // d
    in_specs = [rows(BRANCH_WIDTH), rows(BRANCH_WIDTH), rows(BRANCH_WIDTH),
                rows(d, merge_col), rows(d, merge_col + 1), rows(d, merge_col + 2),
                rows(d), mod_part(2), _vec_spec(d, layer, 1, VEC_NORM_POST),
                resident(wb.shape), resident(wo.shape)]
    args = [y_ret, y_lru, y_mla, proj, proj, proj, x2, mod, vecs, wb, wo]
    out_specs = rows(d)
    out_shape = jax.ShapeDtypeStruct((m, d), F32)
    if emit_next:
        in_specs += [_vec_spec(d, layer + 1, 1, VEC_NORM_PRE), mod_part(0), mod_part(1)]
        args += [vecs, mod_next, mod_next]
        out_specs = (out_specs, rows(d))
        out_shape = (out_shape, jax.ShapeDtypeStruct((m, d), BF16))
    return pl.pallas_call(
        functools.partial(_merge_kernel, tiles_per_batch=seq // tm, emit_next=emit_next),
        grid=(m // tm,),
        in_specs=in_specs,
        out_specs=out_specs,
        out_shape=out_shape,
        compiler_params=_params("arbitrary"),
        name="merge_out",
    )(*args)


def kernel(x, c, positions, ada_w, ada_b, norm_pre, norm_post, w_in, ret_gn, lru_conv_w, lru_conv_b,
           lru_wa, lru_ba, lru_wx, lru_bx, lru_lambda, mla_q_norm, mla_w_uq, mla_kv_norm, mla_w_ukv,
           w_branch, w_out):
    batch, seq, d = x.shape
    depth = w_in.shape[0]
    m = batch * seq
    assert d == D_MODEL and w_in.shape[2] == SRC_WIDTH and batch <= SUBLANE
    assert seq % PROJ_TM == 0 and seq % RET_T == 0 and seq % LRU_T == 0 and seq % MLA_TQ == 0
    assert seq % MERGE_TM == 0 and seq % NORM_TM == 0

    c_pad = jnp.pad(c, ((0, SUBLANE - batch), (0, 0)))
    vecs = _pack_vectors(ada_b, norm_pre, norm_post, ret_gn, lru_conv_b, lru_ba, lru_bx, lru_lambda,
                         mla_q_norm, mla_kv_norm)
    mod = _ada_call(c_pad, ada_w, vecs, 0)
    w_in_t = jnp.swapaxes(w_in, 1, 2)

    x2 = x.reshape(m, d)
    h, cos_r, sin_r, cos_m, sin_m = _prenorm_call(x2, mod, vecs, positions, 0, seq)
    for l in range(depth):
        proj = _inproj_call(h, w_in_t, l)
        y_ret, wb, wo, k_rope, kv_all = _ret_call(proj, cos_r, sin_r, vecs, w_branch, w_out, h, w_in_t,
                                                  cos_m, sin_m, mla_w_ukv, l, batch, seq)
        lru_args = (proj, lru_conv_w, lru_wa, lru_wx, vecs, l, batch, seq)
        if l + 1 < depth:
            y_lru, mod_next = _lru_call(*lru_args, ada=(c_pad, ada_w))
        else:
            y_lru, mod_next = _lru_call(*lru_args), None
        y_mla = _mla_call(k_rope, kv_all, proj, mla_w_uq, vecs, cos_m, sin_m, l, batch, seq)
        out = _merge_call(y_ret, y_lru, y_mla, proj, x2, mod, mod_next, vecs, wb, wo, l, seq)
        mod = mod_next
        if l + 1 < depth:
            x2, h = out
        else:
            x2 = out
    return x2.reshape(batch, seq, d)
```

```python
import functools
import math

import jax
import jax.numpy as jnp
from jax import lax
from jax.experimental import pallas as pl
from jax.experimental.pallas import tpu as pltpu

F32 = jnp.float32
BF16 = jnp.bfloat16

D_MODEL = 2048
CHUNK = 64
ROPE_BASE = 10000.0
NORM_EPS = 1e-6

RET_HEADS = 8
RET_DIM = 128
RET_WIDTH = RET_HEADS * RET_DIM

LRU_WIDTH = 1024
LRU_BLOCKS = 8
LRU_BLOCK_DIM = 128
CONV_WIDTH = 4
LRU_C = 8.0

MLA_HEADS = 8
MLA_NOPE = 128
MLA_ROPE = 64
MLA_V = 128
MLA_LORA = 512
MLA_WIDTH = MLA_HEADS * MLA_V
MLA_QK = MLA_NOPE + MLA_ROPE
MLA_QK_PAD = 256
MLA_SCALE = (MLA_QK ** -0.5) * math.log2(math.e)

LANE = 128
SUBLANE = 8
VMEM_LIMIT = 58 * 1024 * 1024

BRANCH_WIDTH = 1024
assert RET_WIDTH == LRU_WIDTH == MLA_WIDTH == BRANCH_WIDTH

COL_RQ = 0
COL_RK = COL_RQ + RET_WIDTH
COL_RV = COL_RK + RET_WIDTH
COL_RG = COL_RV + RET_WIDTH
COL_LX = COL_RG + RET_WIDTH
COL_LG = COL_LX + LRU_WIDTH
COL_MQ = COL_LG + LRU_WIDTH
COL_MKV = COL_MQ + MLA_LORA
COL_MG = COL_MKV + MLA_LORA
COL_MERGE = COL_MG + MLA_WIDTH
PROJ_WIDTH = COL_MERGE + 3 * D_MODEL
SRC_MKR = COL_MG
SRC_WIDTH = PROJ_WIDTH + MLA_ROPE
PROJ_TN = 1024
PROJ_TM = 2048
ALIGNED_TILES = SRC_MKR // PROJ_TN
CAST_ROWS = 64
assert SRC_MKR % PROJ_TN == 0 and PROJ_WIDTH % PROJ_TN == 0 and COL_MERGE % D_MODEL == 0

NORM_TM = 1024
RET_T = 256
LRU_T = 512
LRU_CLEN = LRU_T // SUBLANE
MLA_ROWS = 256
MLA_TQ = 256
MLA_HPS = 2
MERGE_TM = 256
ADA_TK = 512


def _silu(v):
    return v * jax.nn.sigmoid(v)


def _dot(a, b):
    return jnp.dot(a, b, preferred_element_type=F32)


def _dot_nt(a, b):
    return lax.dot_general(a, b, (((1,), (1,)), ((), ())), preferred_element_type=F32)


def _dot_tn(a, b):
    return lax.dot_general(a, b, (((0,), (0,)), ((), ())), preferred_element_type=F32)


def _rms(x, gain):
    return x * lax.rsqrt(jnp.mean(x * x, axis=-1, keepdims=True) + NORM_EPS) * gain


def _params(*sem):
    return pltpu.CompilerParams(dimension_semantics=sem, vmem_limit_bytes=VMEM_LIMIT)


VEC_ADA_B = 0
VEC_NORM_PRE = 3 * D_MODEL
VEC_NORM_POST = VEC_NORM_PRE + D_MODEL
VEC_RET_GN = VEC_NORM_POST + D_MODEL
VEC_CONV_B = VEC_RET_GN + RET_WIDTH
VEC_BA = VEC_CONV_B + LRU_WIDTH
VEC_BX = VEC_BA + LRU_WIDTH
VEC_LAM = VEC_BX + LRU_WIDTH
VEC_Q_NORM = VEC_LAM + LRU_WIDTH
VEC_KV_NORM = VEC_Q_NORM + MLA_LORA


def _pack_vectors(ada_b, norm_pre, norm_post, ret_gn, conv_b, ba, bx, lam, q_norm, kv_norm):
    packed = jnp.concatenate([ada_b, norm_pre, norm_post, ret_gn, conv_b, ba, bx, lam, q_norm, kv_norm], axis=1)
    return packed[:, None, :]


def _vec_spec(width, layer, ngrid, col):
    assert col % width == 0
    if ngrid == 1:
        return pl.BlockSpec((1, 1, width), lambda i: (layer, 0, col // width))
    return pl.BlockSpec((1, 1, width), lambda i, j: (layer, 0, col // width))


def _ada_accumulate(first, c_ref, w_ref, b_ref, o_ref):
    @pl.when(first)
    def _():
        o_ref[0] = jnp.broadcast_to(b_ref[0], o_ref.shape[1:])

    c_act = _silu(c_ref[0]).astype(BF16)
    o_ref[0] += _dot(c_act, w_ref[0].astype(BF16))


def _ada_kernel(c_ref, w_ref, b_ref, o_ref):
    _ada_accumulate(pl.program_id(0) == 0, c_ref, w_ref, b_ref, o_ref)


def _c_slabs(c_pad, n_slabs):
    rows, d = c_pad.shape
    return c_pad.reshape(rows, n_slabs, d // n_slabs).transpose(1, 0, 2)


def _ada_call(c_pad, ada_w, vecs, layer):
    _, d, n = ada_w.shape
    rows = c_pad.shape[0]
    n_slabs = d // ADA_TK
    return pl.pallas_call(
        _ada_kernel,
        grid=(n_slabs,),
        in_specs=[
            pl.BlockSpec((1, rows, ADA_TK), lambda k: (k, 0, 0)),
            pl.BlockSpec((1, ADA_TK, n), lambda k: (layer, k, 0)),
            _vec_spec(n, layer, 1, VEC_ADA_B),
        ],
        out_specs=pl.BlockSpec((1, rows, n), lambda k: (0, 0, 0)),
        out_shape=jax.ShapeDtypeStruct((1, rows, n), F32),
        compiler_params=_params("arbitrary"),
        name="ada_mod",
    )(_c_slabs(c_pad, n_slabs), ada_w, vecs)


def _modulate(x, gain, shift_ref, scale_ref, b):
    shift = shift_ref[0, pl.ds(b, 1), :]
    scale = scale_ref[0, pl.ds(b, 1), :]
    return _rms(x, gain) * (1.0 + scale) + shift


def _prenorm_kernel(x_ref, g_ref, shift_ref, scale_ref, pos_ref, freq_ref,
                    o_ref, cos_r_ref, sin_r_ref, cos_m_ref, sin_m_ref, *, tiles_per_batch):
    b = pl.program_id(0) // tiles_per_batch
    o_ref[...] = _modulate(x_ref[...], g_ref[0], shift_ref, scale_ref, b).astype(o_ref.dtype)

    ang = freq_ref[...] * pos_ref[0].astype(F32)
    cos_t = jnp.cos(ang)
    sin_t = jnp.sin(ang)
    n_r, n_m = RET_DIM // 2, MLA_ROPE // 2
    c_r, s_r = cos_t[0:n_r], sin_t[0:n_r]
    c_m, s_m = cos_t[n_r:n_r + n_m], sin_t[n_r:n_r + n_m]
    pad = jnp.zeros((LANE - MLA_ROPE, ang.shape[1]), F32)
    cos_r_ref[...] = jnp.concatenate([c_r, c_r], axis=0).T
    sin_r_ref[...] = jnp.concatenate([-s_r, s_r], axis=0).T
    cos_m_ref[...] = jnp.concatenate([c_m, c_m, pad], axis=0).T
    sin_m_ref[...] = jnp.concatenate([-s_m, s_m, pad], axis=0).T


def _prenorm_call(x2, mod, vecs, positions, layer, seq):
    m, d = x2.shape
    rows = mod.shape[1]
    n_tiles = m // NORM_TM

    def inv_freq(dim):
        return ROPE_BASE ** (-jnp.arange(0, dim, 2, dtype=F32) / dim)

    freqs = jnp.concatenate([inv_freq(RET_DIM), inv_freq(MLA_ROPE)])[:, None]
    table = pl.BlockSpec((NORM_TM, LANE), lambda i: (i, 0))
    return pl.pallas_call(
        functools.partial(_prenorm_kernel, tiles_per_batch=seq // NORM_TM),
        grid=(n_tiles,),
        in_specs=[
            pl.BlockSpec((NORM_TM, d), lambda i: (i, 0)),
            _vec_spec(d, layer, 1, VEC_NORM_PRE),
            pl.BlockSpec((1, rows, d), lambda i: (0, 0, 0)),
            pl.BlockSpec((1, rows, d), lambda i: (0, 0, 1)),
            pl.BlockSpec((1, 1, NORM_TM), lambda i: (i, 0, 0)),
            pl.BlockSpec(freqs.shape, lambda i: (0, 0)),
        ],
        out_specs=(pl.BlockSpec((NORM_TM, d), lambda i: (i, 0)), table, table, table, table),
        out_shape=(jax.ShapeDtypeStruct((m, d), BF16),) + (jax.ShapeDtypeStruct((m, LANE), F32),) * 4,
        compiler_params=_params("arbitrary"),
        name="pre_norm",
    )(x2, vecs, mod, mod, positions.reshape(n_tiles, 1, NORM_TM), freqs)


def _inproj_kernel(h_ref, w_ref, wx_ref, o_ref, wb_ref):
    j = pl.program_id(0)
    first_row_tile = pl.program_id(1) == 0
    n_chunks = PROJ_TN // CAST_ROWS

    @pl.when(first_row_tile & (j < ALIGNED_TILES))
    def _():
        def body(i, carry):
            r0 = pl.multiple_of(i * CAST_ROWS, CAST_ROWS)
            wb_ref[pl.ds(r0, CAST_ROWS), :] = w_ref[0, pl.ds(r0, CAST_ROWS), :].astype(BF16)
            return carry

        lax.fori_loop(0, n_chunks, body, 0)

    @pl.when(first_row_tile & (j >= ALIGNED_TILES))
    def _():
        def body(i, carry):
            r0 = pl.multiple_of(i * CAST_ROWS, CAST_ROWS)
            wb_ref[pl.ds(r0, CAST_ROWS), :] = w_ref[0, pl.ds(r0 + MLA_ROPE, CAST_ROWS), :].astype(BF16)
            return carry

        lax.fori_loop(0, n_chunks - 1, body, 0)
        wb_ref[PROJ_TN - MLA_ROPE:, :] = wx_ref[0].astype(BF16)

    o_ref[...] = _dot_nt(h_ref[...], wb_ref[...]).astype(o_ref.dtype)


def _inproj_call(h, w_in_t, layer):
    m, d = h.shape
    per_tile = PROJ_TN // MLA_ROPE
    last_block = SRC_WIDTH // MLA_ROPE - 1
    return pl.pallas_call(
        _inproj_kernel,
        grid=(PROJ_WIDTH // PROJ_TN, m // PROJ_TM),
        in_specs=[
            pl.BlockSpec((PROJ_TM, d), lambda j, i: (i, 0)),
            pl.BlockSpec((1, PROJ_TN, d), lambda j, i: (layer, j, 0)),
            pl.BlockSpec((1, MLA_ROPE, d),
                         lambda j, i: (layer, jnp.minimum((j + 1) * per_tile, last_block), 0)),
        ],
        out_specs=pl.BlockSpec((PROJ_TM, PROJ_TN), lambda j, i: (i, j)),
        out_shape=jax.ShapeDtypeStruct((m, PROJ_WIDTH), BF16),
        scratch_shapes=[pltpu.VMEM((PROJ_TN, d), BF16)],
        compiler_params=_params("arbitrary", "arbitrary"),
        name="in_proj",
    )(h, w_in_t, w_in_t)


_LOG_GAMMA = tuple(math.log1p(-(2.0 ** (-5.0 - h))) for h in range(RET_HEADS))


def _ret_kernel(q_ref, k_ref, v_ref, g_ref, cos_ref, sin_ref, gn_ref, wb_ref, wo_ref,
                h_ref, wkr_ref, cos_m_ref, sin_m_ref, mkv_ref, kvn_ref, wkv_ref, mq_ref, qn_ref, wq_ref,
                o_ref, wb_out, wo_out, kr_out, kv_out, q_out,
                state_ref, dmat_ref, dq_ref, dk_ref, wkr_s, wkv_s, wq_s):
    t_blk = RET_T

    @pl.when((pl.program_id(0) == 0) & (pl.program_id(1) == 0))
    def _():
        wkr = jnp.concatenate([wkr_ref[0], jnp.zeros((LANE - MLA_ROPE, D_MODEL), F32)], axis=0)
        wkr_s[...] = wkr.T.astype(BF16)
        wkv_s[...] = wkv_ref[0].astype(BF16)
        zeros = jnp.zeros((MLA_LORA, MLA_QK_PAD - MLA_QK), F32)
        wq_all = wq_ref[0]
        for hh in range(MLA_HEADS):
            w_head = wq_all[:, hh * MLA_QK:(hh + 1) * MLA_QK]
            wq_s[:, hh * MLA_QK_PAD:(hh + 1) * MLA_QK_PAD] = jnp.concatenate([w_head, zeros], axis=1).astype(BF16)
        ri = lax.broadcasted_iota(jnp.int32, (t_blk, t_blk), 0)
        ci = lax.broadcasted_iota(jnp.int32, (t_blk, t_blk), 1)
        dist = jnp.abs(ri - ci).astype(F32)
        visible = (ci // CHUNK) <= (ri // CHUNK)
        row = lax.broadcasted_iota(jnp.int32, (t_blk, RET_DIM), 0).astype(F32)
        for h in range(RET_HEADS):
            lg = _LOG_GAMMA[h]
            dmat_ref[h] = jnp.where(visible, jnp.exp(lg * dist), 0.0)
            dq_ref[h] = jnp.exp(lg * (row + 1.0))
            dk_ref[h] = jnp.exp(lg * ((t_blk - 1.0) - row))

    @pl.when(pl.program_id(1) == 0)
    def _():
        state_ref[...] = jnp.zeros_like(state_ref)

    wb_out[...] = wb_ref[0].astype(BF16)
    wo_out[...] = wo_ref[0].astype(BF16)
    kr_out[...] = _rope64(_dot(h_ref[...], wkr_s[...]), cos_m_ref[...], sin_m_ref[...]).astype(BF16)
    kv_lat = _rms(mkv_ref[...].astype(F32), kvn_ref[0]).astype(BF16)
    kv_out[...] = _dot(kv_lat, wkv_s[...]).astype(BF16)
    q_lat = _rms(mq_ref[...].astype(F32), qn_ref[0]).astype(BF16)
    q_all = _dot(q_lat, wq_s[...])
    for hh in range(MLA_HEADS):
        c0 = hh * MLA_QK_PAD
        q_out[:, c0:c0 + MLA_NOPE] = (q_all[:, c0:c0 + MLA_NOPE] * MLA_SCALE).astype(BF16)
        q_rot = _rope64(q_all[:, c0 + MLA_NOPE:c0 + MLA_QK_PAD], cos_m_ref[...], sin_m_ref[...])
        q_out[:, c0 + MLA_NOPE:c0 + MLA_QK_PAD] = (q_rot * MLA_SCALE).astype(BF16)

    cos = cos_ref[...]
    sin = sin_ref[...]
    for h in range(RET_HEADS):
        sl = slice(h * RET_DIM, (h + 1) * RET_DIM)
        q = q_ref[:, sl].astype(F32)
        k = k_ref[:, sl].astype(F32)
        q = (q * cos + pltpu.roll(q, RET_DIM // 2, 1) * sin) * (RET_DIM ** -0.5)
        k = k * cos + pltpu.roll(k, RET_DIM // 2, 1) * sin
        v = v_ref[:, sl]
        scores = _dot_nt(q.astype(BF16), k.astype(BF16)) * dmat_ref[h]
        o = _dot(scores.astype(BF16), v)
        state = state_ref[h]
        o = o + _dot((q * dq_ref[h]).astype(BF16), state.astype(BF16))
        k_dec = (k * dk_ref[h]).astype(BF16)
        state_ref[h] = state * math.exp(_LOG_GAMMA[h] * t_blk) + _dot_tn(k_dec, v)
        mean = jnp.mean(o, axis=-1, keepdims=True)
        cen = o - mean
        var = jnp.mean(cen * cen, axis=-1, keepdims=True)
        normed = cen * lax.rsqrt(var + NORM_EPS) * gn_ref[0, :, sl]
        o_ref[:, sl] = (normed * _silu(g_ref[:, sl].astype(F32))).astype(o_ref.dtype)


def _ret_call(proj, cos_r, sin_r, vecs, w_branch, w_out, h, w_in_t, cos_m, sin_m, w_ukv, w_uq,
              layer, batch, seq):
    m = proj.shape[0]
    lora, kv_width = w_ukv.shape[1:]
    q_width = MLA_HEADS * MLA_QK_PAD
    rows = RET_T
    nt = seq // rows
    w = RET_WIDTH
    steps = batch * nt
    wb_rows, wo_rows = w_branch.shape[1] // steps, w_out.shape[1] // steps
    bf16_rows = 2 * SUBLANE
    assert wb_rows * steps == w_branch.shape[1] and wb_rows % bf16_rows == 0
    assert wo_rows * steps == w_out.shape[1] and wo_rows % bf16_rows == 0
    d = w_branch.shape[2]

    def col(c):
        return pl.BlockSpec((rows, w), lambda b, t, c=c: (b * nt + t, c // w))

    tab = pl.BlockSpec((rows, RET_DIM), lambda b, t: (b * nt + t, 0))
    return pl.pallas_call(
        _ret_kernel,
        grid=(batch, nt),
        in_specs=[col(COL_RQ), col(COL_RK), col(COL_RV), col(COL_RG), tab, tab,
                  _vec_spec(w, layer, 2, VEC_RET_GN),
                  pl.BlockSpec((1, wb_rows, d), lambda b, t: (layer, b * nt + t, 0)),
                  pl.BlockSpec((1, wo_rows, d), lambda b, t: (layer, b * nt + t, 0)),
                  pl.BlockSpec((rows, d), lambda b, t: (b * nt + t, 0)),
                  pl.BlockSpec((1, MLA_ROPE, d), lambda b, t: (layer, SRC_MKR // MLA_ROPE, 0)),
                  tab, tab,
                  pl.BlockSpec((rows, lora), lambda b, t: (b * nt + t, COL_MKV // lora)),
                  _vec_spec(lora, layer, 2, VEC_KV_NORM),
                  pl.BlockSpec((1, lora, kv_width), lambda b, t: (layer, 0, 0)),
                  pl.BlockSpec((rows, lora), lambda b, t: (b * nt + t, COL_MQ // lora)),
                  _vec_spec(lora, layer, 2, VEC_Q_NORM),
                  pl.BlockSpec((1, lora, MLA_HEADS * MLA_QK), lambda b, t: (layer, 0, 0))],
        out_specs=(pl.BlockSpec((rows, w), lambda b, t: (b * nt + t, 0)),
                   pl.BlockSpec((wb_rows, d), lambda b, t: (b * nt + t, 0)),
                   pl.BlockSpec((wo_rows, d), lambda b, t: (b * nt + t, 0)),
                   tab,
                   pl.BlockSpec((rows, kv_width), lambda b, t: (b * nt + t, 0)),
                   pl.BlockSpec((rows, q_width), lambda b, t: (b * nt + t, 0))),
        out_shape=(jax.ShapeDtypeStruct((m, w), BF16),
                   jax.ShapeDtypeStruct(w_branch.shape[1:], BF16),
                   jax.ShapeDtypeStruct(w_out.shape[1:], BF16),
                   jax.ShapeDtypeStruct((m, LANE), BF16),
                   jax.ShapeDtypeStruct((m, kv_width), BF16),
                   jax.ShapeDtypeStruct((m, q_width), BF16)),
        scratch_shapes=[
            pltpu.VMEM((RET_HEADS, RET_DIM, RET_DIM), F32),
            pltpu.VMEM((RET_HEADS, RET_T, RET_T), F32),
            pltpu.VMEM((RET_HEADS, RET_T, RET_DIM), F32),
            pltpu.VMEM((RET_HEADS, RET_T, RET_DIM), F32),
            pltpu.VMEM((D_MODEL, LANE), BF16),
            pltpu.VMEM((lora, kv_width), BF16),
            pltpu.VMEM((lora, q_width), BF16),
        ],
        compiler_params=_params("arbitrary", "arbitrary"),
        name="retention",
    )(proj, proj, proj, proj, cos_r, sin_r, vecs, w_branch, w_out, h, w_in_t, cos_m, sin_m,
      proj, vecs, w_ukv, proj, vecs, w_uq)


def _sublane_scan(a, b, row):
    for s in (1, 2, 4):
        keep = row >= s
        a_prev = jnp.where(keep, pltpu.roll(a, s, 0), 1.0)
        b_prev = jnp.where(keep, pltpu.roll(b, s, 0), 0.0)
        b = a * b_prev + b
        a = a * a_prev
    return a, b


def _lru_kernel(x_ref, g_ref, cw_ref, cb_ref, wa_ref, ba_ref, wx_ref, bx_ref, lam_ref, *rest, with_ada):
    if with_ada:
        c_ref, aw_ref, ab_ref, o_ref, mod_ref, perm, perm_t, halo, hcar, a_s, b_s = rest
        _ada_accumulate((pl.program_id(0) == 0) & (pl.program_id(1) == 0), c_ref, aw_ref, ab_ref, mod_ref)
    else:
        o_ref, perm, perm_t, halo, hcar, a_s, b_s = rest
    t_blk = LRU_T
    clen = LRU_CLEN
    taps = CONV_WIDTH - 1
    row = lax.broadcasted_iota(jnp.int32, (SUBLANE, LRU_WIDTH), 0)

    @pl.when((pl.program_id(0) == 0) & (pl.program_id(1) == 0))
    def _():
        r = lax.broadcasted_iota(jnp.int32, (t_blk, t_blk), 0)
        c = lax.broadcasted_iota(jnp.int32, (t_blk, t_blk), 1)
        perm[...] = jnp.where(c == (r % SUBLANE) * clen + r // SUBLANE, 1.0, 0.0).astype(BF16)
        perm_t[...] = jnp.where(c == (r % clen) * SUBLANE + r // clen, 1.0, 0.0).astype(BF16)

    @pl.when(pl.program_id(1) == 0)
    def _():
        halo[...] = jnp.zeros_like(halo)
        hcar[...] = jnp.zeros_like(hcar)

    xp = _dot(perm[...], x_ref[...])
    pieces = []
    for k in range(taps):
        cur = xp[t_blk - (taps - k) * SUBLANE:t_blk - (taps - k - 1) * SUBLANE, :]
        prev = halo[k * SUBLANE:(k + 1) * SUBLANE, :]
        pieces.append(jnp.where(row == 0, pltpu.roll(prev, 1, 0), pltpu.roll(cur, 1, 0)))
    halo[...] = xp[t_blk - taps * SUBLANE:, :]
    xext = jnp.concatenate(pieces + [xp], axis=0)
    xc = cb_ref[0]
    for w in range(CONV_WIDTH):
        xc = xc + cw_ref[0, w:w + 1, :] * xext[w * SUBLANE:w * SUBLANE + t_blk, :]

    neg_lam = -lam_ref[0]
    softplus = jnp.maximum(neg_lam, 0.0) + jnp.log1p(jnp.exp(-jnp.abs(neg_lam)))
    neg_half_rate = (-0.5 * LRU_C) * softplus
    for n in range(LRU_BLOCKS):
        sl = slice(n * LRU_BLOCK_DIM, (n + 1) * LRU_BLOCK_DIM)
        xn = xc[:, sl]
        xn_b = xn.astype(BF16)
        t_r = jnp.tanh(0.5 * (_dot(xn_b, wa_ref[0, n].astype(BF16)) + ba_ref[0, :, sl]))
        t_i = jnp.tanh(0.5 * (_dot(xn_b, wx_ref[0, n].astype(BF16)) + bx_ref[0, :, sl]))
        half_rate = neg_half_rate[:, sl]
        log_a = half_rate * t_r + half_rate
        a = jnp.exp(log_a)
        a_s[:, sl] = a
        var = jnp.tanh(-log_a) * (a * a + 1.0)
        std = jnp.where(var > 0.0, var * lax.rsqrt(var), 0.0)
        half_x = 0.5 * xn
        b_s[:, sl] = std * (half_x * t_i + half_x)

    def body(j, carry):
        h, prod = carry
        r0 = j * SUBLANE
        a = a_s[pl.ds(r0, SUBLANE), :]
        h = a * h + b_s[pl.ds(r0, SUBLANE), :]
        prod = a * prod
        b_s[pl.ds(r0, SUBLANE), :] = h
        a_s[pl.ds(r0, SUBLANE), :] = prod
        return h, prod

    zeros = jnp.zeros((SUBLANE, LRU_WIDTH), F32)
    carry = (zeros, zeros + 1.0)
    for j in range(clen):
        carry = body(j, carry)
    h_end, a_end = carry

    a_inc, h_inc = _sublane_scan(a_end, h_end, row)
    h_prev_tile = hcar[...]
    h_chunk_end = a_inc * h_prev_tile + h_inc
    h_chunk_start = jnp.where(row == 0, h_prev_tile, pltpu.roll(h_chunk_end, 1, 0))
    hcar[...] = jnp.broadcast_to(h_chunk_end[SUBLANE - 1:SUBLANE, :], (SUBLANE, LRU_WIDTH))

    h_true = b_s[...] + a_s[...] * jnp.concatenate([h_chunk_start] * clen, axis=0)
    gp = _dot(perm[...], g_ref[...])
    out_p = (h_true * _silu(gp)).astype(BF16)
    o_ref[...] = _dot(perm_t[...], out_p).astype(o_ref.dtype)


def _lru_call(proj, cw, wa, wx, vecs, layer, batch, seq, ada=None):
    m = proj.shape[0]
    nt = seq // LRU_T
    w = LRU_WIDTH

    def col(c):
        return pl.BlockSpec((LRU_T, w), lambda b, t, c=c: (b * nt + t, c // w))

    def vec(offset):
        return _vec_spec(w, layer, 2, offset)

    def blk():
        return pl.BlockSpec((1, LRU_BLOCKS, LRU_BLOCK_DIM, LRU_BLOCK_DIM), lambda b, t: (layer, 0, 0, 0))

    in_specs = [col(COL_LX), col(COL_LG),
                pl.BlockSpec((1, CONV_WIDTH, w), lambda b, t: (layer, 0, 0)), vec(VEC_CONV_B),
                blk(), vec(VEC_BA), blk(), vec(VEC_BX), vec(VEC_LAM)]
    args = [proj, proj, cw, vecs, wa, vecs, wx, vecs, vecs]
    out_specs = pl.BlockSpec((LRU_T, w), lambda b, t: (b * nt + t, 0))
    out_shape = jax.ShapeDtypeStruct((m, w), BF16)
    if ada is not None:
        c_pad, ada_w = ada
        rows = c_pad.shape[0]
        _, d, n = ada_w.shape
        steps = batch * nt
        slab = d // steps
        assert slab * steps == d and slab % LANE == 0
        in_specs += [pl.BlockSpec((1, rows, slab), lambda b, t: (b * nt + t, 0, 0)),
                     pl.BlockSpec((1, slab, n), lambda b, t: (layer + 1, b * nt + t, 0)),
                     _vec_spec(n, layer + 1, 2, VEC_ADA_B)]
        args += [_c_slabs(c_pad, steps), ada_w, vecs]
        out_specs = (out_specs, pl.BlockSpec((1, rows, n), lambda b, t: (0, 0, 0)))
        out_shape = (out_shape, jax.ShapeDtypeStruct((1, rows, n), F32))
    return pl.pallas_call(
        functools.partial(_lru_kernel, with_ada=ada is not None),
        grid=(batch, nt),
        in_specs=in_specs,
        out_specs=out_specs,
        out_shape=out_shape,
        scratch_shapes=[
            pltpu.VMEM((LRU_T, LRU_T), BF16),
            pltpu.VMEM((LRU_T, LRU_T), BF16),
            pltpu.VMEM(((CONV_WIDTH - 1) * SUBLANE, w), F32),
            pltpu.VMEM((SUBLANE, w), F32),
            pltpu.VMEM((LRU_T, w), F32),
            pltpu.VMEM((LRU_T, w), F32),
        ],
        compiler_params=_params("arbitrary", "arbitrary"),
        name="rg_lru",
    )(*args)


def _rope64(x, cos_t, sin_t):
    swapped = pltpu.roll(x, MLA_ROPE // 2, 1) + pltpu.roll(x, LANE - MLA_ROPE // 2, 1)
    return x * cos_t + swapped * sin_t


def _mla_kernel(krope, kv_ref, q_ref, mg_ref, o_ref, ks, vs, s_scr, p_scr, *, seq):
    pair = pl.program_id(1)

    @pl.when(pair == 0)
    def _():
        for hh in range(MLA_HPS):
            vs[hh, :, MLA_V:] = jnp.ones((seq, MLA_V), BF16)

    tq = MLA_TQ
    ri = lax.broadcasted_iota(jnp.int32, (tq, tq), 0)
    ci = lax.broadcasted_iota(jnp.int32, (tq, tq), 1)
    visible = (ci // CHUNK) <= (ri // CHUNK)
    n_q = seq // tq
    kv_w = MLA_NOPE + MLA_V

    for hh in range(MLA_HPS):
        ks[hh, :, 0:MLA_NOPE] = kv_ref[:, hh * kv_w:hh * kv_w + MLA_NOPE]
        ks[hh, :, MLA_NOPE:] = krope[...]
        vs[hh, :, 0:MLA_V] = kv_ref[:, hh * kv_w + MLA_NOPE:(hh + 1) * kv_w]

    def scores(hh, i):
        q0, q1 = i * tq, (i + 1) * tq
        q_blk = q_ref[q0:q1, hh * MLA_QK_PAD:(hh + 1) * MLA_QK_PAD]
        s_scr[hh, i % 2, :, 0:q1] = _dot_nt(q_blk, ks[hh, 0:q1, :])

    def probs(hh, i):
        q0, q1 = i * tq, (i + 1) * tq
        s_d = jnp.where(visible, s_scr[hh, i % 2, :, q0:q1], -1e30)
        m = jnp.max(s_d, axis=-1, keepdims=True)
        if i > 0:
            m = jnp.maximum(m, jnp.max(s_scr[hh, i % 2, :, 0:q0], axis=-1, keepdims=True))
            p_scr[hh, i % 2, :, 0:q0] = jnp.exp2(s_scr[hh, i % 2, :, 0:q0] - m).astype(BF16)
        p_scr[hh, i % 2, :, q0:q1] = jnp.exp2(s_d - m).astype(BF16)

    def values(hh, i):
        q0, q1 = i * tq, (i + 1) * tq
        cols = slice(hh * MLA_V, (hh + 1) * MLA_V)
        acc = _dot(p_scr[hh, i % 2, :, 0:q1], vs[hh, 0:q1, :])
        out = acc[:, 0:MLA_V] / acc[:, MLA_V:]
        o_ref[q0:q1, cols] = (out * _silu(mg_ref[q0:q1, cols].astype(F32))).astype(o_ref.dtype)

    for hh in range(MLA_HPS):
        scores(hh, 0)
    for i in range(n_q + 1):
        for hh in range(MLA_HPS):
            if i + 1 < n_q:
                scores(hh, i + 1)
            if i < n_q:
                probs(hh, i)
            if i > 0:
                values(hh, i - 1)


def _mla_call(k_rope, kv_all, q_all, proj, batch, seq):
    m = proj.shape[0]
    hps = MLA_HPS
    assert MLA_HEADS % hps == 0 and COL_MG % (hps * MLA_V) == 0
    return pl.pallas_call(
        functools.partial(_mla_kernel, seq=seq),
        grid=(batch, MLA_HEADS // hps),
        in_specs=[
            pl.BlockSpec((seq, LANE), lambda b, hd: (b, 0)),
            pl.BlockSpec((seq, hps * (MLA_NOPE + MLA_V)), lambda b, hd: (b, hd)),
            pl.BlockSpec((seq, hps * MLA_QK_PAD), lambda b, hd: (b, hd)),
            pl.BlockSpec((seq, hps * MLA_V), lambda b, hd: (b, COL_MG // (hps * MLA_V) + hd)),
        ],
        out_specs=pl.BlockSpec((seq, hps * MLA_V), lambda b, hd: (b, hd)),
        out_shape=jax.ShapeDtypeStruct((m, MLA_WIDTH), BF16),
        scratch_shapes=[
            pltpu.VMEM((hps, seq, MLA_QK_PAD), BF16),
            pltpu.VMEM((hps, seq, 2 * MLA_V), BF16),
            pltpu.VMEM((hps, 2, MLA_TQ, seq), F32),
            pltpu.VMEM((hps, 2, MLA_TQ, seq), BF16),
        ],
        compiler_params=_params("arbitrary", "arbitrary"),
        name="mla",
    )(k_rope, kv_all, q_all, proj)


def _merge_kernel(yr_ref, yl_ref, ym_ref, l0_ref, l1_ref, l2_ref, x_ref, res_ref, g_ref,
                  wb_ref, wo_ref, *rest, tiles_per_batch, emit_next):
    b = pl.program_id(0) // tiles_per_batch
    merged = None
    for i, (y_ref, l_ref) in enumerate(((yr_ref, l0_ref), (yl_ref, l1_ref), (ym_ref, l2_ref))):
        z = _dot(y_ref[...], wb_ref[i * BRANCH_WIDTH:(i + 1) * BRANCH_WIDTH, :])
        z = z * jax.nn.sigmoid(l_ref[...].astype(F32))
        merged = z if merged is None else merged + z
    y = _dot(merged.astype(BF16), wo_ref[...])
    post = g_ref[0] * (1.0 + res_ref[0, pl.ds(b, 1), :])
    inv = lax.rsqrt(jnp.mean(y * y, axis=-1, keepdims=True) + NORM_EPS)
    x_new = x_ref[...] + (y * inv) * post
    if emit_next:
        gn_ref, shift_ref, scale_ref, o_ref, h_ref = rest
        pre = gn_ref[0] * (1.0 + scale_ref[0, pl.ds(b, 1), :])
        inv_n = lax.rsqrt(jnp.mean(x_new * x_new, axis=-1, keepdims=True) + NORM_EPS)
        h_ref[...] = ((x_new * inv_n) * pre + shift_ref[0, pl.ds(b, 1), :]).astype(h_ref.dtype)
    else:
        (o_ref,) = rest
    o_ref[...] = x_new


def _merge_call(y_ret, y_lru, y_mla, proj, x2, mod, mod_next, vecs, wb, wo, layer, seq):
    m, d = x2.shape
    tm = MERGE_TM
    mod_rows = mod.shape[1]
    emit_next = mod_next is not None

    def rows(width, c=0):
        return pl.BlockSpec((tm, width), lambda i, c=c: (i, c))

    def resident(shape):
        return pl.BlockSpec(shape, lambda i: (0, 0), pipeline_mode=pl.Buffered(1))

    def mod_part(part):
        return pl.BlockSpec((1, mod_rows, d), lambda i: (0, 0, part))

    merge_col = COL_MERGE // d
    in_specs = [rows(BRANCH_WIDTH), rows(BRANCH_WIDTH), rows(BRANCH_WIDTH),
                rows(d, merge_col), rows(d, merge_col + 1), rows(d, merge_col + 2),
                rows(d), mod_part(2), _vec_spec(d, layer, 1, VEC_NORM_POST),
                resident(wb.shape), resident(wo.shape)]
    args = [y_ret, y_lru, y_mla, proj, proj, proj, x2, mod, vecs, wb, wo]
    out_specs = rows(d)
    out_shape = jax.ShapeDtypeStruct((m, d), F32)
    if emit_next:
        in_specs += [_vec_spec(d, layer + 1, 1, VEC_NORM_PRE), mod_part(0), mod_part(1)]
        args += [vecs, mod_next, mod_next]
        out_specs = (out_specs, rows(d))
        out_shape = (out_shape, jax.ShapeDtypeStruct((m, d), BF16))
    return pl.pallas_call(
        functools.partial(_merge_kernel, tiles_per_batch=seq // tm, emit_next=emit_next),
        grid=(m // tm,),
        in_specs=in_specs,
        out_specs=out_specs,
        out_shape=out_shape,
        compiler_params=_params("arbitrary"),
        name="merge_out",
    )(*args)


def kernel(x, c, positions, ada_w, ada_b, norm_pre, norm_post, w_in, ret_gn, lru_conv_w, lru_conv_b,
           lru_wa, lru_ba, lru_wx, lru_bx, lru_lambda, mla_q_norm, mla_w_uq, mla_kv_norm, mla_w_ukv,
           w_branch, w_out):
    batch, seq, d = x.shape
    depth = w_in.shape[0]
    m = batch * seq
    assert d == D_MODEL and w_in.shape[2] == SRC_WIDTH and batch <= SUBLANE
    assert seq % PROJ_TM == 0 and seq % RET_T == 0 and seq % LRU_T == 0 and seq % MLA_TQ == 0
    assert seq % MERGE_TM == 0 and seq % NORM_TM == 0

    c_pad = jnp.pad(c, ((0, SUBLANE - batch), (0, 0)))
    vecs = _pack_vectors(ada_b, norm_pre, norm_post, ret_gn, lru_conv_b, lru_ba, lru_bx, lru_lambda,
                         mla_q_norm, mla_kv_norm)
    mod = _ada_call(c_pad, ada_w, vecs, 0)
    w_in_t = jnp.swapaxes(w_in, 1, 2)

    x2 = x.reshape(m, d)
    h, cos_r, sin_r, cos_m, sin_m = _prenorm_call(x2, mod, vecs, positions, 0, seq)
    for l in range(depth):
        proj = _inproj_call(h, w_in_t, l)
        y_ret, wb, wo, k_rope, kv_all, q_all = _ret_call(proj, cos_r, sin_r, vecs, w_branch, w_out, h, w_in_t,
                                                         cos_m, sin_m, mla_w_ukv, mla_w_uq, l, batch, seq)
        lru_args = (proj, lru_conv_w, lru_wa, lru_wx, vecs, l, batch, seq)
        if l + 1 < depth:
            y_lru, mod_next = _lru_call(*lru_args, ada=(c_pad, ada_w))
        else:
            y_lru, mod_next = _lru_call(*lru_args), None
        y_mla = _mla_call(k_rope, kv_all, q_all, proj, batch, seq)
        out = _merge_call(y_ret, y_lru, y_mla, proj, x2, mod, mod_next, vecs, wb, wo, l, seq)
        mod = mod_next
        if l + 1 < depth:
            x2, h = out
        else:
            x2 = out
    return x2.reshape(batch, seq, d)
```

```python
import functools
import math

import jax
import jax.numpy as jnp
from jax import lax
from jax.experimental import pallas as pl
from jax.experimental.pallas import tpu as pltpu

F32 = jnp.float32
BF16 = jnp.bfloat16

D_MODEL = 2048
CHUNK = 64
ROPE_BASE = 10000.0
NORM_EPS = 1e-6

RET_HEADS = 8
RET_DIM = 128
RET_WIDTH = RET_HEADS * RET_DIM

LRU_WIDTH = 1024
LRU_BLOCKS = 8
LRU_BLOCK_DIM = 128
CONV_WIDTH = 4
LRU_C = 8.0

MLA_HEADS = 8
MLA_NOPE = 128
MLA_ROPE = 64
MLA_V = 128
MLA_LORA = 512
MLA_WIDTH = MLA_HEADS * MLA_V
MLA_QK = MLA_NOPE + MLA_ROPE
MLA_QK_PAD = 256
MLA_SCALE = (MLA_QK ** -0.5) * math.log2(math.e)

LANE = 128
SUBLANE = 8
VMEM_LIMIT = 58 * 1024 * 1024

BRANCH_WIDTH = 1024
assert RET_WIDTH == LRU_WIDTH == MLA_WIDTH == BRANCH_WIDTH

COL_RQ = 0
COL_RK = COL_RQ + RET_WIDTH
COL_RV = COL_RK + RET_WIDTH
COL_RG = COL_RV + RET_WIDTH
COL_LX = COL_RG + RET_WIDTH
COL_LG = COL_LX + LRU_WIDTH
COL_MQ = COL_LG + LRU_WIDTH
COL_MKV = COL_MQ + MLA_LORA
COL_MG = COL_MKV + MLA_LORA
COL_MERGE = COL_MG + MLA_WIDTH
PROJ_WIDTH = COL_MERGE + 3 * D_MODEL
SRC_MKR = COL_MG
SRC_WIDTH = PROJ_WIDTH + MLA_ROPE
PROJ_TN = 1024
PROJ_TM = 2048
ALIGNED_TILES = SRC_MKR // PROJ_TN
CAST_ROWS = 64
assert SRC_MKR % PROJ_TN == 0 and PROJ_WIDTH % PROJ_TN == 0 and COL_MERGE % D_MODEL == 0

NORM_TM = 1024
RET_T = 256
LRU_T = 512
LRU_CLEN = LRU_T // SUBLANE
MLA_ROWS = 256
MLA_TQ = 256
MLA_HPS = 2
MERGE_TM = 256
ADA_TK = 512


def _silu(v):
    return v * jax.nn.sigmoid(v)


def _dot(a, b):
    return jnp.dot(a, b, preferred_element_type=F32)


def _dot_nt(a, b):
    return lax.dot_general(a, b, (((1,), (1,)), ((), ())), preferred_element_type=F32)


def _dot_tn(a, b):
    return lax.dot_general(a, b, (((0,), (0,)), ((), ())), preferred_element_type=F32)


def _rms(x, gain):
    return x * lax.rsqrt(jnp.mean(x * x, axis=-1, keepdims=True) + NORM_EPS) * gain


def _params(*sem):
    return pltpu.CompilerParams(dimension_semantics=sem, vmem_limit_bytes=VMEM_LIMIT)


VEC_ADA_B = 0
VEC_NORM_PRE = 3 * D_MODEL
VEC_NORM_POST = VEC_NORM_PRE + D_MODEL
VEC_RET_GN = VEC_NORM_POST + D_MODEL
VEC_CONV_B = VEC_RET_GN + RET_WIDTH
VEC_BA = VEC_CONV_B + LRU_WIDTH
VEC_BX = VEC_BA + LRU_WIDTH
VEC_LAM = VEC_BX + LRU_WIDTH
VEC_Q_NORM = VEC_LAM + LRU_WIDTH
VEC_KV_NORM = VEC_Q_NORM + MLA_LORA


def _pack_vectors(ada_b, norm_pre, norm_post, ret_gn, conv_b, ba, bx, lam, q_norm, kv_norm):
    packed = jnp.concatenate([ada_b, norm_pre, norm_post, ret_gn, conv_b, ba, bx, lam, q_norm, kv_norm], axis=1)
    return packed[:, None, :]


def _vec_spec(width, layer, ngrid, col):
    assert col % width == 0
    if ngrid == 1:
        return pl.BlockSpec((1, 1, width), lambda i: (layer, 0, col // width))
    return pl.BlockSpec((1, 1, width), lambda i, j: (layer, 0, col // width))


def _ada_accumulate(first, c_ref, w_ref, b_ref, o_ref):
    @pl.when(first)
    def _():
        o_ref[0] = jnp.broadcast_to(b_ref[0], o_ref.shape[1:])

    c_act = _silu(c_ref[0]).astype(BF16)
    o_ref[0] += _dot(c_act, w_ref[0].astype(BF16))


def _ada_kernel(c_ref, w_ref, b_ref, o_ref):
    _ada_accumulate(pl.program_id(0) == 0, c_ref, w_ref, b_ref, o_ref)


def _c_slabs(c_pad, n_slabs):
    rows, d = c_pad.shape
    return c_pad.reshape(rows, n_slabs, d // n_slabs).transpose(1, 0, 2)


def _ada_call(c_pad, ada_w, vecs, layer):
    _, d, n = ada_w.shape
    rows = c_pad.shape[0]
    n_slabs = d // ADA_TK
    return pl.pallas_call(
        _ada_kernel,
        grid=(n_slabs,),
        in_specs=[
            pl.BlockSpec((1, rows, ADA_TK), lambda k: (k, 0, 0)),
            pl.BlockSpec((1, ADA_TK, n), lambda k: (layer, k, 0)),
            _vec_spec(n, layer, 1, VEC_ADA_B),
        ],
        out_specs=pl.BlockSpec((1, rows, n), lambda k: (0, 0, 0)),
        out_shape=jax.ShapeDtypeStruct((1, rows, n), F32),
        compiler_params=_params("arbitrary"),
        name="ada_mod",
    )(_c_slabs(c_pad, n_slabs), ada_w, vecs)


def _modulate(x, gain, shift_ref, scale_ref, b):
    shift = shift_ref[0, pl.ds(b, 1), :]
    scale = scale_ref[0, pl.ds(b, 1), :]
    return _rms(x, gain) * (1.0 + scale) + shift


def _prenorm_kernel(x_ref, g_ref, shift_ref, scale_ref, pos_ref, freq_ref,
                    o_ref, cos_r_ref, sin_r_ref, cos_m_ref, sin_m_ref, *, tiles_per_batch):
    b = pl.program_id(0) // tiles_per_batch
    o_ref[...] = _modulate(x_ref[...], g_ref[0], shift_ref, scale_ref, b).astype(o_ref.dtype)

    ang = freq_ref[...] * pos_ref[0].astype(F32)
    cos_t = jnp.cos(ang)
    sin_t = jnp.sin(ang)
    n_r, n_m = RET_DIM // 2, MLA_ROPE // 2
    c_r, s_r = cos_t[0:n_r], sin_t[0:n_r]
    c_m, s_m = cos_t[n_r:n_r + n_m], sin_t[n_r:n_r + n_m]
    pad = jnp.zeros((LANE - MLA_ROPE, ang.shape[1]), F32)
    cos_r_ref[...] = jnp.concatenate([c_r, c_r], axis=0).T
    sin_r_ref[...] = jnp.concatenate([-s_r, s_r], axis=0).T
    cos_m_ref[...] = jnp.concatenate([c_m, c_m, pad], axis=0).T
    sin_m_ref[...] = jnp.concatenate([-s_m, s_m, pad], axis=0).T


def _prenorm_call(x2, mod, vecs, positions, layer, seq):
    m, d = x2.shape
    rows = mod.shape[1]
    n_tiles = m // NORM_TM

    def inv_freq(dim):
        return ROPE_BASE ** (-jnp.arange(0, dim, 2, dtype=F32) / dim)

    freqs = jnp.concatenate([inv_freq(RET_DIM), inv_freq(MLA_ROPE)])[:, None]
    table = pl.BlockSpec((NORM_TM, LANE), lambda i: (i, 0))
    return pl.pallas_call(
        functools.partial(_prenorm_kernel, tiles_per_batch=seq // NORM_TM),
        grid=(n_tiles,),
        in_specs=[
            pl.BlockSpec((NORM_TM, d), lambda i: (i, 0)),
            _vec_spec(d, layer, 1, VEC_NORM_PRE),
            pl.BlockSpec((1, rows, d), lambda i: (0, 0, 0)),
            pl.BlockSpec((1, rows, d), lambda i: (0, 0, 1)),
            pl.BlockSpec((1, 1, NORM_TM), lambda i: (i, 0, 0)),
            pl.BlockSpec(freqs.shape, lambda i: (0, 0)),
        ],
        out_specs=(pl.BlockSpec((NORM_TM, d), lambda i: (i, 0)), table, table, table, table),
        out_shape=(jax.ShapeDtypeStruct((m, d), BF16),) + (jax.ShapeDtypeStruct((m, LANE), F32),) * 4,
        compiler_params=_params("arbitrary"),
        name="pre_norm",
    )(x2, vecs, mod, mod, positions.reshape(n_tiles, 1, NORM_TM), freqs)


def _inproj_kernel(h_ref, w_ref, wx_ref, o_ref, wb_ref):
    j = pl.program_id(0)
    first_row_tile = pl.program_id(1) == 0
    n_chunks = PROJ_TN // CAST_ROWS

    @pl.when(first_row_tile & (j < ALIGNED_TILES))
    def _():
        def body(i, carry):
            r0 = pl.multiple_of(i * CAST_ROWS, CAST_ROWS)
            wb_ref[pl.ds(r0, CAST_ROWS), :] = w_ref[0, pl.ds(r0, CAST_ROWS), :].astype(BF16)
            return carry

        lax.fori_loop(0, n_chunks, body, 0)

    @pl.when(first_row_tile & (j >= ALIGNED_TILES))
    def _():
        def body(i, carry):
            r0 = pl.multiple_of(i * CAST_ROWS, CAST_ROWS)
            wb_ref[pl.ds(r0, CAST_ROWS), :] = w_ref[0, pl.ds(r0 + MLA_ROPE, CAST_ROWS), :].astype(BF16)
            return carry

        lax.fori_loop(0, n_chunks - 1, body, 0)
        wb_ref[PROJ_TN - MLA_ROPE:, :] = wx_ref[0].astype(BF16)

    o_ref[...] = _dot_nt(h_ref[...], wb_ref[...]).astype(o_ref.dtype)


def _inproj_call(h, w_in_t, layer):
    m, d = h.shape
    per_tile = PROJ_TN // MLA_ROPE
    last_block = SRC_WIDTH // MLA_ROPE - 1
    return pl.pallas_call(
        _inproj_kernel,
        grid=(PROJ_WIDTH // PROJ_TN, m // PROJ_TM),
        in_specs=[
            pl.BlockSpec((PROJ_TM, d), lambda j, i: (i, 0)),
            pl.BlockSpec((1, PROJ_TN, d), lambda j, i: (layer, j, 0)),
            pl.BlockSpec((1, MLA_ROPE, d),
                         lambda j, i: (layer, jnp.minimum((j + 1) * per_tile, last_block), 0)),
        ],
        out_specs=pl.BlockSpec((PROJ_TM, PROJ_TN), lambda j, i: (i, j)),
        out_shape=jax.ShapeDtypeStruct((m, PROJ_WIDTH), BF16),
        scratch_shapes=[pltpu.VMEM((PROJ_TN, d), BF16)],
        compiler_params=_params("arbitrary", "arbitrary"),
        name="in_proj",
    )(h, w_in_t, w_in_t)


_LOG_GAMMA = tuple(math.log1p(-(2.0 ** (-5.0 - h))) for h in range(RET_HEADS))


def _ret_kernel(q_ref, k_ref, v_ref, g_ref, cos_ref, sin_ref, gn_ref,
                h_ref, wkr_ref, cos_m_ref, sin_m_ref, mkv_ref, kvn_ref, wkv_ref, mq_ref, qn_ref, wq_ref,
                o_ref, kr_out, kv_out, q_out,
                state_ref, dmat_ref, dq_ref, dk_ref, wkr_s, wkv_s, wq_s):
    t_blk = RET_T

    @pl.when((pl.program_id(0) == 0) & (pl.program_id(1) == 0))
    def _():
        wkr = jnp.concatenate([wkr_ref[0], jnp.zeros((LANE - MLA_ROPE, D_MODEL), F32)], axis=0)
        wkr_s[...] = wkr.T.astype(BF16)
        wkv_s[...] = wkv_ref[0].astype(BF16)
        zeros = jnp.zeros((MLA_LORA, MLA_QK_PAD - MLA_QK), F32)
        wq_all = wq_ref[0]
        for hh in range(MLA_HEADS):
            w_head = wq_all[:, hh * MLA_QK:(hh + 1) * MLA_QK]
            wq_s[:, hh * MLA_QK_PAD:(hh + 1) * MLA_QK_PAD] = jnp.concatenate([w_head, zeros], axis=1).astype(BF16)
        ri = lax.broadcasted_iota(jnp.int32, (t_blk, t_blk), 0)
        ci = lax.broadcasted_iota(jnp.int32, (t_blk, t_blk), 1)
        dist = jnp.abs(ri - ci).astype(F32)
        visible = (ci // CHUNK) <= (ri // CHUNK)
        row = lax.broadcasted_iota(jnp.int32, (t_blk, RET_DIM), 0).astype(F32)
        for h in range(RET_HEADS):
            lg = _LOG_GAMMA[h]
            dmat_ref[h] = jnp.where(visible, jnp.exp(lg * dist), 0.0)
            dq_ref[h] = jnp.exp(lg * (row + 1.0))
            dk_ref[h] = jnp.exp(lg * ((t_blk - 1.0) - row))

    @pl.when(pl.program_id(1) == 0)
    def _():
        state_ref[...] = jnp.zeros_like(state_ref)

    kr_out[...] = _rope64(_dot(h_ref[...], wkr_s[...]), cos_m_ref[...], sin_m_ref[...]).astype(BF16)
    kv_lat = _rms(mkv_ref[...].astype(F32), kvn_ref[0]).astype(BF16)
    kv_out[...] = _dot(kv_lat, wkv_s[...]).astype(BF16)
    q_lat = _rms(mq_ref[...].astype(F32), qn_ref[0]).astype(BF16)
    q_all = _dot(q_lat, wq_s[...])
    for hh in range(MLA_HEADS):
        c0 = hh * MLA_QK_PAD
        q_out[:, c0:c0 + MLA_NOPE] = (q_all[:, c0:c0 + MLA_NOPE] * MLA_SCALE).astype(BF16)
        q_rot = _rope64(q_all[:, c0 + MLA_NOPE:c0 + MLA_QK_PAD], cos_m_ref[...], sin_m_ref[...])
        q_out[:, c0 + MLA_NOPE:c0 + MLA_QK_PAD] = (q_rot * MLA_SCALE).astype(BF16)

    cos = cos_ref[...]
    sin = sin_ref[...]
    for h in range(RET_HEADS):
        sl = slice(h * RET_DIM, (h + 1) * RET_DIM)
        q = q_ref[:, sl].astype(F32)
        k = k_ref[:, sl].astype(F32)
        q = (q * cos + pltpu.roll(q, RET_DIM // 2, 1) * sin) * (RET_DIM ** -0.5)
        k = k * cos + pltpu.roll(k, RET_DIM // 2, 1) * sin
        v = v_ref[:, sl]
        scores = _dot_nt(q.astype(BF16), k.astype(BF16)) * dmat_ref[h]
        o = _dot(scores.astype(BF16), v)
        state = state_ref[h]
        o = o + _dot((q * dq_ref[h]).astype(BF16), state.astype(BF16))
        k_dec = (k * dk_ref[h]).astype(BF16)
        state_ref[h] = state * math.exp(_LOG_GAMMA[h] * t_blk) + _dot_tn(k_dec, v)
        mean = jnp.mean(o, axis=-1, keepdims=True)
        cen = o - mean
        var = jnp.mean(cen * cen, axis=-1, keepdims=True)
        normed = cen * lax.rsqrt(var + NORM_EPS) * gn_ref[0, :, sl]
        o_ref[:, sl] = (normed * _silu(g_ref[:, sl].astype(F32))).astype(o_ref.dtype)


def _ret_call(proj, cos_r, sin_r, vecs, h, w_in_t, cos_m, sin_m, w_ukv, w_uq, layer, batch, seq):
    m, d = h.shape
    lora, kv_width = w_ukv.shape[1:]
    q_width = MLA_HEADS * MLA_QK_PAD
    rows = RET_T
    nt = seq // rows
    w = RET_WIDTH

    def col(c):
        return pl.BlockSpec((rows, w), lambda b, t, c=c: (b * nt + t, c // w))

    tab = pl.BlockSpec((rows, RET_DIM), lambda b, t: (b * nt + t, 0))
    return pl.pallas_call(
        _ret_kernel,
        grid=(batch, nt),
        in_specs=[col(COL_RQ), col(COL_RK), col(COL_RV), col(COL_RG), tab, tab,
                  _vec_spec(w, layer, 2, VEC_RET_GN),
                  pl.BlockSpec((rows, d), lambda b, t: (b * nt + t, 0)),
                  pl.BlockSpec((1, MLA_ROPE, d), lambda b, t: (layer, SRC_MKR // MLA_ROPE, 0)),
                  tab, tab,
                  pl.BlockSpec((rows, lora), lambda b, t: (b * nt + t, COL_MKV // lora)),
                  _vec_spec(lora, layer, 2, VEC_KV_NORM),
                  pl.BlockSpec((1, lora, kv_width), lambda b, t: (layer, 0, 0)),
                  pl.BlockSpec((rows, lora), lambda b, t: (b * nt + t, COL_MQ // lora)),
                  _vec_spec(lora, layer, 2, VEC_Q_NORM),
                  pl.BlockSpec((1, lora, MLA_HEADS * MLA_QK), lambda b, t: (layer, 0, 0))],
        out_specs=(pl.BlockSpec((rows, w), lambda b, t: (b * nt + t, 0)),
                   tab,
                   pl.BlockSpec((rows, kv_width), lambda b, t: (b * nt + t, 0)),
                   pl.BlockSpec((rows, q_width), lambda b, t: (b * nt + t, 0))),
        out_shape=(jax.ShapeDtypeStruct((m, w), BF16),
                   jax.ShapeDtypeStruct((m, LANE), BF16),
                   jax.ShapeDtypeStruct((m, kv_width), BF16),
                   jax.ShapeDtypeStruct((m, q_width), BF16)),
        scratch_shapes=[
            pltpu.VMEM((RET_HEADS, RET_DIM, RET_DIM), F32),
            pltpu.VMEM((RET_HEADS, RET_T, RET_T), F32),
            pltpu.VMEM((RET_HEADS, RET_T, RET_DIM), F32),
            pltpu.VMEM((RET_HEADS, RET_T, RET_DIM), F32),
            pltpu.VMEM((D_MODEL, LANE), BF16),
            pltpu.VMEM((lora, kv_width), BF16),
            pltpu.VMEM((lora, q_width), BF16),
        ],
        compiler_params=_params("arbitrary", "arbitrary"),
        name="retention",
    )(proj, proj, proj, proj, cos_r, sin_r, vecs, h, w_in_t, cos_m, sin_m,
      proj, vecs, w_ukv, proj, vecs, w_uq)


def _sublane_scan(a, b, row):
    for s in (1, 2, 4):
        keep = row >= s
        a_prev = jnp.where(keep, pltpu.roll(a, s, 0), 1.0)
        b_prev = jnp.where(keep, pltpu.roll(b, s, 0), 0.0)
        b = a * b_prev + b
        a = a * a_prev
    return a, b


def _lru_kernel(x_ref, g_ref, cw_ref, cb_ref, wa_ref, ba_ref, wx_ref, bx_ref, lam_ref, *rest, with_ada):
    if with_ada:
        c_ref, aw_ref, ab_ref, o_ref, mod_ref, perm, perm_t, halo, hcar, a_s, b_s = rest
        _ada_accumulate((pl.program_id(0) == 0) & (pl.program_id(1) == 0), c_ref, aw_ref, ab_ref, mod_ref)
    else:
        o_ref, perm, perm_t, halo, hcar, a_s, b_s = rest
    t_blk = LRU_T
    clen = LRU_CLEN
    taps = CONV_WIDTH - 1
    row = lax.broadcasted_iota(jnp.int32, (SUBLANE, LRU_WIDTH), 0)

    @pl.when((pl.program_id(0) == 0) & (pl.program_id(1) == 0))
    def _():
        r = lax.broadcasted_iota(jnp.int32, (t_blk, t_blk), 0)
        c = lax.broadcasted_iota(jnp.int32, (t_blk, t_blk), 1)
        perm[...] = jnp.where(c == (r % SUBLANE) * clen + r // SUBLANE, 1.0, 0.0).astype(BF16)
        perm_t[...] = jnp.where(c == (r % clen) * SUBLANE + r // clen, 1.0, 0.0).astype(BF16)

    @pl.when(pl.program_id(1) == 0)
    def _():
        halo[...] = jnp.zeros_like(halo)
        hcar[...] = jnp.zeros_like(hcar)

    xp = _dot(perm[...], x_ref[...])
    pieces = []
    for k in range(taps):
        cur = xp[t_blk - (taps - k) * SUBLANE:t_blk - (taps - k - 1) * SUBLANE, :]
        prev = halo[k * SUBLANE:(k + 1) * SUBLANE, :]
        pieces.append(jnp.where(row == 0, pltpu.roll(prev, 1, 0), pltpu.roll(cur, 1, 0)))
    halo[...] = xp[t_blk - taps * SUBLANE:, :]
    xext = jnp.concatenate(pieces + [xp], axis=0)
    xc = cb_ref[0]
    for w in range(CONV_WIDTH):
        xc = xc + cw_ref[0, w:w + 1, :] * xext[w * SUBLANE:w * SUBLANE + t_blk, :]

    neg_lam = -lam_ref[0]
    softplus = jnp.maximum(neg_lam, 0.0) + jnp.log1p(jnp.exp(-jnp.abs(neg_lam)))
    neg_half_rate = (-0.5 * LRU_C) * softplus
    for n in range(LRU_BLOCKS):
        sl = slice(n * LRU_BLOCK_DIM, (n + 1) * LRU_BLOCK_DIM)
        xn = xc[:, sl]
        xn_b = xn.astype(BF16)
        t_r = jnp.tanh(0.5 * (_dot(xn_b, wa_ref[0, n].astype(BF16)) + ba_ref[0, :, sl]))
        t_i = jnp.tanh(0.5 * (_dot(xn_b, wx_ref[0, n].astype(BF16)) + bx_ref[0, :, sl]))
        half_rate = neg_half_rate[:, sl]
        log_a = half_rate * t_r + half_rate
        a = jnp.exp(log_a)
        a_s[:, sl] = a
        var = jnp.tanh(-log_a) * (a * a + 1.0)
        std = jnp.where(var > 0.0, var * lax.rsqrt(var), 0.0)
        half_x = 0.5 * xn
        b_s[:, sl] = std * (half_x * t_i + half_x)

    def body(j, carry):
        h, prod = carry
        r0 = j * SUBLANE
        a = a_s[pl.ds(r0, SUBLANE), :]
        h = a * h + b_s[pl.ds(r0, SUBLANE), :]
        prod = a * prod
        b_s[pl.ds(r0, SUBLANE), :] = h
        a_s[pl.ds(r0, SUBLANE), :] = prod
        return h, prod

    zeros = jnp.zeros((SUBLANE, LRU_WIDTH), F32)
    carry = (zeros, zeros + 1.0)
    for j in range(clen):
        carry = body(j, carry)
    h_end, a_end = carry

    a_inc, h_inc = _sublane_scan(a_end, h_end, row)
    h_prev_tile = hcar[...]
    h_chunk_end = a_inc * h_prev_tile + h_inc
    h_chunk_start = jnp.where(row == 0, h_prev_tile, pltpu.roll(h_chunk_end, 1, 0))
    hcar[...] = jnp.broadcast_to(h_chunk_end[SUBLANE - 1:SUBLANE, :], (SUBLANE, LRU_WIDTH))

    h_true = b_s[...] + a_s[...] * jnp.concatenate([h_chunk_start] * clen, axis=0)
    gp = _dot(perm[...], g_ref[...])
    out_p = (h_true * _silu(gp)).astype(BF16)
    o_ref[...] = _dot(perm_t[...], out_p).astype(o_ref.dtype)


def _lru_call(proj, cw, wa, wx, vecs, layer, batch, seq, ada=None):
    m = proj.shape[0]
    nt = seq // LRU_T
    w = LRU_WIDTH

    def col(c):
        return pl.BlockSpec((LRU_T, w), lambda b, t, c=c: (b * nt + t, c // w))

    def vec(offset):
        return _vec_spec(w, layer, 2, offset)

    def blk():
        return pl.BlockSpec((1, LRU_BLOCKS, LRU_BLOCK_DIM, LRU_BLOCK_DIM), lambda b, t: (layer, 0, 0, 0))

    in_specs = [col(COL_LX), col(COL_LG),
                pl.BlockSpec((1, CONV_WIDTH, w), lambda b, t: (layer, 0, 0)), vec(VEC_CONV_B),
                blk(), vec(VEC_BA), blk(), vec(VEC_BX), vec(VEC_LAM)]
    args = [proj, proj, cw, vecs, wa, vecs, wx, vecs, vecs]
    out_specs = pl.BlockSpec((LRU_T, w), lambda b, t: (b * nt + t, 0))
    out_shape = jax.ShapeDtypeStruct((m, w), BF16)
    if ada is not None:
        c_pad, ada_w = ada
        rows = c_pad.shape[0]
        _, d, n = ada_w.shape
        steps = batch * nt
        slab = d // steps
        assert slab * steps == d and slab % LANE == 0
        in_specs += [pl.BlockSpec((1, rows, slab), lambda b, t: (b * nt + t, 0, 0)),
                     pl.BlockSpec((1, slab, n), lambda b, t: (layer + 1, b * nt + t, 0)),
                     _vec_spec(n, layer + 1, 2, VEC_ADA_B)]
        args += [_c_slabs(c_pad, steps), ada_w, vecs]
        out_specs = (out_specs, pl.BlockSpec((1, rows, n), lambda b, t: (0, 0, 0)))
        out_shape = (out_shape, jax.ShapeDtypeStruct((1, rows, n), F32))
    return pl.pallas_call(
        functools.partial(_lru_kernel, with_ada=ada is not None),
        grid=(batch, nt),
        in_specs=in_specs,
        out_specs=out_specs,
        out_shape=out_shape,
        scratch_shapes=[
            pltpu.VMEM((LRU_T, LRU_T), BF16),
            pltpu.VMEM((LRU_T, LRU_T), BF16),
            pltpu.VMEM(((CONV_WIDTH - 1) * SUBLANE, w), F32),
            pltpu.VMEM((SUBLANE, w), F32),
            pltpu.VMEM((LRU_T, w), F32),
            pltpu.VMEM((LRU_T, w), F32),
        ],
        compiler_params=_params("arbitrary", "arbitrary"),
        name="rg_lru",
    )(*args)


def _rope64(x, cos_t, sin_t):
    swapped = pltpu.roll(x, MLA_ROPE // 2, 1) + pltpu.roll(x, LANE - MLA_ROPE // 2, 1)
    return x * cos_t + swapped * sin_t


def _mla_kernel(krope, kv_ref, q_ref, mg_ref, wb_ref, wo_ref, o_ref, wb_out, wo_out,
                ks, vs, s_scr, p_scr, *, seq):
    pair = pl.program_id(1)
    wb_out[...] = wb_ref[0].astype(BF16)
    wo_out[...] = wo_ref[0].astype(BF16)

    @pl.when(pair == 0)
    def _():
        for hh in range(MLA_HPS):
            vs[hh, :, MLA_V:] = jnp.ones((seq, MLA_V), BF16)

    tq = MLA_TQ
    ri = lax.broadcasted_iota(jnp.int32, (tq, tq), 0)
    ci = lax.broadcasted_iota(jnp.int32, (tq, tq), 1)
    visible = (ci // CHUNK) <= (ri // CHUNK)
    n_q = seq // tq
    kv_w = MLA_NOPE + MLA_V

    for hh in range(MLA_HPS):
        ks[hh, :, 0:MLA_NOPE] = kv_ref[:, hh * kv_w:hh * kv_w + MLA_NOPE]
        ks[hh, :, MLA_NOPE:] = krope[...]
        vs[hh, :, 0:MLA_V] = kv_ref[:, hh * kv_w + MLA_NOPE:(hh + 1) * kv_w]

    def scores(hh, i):
        q0, q1 = i * tq, (i + 1) * tq
        q_blk = q_ref[q0:q1, hh * MLA_QK_PAD:(hh + 1) * MLA_QK_PAD]
        s_scr[hh, i % 2, :, 0:q1] = _dot_nt(q_blk, ks[hh, 0:q1, :])

    def probs(hh, i):
        q0, q1 = i * tq, (i + 1) * tq
        s_d = jnp.where(visible, s_scr[hh, i % 2, :, q0:q1], -1e30)
        m = jnp.max(s_d, axis=-1, keepdims=True)
        if i > 0:
            m = jnp.maximum(m, jnp.max(s_scr[hh, i % 2, :, 0:q0], axis=-1, keepdims=True))
            p_scr[hh, i % 2, :, 0:q0] = jnp.exp2(s_scr[hh, i % 2, :, 0:q0] - m).astype(BF16)
        p_scr[hh, i % 2, :, q0:q1] = jnp.exp2(s_d - m).astype(BF16)

    def values(hh, i):
        q0, q1 = i * tq, (i + 1) * tq
        cols = slice(hh * MLA_V, (hh + 1) * MLA_V)
        acc = _dot(p_scr[hh, i % 2, :, 0:q1], vs[hh, 0:q1, :])
        out = acc[:, 0:MLA_V] / acc[:, MLA_V:]
        o_ref[q0:q1, cols] = (out * _silu(mg_ref[q0:q1, cols].astype(F32))).astype(o_ref.dtype)

    for hh in range(MLA_HPS):
        scores(hh, 0)
    for i in range(n_q + 1):
        for hh in range(MLA_HPS):
            if i + 1 < n_q:
                scores(hh, i + 1)
            if i < n_q:
                probs(hh, i)
            if i > 0:
                values(hh, i - 1)


def _mla_call(k_rope, kv_all, q_all, proj, w_branch, w_out, layer, batch, seq):
    m = proj.shape[0]
    hps = MLA_HPS
    assert MLA_HEADS % hps == 0 and COL_MG % (hps * MLA_V) == 0
    pairs = MLA_HEADS // hps
    steps = batch * pairs
    wb_rows, wo_rows = w_branch.shape[1] // steps, w_out.shape[1] // steps
    bf16_rows = 2 * SUBLANE
    assert wb_rows * steps == w_branch.shape[1] and wb_rows % bf16_rows == 0
    assert wo_rows * steps == w_out.shape[1] and wo_rows % bf16_rows == 0
    d = w_branch.shape[2]
    return pl.pallas_call(
        functools.partial(_mla_kernel, seq=seq),
        grid=(batch, pairs),
        in_specs=[
            pl.BlockSpec((seq, LANE), lambda b, hd: (b, 0)),
            pl.BlockSpec((seq, hps * (MLA_NOPE + MLA_V)), lambda b, hd: (b, hd)),
            pl.BlockSpec((seq, hps * MLA_QK_PAD), lambda b, hd: (b, hd)),
            pl.BlockSpec((seq, hps * MLA_V), lambda b, hd: (b, COL_MG // (hps * MLA_V) + hd)),
            pl.BlockSpec((1, wb_rows, d), lambda b, hd: (layer, b * pairs + hd, 0)),
            pl.BlockSpec((1, wo_rows, d), lambda b, hd: (layer, b * pairs + hd, 0)),
        ],
        out_specs=(pl.BlockSpec((seq, hps * MLA_V), lambda b, hd: (b, hd)),
                   pl.BlockSpec((wb_rows, d), lambda b, hd: (b * pairs + hd, 0)),
                   pl.BlockSpec((wo_rows, d), lambda b, hd: (b * pairs + hd, 0))),
        out_shape=(jax.ShapeDtypeStruct((m, MLA_WIDTH), BF16),
                   jax.ShapeDtypeStruct(w_branch.shape[1:], BF16),
                   jax.ShapeDtypeStruct(w_out.shape[1:], BF16)),
        scratch_shapes=[
            pltpu.VMEM((hps, seq, MLA_QK_PAD), BF16),
            pltpu.VMEM((hps, seq, 2 * MLA_V), BF16),
            pltpu.VMEM((hps, 2, MLA_TQ, seq), F32),
            pltpu.VMEM((hps, 2, MLA_TQ, seq), BF16),
        ],
        compiler_params=_params("arbitrary", "arbitrary"),
        name="mla",
    )(k_rope, kv_all, q_all, proj, w_branch, w_out)


def _merge_kernel(yr_ref, yl_ref, ym_ref, l0_ref, l1_ref, l2_ref, x_ref, res_ref, g_ref,
                  wb_ref, wo_ref, *rest, tiles_per_batch, emit_next):
    b = pl.program_id(0) // tiles_per_batch
    merged = None
    for i, (y_ref, l_ref) in enumerate(((yr_ref, l0_ref), (yl_ref, l1_ref), (ym_ref, l2_ref))):
        z = _dot(y_ref[...], wb_ref[i * BRANCH_WIDTH:(i + 1) * BRANCH_WIDTH, :])
        z = z * jax.nn.sigmoid(l_ref[...].astype(F32))
        merged = z if merged is None else merged + z
    y = _dot(merged.astype(BF16), wo_ref[...])
    post = g_ref[0] * (1.0 + res_ref[0, pl.ds(b, 1), :])
    inv = lax.rsqrt(jnp.mean(y * y, axis=-1, keepdims=True) + NORM_EPS)
    x_new = x_ref[...] + (y * inv) * post
    if emit_next:
        gn_ref, shift_ref, scale_ref, o_ref, h_ref = rest
        pre = gn_ref[0] * (1.0 + scale_ref[0, pl.ds(b, 1), :])
        inv_n = lax.rsqrt(jnp.mean(x_new * x_new, axis=-1, keepdims=True) + NORM_EPS)
        h_ref[...] = ((x_new * inv_n) * pre + shift_ref[0, pl.ds(b, 1), :]).astype(h_ref.dtype)
    else:
        (o_ref,) = rest
    o_ref[...] = x_new


def _merge_call(y_ret, y_lru, y_mla, proj, x2, mod, mod_next, vecs, wb, wo, layer, seq):
    m, d = x2.shape
    tm = MERGE_TM
    mod_rows = mod.shape[1]
    emit_next = mod_next is not None

    def rows(width, c=0):
        return pl.BlockSpec((tm, width), lambda i, c=c: (i, c))

    def resident(shape):
        return pl.BlockSpec(shape, lambda i: (0, 0), pipeline_mode=pl.Buffered(1))

    def mod_part(part):
        return pl.BlockSpec((1, mod_rows, d), lambda i: (0, 0, part))

    merge_col = COL_MERGE // d
    in_specs = [rows(BRANCH_WIDTH), rows(BRANCH_WIDTH), rows(BRANCH_WIDTH),
                rows(d, merge_col), rows(d, merge_col + 1), rows(d, merge_col + 2),
                rows(d), mod_part(2), _vec_spec(d, layer, 1, VEC_NORM_POST),
                resident(wb.shape), resident(wo.shape)]
    args = [y_ret, y_lru, y_mla, proj, proj, proj, x2, mod, vecs, wb, wo]
    out_specs = rows(d)
    out_shape = jax.ShapeDtypeStruct((m, d), F32)
    if emit_next:
        in_specs += [_vec_spec(d, layer + 1, 1, VEC_NORM_PRE), mod_part(0), mod_part(1)]
        args += [vecs, mod_next, mod_next]
        out_specs = (out_specs, rows(d))
        out_shape = (out_shape, jax.ShapeDtypeStruct((m, d), BF16))
    return pl.pallas_call(
        functools.partial(_merge_kernel, tiles_per_batch=seq // tm, emit_next=emit_next),
        grid=(m // tm,),
        in_specs=in_specs,
        out_specs=out_specs,
        out_shape=out_shape,
        compiler_params=_params("arbitrary"),
        name="merge_out",
    )(*args)


def kernel(x, c, positions, ada_w, ada_b, norm_pre, norm_post, w_in, ret_gn, lru_conv_w, lru_conv_b,
           lru_wa, lru_ba, lru_wx, lru_bx, lru_lambda, mla_q_norm, mla_w_uq, mla_kv_norm, mla_w_ukv,
           w_branch, w_out):
    batch, seq, d = x.shape
    depth = w_in.shape[0]
    m = batch * seq
    assert d == D_MODEL and w_in.shape[2] == SRC_WIDTH and batch <= SUBLANE
    assert seq % PROJ_TM == 0 and seq % RET_T == 0 and seq % LRU_T == 0 and seq % MLA_TQ == 0
    assert seq % MERGE_TM == 0 and seq % NORM_TM == 0

    c_pad = jnp.pad(c, ((0, SUBLANE - batch), (0, 0)))
    vecs = _pack_vectors(ada_b, norm_pre, norm_post, ret_gn, lru_conv_b, lru_ba, lru_bx, lru_lambda,
                         mla_q_norm, mla_kv_norm)
    mod = _ada_call(c_pad, ada_w, vecs, 0)
    w_in_t = jnp.swapaxes(w_in, 1, 2)

    x2 = x.reshape(m, d)
    h, cos_r, sin_r, cos_m, sin_m = _prenorm_call(x2, mod, vecs, positions, 0, seq)
    for l in range(depth):
        proj = _inproj_call(h, w_in_t, l)
        y_ret, k_rope, kv_all, q_all = _ret_call(proj, cos_r, sin_r, vecs, h, w_in_t, cos_m, sin_m,
                                                 mla_w_ukv, mla_w_uq, l, batch, seq)
        lru_args = (proj, lru_conv_w, lru_wa, lru_wx, vecs, l, batch, seq)
        if l + 1 < depth:
            y_lru, mod_next = _lru_call(*lru_args, ada=(c_pad, ada_w))
        else:
            y_lru, mod_next = _lru_call(*lru_args), None
        y_mla, wb, wo = _mla_call(k_rope, kv_all, q_all, proj, w_branch, w_out, l, batch, seq)
        out = _merge_call(y_ret, y_lru, y_mla, proj, x2, mod, mod_next, vecs, wb, wo, l, seq)
        mod = mod_next
        if l + 1 < depth:
            x2, h = out
        else:
            x2 = out
    return x2.reshape(batch, seq, d)
```
